```python
import functools
import jax, jax.numpy as jnp
from jax import lax
import numpy as np

D_MODEL = 2048
BATCH = 2
SEQ = 4096
DEPTH = 2
DEC_BATCH = 32
DEC_SEQ = 1
PAST_LEN = 16384
PAGE_SIZE = 128

ATT_WIDTH = D_MODEL // 2
RWKV_WIDTH = D_MODEL - ATT_WIDTH
HEAD_DIM = 64
N_Q_HEADS = ATT_WIDTH // HEAD_DIM
N_KV_HEADS = 4
GQA_GROUP = N_Q_HEADS // N_KV_HEADS
KV_WIDTH = N_KV_HEADS * HEAD_DIM
WINDOW = 128
BLOCK = WINDOW
ROT_DIM = HEAD_DIM // 4
ROPE_THETA = 500000.0
RW_HEAD = 64
RW_HEADS = RWKV_WIDTH // RW_HEAD
DECAY_LORA = 64
ICLR_LORA = 64
NORM_EPS = 1e-5
GN_EPS = 64e-5
NEG_BIG = -1e30

Q_OFF = 0
KA_OFF = Q_OFF + ATT_WIDTH
VA_OFF = KA_OFF + KV_WIDTH
R_OFF = VA_OFF + KV_WIDTH
KR_OFF = R_OFF + RWKV_WIDTH
VR_OFF = KR_OFF + RWKV_WIDTH
WD_OFF = VR_OFF + RWKV_WIDTH
AD_OFF = WD_OFF + DECAY_LORA
GA_OFF = AD_OFF + ICLR_LORA
GR_OFF = GA_OFF + ATT_WIDTH
IN_WIDTH = GR_OFF + RWKV_WIDTH
SHIFT_DIM = GA_OFF - R_OFF

kernel_name = "hymba_swa_sink_rwkv7_adaln_step"


def rms_norm(x, g):
    xf = x.astype(jnp.float32)
    y = xf * lax.rsqrt(jnp.mean(xf * xf, axis=-1, keepdims=True) + NORM_EPS)
    return (y * g.astype(jnp.float32)).astype(x.dtype)


def apply_partial_rope(x, pos):
    half = ROT_DIM // 2
    inv_freq = ROPE_THETA ** (-jnp.arange(half, dtype=jnp.float32) * (2.0 / ROT_DIM))
    ang = pos.astype(jnp.float32)[:, None] * inv_freq[None, :]
    cos = jnp.cos(ang)[:, None, :]
    sin = jnp.sin(ang)[:, None, :]
    xf = x.astype(jnp.float32)
    x1 = xf[..., :half]
    x2 = xf[..., half:ROT_DIM]
    out = jnp.concatenate([x1 * cos - x2 * sin, x2 * cos + x1 * sin, xf[..., ROT_DIM:]], axis=-1)
    return out.astype(x.dtype)


def sink_softmax(scores, mask, sink):
    s = jnp.where(mask, scores.astype(jnp.float32), NEG_BIG)
    m = jnp.maximum(jnp.max(s, axis=-1, keepdims=True), sink)
    p = jnp.exp(s - m)
    return p / (jnp.sum(p, axis=-1, keepdims=True) + jnp.exp(sink - m))


def attention_prompt(q, k, v, sinks):
    Bn, T = q.shape[0], q.shape[1]
    nb = T // BLOCK
    qb = q.reshape(Bn, nb, BLOCK, N_KV_HEADS, GQA_GROUP, HEAD_DIM)

    def band(t):
        tb = t.reshape(Bn, nb, BLOCK, N_KV_HEADS, HEAD_DIM)
        prev = jnp.concatenate([jnp.zeros_like(tb[:, :1]), tb[:, :-1]], axis=1)
        return jnp.concatenate([prev, tb], axis=2)

    kb, vb = band(k), band(v)
    scores = jnp.einsum('bnqkgd,bnskd->bnkgqs', qb, kb) * (HEAD_DIM ** -0.5)
    qi = jnp.arange(BLOCK)[:, None]
    kj = jnp.arange(2 * BLOCK)[None, :]
    rel = BLOCK + qi - kj
    blk = jnp.arange(nb)[:, None, None]
    mask = (rel >= 0) & (rel <= WINDOW) & ((blk > 0) | (kj >= BLOCK))
    sink = sinks.astype(jnp.float32).reshape(1, 1, N_KV_HEADS, GQA_GROUP, 1, 1)
    p = sink_softmax(scores, mask[None, :, None, None], sink).astype(v.dtype)
    out = jnp.einsum('bnkgqs,bnskd->bnqkgd', p, vb).reshape(Bn, T, ATT_WIDTH)
    return out, k[:, -WINDOW:], v[:, -WINDOW:]


def attention_sample(q, k, v, sinks, ck, cv, pos):
    Bd, S = q.shape[0], q.shape[1]
    kall = jnp.concatenate([ck.astype(k.dtype), k], axis=1)
    vall = jnp.concatenate([cv.astype(v.dtype), v], axis=1)
    kpos = jnp.concatenate([PAST_LEN - WINDOW + jnp.arange(WINDOW, dtype=jnp.int32), pos])
    rel = pos[:, None] - kpos[None, :]
    mask = (rel >= 0) & (rel <= WINDOW)
    qg = q.reshape(Bd, S, N_KV_HEADS, GQA_GROUP, HEAD_DIM)
    scores = jnp.einsum('bqkgd,bskd->bkgqs', qg, kall) * (HEAD_DIM ** -0.5)
    sink = sinks.astype(jnp.float32).reshape(1, N_KV_HEADS, GQA_GROUP, 1, 1)
    p = sink_softmax(scores, mask[None, None, None], sink).astype(v.dtype)
    out = jnp.einsum('bkgqs,bskd->bqkgd', p, vall).reshape(Bd, S, ATT_WIDTH)
    return out, kall[:, -WINDOW:], vall[:, -WINDOW:]


def wkv_recurrence(r, decay, k, v, kk, a, S0):
    def step(S, inp):
        r_t, w_t, k_t, v_t, kk_t, a_t = inp
        sa = jnp.einsum('bhij,bhj->bhi', S, -kk_t)
        S = (S * w_t[:, :, None, :] + sa[..., None] * (kk_t * a_t)[:, :, None, :]
             + v_t[..., None] * k_t[:, :, None, :])
        return S, jnp.einsum('bhij,bhj->bhi', S, r_t)

    xs = tuple(jnp.moveaxis(t, 1, 0) for t in (r, decay, k, v, kk, a))
    S_T, ys = lax.scan(step, S0, xs)
    return jnp.moveaxis(ys, 0, 1), S_T


def hybrid_layer(x, c, pos, attend, S0, shift_prev, norm_g, w_ada, b_ada, w_in, mu_shift,
                 w0, w_decay, a0, w_iclr, k_k, k_a, r_k, ln_w, ln_b, sinks, w_out):
    f32 = jnp.float32
    Bn, T = x.shape[0], x.shape[1]
    mod = jax.nn.silu(c) @ w_ada + b_ada
    shift, scale, gate = jnp.split(mod, 3, axis=-1)
    h = rms_norm(x, norm_g) * (1 + scale[:, None, :]) + shift[:, None, :]
    proj = h @ w_in

    q = apply_partial_rope(proj[..., Q_OFF:KA_OFF].reshape(Bn, T, N_Q_HEADS, HEAD_DIM), pos)
    k = apply_partial_rope(proj[..., KA_OFF:VA_OFF].reshape(Bn, T, N_KV_HEADS, HEAD_DIM), pos)
    v = proj[..., VA_OFF:R_OFF].reshape(Bn, T, N_KV_HEADS, HEAD_DIM)
    att, k_buf, v_buf = attend(q, k, v, sinks)

    cur = proj[..., R_OFF:GA_OFF]
    prev = jnp.concatenate([shift_prev[:, None, :].astype(cur.dtype), cur[:, :-1]], axis=1)
    mixed = (cur + (prev - cur) * mu_shift).astype(f32)
    r = mixed[..., 0:KR_OFF - R_OFF]
    kr = mixed[..., KR_OFF - R_OFF:VR_OFF - R_OFF]
    vr = mixed[..., VR_OFF - R_OFF:WD_OFF - R_OFF]
    wd = mixed[..., WD_OFF - R_OFF:AD_OFF - R_OFF]
    ad = mixed[..., AD_OFF - R_OFF:]
    w_log = -jax.nn.softplus(-(w0.astype(f32) + jnp.tanh(wd) @ w_decay.astype(f32))) - 0.5
    decay = jnp.exp(-jnp.exp(w_log))
    a = jax.nn.sigmoid(a0.astype(f32) + ad @ w_iclr.astype(f32))
    heads = lambda t: t.reshape(Bn, T, RW_HEADS, RW_HEAD)
    kk = heads(kr * k_k.astype(f32))
    kk = kk / jnp.maximum(jnp.sqrt(jnp.sum(kk * kk, axis=-1, keepdims=True)), 1e-12)
    k_eff = kr * (1 + (a - 1) * k_a.astype(f32))
    r_h, k_h, v_h = heads(r), heads(k_eff), heads(vr)
    ys, S_T = wkv_recurrence(r_h, heads(decay), k_h, v_h, kk, heads(a), S0.astype(f32))
    mu = jnp.mean(ys, axis=-1, keepdims=True)
    var = jnp.mean(jnp.square(ys - mu), axis=-1, keepdims=True)
    yn = ((ys - mu) * lax.rsqrt(var + GN_EPS)).reshape(Bn, T, RWKV_WIDTH)
    yn = yn * ln_w.astype(f32) + ln_b.astype(f32)
    bonus = jnp.sum(r_h * k_h * r_k.astype(f32), axis=-1, keepdims=True) * v_h
    y_rw = (yn + bonus.reshape(Bn, T, RWKV_WIDTH)).astype(x.dtype)

    g_att = jax.nn.silu(proj[..., GA_OFF:GR_OFF])
    g_rw = jax.nn.silu(proj[..., GR_OFF:IN_WIDTH])
    out = jnp.concatenate([att * g_att, y_rw * g_rw], axis=-1) @ w_out
    x = x + gate[:, None, :] * out
    return x, k_buf, v_buf, S_T, cur[:, -1]


def setup_inputs(seed: int = 0) -> dict:
    key = jax.random.key(seed)
    ks = jax.random.split(key, 25)
    f32 = jnp.float32

    def nrm(k, shape, s):
        return jax.random.normal(k, shape, f32) * s

    return {
        "x_prompt": nrm(ks[0], (BATCH, SEQ, D_MODEL), 1.0),
        "x_sample": nrm(ks[1], (DEC_BATCH, DEC_SEQ, D_MODEL), 1.0),
        "cache_k": nrm(ks[2], (DEPTH, DEC_BATCH, WINDOW, N_KV_HEADS, HEAD_DIM), 1.0),
        "cache_v": nrm(ks[3], (DEPTH, DEC_BATCH, WINDOW, N_KV_HEADS, HEAD_DIM), 1.0),
        "state_wkv": nrm(ks[4], (DEPTH, DEC_BATCH, RW_HEADS, RW_HEAD, RW_HEAD), 1.0),
        "state_shift": nrm(ks[5], (DEPTH, DEC_BATCH, SHIFT_DIM), 1.0),
        "c_prompt": nrm(ks[6], (BATCH, D_MODEL), 1.0),
        "c_sample": nrm(ks[7], (DEC_BATCH, D_MODEL), 1.0),
        "norm_g": 1.0 + nrm(ks[8], (DEPTH, D_MODEL), 0.01),
        "w_ada": nrm(ks[9], (DEPTH, D_MODEL, 3 * D_MODEL), 0.5 * D_MODEL ** -0.5),
        "b_ada": nrm(ks[10], (DEPTH, 3 * D_MODEL), 0.01),
        "w_in": nrm(ks[11], (DEPTH, D_MODEL, IN_WIDTH), D_MODEL ** -0.5),
        "mu_shift": jax.random.uniform(ks[12], (DEPTH, SHIFT_DIM), f32),
        "w0": nrm(ks[13], (DEPTH, RWKV_WIDTH), 0.5),
        "w_decay": nrm(ks[14], (DEPTH, DECAY_LORA, RWKV_WIDTH), 0.1),
        "a0": nrm(ks[15], (DEPTH, RWKV_WIDTH), 0.1),
        "w_iclr": nrm(ks[16], (DEPTH, ICLR_LORA, RWKV_WIDTH), 0.1),
        "k_k": 1.0 + nrm(ks[17], (DEPTH, RWKV_WIDTH), 0.1),
        "k_a": 1.0 + nrm(ks[18], (DEPTH, RWKV_WIDTH), 0.1),
        "r_k": nrm(ks[19], (DEPTH, RW_HEADS, RW_HEAD), 0.1),
        "ln_w": 1.0 + nrm(ks[20], (DEPTH, RWKV_WIDTH), 0.1),
        "ln_b": nrm(ks[21], (DEPTH, RWKV_WIDTH), 0.01),
        "sinks": nrm(ks[22], (DEPTH, N_Q_HEADS), 0.5),
        "w_out": nrm(ks[23], (DEPTH, D_MODEL, D_MODEL), D_MODEL ** -0.5),
        "final_g": 1.0 + nrm(ks[24], (D_MODEL,), 0.01),
    }


def reference(x_prompt, x_sample, cache_k, cache_v, state_wkv, state_shift, c_prompt, c_sample,
              norm_g, w_ada, b_ada, w_in, mu_shift, w0, w_decay, a0, w_iclr, k_k, k_a, r_k,
              ln_w, ln_b, sinks, w_out, final_g):
    Bp, Tp = x_prompt.shape[0], x_prompt.shape[1]
    Ts = x_sample.shape[1]
    pos_p = jnp.arange(Tp, dtype=jnp.int32)
    pos_s = PAST_LEN + jnp.arange(Ts, dtype=jnp.int32)
    S0_p = jnp.zeros((Bp, RW_HEADS, RW_HEAD, RW_HEAD), jnp.float32)
    shift0_p = jnp.zeros((Bp, SHIFT_DIM), x_prompt.dtype)

    hp, hs = x_prompt, x_sample
    kp_l, vp_l, sp_l, shp_l = [], [], [], []
    ks_l, vs_l, ss_l, shs_l = [], [], [], []
    for l in range(DEPTH):
        lw = (norm_g[l], w_ada[l], b_ada[l], w_in[l], mu_shift[l], w0[l], w_decay[l], a0[l],
              w_iclr[l], k_k[l], k_a[l], r_k[l], ln_w[l], ln_b[l], sinks[l], w_out[l])
        hp, kb, vb, S_T, sh = hybrid_layer(hp, c_prompt, pos_p, attention_prompt, S0_p, shift0_p, *lw)
        kp_l.append(kb); vp_l.append(vb); sp_l.append(S_T); shp_l.append(sh)
        attend_s = functools.partial(attention_sample, ck=cache_k[l], cv=cache_v[l], pos=pos_s)
        hs, kb, vb, S_T, sh = hybrid_layer(hs, c_sample, pos_s, attend_s, state_wkv[l], state_shift[l], *lw)
        ks_l.append(kb); vs_l.append(vb); ss_l.append(S_T); shs_l.append(sh)

    y_prompt = rms_norm(hp, final_g)
    y_sample = rms_norm(hs, final_g)
    new_cache_k_prompt = jnp.stack(kp_l)
    new_cache_v_prompt = jnp.stack(vp_l)
    new_state_wkv_prompt = jnp.stack(sp_l)
    new_state_shift_prompt = jnp.stack(shp_l)
    new_cache_k_sample = jnp.stack(ks_l)
    new_cache_v_sample = jnp.stack(vs_l)
    new_state_wkv_sample = jnp.stack(ss_l)
    new_state_shift_sample = jnp.stack(shs_l)
    return (y_prompt, y_sample, new_cache_k_prompt, new_cache_v_prompt, new_state_wkv_prompt,
            new_state_shift_prompt, new_cache_k_sample, new_cache_v_sample, new_state_wkv_sample,
            new_state_shift_sample)
```

```python
import functools

import jax
import jax.numpy as jnp
from jax import lax
from jax.experimental import pallas as pl
from jax.experimental.pallas import tpu as pltpu

f32 = jnp.float32
bf16 = jnp.bfloat16

D_MODEL = 2048
DEPTH = 2
PAST_LEN = 16384
ATT_WIDTH = 1024
RWKV_WIDTH = 1024
HEAD_DIM = 64
N_Q_HEADS = 16
N_KV_HEADS = 4
GQA_GROUP = 4
KV_WIDTH = 256
WINDOW = 128
ROT_DIM = 16
ROPE_THETA = 500000.0
RW_HEAD = 64
RW_HEADS = 16
DECAY_LORA = 64
ICLR_LORA = 64
LORA_PAIR = DECAY_LORA + ICLR_LORA
NORM_EPS = 1e-5
GN_EPS = 64e-5
NEG_BIG = -1e30

Q_OFF = 0
KA_OFF = Q_OFF + ATT_WIDTH
VA_OFF = KA_OFF + KV_WIDTH
R_OFF = VA_OFF + KV_WIDTH
KR_OFF = R_OFF + RWKV_WIDTH
VR_OFF = KR_OFF + RWKV_WIDTH
WD_OFF = VR_OFF + RWKV_WIDTH
AD_OFF = WD_OFF + DECAY_LORA
GA_OFF = AD_OFF + ICLR_LORA
GR_OFF = GA_OFF + ATT_WIDTH
IN_WIDTH = GR_OFF + RWKV_WIDTH
SHIFT_DIM = GA_OFF - R_OFF

LANES = 128
P_Q = 0
P_GA = 1024
P_GR = 2048
P_R = 3072
P_KR = 4096
P_VR = 5120
P_KA = 6144
P_VA = 6400
P_WA = 6656
P_WIDTH = 7168
PROJ_TN = 1024

VMEM_LIMIT = 56 * 1024 * 1024


def _silu(x):
    return x * jax.nn.sigmoid(x)


def _cparams(sem):
    return pltpu.CompilerParams(dimension_semantics=sem, vmem_limit_bytes=VMEM_LIMIT)


def _ada_kernel(c_ref, w_ref, b_ref, o_ref):
    c = c_ref[...]
    o_ref[0] = jnp.dot(_silu(c), w_ref[0], precision=lax.Precision.HIGHEST,
                       preferred_element_type=f32) + b_ref[0]


def _ada(c_all, w_ada, b_ada):
    rows = c_all.shape[0]
    tn = 768
    n_out = w_ada.shape[2]
    return pl.pallas_call(
        _ada_kernel,
        grid=(DEPTH, n_out // tn),
        in_specs=[
            pl.BlockSpec((rows, D_MODEL), lambda l, n: (0, 0)),
            pl.BlockSpec((1, D_MODEL, tn), lambda l, n: (l, 0, n)),
            pl.BlockSpec((1, 1, tn), lambda l, n: (l, 0, n)),
        ],
        out_specs=pl.BlockSpec((1, rows, tn), lambda l, n: (l, 0, n)),
        out_shape=jax.ShapeDtypeStruct((DEPTH, rows, n_out), f32),
        compiler_params=_cparams(("parallel", "parallel")),
        name="ada_mod",
    )(c_all, w_ada, b_ada.reshape(DEPTH, 1, n_out))


def _norm_proj_kernel(x_ref, g_ref, scale_ref, shift_ref, w_ref, o_ref, h_ref):
    @pl.when(pl.program_id(2) == 0)
    def _():
        x = x_ref[0]
        ms = jnp.mean(x * x, axis=-1, keepdims=True)
        y = x * lax.rsqrt(ms + NORM_EPS) * g_ref[...]
        h_ref[...] = (y * (1.0 + scale_ref[0]) + shift_ref[0]).astype(bf16)

    o_ref[0] = jnp.dot(h_ref[...], w_ref[...], preferred_element_type=f32)


def _mod_spec(mod, tm):
    if mod.shape[1] == 1:
        return pl.BlockSpec((1, 1, D_MODEL), lambda g, m, *_: (g, 0, 0))
    return pl.BlockSpec((1, tm, D_MODEL), lambda g, m, *_: (g, m, 0))


def _norm_proj(x, norm_g, scale, shift, w_bf, tm):
    G, R, _ = x.shape
    return pl.pallas_call(
        _norm_proj_kernel,
        grid=(G, R // tm, P_WIDTH // PROJ_TN),
        in_specs=[
            pl.BlockSpec((1, tm, D_MODEL), lambda g, m, n: (g, m, 0)),
            pl.BlockSpec((1, D_MODEL), lambda g, m, n: (0, 0)),
            _mod_spec(scale, tm),
            _mod_spec(shift, tm),
            pl.BlockSpec((D_MODEL, PROJ_TN), lambda g, m, n: (0, n)),
        ],
        out_specs=pl.BlockSpec((1, tm, PROJ_TN), lambda g, m, n: (g, m, n)),
        out_shape=jax.ShapeDtypeStruct((G, R, P_WIDTH), f32),
        scratch_shapes=[pltpu.VMEM((tm, D_MODEL), bf16)],
        compiler_params=_cparams(("parallel", "parallel", "arbitrary")),
        name="norm_proj",
    )(x, norm_g.reshape(1, D_MODEL), scale, shift, w_bf)


def _rope_tables(pos):
    half = ROT_DIM // 2
    inv_freq = ROPE_THETA ** (-jnp.arange(half, dtype=f32) * (2.0 / ROT_DIM))
    ang = pos.astype(f32)[:, None] * inv_freq[None, :]
    cos, sin = jnp.cos(ang), jnp.sin(ang)
    t = pos.shape[0]
    z8 = jnp.zeros((t, half), f32)
    rest = HEAD_DIM - ROT_DIM
    cos64 = jnp.concatenate([cos, cos, jnp.ones((t, rest), f32)], axis=1)
    up64 = jnp.concatenate([-sin, z8, jnp.zeros((t, rest), f32)], axis=1)
    dn64 = jnp.concatenate([z8, sin, jnp.zeros((t, rest), f32)], axis=1)
    rep = LANES // HEAD_DIM
    return jnp.stack([jnp.tile(a, (1, rep)) for a in (cos64, up64, dn64)])


def _rope(x, tab):
    w = x.shape[1]
    reps = w // LANES
    cosf, up, dn = (jnp.concatenate([tab[i]] * reps, axis=1) for i in range(3))
    half = ROT_DIM // 2
    return x * cosf + pltpu.roll(x, w - half, 1) * up + pltpu.roll(x, half, 1) * dn


def _attn_prompt_kernel(sinks_ref, q_ref, kc_ref, kp_ref, vc_ref, vp_ref, ga_ref, tc_ref, tp_ref,
                        o_ref, krot_ref):
    n = pl.program_id(1)
    q = (_rope(q_ref[0], tc_ref[...]) * (HEAD_DIM ** -0.5)).astype(bf16)
    kc = _rope(kc_ref[0], tc_ref[...])
    kp = _rope(kp_ref[0], tp_ref[...])
    krot_ref[0] = kc
    kcat = jnp.concatenate([kp, kc], axis=0).astype(bf16)
    vcat = jnp.concatenate([vp_ref[0], vc_ref[0]], axis=0).astype(bf16)
    qi = lax.broadcasted_iota(jnp.int32, (WINDOW, 2 * WINDOW), 0)
    kj = lax.broadcasted_iota(jnp.int32, (WINDOW, 2 * WINDOW), 1)
    rel = WINDOW + qi - kj
    mask = (rel >= 0) & (rel <= WINDOW) & ((kj >= WINDOW) | (n > 0))
    outs = []
    for h in range(N_Q_HEADS):
        g = h // GQA_GROUP
        kg = kcat[:, g * HEAD_DIM:(g + 1) * HEAD_DIM]
        vg = vcat[:, g * HEAD_DIM:(g + 1) * HEAD_DIM]
        s = lax.dot_general(q[:, h * HEAD_DIM:(h + 1) * HEAD_DIM], kg, (((1,), (1,)), ((), ())),
                            preferred_element_type=f32)
        s = jnp.where(mask, s, NEG_BIG)
        sink = sinks_ref[h]
        m = jnp.maximum(jnp.max(s, axis=-1, keepdims=True), sink)
        p = jnp.exp(s - m)
        den = jnp.sum(p, axis=-1, keepdims=True) + jnp.exp(sink - m)
        outs.append(jnp.dot(p.astype(bf16), vg, preferred_element_type=f32) / den)
    ga = ga_ref[0]
    o_ref[0] = jnp.concatenate(outs, axis=1) * _silu(ga)


def _attn_prompt(proj, sinks, tab):
    B, T, _ = proj.shape
    nb = T // WINDOW
    kvb = KV_WIDTH
    prev = lambda b, n: jnp.maximum(n - 1, 0)
    return pl.pallas_call(
        _attn_prompt_kernel,
        grid=(B, nb),
        in_specs=[
            pl.BlockSpec(memory_space=pltpu.SMEM),
            pl.BlockSpec((1, WINDOW, ATT_WIDTH), lambda b, n: (b, n, P_Q // ATT_WIDTH)),
            pl.BlockSpec((1, WINDOW, kvb), lambda b, n: (b, n, P_KA // kvb)),
            pl.BlockSpec((1, WINDOW, kvb), lambda b, n: (b, prev(b, n), P_KA // kvb)),
            pl.BlockSpec((1, WINDOW, kvb), lambda b, n: (b, n, P_VA // kvb)),
            pl.BlockSpec((1, WINDOW, kvb), lambda b, n: (b, prev(b, n), P_VA // kvb)),
            pl.BlockSpec((1, WINDOW, ATT_WIDTH), lambda b, n: (b, n, P_GA // ATT_WIDTH)),
            pl.BlockSpec((3, WINDOW, LANES), lambda b, n: (0, n, 0)),
            pl.BlockSpec((3, WINDOW, LANES), lambda b, n: (0, prev(b, n), 0)),
        ],
        out_specs=[
            pl.BlockSpec((1, WINDOW, ATT_WIDTH), lambda b, n: (b, n, 0)),
            pl.BlockSpec((1, WINDOW, kvb), lambda b, n: (b, 0, 0)),
        ],
        out_shape=[
            jax.ShapeDtypeStruct((B, T, ATT_WIDTH), f32),
            jax.ShapeDtypeStruct((B, WINDOW, kvb), f32),
        ],
        compiler_params=_cparams(("parallel", "arbitrary")),
        name="attn_prompt",
    )(sinks, proj, proj, proj, proj, proj, proj, tab, tab)


def _attn_sample_kernel(sinks_ref, q_ref, kn_ref, vn_ref, ga_ref, ck_ref, cv_ref, tab_ref,
                        o_ref, nk_ref, nv_ref):
    q = _rope(q_ref[0], tab_ref[...]) * (HEAD_DIM ** -0.5)
    kn = _rope(kn_ref[0], tab_ref[...])
    vn = vn_ref[0]
    ck = ck_ref[0]
    cv = cv_ref[0]
    ckb = ck.astype(bf16)
    cvb = cv.astype(bf16)
    outs = []
    for g in range(N_KV_HEADS):
        lo, hi = g * HEAD_DIM, (g + 1) * HEAD_DIM
        qg = jnp.concatenate(
            [q[:, (g * GQA_GROUP + i) * HEAD_DIM:(g * GQA_GROUP + i + 1) * HEAD_DIM]
             for i in range(GQA_GROUP)], axis=0)
        s = lax.dot_general(qg.astype(bf16), ckb[:, lo:hi], (((1,), (1,)), ((), ())),
                            preferred_element_type=f32)
        s_new = jnp.sum(qg * kn[:, lo:hi], axis=-1, keepdims=True)
        sink = sinks_ref[g * GQA_GROUP:(g + 1) * GQA_GROUP, :]
        m = jnp.maximum(jnp.maximum(jnp.max(s, axis=-1, keepdims=True), s_new), sink)
        p = jnp.exp(s - m)
        p_new = jnp.exp(s_new - m)
        den = jnp.sum(p, axis=-1, keepdims=True) + p_new + jnp.exp(sink - m)
        o = jnp.dot(p.astype(bf16), cvb[:, lo:hi], preferred_element_type=f32) + p_new * vn[:, lo:hi]
        o = o / den
        outs.extend(o[i:i + 1, :] for i in range(GQA_GROUP))
    ga = ga_ref[0]
    o_ref[0] = jnp.concatenate(outs, axis=1) * _silu(ga)
    row = lax.broadcasted_iota(jnp.int32, (WINDOW, KV_WIDTH), 0)
    last = row == WINDOW - 1
    nk_ref[0] = jnp.where(last, kn, pltpu.roll(ck, WINDOW - 1, 0))
    nv_ref[0] = jnp.where(last, vn, pltpu.roll(cv, WINDOW - 1, 0))


def _attn_sample(q, kn, vn, ga, ck, cv, sinks, tab):
    Bd = q.shape[0]
    row = lambda w: pl.BlockSpec((1, 1, w), lambda b: (b, 0, 0))
    cache = pl.BlockSpec((1, WINDOW, KV_WIDTH), lambda b: (b, 0, 0))
    return pl.pallas_call(
        _attn_sample_kernel,
        grid=(Bd,),
        in_specs=[
            pl.BlockSpec((N_Q_HEADS, 1), lambda b: (0, 0)),
            row(ATT_WIDTH), row(KV_WIDTH), row(KV_WIDTH), row(ATT_WIDTH), cache, cache,
            pl.BlockSpec((3, 1, LANES), lambda b: (0, 0, 0)),
        ],
        out_specs=[row(ATT_WIDTH), cache, cache],
        out_shape=[
            jax.ShapeDtypeStruct((Bd, 1, ATT_WIDTH), f32),
            jax.ShapeDtypeStruct((Bd, WINDOW, KV_WIDTH), f32),
            jax.ShapeDtypeStruct((Bd, WINDOW, KV_WIDTH), f32),
        ],
        compiler_params=_cparams(("parallel",)),
        name="attn_sample",
    )(sinks.reshape(N_Q_HEADS, 1), q, kn, vn, ga, ck, cv, tab)


def _rwkv_prep_kernel(shift_rows, r_ref, kr_ref, vr_ref, wa_ref, pr_ref, pkr_ref, pvr_ref, pwa_ref,
                      mu_r_ref, mu_kr_ref, mu_vr_ref, mu_wa_ref, w0_ref, a0_ref, kk_ref, ka_ref,
                      wd_ref, wi_ref, ro_ref, wo_ref, ko_ref, vo_ref, kko_ref, ao_ref):
    m = pl.program_id(1)

    def mixed(cur_ref, prev_ref, mu_ref):
        cur = cur_ref[0]
        if shift_rows:
            above = jnp.where(m == 0, 0.0, prev_ref[0][7:8, :])
            rowi = lax.broadcasted_iota(jnp.int32, cur.shape, 0)
            prev = jnp.where(rowi == 0, above, pltpu.roll(cur, 1, 0))
        else:
            prev = prev_ref[0]
        return cur + (prev - cur) * mu_ref[...]

    r = mixed(r_ref, pr_ref, mu_r_ref)
    kr = mixed(kr_ref, pkr_ref, mu_kr_ref)
    vr = mixed(vr_ref, pvr_ref, mu_vr_ref)
    wa = mixed(wa_ref, pwa_ref, mu_wa_ref)
    hi = lax.Precision.HIGHEST
    z = w0_ref[...] + jnp.dot(jnp.tanh(wa), wd_ref[...], precision=hi, preferred_element_type=f32)
    nz = -z
    softplus = jnp.maximum(nz, 0.0) + jnp.log1p(jnp.exp(-jnp.abs(nz)))
    w_log = -softplus - 0.5
    a = jax.nn.sigmoid(a0_ref[...] + jnp.dot(wa, wi_ref[...], precision=hi, preferred_element_type=f32))
    ro_ref[0] = r
    wo_ref[0] = jnp.exp(-jnp.exp(w_log))
    ko_ref[0] = kr * (1.0 + (a - 1.0) * ka_ref[...])
    vo_ref[0] = vr
    kko_ref[0] = kr * kk_ref[...]
    ao_ref[0] = a


def _rwkv_prep(proj, prev, mu, w0, a0, k_k, k_a, wd_pad, wi_pad, tm):
    G, R, _ = proj.shape
    W = RWKV_WIDTH
    shift_rows = prev is None
    cur_specs = [
        pl.BlockSpec((1, tm, W), lambda g, m: (g, m, P_R // W)),
        pl.BlockSpec((1, tm, W), lambda g, m: (g, m, P_KR // W)),
        pl.BlockSpec((1, tm, W), lambda g, m: (g, m, P_VR // W)),
        pl.BlockSpec((1, tm, LANES), lambda g, m: (g, m, P_WA // LANES)),
    ]
    if shift_rows:
        sub = 8
        above = lambda m: jnp.maximum(m * (tm // sub) - 1, 0)
        prev_specs = [
            pl.BlockSpec((1, sub, W), lambda g, m: (g, above(m), P_R // W)),
            pl.BlockSpec((1, sub, W), lambda g, m: (g, above(m), P_KR // W)),
            pl.BlockSpec((1, sub, W), lambda g, m: (g, above(m), P_VR // W)),
            pl.BlockSpec((1, sub, LANES), lambda g, m: (g, above(m), P_WA // LANES)),
        ]
        prev_args = [proj] * 4
    else:
        prev_specs = [
            pl.BlockSpec((1, tm, W), lambda g, m: (g, m, 0)),
            pl.BlockSpec((1, tm, W), lambda g, m: (g, m, 0)),
            pl.BlockSpec((1, tm, W), lambda g, m: (g, m, 0)),
            pl.BlockSpec((1, tm, LANES), lambda g, m: (g, m, 0)),
        ]
        prev_args = list(prev)
    vec = lambda w: pl.BlockSpec((1, w), lambda g, m: (0, 0))
    lora = pl.BlockSpec((LORA_PAIR, W), lambda g, m: (0, 0))
    out_spec = pl.BlockSpec((1, tm, W), lambda g, m: (g, m, 0))
    return pl.pallas_call(
        functools.partial(_rwkv_prep_kernel, shift_rows),
        grid=(G, R // tm),
        in_specs=cur_specs + prev_specs + [vec(W), vec(W), vec(W), vec(LANES),
                                           vec(W), vec(W), vec(W), vec(W), lora, lora],
        out_specs=[out_spec] * 6,
        out_shape=[jax.ShapeDtypeStruct((G, R, W), f32)] * 6,
        compiler_params=_cparams(("parallel", "arbitrary")),
        name="rwkv_prep",
    )(proj, proj, proj, proj, *prev_args, *mu, w0, a0, k_k, k_a, wd_pad, wi_pad)


def _wkv_kernel(tc, r_ref, w_ref, k_ref, v_ref, kk_ref, a_ref, rk_ref, lnw_ref, lnb_ref, s0_ref,
                y_ref, s_ref):
    @pl.when(pl.program_id(1) == 0)
    def _():
        s_ref[...] = s0_ref[...]

    eye = (lax.broadcasted_iota(jnp.int32, (RW_HEAD, RW_HEAD), 0)
           == lax.broadcasted_iota(jnp.int32, (RW_HEAD, RW_HEAD), 1)).astype(f32)
    rk = rk_ref[...]
    lnw = lnw_ref[...]
    lnb = lnb_ref[...]

    def step(t, carry):
        r, w, k, v, kkr, a = (ref[0, t] for ref in (r_ref, w_ref, k_ref, v_ref, kk_ref, a_ref))
        norm = jnp.sqrt(jnp.sum(kkr * kkr, axis=-1, keepdims=True))
        kk = kkr / jnp.maximum(norm, 1e-12)
        b = kk * a
        bonus = jnp.sum(r * k * rk, axis=-1, keepdims=True) * v
        rows = []
        for h in range(RW_HEADS):
            hs = slice(h, h + 1)
            S = s_ref[0, h]
            sa = jnp.sum(S * (-kk[hs]), axis=-1, keepdims=True)
            v_col = jnp.sum(eye * v[hs], axis=-1, keepdims=True)
            S = S * w[hs] + sa * b[hs] + v_col * k[hs]
            s_ref[0, h] = S
            y = jnp.sum(S * r[hs], axis=-1, keepdims=True)
            mu = jnp.mean(y, axis=0, keepdims=True)
            var = jnp.mean(jnp.square(y - mu), axis=0, keepdims=True)
            yn = (y - mu) * lax.rsqrt(var + GN_EPS)
            rows.append(jnp.sum(yn * eye, axis=0, keepdims=True))
        y_ref[0, t] = jnp.concatenate(rows, axis=0) * lnw + lnb + bonus
        return carry

    lax.fori_loop(0, tc, step, 0)


def _wkv(r, w, k, v, kk, a, r_k, ln_w, ln_b, s0, tc):
    B, T = r.shape[0], r.shape[1]
    hd = (RW_HEADS, RW_HEAD)
    heads = lambda t: t.reshape(B, T, *hd)
    seq = pl.BlockSpec((1, tc, *hd), lambda b, c: (b, c, 0, 0))
    par = pl.BlockSpec(hd, lambda b, c: (0, 0))
    state = pl.BlockSpec((1, RW_HEADS, RW_HEAD, RW_HEAD), lambda b, c: (b, 0, 0, 0))
    y, s_t = pl.pallas_call(
        functools.partial(_wkv_kernel, tc),
        grid=(B, T // tc),
        in_specs=[seq] * 6 + [par] * 3 + [state],
        out_specs=[seq, state],
        out_shape=[
            jax.ShapeDtypeStruct((B, T, *hd), f32),
            jax.ShapeDtypeStruct((B, RW_HEADS, RW_HEAD, RW_HEAD), f32),
        ],
        compiler_params=_cparams(("parallel", "arbitrary")),
        name="wkv_steps",
    )(heads(r), heads(w), heads(k), heads(v), heads(kk), heads(a),
      r_k, ln_w.reshape(hd), ln_b.reshape(hd), s0)
    return y.reshape(B, T, RWKV_WIDTH), s_t


def _post_kernel(final, att_ref, y_ref, gr_ref, x_ref, gate_ref, w_ref, fg_ref, o_ref):
    cat = jnp.concatenate([att_ref[0], y_ref[0] * _silu(gr_ref[0])], axis=1).astype(bf16)
    out = jnp.dot(cat, w_ref[...], preferred_element_type=f32)
    x = x_ref[0] + gate_ref[0] * out
    if final:
        ms = jnp.mean(x * x, axis=-1, keepdims=True)
        x = x * lax.rsqrt(ms + NORM_EPS) * fg_ref[...]
    o_ref[0] = x


def _post(att, y_rw, proj, x, gate, w_out_bf, final_g, final, tm):
    G, R, _ = x.shape
    W = RWKV_WIDTH
    return pl.pallas_call(
        functools.partial(_post_kernel, final),
        grid=(G, R // tm),
        in_specs=[
            pl.BlockSpec((1, tm, ATT_WIDTH), lambda g, m: (g, m, 0)),
            pl.BlockSpec((1, tm, W), lambda g, m: (g, m, 0)),
            pl.BlockSpec((1, tm, W), lambda g, m: (g, m, P_GR // W)),
            pl.BlockSpec((1, tm, D_MODEL), lambda g, m: (g, m, 0)),
            _mod_spec(gate, tm),
            pl.BlockSpec((D_MODEL, D_MODEL), lambda g, m: (0, 0)),
            pl.BlockSpec((1, D_MODEL), lambda g, m: (0, 0)),
        ],
        out_specs=pl.BlockSpec((1, tm, D_MODEL), lambda g, m: (g, m, 0)),
        out_shape=jax.ShapeDtypeStruct((G, R, D_MODEL), f32),
        compiler_params=_cparams(("parallel", "parallel")),
        name="post_proj",
    )(att, y_rw, proj, x, gate, w_out_bf, final_g.reshape(1, D_MODEL))


def _arrange_w_in(w):
    pad = jnp.zeros((w.shape[0], P_WIDTH - IN_WIDTH), w.dtype)
    parts = [w[:, Q_OFF:KA_OFF], w[:, GA_OFF:GR_OFF], w[:, GR_OFF:IN_WIDTH], w[:, R_OFF:WD_OFF],
             w[:, KA_OFF:R_OFF], w[:, WD_OFF:GA_OFF], pad]
    return jnp.concatenate(parts, axis=1).astype(bf16)


def _shift_cols(t):
    return jnp.concatenate([t[..., P_R:P_KA], t[..., P_WA:P_WA + LORA_PAIR]], axis=-1)


def kernel(x_prompt, x_sample, cache_k, cache_v, state_wkv, state_shift, c_prompt, c_sample,
           norm_g, w_ada, b_ada, w_in, mu_shift, w0, w_decay, a0, w_iclr, k_k, k_a, r_k,
           ln_w, ln_b, sinks, w_out, final_g):
    Bp, Tp = x_prompt.shape[0], x_prompt.shape[1]
    Bd = x_sample.shape[0]
    W = RWKV_WIDTH

    n_c = Bp + Bd
    c_rows = -(-n_c // 8) * 8
    c_all = jnp.concatenate([c_prompt, c_sample, jnp.zeros((c_rows - n_c, D_MODEL), f32)], axis=0)
    mod = _ada(c_all, w_ada, b_ada)

    tab_p = _rope_tables(jnp.arange(Tp, dtype=jnp.int32))
    tab_s = _rope_tables(PAST_LEN + jnp.arange(1, dtype=jnp.int32))

    hp = x_prompt
    hs = x_sample.reshape(1, Bd, D_MODEL)
    s0_p = jnp.zeros((Bp, RW_HEADS, RW_HEAD, RW_HEAD), f32)
    outs = {k: [] for k in ("kp", "vp", "sp", "shp", "ks", "vs", "ss", "shs")}
    for l in range(DEPTH):
        final = l == DEPTH - 1
        w_bf = _arrange_w_in(w_in[l])
        w_out_bf = w_out[l].astype(bf16)
        mu_l = mu_shift[l]
        mu = [mu_l[0:W].reshape(1, W), mu_l[W:2 * W].reshape(1, W), mu_l[2 * W:3 * W].reshape(1, W),
              mu_l[3 * W:].reshape(1, LORA_PAIR)]
        vecs = [t[l].reshape(1, W) for t in (w0, a0, k_k, k_a)]
        wd_pad = jnp.concatenate([w_decay[l], jnp.zeros((ICLR_LORA, W), f32)], axis=0)
        wi_pad = jnp.concatenate([jnp.zeros((DECAY_LORA, W), f32), w_iclr[l]], axis=0)
        shift_p, scale_p, gate_p = (mod[l, :Bp, i * D_MODEL:(i + 1) * D_MODEL].reshape(Bp, 1, D_MODEL)
                                    for i in range(3))
        shift_s, scale_s, gate_s = (mod[l, Bp:n_c, i * D_MODEL:(i + 1) * D_MODEL].reshape(1, Bd, D_MODEL)
                                    for i in range(3))

        proj = _norm_proj(hp, norm_g[l], scale_p, shift_p, w_bf, tm=1024)
        att, k_rot = _attn_prompt(proj, sinks[l], tab_p)
        prep = _rwkv_prep(proj, None, mu, *vecs, wd_pad, wi_pad, tm=256)
        y_rw, s_t = _wkv(*prep, r_k[l], ln_w[l], ln_b[l], s0_p, tc=128)
        hp = _post(att, y_rw, proj, hp, gate_p, w_out_bf, final_g, final, tm=256)
        outs["kp"].append(k_rot.reshape(Bp, WINDOW, N_KV_HEADS, HEAD_DIM))
        outs["vp"].append(proj[:, Tp - WINDOW:, P_VA:P_VA + KV_WIDTH].reshape(Bp, WINDOW, N_KV_HEADS, HEAD_DIM))
        outs["sp"].append(s_t)
        outs["shp"].append(_shift_cols(proj[:, Tp - 1]))

        proj = _norm_proj(hs, norm_g[l], scale_s, shift_s, w_bf, tm=Bd)
        rows = proj[0]
        att, nk, nv = _attn_sample(
            rows[:, None, P_Q:P_Q + ATT_WIDTH], rows[:, None, P_KA:P_KA + KV_WIDTH],
            rows[:, None, P_VA:P_VA + KV_WIDTH], rows[:, None, P_GA:P_GA + ATT_WIDTH],
            cache_k[l].reshape(Bd, WINDOW, KV_WIDTH), cache_v[l].reshape(Bd, WINDOW, KV_WIDTH),
            sinks[l], tab_s)
        sh = state_shift[l]
        prev = [sh[None, :, 0:W], sh[None, :, W:2 * W], sh[None, :, 2 * W:3 * W], sh[None, :, 3 * W:]]
        prep = _rwkv_prep(proj, prev, mu, *vecs, wd_pad, wi_pad, tm=Bd)
        prep = [t.reshape(Bd, 1, W) for t in prep]
        y_rw, s_t = _wkv(*prep, r_k[l], ln_w[l], ln_b[l], state_wkv[l], tc=1)
        hs = _post(att.reshape(1, Bd, ATT_WIDTH), y_rw.reshape(1, Bd, W), proj, hs, gate_s, w_out_bf,
                   final_g, final, tm=Bd)
        outs["ks"].append(nk.reshape(Bd, WINDOW, N_KV_HEADS, HEAD_DIM))
        outs["vs"].append(nv.reshape(Bd, WINDOW, N_KV_HEADS, HEAD_DIM))
        outs["ss"].append(s_t)
        outs["shs"].append(_shift_cols(rows))

    st = lambda k: jnp.stack(outs[k])
    return (hp, hs.reshape(Bd, 1, D_MODEL), st("kp"), st("vp"), st("sp"), st("shp"),
            st("ks"), st("vs"), st("ss"), st("shs"))
```

```python
import functools

import jax
import jax.numpy as jnp
from jax import lax
from jax.experimental import pallas as pl
from jax.experimental.pallas import tpu as pltpu

f32 = jnp.float32
bf16 = jnp.bfloat16

D_MODEL = 2048
DEPTH = 2
PAST_LEN = 16384
ATT_WIDTH = 1024
RWKV_WIDTH = 1024
HEAD_DIM = 64
N_Q_HEADS = 16
N_KV_HEADS = 4
GQA_GROUP = 4
KV_WIDTH = 256
WINDOW = 128
ROT_DIM = 16
ROPE_THETA = 500000.0
RW_HEAD = 64
RW_HEADS = 16
DECAY_LORA = 64
ICLR_LORA = 64
LORA_PAIR = DECAY_LORA + ICLR_LORA
NORM_EPS = 1e-5
GN_EPS = 64e-5
NEG_BIG = -1e30

Q_OFF = 0
KA_OFF = Q_OFF + ATT_WIDTH
VA_OFF = KA_OFF + KV_WIDTH
R_OFF = VA_OFF + KV_WIDTH
KR_OFF = R_OFF + RWKV_WIDTH
VR_OFF = KR_OFF + RWKV_WIDTH
WD_OFF = VR_OFF + RWKV_WIDTH
AD_OFF = WD_OFF + DECAY_LORA
GA_OFF = AD_OFF + ICLR_LORA
GR_OFF = GA_OFF + ATT_WIDTH
IN_WIDTH = GR_OFF + RWKV_WIDTH
SHIFT_DIM = GA_OFF - R_OFF

LANES = 128
P_Q = 0
P_GA = 1024
P_GR = 2048
P_R = 3072
P_KR = 4096
P_VR = 5120
P_KA = 6144
P_VA = 6400
P_WA = 6656
P_WIDTH = 7168
PROJ_TN = 1024

VMEM_LIMIT = 56 * 1024 * 1024


def _silu(x):
    return x * jax.nn.sigmoid(x)


def _cparams(sem):
    return pltpu.CompilerParams(dimension_semantics=sem, vmem_limit_bytes=VMEM_LIMIT)


def _ada_kernel(c_ref, w_ref, b_ref, o_ref):
    c = c_ref[...]
    o_ref[0] = jnp.dot(_silu(c), w_ref[0], precision=lax.Precision.HIGHEST,
                       preferred_element_type=f32) + b_ref[0]


def _ada(c_all, w_ada, b_ada):
    rows = c_all.shape[0]
    tn = 768
    n_out = w_ada.shape[2]
    return pl.pallas_call(
        _ada_kernel,
        grid=(DEPTH, n_out // tn),
        in_specs=[
            pl.BlockSpec((rows, D_MODEL), lambda l, n: (0, 0)),
            pl.BlockSpec((1, D_MODEL, tn), lambda l, n: (l, 0, n)),
            pl.BlockSpec((1, 1, tn), lambda l, n: (l, 0, n)),
        ],
        out_specs=pl.BlockSpec((1, rows, tn), lambda l, n: (l, 0, n)),
        out_shape=jax.ShapeDtypeStruct((DEPTH, rows, n_out), f32),
        compiler_params=_cparams(("parallel", "parallel")),
        name="ada_mod",
    )(c_all, w_ada, b_ada.reshape(DEPTH, 1, n_out))


def _norm_proj_kernel(x_ref, g_ref, scale_ref, shift_ref, w_ref, o_ref, h_ref):
    @pl.when(pl.program_id(2) == 0)
    def _():
        x = x_ref[0]
        ms = jnp.mean(x * x, axis=-1, keepdims=True)
        y = x * lax.rsqrt(ms + NORM_EPS) * g_ref[...]
        h_ref[...] = (y * (1.0 + scale_ref[0]) + shift_ref[0]).astype(bf16)

    o_ref[0] = jnp.dot(h_ref[...], w_ref[...], preferred_element_type=f32)


def _mod_spec(mod, tm):
    if mod.shape[1] == 1:
        return pl.BlockSpec((1, 1, D_MODEL), lambda g, m, *_: (g, 0, 0))
    return pl.BlockSpec((1, tm, D_MODEL), lambda g, m, *_: (g, m, 0))


def _norm_proj(x, norm_g, scale, shift, w_bf, tm):
    G, R, _ = x.shape
    return pl.pallas_call(
        _norm_proj_kernel,
        grid=(G, R // tm, P_WIDTH // PROJ_TN),
        in_specs=[
            pl.BlockSpec((1, tm, D_MODEL), lambda g, m, n: (g, m, 0)),
            pl.BlockSpec((1, D_MODEL), lambda g, m, n: (0, 0)),
            _mod_spec(scale, tm),
            _mod_spec(shift, tm),
            pl.BlockSpec((D_MODEL, PROJ_TN), lambda g, m, n: (0, n)),
        ],
        out_specs=pl.BlockSpec((1, tm, PROJ_TN), lambda g, m, n: (g, m, n)),
        out_shape=jax.ShapeDtypeStruct((G, R, P_WIDTH), f32),
        scratch_shapes=[pltpu.VMEM((tm, D_MODEL), bf16)],
        compiler_params=_cparams(("parallel", "parallel", "arbitrary")),
        name="norm_proj",
    )(x, norm_g.reshape(1, D_MODEL), scale, shift, w_bf)


def _rope_tables(pos):
    half = ROT_DIM // 2
    inv_freq = ROPE_THETA ** (-jnp.arange(half, dtype=f32) * (2.0 / ROT_DIM))
    ang = pos.astype(f32)[:, None] * inv_freq[None, :]
    cos, sin = jnp.cos(ang), jnp.sin(ang)
    t = pos.shape[0]
    z8 = jnp.zeros((t, half), f32)
    rest = HEAD_DIM - ROT_DIM
    cos64 = jnp.concatenate([cos, cos, jnp.ones((t, rest), f32)], axis=1)
    up64 = jnp.concatenate([-sin, z8, jnp.zeros((t, rest), f32)], axis=1)
    dn64 = jnp.concatenate([z8, sin, jnp.zeros((t, rest), f32)], axis=1)
    rep = LANES // HEAD_DIM
    return jnp.stack([jnp.tile(a, (1, rep)) for a in (cos64, up64, dn64)])


def _rope(x, tab):
    w = x.shape[1]
    reps = w // LANES
    cosf, up, dn = (jnp.concatenate([tab[i]] * reps, axis=1) for i in range(3))
    half = ROT_DIM // 2
    return x * cosf + pltpu.roll(x, w - half, 1) * up + pltpu.roll(x, half, 1) * dn


def _attn_prompt_kernel(sinks_ref, q_ref, kc_ref, kp_ref, vc_ref, vp_ref, ga_ref, tc_ref, tp_ref,
                        o_ref, krot_ref):
    n = pl.program_id(1)
    q = (_rope(q_ref[0], tc_ref[...]) * (HEAD_DIM ** -0.5)).astype(bf16)
    kc = _rope(kc_ref[0], tc_ref[...])
    kp = _rope(kp_ref[0], tp_ref[...])
    krot_ref[0] = kc
    kcat = jnp.concatenate([kp, kc], axis=0).astype(bf16)
    vcat = jnp.concatenate([vp_ref[0], vc_ref[0]], axis=0).astype(bf16)
    qi = lax.broadcasted_iota(jnp.int32, (WINDOW, 2 * WINDOW), 0)
    kj = lax.broadcasted_iota(jnp.int32, (WINDOW, 2 * WINDOW), 1)
    rel = WINDOW + qi - kj
    mask = (rel >= 0) & (rel <= WINDOW) & ((kj >= WINDOW) | (n > 0))
    outs = []
    for h in range(N_Q_HEADS):
        g = h // GQA_GROUP
        kg = kcat[:, g * HEAD_DIM:(g + 1) * HEAD_DIM]
        vg = vcat[:, g * HEAD_DIM:(g + 1) * HEAD_DIM]
        s = lax.dot_general(q[:, h * HEAD_DIM:(h + 1) * HEAD_DIM], kg, (((1,), (1,)), ((), ())),
                            preferred_element_type=f32)
        s = jnp.where(mask, s, NEG_BIG)
        sink = sinks_ref[h]
        m = jnp.maximum(jnp.max(s, axis=-1, keepdims=True), sink)
        p = jnp.exp(s - m)
        den = jnp.sum(p, axis=-1, keepdims=True) + jnp.exp(sink - m)
        outs.append(jnp.dot(p.astype(bf16), vg, preferred_element_type=f32) / den)
    ga = ga_ref[0]
    o_ref[0] = jnp.concatenate(outs, axis=1) * _silu(ga)


def _attn_prompt(proj, sinks, tab):
    B, T, _ = proj.shape
    nb = T // WINDOW
    kvb = KV_WIDTH
    prev = lambda b, n: jnp.maximum(n - 1, 0)
    return pl.pallas_call(
        _attn_prompt_kernel,
        grid=(B, nb),
        in_specs=[
            pl.BlockSpec(memory_space=pltpu.SMEM),
            pl.BlockSpec((1, WINDOW, ATT_WIDTH), lambda b, n: (b, n, P_Q // ATT_WIDTH)),
            pl.BlockSpec((1, WINDOW, kvb), lambda b, n: (b, n, P_KA // kvb)),
            pl.BlockSpec((1, WINDOW, kvb), lambda b, n: (b, prev(b, n), P_KA // kvb)),
            pl.BlockSpec((1, WINDOW, kvb), lambda b, n: (b, n, P_VA // kvb)),
            pl.BlockSpec((1, WINDOW, kvb), lambda b, n: (b, prev(b, n), P_VA // kvb)),
            pl.BlockSpec((1, WINDOW, ATT_WIDTH), lambda b, n: (b, n, P_GA // ATT_WIDTH)),
            pl.BlockSpec((3, WINDOW, LANES), lambda b, n: (0, n, 0)),
            pl.BlockSpec((3, WINDOW, LANES), lambda b, n: (0, prev(b, n), 0)),
        ],
        out_specs=[
            pl.BlockSpec((1, WINDOW, ATT_WIDTH), lambda b, n: (b, n, 0)),
            pl.BlockSpec((1, WINDOW, kvb), lambda b, n: (b, 0, 0)),
        ],
        out_shape=[
            jax.ShapeDtypeStruct((B, T, ATT_WIDTH), f32),
            jax.ShapeDtypeStruct((B, WINDOW, kvb), f32),
        ],
        compiler_params=_cparams(("parallel", "arbitrary")),
        name="attn_prompt",
    )(sinks, proj, proj, proj, proj, proj, proj, tab, tab)


def _attn_sample_kernel(sinks_ref, q_ref, kn_ref, vn_ref, ga_ref, ck_ref, cv_ref, tab_ref,
                        o_ref, nk_ref, nv_ref):
    q = _rope(q_ref[0], tab_ref[...]) * (HEAD_DIM ** -0.5)
    kn = _rope(kn_ref[0], tab_ref[...])
    vn = vn_ref[0]
    ck = ck_ref[0]
    cv = cv_ref[0]
    ckb = ck.astype(bf16)
    cvb = cv.astype(bf16)
    outs = []
    for g in range(N_KV_HEADS):
        lo, hi = g * HEAD_DIM, (g + 1) * HEAD_DIM
        qg = jnp.concatenate(
            [q[:, (g * GQA_GROUP + i) * HEAD_DIM:(g * GQA_GROUP + i + 1) * HEAD_DIM]
             for i in range(GQA_GROUP)], axis=0)
        s = lax.dot_general(qg.astype(bf16), ckb[:, lo:hi], (((1,), (1,)), ((), ())),
                            preferred_element_type=f32)
        s_new = jnp.sum(qg * kn[:, lo:hi], axis=-1, keepdims=True)
        sink = sinks_ref[g * GQA_GROUP:(g + 1) * GQA_GROUP, :]
        m = jnp.maximum(jnp.maximum(jnp.max(s, axis=-1, keepdims=True), s_new), sink)
        p = jnp.exp(s - m)
        p_new = jnp.exp(s_new - m)
        den = jnp.sum(p, axis=-1, keepdims=True) + p_new + jnp.exp(sink - m)
        o = jnp.dot(p.astype(bf16), cvb[:, lo:hi], preferred_element_type=f32) + p_new * vn[:, lo:hi]
        o = o / den
        outs.extend(o[i:i + 1, :] for i in range(GQA_GROUP))
    ga = ga_ref[0]
    o_ref[0] = jnp.concatenate(outs, axis=1) * _silu(ga)
    row = lax.broadcasted_iota(jnp.int32, (WINDOW, KV_WIDTH), 0)
    last = row == WINDOW - 1
    nk_ref[0] = jnp.where(last, kn, pltpu.roll(ck, WINDOW - 1, 0))
    nv_ref[0] = jnp.where(last, vn, pltpu.roll(cv, WINDOW - 1, 0))


def _attn_sample(q, kn, vn, ga, ck, cv, sinks, tab):
    Bd = q.shape[0]
    row = lambda w: pl.BlockSpec((1, 1, w), lambda b: (b, 0, 0))
    cache = pl.BlockSpec((1, WINDOW, KV_WIDTH), lambda b: (b, 0, 0))
    return pl.pallas_call(
        _attn_sample_kernel,
        grid=(Bd,),
        in_specs=[
            pl.BlockSpec((N_Q_HEADS, 1), lambda b: (0, 0)),
            row(ATT_WIDTH), row(KV_WIDTH), row(KV_WIDTH), row(ATT_WIDTH), cache, cache,
            pl.BlockSpec((3, 1, LANES), lambda b: (0, 0, 0)),
        ],
        out_specs=[row(ATT_WIDTH), cache, cache],
        out_shape=[
            jax.ShapeDtypeStruct((Bd, 1, ATT_WIDTH), f32),
            jax.ShapeDtypeStruct((Bd, WINDOW, KV_WIDTH), f32),
            jax.ShapeDtypeStruct((Bd, WINDOW, KV_WIDTH), f32),
        ],
        compiler_params=_cparams(("parallel",)),
        name="attn_sample",
    )(sinks.reshape(N_Q_HEADS, 1), q, kn, vn, ga, ck, cv, tab)


def _rwkv_prep_kernel(shift_rows, r_ref, kr_ref, vr_ref, wa_ref, pr_ref, pkr_ref, pvr_ref, pwa_ref,
                      mu_r_ref, mu_kr_ref, mu_vr_ref, mu_wa_ref, w0_ref, a0_ref, kk_ref, ka_ref,
                      wd_ref, wi_ref, ro_ref, wo_ref, ko_ref, vo_ref, kko_ref, ao_ref):
    m = pl.program_id(1)

    def mixed(cur_ref, prev_ref, mu_ref):
        cur = cur_ref[0]
        if shift_rows:
            above = jnp.where(m == 0, 0.0, prev_ref[0][7:8, :])
            rowi = lax.broadcasted_iota(jnp.int32, cur.shape, 0)
            prev = jnp.where(rowi == 0, above, pltpu.roll(cur, 1, 0))
        else:
            prev = prev_ref[0]
        return cur + (prev - cur) * mu_ref[...]

    r = mixed(r_ref, pr_ref, mu_r_ref)
    kr = mixed(kr_ref, pkr_ref, mu_kr_ref)
    vr = mixed(vr_ref, pvr_ref, mu_vr_ref)
    wa = mixed(wa_ref, pwa_ref, mu_wa_ref)
    hi = lax.Precision.HIGHEST
    z = w0_ref[...] + jnp.dot(jnp.tanh(wa), wd_ref[...], precision=hi, preferred_element_type=f32)
    nz = -z
    softplus = jnp.maximum(nz, 0.0) + jnp.log1p(jnp.exp(-jnp.abs(nz)))
    w_log = -softplus - 0.5
    a = jax.nn.sigmoid(a0_ref[...] + jnp.dot(wa, wi_ref[...], precision=hi, preferred_element_type=f32))
    ro_ref[0] = r
    wo_ref[0] = -jnp.exp(w_log)
    ko_ref[0] = kr * (1.0 + (a - 1.0) * ka_ref[...])
    vo_ref[0] = vr
    kko_ref[0] = kr * kk_ref[...]
    ao_ref[0] = a


def _rwkv_prep(proj, prev, mu, w0, a0, k_k, k_a, wd_pad, wi_pad, tm):
    G, R, _ = proj.shape
    W = RWKV_WIDTH
    shift_rows = prev is None
    cur_specs = [
        pl.BlockSpec((1, tm, W), lambda g, m: (g, m, P_R // W)),
        pl.BlockSpec((1, tm, W), lambda g, m: (g, m, P_KR // W)),
        pl.BlockSpec((1, tm, W), lambda g, m: (g, m, P_VR // W)),
        pl.BlockSpec((1, tm, LANES), lambda g, m: (g, m, P_WA // LANES)),
    ]
    if shift_rows:
        sub = 8
        above = lambda m: jnp.maximum(m * (tm // sub) - 1, 0)
        prev_specs = [
            pl.BlockSpec((1, sub, W), lambda g, m: (g, above(m), P_R // W)),
            pl.BlockSpec((1, sub, W), lambda g, m: (g, above(m), P_KR // W)),
            pl.BlockSpec((1, sub, W), lambda g, m: (g, above(m), P_VR // W)),
            pl.BlockSpec((1, sub, LANES), lambda g, m: (g, above(m), P_WA // LANES)),
        ]
        prev_args = [proj] * 4
    else:
        prev_specs = [
            pl.BlockSpec((1, tm, W), lambda g, m: (g, m, 0)),
            pl.BlockSpec((1, tm, W), lambda g, m: (g, m, 0)),
            pl.BlockSpec((1, tm, W), lambda g, m: (g, m, 0)),
            pl.BlockSpec((1, tm, LANES), lambda g, m: (g, m, 0)),
        ]
        prev_args = list(prev)
    vec = lambda w: pl.BlockSpec((1, w), lambda g, m: (0, 0))
    lora = pl.BlockSpec((LORA_PAIR, W), lambda g, m: (0, 0))
    out_spec = pl.BlockSpec((1, tm, W), lambda g, m: (g, m, 0))
    return pl.pallas_call(
        functools.partial(_rwkv_prep_kernel, shift_rows),
        grid=(G, R // tm),
        in_specs=cur_specs + prev_specs + [vec(W), vec(W), vec(W), vec(LANES),
                                           vec(W), vec(W), vec(W), vec(W), lora, lora],
        out_specs=[out_spec] * 6,
        out_shape=[jax.ShapeDtypeStruct((G, R, W), f32)] * 6,
        compiler_params=_cparams(("parallel", "arbitrary")),
        name="rwkv_prep",
    )(proj, proj, proj, proj, *prev_args, *mu, w0, a0, k_k, k_a, wd_pad, wi_pad)


def _wkv_kernel(tc, r_ref, w_ref, k_ref, v_ref, kk_ref, a_ref, rk_ref, lnw_ref, lnb_ref, s0_ref,
                y_ref, s_ref):
    @pl.when(pl.program_id(1) == 0)
    def _():
        s_ref[...] = s0_ref[...]

    eye = (lax.broadcasted_iota(jnp.int32, (RW_HEAD, RW_HEAD), 0)
           == lax.broadcasted_iota(jnp.int32, (RW_HEAD, RW_HEAD), 1)).astype(f32)
    rk = rk_ref[...]
    lnw = lnw_ref[...]
    lnb = lnb_ref[...]

    def step(t, carry):
        r, lw, k, v, kkr, a = (ref[0, t] for ref in (r_ref, w_ref, k_ref, v_ref, kk_ref, a_ref))
        w = jnp.exp(lw)
        norm = jnp.sqrt(jnp.sum(kkr * kkr, axis=-1, keepdims=True))
        kk = kkr / jnp.maximum(norm, 1e-12)
        b = kk * a
        bonus = jnp.sum(r * k * rk, axis=-1, keepdims=True) * v
        rows = []
        for h in range(RW_HEADS):
            hs = slice(h, h + 1)
            S = s_ref[0, h]
            sa = jnp.sum(S * (-kk[hs]), axis=-1, keepdims=True)
            v_col = jnp.sum(eye * v[hs], axis=-1, keepdims=True)
            S = S * w[hs] + sa * b[hs] + v_col * k[hs]
            s_ref[0, h] = S
            y = jnp.sum(S * r[hs], axis=-1, keepdims=True)
            mu = jnp.mean(y, axis=0, keepdims=True)
            var = jnp.mean(jnp.square(y - mu), axis=0, keepdims=True)
            yn = (y - mu) * lax.rsqrt(var + GN_EPS)
            rows.append(jnp.sum(yn * eye, axis=0, keepdims=True))
        y_ref[0, t] = jnp.concatenate(rows, axis=0) * lnw + lnb + bonus
        return carry

    lax.fori_loop(0, tc, step, 0)


def _wkv(r, w, k, v, kk, a, r_k, ln_w, ln_b, s0, tc):
    B, T = r.shape[0], r.shape[1]
    hd = (RW_HEADS, RW_HEAD)
    heads = lambda t: t.reshape(B, T, *hd)
    seq = pl.BlockSpec((1, tc, *hd), lambda b, c: (b, c, 0, 0))
    par = pl.BlockSpec(hd, lambda b, c: (0, 0))
    state = pl.BlockSpec((1, RW_HEADS, RW_HEAD, RW_HEAD), lambda b, c: (b, 0, 0, 0))
    y, s_t = pl.pallas_call(
        functools.partial(_wkv_kernel, tc),
        grid=(B, T // tc),
        in_specs=[seq] * 6 + [par] * 3 + [state],
        out_specs=[seq, state],
        out_shape=[
            jax.ShapeDtypeStruct((B, T, *hd), f32),
            jax.ShapeDtypeStruct((B, RW_HEADS, RW_HEAD, RW_HEAD), f32),
        ],
        compiler_params=_cparams(("parallel", "arbitrary")),
        name="wkv_steps",
    )(heads(r), heads(w), heads(k), heads(v), heads(kk), heads(a),
      r_k, ln_w.reshape(hd), ln_b.reshape(hd), s0)
    return y.reshape(B, T, RWKV_WIDTH), s_t


CHUNK = 64
PAIR = 2 * RW_HEAD
N_PAIRS = RW_HEADS // 2
_NN = (((1,), (0,)), ((), ()))
_NT = (((1,), (1,)), ((), ()))
_TN = (((0,), (0,)), ((), ()))
_HI = lax.Precision.HIGHEST


def _mm(a, b, dims=_NN):
    return lax.dot_general(a, b, dims, precision=_HI, preferred_element_type=f32)


def _mmb(a, b, dims=_NN):
    return lax.dot_general(a.astype(bf16), b.astype(bf16), dims, preferred_element_type=f32)


def _wkv_chunk_kernel(r_ref, lw_ref, k_ref, v_ref, kk_ref, a_ref, rk_ref, lnw_ref, lnb_ref, s0_ref,
                      y_ref, s_ref, sp_ref):
    C = CHUNK
    c = pl.program_id(1)

    @pl.when(c == 0)
    def _():
        for p in range(N_PAIRS):
            sp_ref[p] = jnp.concatenate([s0_ref[0, 2 * p], s0_ref[0, 2 * p + 1]], axis=1)

    row = lax.broadcasted_iota(jnp.int32, (PAIR, PAIR), 0)
    col = lax.broadcasted_iota(jnp.int32, (PAIR, PAIR), 1)
    tril = row >= col
    stril = row > col
    eye = (row == col).astype(f32)
    ones_bd = ((row < RW_HEAD) == (col < RW_HEAD)).astype(f32)
    lane_lo = lax.broadcasted_iota(jnp.int32, (C, PAIR), 1) < RW_HEAD

    def bd(x):
        return jnp.concatenate([jnp.where(lane_lo, x, 0.0), jnp.where(lane_lo, 0.0, x)], axis=0)

    lw = lw_ref[0]
    g = _mm(tril[:C, :C].astype(f32), lw)
    e_g = jnp.exp(g)
    e_ng = jnp.exp(-g)
    e_gm = jnp.exp(g - lw)
    e_hat = jnp.exp(g[C - 1:C, :] - g)
    e_end = e_g[C - 1:C, :]

    pairs = range(N_PAIRS)
    sls = [slice(p * PAIR, (p + 1) * PAIR) for p in pairs]
    ins = [[ref[0, :, sl] for ref in (r_ref, k_ref, v_ref, kk_ref, a_ref)] for sl in sls]
    norms = [jnp.sqrt(_mm(x[3] * x[3], ones_bd)) for x in ins]
    at, rt, bt, kt, bh, kh, vb = ([] for _ in range(7))
    for (r, k, v, kkr, icl), norm, sl in zip(ins, norms, sls):
        kk = kkr / jnp.maximum(norm, 1e-12)
        b = kk * icl
        at.append(bd(-kk * e_gm[:, sl]).astype(bf16))
        rt.append(bd(r * e_g[:, sl]).astype(bf16))
        bt.append(bd(b * e_ng[:, sl]).astype(bf16))
        kt.append(bd(k * e_ng[:, sl]).astype(bf16))
        bh.append(bd(b * e_hat[:, sl]).astype(bf16))
        kh.append(bd(k * e_hat[:, sl]).astype(bf16))
        vb.append(bd(v).astype(bf16))

    tril2 = jnp.concatenate([tril, tril], axis=1)
    gram = [_mmb(jnp.concatenate([at[p], rt[p]], axis=0), jnp.concatenate([bt[p], kt[p]], axis=0), _NT)
            for p in pairs]
    lmat = [jnp.where(stril, gm[:PAIR, :PAIR], 0.0) for gm in gram]
    mv = [_mmb(jnp.where(stril, gram[p][:PAIR, PAIR:], 0.0), vb[p]) for p in pairs]
    lower = [jnp.where(tril2, gm[PAIR:, :], 0.0).astype(bf16) for gm in gram]

    tinv = [eye + lm for lm in lmat]
    pw = lmat
    for _ in range(5):
        pw = [_mmb(x, x) for x in pw]
        tinv = [t + _mmb(t, x) for t, x in zip(tinv, pw)]

    wx = [_mmb(tinv[p], jnp.concatenate([at[p], mv[p].astype(bf16)], axis=1)) for p in pairs]
    s_old = [sp_ref[p] for p in pairs]
    uy0 = [_mmb(jnp.concatenate([wx[p][:, :PAIR].astype(bf16), rt[p]], axis=0), bd(s_old[p]), _NT)
           for p in pairs]
    uv = [jnp.concatenate([(uy0[p][:PAIR] + wx[p][:, PAIR:]).astype(bf16), vb[p]], axis=0) for p in pairs]
    ys = [uy0[p][PAIR:] + _mmb(lower[p], uv[p]) for p in pairs]
    s_add = [_mmb(uv[p], jnp.concatenate([bh[p], kh[p]], axis=0), _TN) for p in pairs]
    for p in pairs:
        sp_ref[p] = s_old[p] * e_end[:, sls[p]] + s_add[p][:RW_HEAD] + s_add[p][RW_HEAD:]

    ys = [y[:C] + y[C:] for y in ys]
    mus = [_mm(y, ones_bd) * (1.0 / RW_HEAD) for y in ys]
    ds = [y - mu for y, mu in zip(ys, mus)]
    var = [_mm(d * d, ones_bd) * (1.0 / RW_HEAD) for d in ds]
    bonus = [_mm(x[0] * x[1] * rk_ref[:, sl], ones_bd) * x[2] for x, sl in zip(ins, sls)]
    for p in pairs:
        sl = sls[p]
        y_ref[0, :, sl] = ds[p] * lax.rsqrt(var[p] + GN_EPS) * lnw_ref[:, sl] + lnb_ref[:, sl] + bonus[p]

    @pl.when(c == pl.num_programs(1) - 1)
    def _():
        for p in range(N_PAIRS):
            s = sp_ref[p]
            s_ref[0, 2 * p] = s[:, :RW_HEAD]
            s_ref[0, 2 * p + 1] = s[:, RW_HEAD:]


def _wkv_chunked(r, lw, k, v, kk, a, r_k, ln_w, ln_b, s0):
    B, T, W = r.shape
    seq = pl.BlockSpec((1, CHUNK, W), lambda b, c: (b, c, 0))
    par = pl.BlockSpec((1, W), lambda b, c: (0, 0))
    state = pl.BlockSpec((1, RW_HEADS, RW_HEAD, RW_HEAD), lambda b, c: (b, 0, 0, 0))
    return pl.pallas_call(
        _wkv_chunk_kernel,
        grid=(B, T // CHUNK),
        in_specs=[seq] * 6 + [par] * 3 + [state],
        out_specs=[seq, state],
        out_shape=[
            jax.ShapeDtypeStruct((B, T, W), f32),
            jax.ShapeDtypeStruct((B, RW_HEADS, RW_HEAD, RW_HEAD), f32),
        ],
        scratch_shapes=[pltpu.VMEM((N_PAIRS, RW_HEAD, PAIR), f32)],
        compiler_params=_cparams(("parallel", "arbitrary")),
        name="wkv_chunks",
    )(r, lw, k, v, kk, a, r_k.reshape(1, W), ln_w.reshape(1, W), ln_b.reshape(1, W), s0)


def _post_kernel(final, att_ref, y_ref, gr_ref, x_ref, gate_ref, w_ref, fg_ref, o_ref):
    cat = jnp.concatenate([att_ref[0], y_ref[0] * _silu(gr_ref[0])], axis=1).astype(bf16)
    out = jnp.dot(cat, w_ref[...], preferred_element_type=f32)
    x = x_ref[0] + gate_ref[0] * out
    if final:
        ms = jnp.mean(x * x, axis=-1, keepdims=True)
        x = x * lax.rsqrt(ms + NORM_EPS) * fg_ref[...]
    o_ref[0] = x


def _post(att, y_rw, proj, x, gate, w_out_bf, final_g, final, tm):
    G, R, _ = x.shape
    W = RWKV_WIDTH
    return pl.pallas_call(
        functools.partial(_post_kernel, final),
        grid=(G, R // tm),
        in_specs=[
            pl.BlockSpec((1, tm, ATT_WIDTH), lambda g, m: (g, m, 0)),
            pl.BlockSpec((1, tm, W), lambda g, m: (g, m, 0)),
            pl.BlockSpec((1, tm, W), lambda g, m: (g, m, P_GR // W)),
            pl.BlockSpec((1, tm, D_MODEL), lambda g, m: (g, m, 0)),
            _mod_spec(gate, tm),
            pl.BlockSpec((D_MODEL, D_MODEL), lambda g, m: (0, 0)),
            pl.BlockSpec((1, D_MODEL), lambda g, m: (0, 0)),
        ],
        out_specs=pl.BlockSpec((1, tm, D_MODEL), lambda g, m: (g, m, 0)),
        out_shape=jax.ShapeDtypeStruct((G, R, D_MODEL), f32),
        compiler_params=_cparams(("parallel", "parallel")),
        name="post_proj",
    )(att, y_rw, proj, x, gate, w_out_bf, final_g.reshape(1, D_MODEL))


def _arrange_w_in(w):
    pad = jnp.zeros((w.shape[0], P_WIDTH - IN_WIDTH), w.dtype)
    parts = [w[:, Q_OFF:KA_OFF], w[:, GA_OFF:GR_OFF], w[:, GR_OFF:IN_WIDTH], w[:, R_OFF:WD_OFF],
             w[:, KA_OFF:R_OFF], w[:, WD_OFF:GA_OFF], pad]
    return jnp.concatenate(parts, axis=1).astype(bf16)


def _shift_cols(t):
    return jnp.concatenate([t[..., P_R:P_KA], t[..., P_WA:P_WA + LORA_PAIR]], axis=-1)


def kernel(x_prompt, x_sample, cache_k, cache_v, state_wkv, state_shift, c_prompt, c_sample,
           norm_g, w_ada, b_ada, w_in, mu_shift, w0, w_decay, a0, w_iclr, k_k, k_a, r_k,
           ln_w, ln_b, sinks, w_out, final_g):
    Bp, Tp = x_prompt.shape[0], x_prompt.shape[1]
    Bd = x_sample.shape[0]
    W = RWKV_WIDTH

    n_c = Bp + Bd
    c_rows = -(-n_c // 8) * 8
    c_all = jnp.concatenate([c_prompt, c_sample, jnp.zeros((c_rows - n_c, D_MODEL), f32)], axis=0)
    mod = _ada(c_all, w_ada, b_ada)

    tab_p = _rope_tables(jnp.arange(Tp, dtype=jnp.int32))
    tab_s = _rope_tables(PAST_LEN + jnp.arange(1, dtype=jnp.int32))

    hp = x_prompt
    hs = x_sample.reshape(1, Bd, D_MODEL)
    s0_p = jnp.zeros((Bp, RW_HEADS, RW_HEAD, RW_HEAD), f32)
    outs = {k: [] for k in ("kp", "vp", "sp", "shp", "ks", "vs", "ss", "shs")}
    for l in range(DEPTH):
        final = l == DEPTH - 1
        w_bf = _arrange_w_in(w_in[l])
        w_out_bf = w_out[l].astype(bf16)
        mu_l = mu_shift[l]
        mu = [mu_l[0:W].reshape(1, W), mu_l[W:2 * W].reshape(1, W), mu_l[2 * W:3 * W].reshape(1, W),
              mu_l[3 * W:].reshape(1, LORA_PAIR)]
        vecs = [t[l].reshape(1, W) for t in (w0, a0, k_k, k_a)]
        wd_pad = jnp.concatenate([w_decay[l], jnp.zeros((ICLR_LORA, W), f32)], axis=0)
        wi_pad = jnp.concatenate([jnp.zeros((DECAY_LORA, W), f32), w_iclr[l]], axis=0)
        shift_p, scale_p, gate_p = (mod[l, :Bp, i * D_MODEL:(i + 1) * D_MODEL].reshape(Bp, 1, D_MODEL)
                                    for i in range(3))
        shift_s, scale_s, gate_s = (mod[l, Bp:n_c, i * D_MODEL:(i + 1) * D_MODEL].reshape(1, Bd, D_MODEL)
                                    for i in range(3))

        proj = _norm_proj(hp, norm_g[l], scale_p, shift_p, w_bf, tm=1024)
        att, k_rot = _attn_prompt(proj, sinks[l], tab_p)
        prep = _rwkv_prep(proj, None, mu, *vecs, wd_pad, wi_pad, tm=256)
        y_rw, s_t = _wkv_chunked(*prep, r_k[l], ln_w[l], ln_b[l], s0_p)
        hp = _post(att, y_rw, proj, hp, gate_p, w_out_bf, final_g, final, tm=256)
        outs["kp"].append(k_rot.reshape(Bp, WINDOW, N_KV_HEADS, HEAD_DIM))
        outs["vp"].append(proj[:, Tp - WINDOW:, P_VA:P_VA + KV_WIDTH].reshape(Bp, WINDOW, N_KV_HEADS, HEAD_DIM))
        outs["sp"].append(s_t)
        outs["shp"].append(_shift_cols(proj[:, Tp - 1]))

        proj = _norm_proj(hs, norm_g[l], scale_s, shift_s, w_bf, tm=Bd)
        rows = proj[0]
        att, nk, nv = _attn_sample(
            rows[:, None, P_Q:P_Q + ATT_WIDTH], rows[:, None, P_KA:P_KA + KV_WIDTH],
            rows[:, None, P_VA:P_VA + KV_WIDTH], rows[:, None, P_GA:P_GA + ATT_WIDTH],
            cache_k[l].reshape(Bd, WINDOW, KV_WIDTH), cache_v[l].reshape(Bd, WINDOW, KV_WIDTH),
            sinks[l], tab_s)
        sh = state_shift[l]
        prev = [sh[None, :, 0:W], sh[None, :, W:2 * W], sh[None, :, 2 * W:3 * W], sh[None, :, 3 * W:]]
        prep = _rwkv_prep(proj, prev, mu, *vecs, wd_pad, wi_pad, tm=Bd)
        prep = [t.reshape(Bd, 1, W) for t in prep]
        y_rw, s_t = _wkv(*prep, r_k[l], ln_w[l], ln_b[l], state_wkv[l], tc=1)
        hs = _post(att.reshape(1, Bd, ATT_WIDTH), y_rw.reshape(1, Bd, W), proj, hs, gate_s, w_out_bf,
                   final_g, final, tm=Bd)
        outs["ks"].append(nk.reshape(Bd, WINDOW, N_KV_HEADS, HEAD_DIM))
        outs["vs"].append(nv.reshape(Bd, WINDOW, N_KV_HEADS, HEAD_DIM))
        outs["ss"].append(s_t)
        outs["shs"].append(_shift_cols(rows))

    st = lambda k: jnp.stack(outs[k])
    return (hp, hs.reshape(Bd, 1, D_MODEL), st("kp"), st("vp"), st("sp"), st("shp"),
            st("ks"), st("vs"), st("ss"), st("shs"))
```

```python
import functools

import jax
import jax.numpy as jnp
from jax import lax
from jax.experimental import pallas as pl
from jax.experimental.pallas import tpu as pltpu

f32 = jnp.float32
bf16 = jnp.bfloat16

D_MODEL = 2048
DEPTH = 2
PAST_LEN = 16384
ATT_WIDTH = 1024
RWKV_WIDTH = 1024
HEAD_DIM = 64
N_Q_HEADS = 16
N_KV_HEADS = 4
GQA_GROUP = 4
KV_WIDTH = 256
WINDOW = 128
ROT_DIM = 16
ROPE_THETA = 500000.0
RW_HEAD = 64
RW_HEADS = 16
DECAY_LORA = 64
ICLR_LORA = 64
LORA_PAIR = DECAY_LORA + ICLR_LORA
NORM_EPS = 1e-5
GN_EPS = 64e-5
NEG_BIG = -1e30

Q_OFF = 0
KA_OFF = Q_OFF + ATT_WIDTH
VA_OFF = KA_OFF + KV_WIDTH
R_OFF = VA_OFF + KV_WIDTH
KR_OFF = R_OFF + RWKV_WIDTH
VR_OFF = KR_OFF + RWKV_WIDTH
WD_OFF = VR_OFF + RWKV_WIDTH
AD_OFF = WD_OFF + DECAY_LORA
GA_OFF = AD_OFF + ICLR_LORA
GR_OFF = GA_OFF + ATT_WIDTH
IN_WIDTH = GR_OFF + RWKV_WIDTH
SHIFT_DIM = GA_OFF - R_OFF

LANES = 128
P_Q = 0
P_GA = 1024
P_GR = 2048
P_R = 3072
P_KR = 4096
P_VR = 5120
P_KA = 6144
P_VA = 6400
P_WA = 6656
P_WIDTH = 7168
PROJ_TN = 1024

VMEM_LIMIT = 56 * 1024 * 1024


def _silu(x):
    return x * jax.nn.sigmoid(x)


def _cparams(sem):
    return pltpu.CompilerParams(dimension_semantics=sem, vmem_limit_bytes=VMEM_LIMIT)


def _ada_kernel(c_ref, w_ref, b_ref, o_ref):
    c = c_ref[...]
    o_ref[0] = jnp.dot(_silu(c), w_ref[0], precision=lax.Precision.HIGHEST,
                       preferred_element_type=f32) + b_ref[0]


def _ada(c_all, w_ada, b_ada):
    rows = c_all.shape[0]
    tn = 768
    n_out = w_ada.shape[2]
    return pl.pallas_call(
        _ada_kernel,
        grid=(DEPTH, n_out // tn),
        in_specs=[
            pl.BlockSpec((rows, D_MODEL), lambda l, n: (0, 0)),
            pl.BlockSpec((1, D_MODEL, tn), lambda l, n: (l, 0, n)),
            pl.BlockSpec((1, 1, tn), lambda l, n: (l, 0, n)),
        ],
        out_specs=pl.BlockSpec((1, rows, tn), lambda l, n: (l, 0, n)),
        out_shape=jax.ShapeDtypeStruct((DEPTH, rows, n_out), f32),
        compiler_params=_cparams(("parallel", "parallel")),
        name="ada_mod",
    )(c_all, w_ada, b_ada.reshape(DEPTH, 1, n_out))


def _rope(x, tab):
    w = x.shape[1]
    reps = w // LANES
    cosf, up, dn = (jnp.concatenate([tab[i]] * reps, axis=1) for i in range(3))
    half = ROT_DIM // 2
    return x * cosf + pltpu.roll(x, w - half, 1) * up + pltpu.roll(x, half, 1) * dn


def _norm_proj_kernel(x_ref, g_ref, scale_ref, shift_ref, w_ref, tab_ref, o_ref, h_ref):
    n = pl.program_id(2)

    @pl.when(n == 0)
    def _():
        x = x_ref[0]
        ms = jnp.mean(x * x, axis=-1, keepdims=True)
        y = x * lax.rsqrt(ms + NORM_EPS) * g_ref[...]
        h_ref[...] = (y * (1.0 + scale_ref[0]) + shift_ref[0]).astype(bf16)

    res = jnp.dot(h_ref[...], w_ref[...], preferred_element_type=f32)
    q_tile = P_Q // PROJ_TN
    k_tile = P_KA // PROJ_TN

    @pl.when(n == q_tile)
    def _():
        o_ref[0] = _rope(res, tab_ref[...]) * (HEAD_DIM ** -0.5)

    @pl.when(n == k_tile)
    def _():
        o_ref[0] = jnp.concatenate([_rope(res[:, :KV_WIDTH], tab_ref[...]), res[:, KV_WIDTH:]], axis=1)

    @pl.when((n != q_tile) & (n != k_tile))
    def _():
        o_ref[0] = res


def _mod_spec(mod, tm):
    if mod.shape[1] == 1:
        return pl.BlockSpec((1, 1, D_MODEL), lambda g, m, *_: (g, 0, 0))
    return pl.BlockSpec((1, tm, D_MODEL), lambda g, m, *_: (g, m, 0))


def _norm_proj(x, norm_g, scale, shift, w_bf, tab, tm):
    G, R, _ = x.shape
    return pl.pallas_call(
        _norm_proj_kernel,
        grid=(G, R // tm, P_WIDTH // PROJ_TN),
        in_specs=[
            pl.BlockSpec((1, tm, D_MODEL), lambda g, m, n: (g, m, 0)),
            pl.BlockSpec((1, D_MODEL), lambda g, m, n: (0, 0)),
            _mod_spec(scale, tm),
            _mod_spec(shift, tm),
            pl.BlockSpec((D_MODEL, PROJ_TN), lambda g, m, n: (0, n)),
            pl.BlockSpec((3, tm, LANES), lambda g, m, n: (0, m, 0)),
        ],
        out_specs=pl.BlockSpec((1, tm, PROJ_TN), lambda g, m, n: (g, m, n)),
        out_shape=jax.ShapeDtypeStruct((G, R, P_WIDTH), f32),
        scratch_shapes=[pltpu.VMEM((tm, D_MODEL), bf16)],
        compiler_params=_cparams(("parallel", "parallel", "arbitrary")),
        name="norm_proj",
    )(x, norm_g.reshape(1, D_MODEL), scale, shift, w_bf, tab)


def _rope_tables(pos):
    half = ROT_DIM // 2
    inv_freq = ROPE_THETA ** (-jnp.arange(half, dtype=f32) * (2.0 / ROT_DIM))
    ang = pos.astype(f32)[:, None] * inv_freq[None, :]
    cos, sin = jnp.cos(ang), jnp.sin(ang)
    t = pos.shape[0]
    z8 = jnp.zeros((t, half), f32)
    rest = HEAD_DIM - ROT_DIM
    cos64 = jnp.concatenate([cos, cos, jnp.ones((t, rest), f32)], axis=1)
    up64 = jnp.concatenate([-sin, z8, jnp.zeros((t, rest), f32)], axis=1)
    dn64 = jnp.concatenate([z8, sin, jnp.zeros((t, rest), f32)], axis=1)
    rep = LANES // HEAD_DIM
    return jnp.stack([jnp.tile(a, (1, rep)) for a in (cos64, up64, dn64)])


def _attn_prompt_kernel(sinks_ref, q_ref, kc_ref, kp_ref, vc_ref, vp_ref, ga_ref, o_ref):
    n = pl.program_id(1)
    wn = WINDOW
    half = HEAD_DIM
    q = q_ref[0].astype(bf16)
    k_t = jnp.concatenate([kp_ref[0], kc_ref[0]], axis=0).T.astype(bf16)
    vcat = jnp.concatenate([vp_ref[0], vc_ref[0]], axis=0)
    ga = ga_ref[0]

    qi = lax.broadcasted_iota(jnp.int32, (2 * wn, 2 * wn), 0) & (wn - 1)
    kj = lax.broadcasted_iota(jnp.int32, (2 * wn, 2 * wn), 1)
    rel = wn + qi - kj
    mask = (rel >= 0) & (rel <= wn) & ((kj >= wn) | (n > 0))
    top = lax.broadcasted_iota(jnp.int32, (2 * wn, 1), 0) < wn
    lo = lax.broadcasted_iota(jnp.int32, (2 * wn, LANES), 1) < half
    zeros_k = jnp.zeros((half, 2 * wn), bf16)
    ones_lo = jnp.where(lo, 1.0, 0.0).astype(bf16)
    ones_hi = jnp.where(lo, 0.0, 1.0).astype(bf16)

    for j in range(N_KV_HEADS // 2):
        vblk = vcat[:, j * LANES:(j + 1) * LANES]
        vswap = pltpu.roll(vblk, half, 1)
        for g in (2 * j, 2 * j + 1):
            own, other = (vblk, vswap) if g % 2 == 0 else (vswap, vblk)
            v_lo = jnp.where(lo, own, 0.0).astype(bf16)
            v_hi = jnp.where(lo, 0.0, other).astype(bf16)
            rhs_pv = jnp.concatenate([jnp.concatenate([v_lo, ones_lo], axis=1),
                                      jnp.concatenate([v_hi, ones_hi], axis=1)], axis=0)
            kg = k_t[g * half:(g + 1) * half, :]
            rhs_qk = jnp.concatenate([jnp.concatenate([kg, zeros_k], axis=0),
                                      jnp.concatenate([zeros_k, kg], axis=0)], axis=1)
            b0, b1 = 2 * g, 2 * g + 1
            qg = jnp.concatenate([q[:, b0 * LANES:(b0 + 1) * LANES], q[:, b1 * LANES:(b1 + 1) * LANES]],
                                 axis=0)
            s_all = jnp.dot(qg, rhs_qk, preferred_element_type=f32)
            ps, es = [], []
            for hh in range(2):
                s = jnp.where(mask, s_all[:, hh * 2 * wn:(hh + 1) * 2 * wn], NEG_BIG)
                sink = jnp.where(top, sinks_ref[2 * b0 + hh], sinks_ref[2 * b1 + hh])
                m = jnp.maximum(jnp.max(s, axis=-1, keepdims=True), sink)
                ps.append(jnp.exp(s - m).astype(bf16))
                es.append(jnp.exp(sink - m))
            res = jnp.dot(jnp.concatenate(ps, axis=1), rhs_pv, preferred_element_type=f32)
            out = res[:, :LANES] / (res[:, LANES:] + jnp.where(lo, es[0], es[1]))
            for i, blk in enumerate((b0, b1)):
                sl = slice(blk * LANES, (blk + 1) * LANES)
                o_ref[0, :, sl] = (out[i * wn:(i + 1) * wn] * _silu(ga[:, sl])).astype(o_ref.dtype)


def _attn_prompt(proj, sinks):
    B, T, _ = proj.shape
    nb = T // WINDOW
    kvb = KV_WIDTH
    prev = lambda b, n: jnp.maximum(n - 1, 0)
    return pl.pallas_call(
        _attn_prompt_kernel,
        grid=(B, nb),
        in_specs=[
            pl.BlockSpec(memory_space=pltpu.SMEM),
            pl.BlockSpec((1, WINDOW, ATT_WIDTH), lambda b, n: (b, n, P_Q // ATT_WIDTH)),
            pl.BlockSpec((1, WINDOW, kvb), lambda b, n: (b, n, P_KA // kvb)),
            pl.BlockSpec((1, WINDOW, kvb), lambda b, n: (b, prev(b, n), P_KA // kvb)),
            pl.BlockSpec((1, WINDOW, kvb), lambda b, n: (b, n, P_VA // kvb)),
            pl.BlockSpec((1, WINDOW, kvb), lambda b, n: (b, prev(b, n), P_VA // kvb)),
            pl.BlockSpec((1, WINDOW, ATT_WIDTH), lambda b, n: (b, n, P_GA // ATT_WIDTH)),
        ],
        out_specs=pl.BlockSpec((1, WINDOW, ATT_WIDTH), lambda b, n: (b, n, 0)),
        out_shape=jax.ShapeDtypeStruct((B, T, ATT_WIDTH), bf16),
        compiler_params=_cparams(("parallel", "arbitrary")),
        name="attn_prompt",
    )(sinks, proj, proj, proj, proj, proj, proj)


def _attn_sample_kernel(sinks_ref, q_ref, kn_ref, vn_ref, ga_ref, ck_ref, cv_ref,
                        o_ref, nk_ref, nv_ref):
    q = q_ref[0]
    kn = kn_ref[0]
    vn = vn_ref[0]
    ck = ck_ref[0]
    cv = cv_ref[0]
    ckb = ck.astype(bf16)
    cvb = cv.astype(bf16)
    outs = []
    for g in range(N_KV_HEADS):
        lo, hi = g * HEAD_DIM, (g + 1) * HEAD_DIM
        qg = jnp.concatenate(
            [q[:, (g * GQA_GROUP + i) * HEAD_DIM:(g * GQA_GROUP + i + 1) * HEAD_DIM]
             for i in range(GQA_GROUP)], axis=0)
        s = lax.dot_general(qg.astype(bf16), ckb[:, lo:hi], (((1,), (1,)), ((), ())),
                            preferred_element_type=f32)
        s_new = jnp.sum(qg * kn[:, lo:hi], axis=-1, keepdims=True)
        sink = sinks_ref[g * GQA_GROUP:(g + 1) * GQA_GROUP, :]
        m = jnp.maximum(jnp.maximum(jnp.max(s, axis=-1, keepdims=True), s_new), sink)
        p = jnp.exp(s - m)
        p_new = jnp.exp(s_new - m)
        den = jnp.sum(p, axis=-1, keepdims=True) + p_new + jnp.exp(sink - m)
        o = jnp.dot(p.astype(bf16), cvb[:, lo:hi], preferred_element_type=f32) + p_new * vn[:, lo:hi]
        o = o / den
        outs.extend(o[i:i + 1, :] for i in range(GQA_GROUP))
    ga = ga_ref[0]
    o_ref[0] = (jnp.concatenate(outs, axis=1) * _silu(ga)).astype(o_ref.dtype)
    row = lax.broadcasted_iota(jnp.int32, (WINDOW, KV_WIDTH), 0)
    last = row == WINDOW - 1
    nk_ref[0] = jnp.where(last, kn, pltpu.roll(ck, WINDOW - 1, 0))
    nv_ref[0] = jnp.where(last, vn, pltpu.roll(cv, WINDOW - 1, 0))


def _attn_sample(q, kn, vn, ga, ck, cv, sinks):
    Bd = q.shape[0]
    row = lambda w: pl.BlockSpec((1, 1, w), lambda b: (b, 0, 0))
    cache = pl.BlockSpec((1, WINDOW, KV_WIDTH), lambda b: (b, 0, 0))
    return pl.pallas_call(
        _attn_sample_kernel,
        grid=(Bd,),
        in_specs=[
            pl.BlockSpec((N_Q_HEADS, 1), lambda b: (0, 0)),
            row(ATT_WIDTH), row(KV_WIDTH), row(KV_WIDTH), row(ATT_WIDTH), cache, cache,
        ],
        out_specs=[row(ATT_WIDTH), cache, cache],
        out_shape=[
            jax.ShapeDtypeStruct((Bd, 1, ATT_WIDTH), bf16),
            jax.ShapeDtypeStruct((Bd, WINDOW, KV_WIDTH), f32),
            jax.ShapeDtypeStruct((Bd, WINDOW, KV_WIDTH), f32),
        ],
        compiler_params=_cparams(("parallel",)),
        name="attn_sample",
    )(sinks.reshape(N_Q_HEADS, 1), q, kn, vn, ga, ck, cv)


def _rwkv_prep_kernel(shift_rows, r_ref, kr_ref, vr_ref, wa_ref, pr_ref, pkr_ref, pvr_ref, pwa_ref,
                      mu_r_ref, mu_kr_ref, mu_vr_ref, mu_wa_ref, w0_ref, a0_ref, kk_ref, ka_ref,
                      wd_ref, wi_ref, ro_ref, wo_ref, ko_ref, vo_ref, kko_ref, ao_ref):
    m = pl.program_id(1)

    def mixed(cur_ref, prev_ref, mu_ref):
        cur = cur_ref[0]
        if shift_rows:
            above = jnp.where(m == 0, 0.0, prev_ref[0][7:8, :])
            rowi = lax.broadcasted_iota(jnp.int32, cur.shape, 0)
            prev = jnp.where(rowi == 0, above, pltpu.roll(cur, 1, 0))
        else:
            prev = prev_ref[0]
        return cur + (prev - cur) * mu_ref[...]

    r = mixed(r_ref, pr_ref, mu_r_ref)
    kr = mixed(kr_ref, pkr_ref, mu_kr_ref)
    vr = mixed(vr_ref, pvr_ref, mu_vr_ref)
    wa = mixed(wa_ref, pwa_ref, mu_wa_ref)
    hi = lax.Precision.HIGHEST
    z = w0_ref[...] + jnp.dot(jnp.tanh(wa), wd_ref[...], precision=hi, preferred_element_type=f32)
    nz = -z
    softplus = jnp.maximum(nz, 0.0) + jnp.log1p(jnp.exp(-jnp.abs(nz)))
    w_log = -softplus - 0.5
    a = jax.nn.sigmoid(a0_ref[...] + jnp.dot(wa, wi_ref[...], precision=hi, preferred_element_type=f32))
    ro_ref[0] = r
    wo_ref[0] = -jnp.exp(w_log)
    ko_ref[0] = kr * (1.0 + (a - 1.0) * ka_ref[...])
    vo_ref[0] = vr
    kko_ref[0] = kr * kk_ref[...]
    ao_ref[0] = a


def _rwkv_prep(proj, prev, mu, w0, a0, k_k, k_a, wd_pad, wi_pad, tm):
    G, R, _ = proj.shape
    W = RWKV_WIDTH
    shift_rows = prev is None
    cur_specs = [
        pl.BlockSpec((1, tm, W), lambda g, m: (g, m, P_R // W)),
        pl.BlockSpec((1, tm, W), lambda g, m: (g, m, P_KR // W)),
        pl.BlockSpec((1, tm, W), lambda g, m: (g, m, P_VR // W)),
        pl.BlockSpec((1, tm, LANES), lambda g, m: (g, m, P_WA // LANES)),
    ]
    if shift_rows:
        sub = 8
        above = lambda m: jnp.maximum(m * (tm // sub) - 1, 0)
        prev_specs = [
            pl.BlockSpec((1, sub, W), lambda g, m: (g, above(m), P_R // W)),
            pl.BlockSpec((1, sub, W), lambda g, m: (g, above(m), P_KR // W)),
            pl.BlockSpec((1, sub, W), lambda g, m: (g, above(m), P_VR // W)),
            pl.BlockSpec((1, sub, LANES), lambda g, m: (g, above(m), P_WA // LANES)),
        ]
        prev_args = [proj] * 4
    else:
        prev_specs = [
            pl.BlockSpec((1, tm, W), lambda g, m: (g, m, 0)),
            pl.BlockSpec((1, tm, W), lambda g, m: (g, m, 0)),
            pl.BlockSpec((1, tm, W), lambda g, m: (g, m, 0)),
            pl.BlockSpec((1, tm, LANES), lambda g, m: (g, m, 0)),
        ]
        prev_args = list(prev)
    vec = lambda w: pl.BlockSpec((1, w), lambda g, m: (0, 0))
    lora = pl.BlockSpec((LORA_PAIR, W), lambda g, m: (0, 0))
    out_spec = pl.BlockSpec((1, tm, W), lambda g, m: (g, m, 0))
    return pl.pallas_call(
        functools.partial(_rwkv_prep_kernel, shift_rows),
        grid=(G, R // tm),
        in_specs=cur_specs + prev_specs + [vec(W), vec(W), vec(W), vec(LANES),
                                           vec(W), vec(W), vec(W), vec(W), lora, lora],
        out_specs=[out_spec] * 6,
        out_shape=[jax.ShapeDtypeStruct((G, R, W), f32)] * 6,
        compiler_params=_cparams(("parallel", "arbitrary")),
        name="rwkv_prep",
    )(proj, proj, proj, proj, *prev_args, *mu, w0, a0, k_k, k_a, wd_pad, wi_pad)


def _wkv_kernel(tc, r_ref, w_ref, k_ref, v_ref, kk_ref, a_ref, rk_ref, lnw_ref, lnb_ref, s0_ref,
                y_ref, s_ref):
    @pl.when(pl.program_id(1) == 0)
    def _():
        s_ref[...] = s0_ref[...]

    eye = (lax.broadcasted_iota(jnp.int32, (RW_HEAD, RW_HEAD), 0)
           == lax.broadcasted_iota(jnp.int32, (RW_HEAD, RW_HEAD), 1)).astype(f32)
    rk = rk_ref[...]
    lnw = lnw_ref[...]
    lnb = lnb_ref[...]

    def step(t, carry):
        r, lw, k, v, kkr, a = (ref[0, t] for ref in (r_ref, w_ref, k_ref, v_ref, kk_ref, a_ref))
        w = jnp.exp(lw)
        norm = jnp.sqrt(jnp.sum(kkr * kkr, axis=-1, keepdims=True))
        kk = kkr / jnp.maximum(norm, 1e-12)
        b = kk * a
        bonus = jnp.sum(r * k * rk, axis=-1, keepdims=True) * v
        rows = []
        for h in range(RW_HEADS):
            hs = slice(h, h + 1)
            S = s_ref[0, h]
            sa = jnp.sum(S * (-kk[hs]), axis=-1, keepdims=True)
            v_col = jnp.sum(eye * v[hs], axis=-1, keepdims=True)
            S = S * w[hs] + sa * b[hs] + v_col * k[hs]
            s_ref[0, h] = S
            y = jnp.sum(S * r[hs], axis=-1, keepdims=True)
            mu = jnp.mean(y, axis=0, keepdims=True)
            var = jnp.mean(jnp.square(y - mu), axis=0, keepdims=True)
            yn = (y - mu) * lax.rsqrt(var + GN_EPS)
            rows.append(jnp.sum(yn * eye, axis=0, keepdims=True))
        y_ref[0, t] = jnp.concatenate(rows, axis=0) * lnw + lnb + bonus
        return carry

    lax.fori_loop(0, tc, step, 0)


def _wkv(r, w, k, v, kk, a, r_k, ln_w, ln_b, s0, tc):
    B, T = r.shape[0], r.shape[1]
    hd = (RW_HEADS, RW_HEAD)
    heads = lambda t: t.reshape(B, T, *hd)
    seq = pl.BlockSpec((1, tc, *hd), lambda b, c: (b, c, 0, 0))
    par = pl.BlockSpec(hd, lambda b, c: (0, 0))
    state = pl.BlockSpec((1, RW_HEADS, RW_HEAD, RW_HEAD), lambda b, c: (b, 0, 0, 0))
    y, s_t = pl.pallas_call(
        functools.partial(_wkv_kernel, tc),
        grid=(B, T // tc),
        in_specs=[seq] * 6 + [par] * 3 + [state],
        out_specs=[seq, state],
        out_shape=[
            jax.ShapeDtypeStruct((B, T, *hd), f32),
            jax.ShapeDtypeStruct((B, RW_HEADS, RW_HEAD, RW_HEAD), f32),
        ],
        compiler_params=_cparams(("parallel", "arbitrary")),
        name="wkv_steps",
    )(heads(r), heads(w), heads(k), heads(v), heads(kk), heads(a),
      r_k, ln_w.reshape(hd), ln_b.reshape(hd), s0)
    return y.reshape(B, T, RWKV_WIDTH), s_t


CHUNK = 64
PAIR = 2 * RW_HEAD
N_PAIRS = RW_HEADS // 2
_NN = (((1,), (0,)), ((), ()))
_NT = (((1,), (1,)), ((), ()))
_TN = (((0,), (0,)), ((), ()))
_HI = lax.Precision.HIGHEST


def _mm(a, b, dims=_NN):
    return lax.dot_general(a, b, dims, precision=_HI, preferred_element_type=f32)


def _mmb(a, b, dims=_NN):
    return lax.dot_general(a.astype(bf16), b.astype(bf16), dims, preferred_element_type=f32)


def _wkv_chunk_kernel(r_ref, lw_ref, k_ref, v_ref, kk_ref, a_ref, gr_ref, rk_ref, lnw_ref, lnb_ref,
                      s0_ref, y_ref, s_ref, sp_ref):
    C = CHUNK
    c = pl.program_id(1)

    @pl.when(c == 0)
    def _():
        for p in range(N_PAIRS):
            sp_ref[p] = jnp.concatenate([s0_ref[0, 2 * p], s0_ref[0, 2 * p + 1]], axis=1)

    row = lax.broadcasted_iota(jnp.int32, (PAIR, PAIR), 0)
    col = lax.broadcasted_iota(jnp.int32, (PAIR, PAIR), 1)
    tril = row >= col
    stril = row > col
    eye = (row == col).astype(f32)
    ones_bd = ((row < RW_HEAD) == (col < RW_HEAD)).astype(f32)
    lane_lo = lax.broadcasted_iota(jnp.int32, (C, PAIR), 1) < RW_HEAD

    def bd(x):
        return jnp.concatenate([jnp.where(lane_lo, x, 0.0), jnp.where(lane_lo, 0.0, x)], axis=0)

    lw = lw_ref[0]
    g = _mm(tril[:C, :C].astype(f32), lw)
    e_g = jnp.exp(g)
    e_ng = jnp.exp(-g)
    e_gm = jnp.exp(g - lw)
    e_hat = jnp.exp(g[C - 1:C, :] - g)
    e_end = e_g[C - 1:C, :]

    pairs = range(N_PAIRS)
    sls = [slice(p * PAIR, (p + 1) * PAIR) for p in pairs]
    ins = [[ref[0, :, sl] for ref in (r_ref, k_ref, v_ref, kk_ref, a_ref)] for sl in sls]
    norms = [jnp.sqrt(_mm(x[3] * x[3], ones_bd)) for x in ins]
    at, rt, bt, kt, bh, kh, vb = ([] for _ in range(7))
    for (r, k, v, kkr, icl), norm, sl in zip(ins, norms, sls):
        kk = kkr / jnp.maximum(norm, 1e-12)
        b = kk * icl
        at.append(bd(-kk * e_gm[:, sl]).astype(bf16))
        rt.append(bd(r * e_g[:, sl]).astype(bf16))
        bt.append(bd(b * e_ng[:, sl]).astype(bf16))
        kt.append(bd(k * e_ng[:, sl]).astype(bf16))
        bh.append(bd(b * e_hat[:, sl]).astype(bf16))
        kh.append(bd(k * e_hat[:, sl]).astype(bf16))
        vb.append(bd(v).astype(bf16))

    tril2 = jnp.concatenate([tril, tril], axis=1)
    gram = [_mmb(jnp.concatenate([at[p], rt[p]], axis=0), jnp.concatenate([bt[p], kt[p]], axis=0), _NT)
            for p in pairs]
    lmat = [jnp.where(stril, gm[:PAIR, :PAIR], 0.0) for gm in gram]
    mv = [_mmb(jnp.where(stril, gram[p][:PAIR, PAIR:], 0.0), vb[p]) for p in pairs]
    lower = [jnp.where(tril2, gm[PAIR:, :], 0.0).astype(bf16) for gm in gram]

    tinv = [eye + lm for lm in lmat]
    pw = [_mmb(lm, lm) for lm in lmat]
    for _ in range(4):
        z = [_mmb(jnp.concatenate([x.astype(bf16), t.astype(bf16)], axis=0), x) for t, x in zip(tinv, pw)]
        pw = [zz[:PAIR] for zz in z]
        tinv = [t + zz[PAIR:] for t, zz in zip(tinv, z)]
    tinv = [t + _mmb(t, x) for t, x in zip(tinv, pw)]

    wx = [_mmb(tinv[p], jnp.concatenate([at[p], mv[p].astype(bf16)], axis=1)) for p in pairs]
    s_old = [sp_ref[p] for p in pairs]
    uy0 = [_mmb(jnp.concatenate([wx[p][:, :PAIR].astype(bf16), rt[p]], axis=0), bd(s_old[p]), _NT)
           for p in pairs]
    uv = [jnp.concatenate([(uy0[p][:PAIR] + wx[p][:, PAIR:]).astype(bf16), vb[p]], axis=0) for p in pairs]
    ys = [uy0[p][PAIR:] + _mmb(lower[p], uv[p]) for p in pairs]
    s_add = [_mmb(uv[p], jnp.concatenate([bh[p], kh[p]], axis=0), _TN) for p in pairs]
    for p in pairs:
        sp_ref[p] = s_old[p] * e_end[:, sls[p]] + s_add[p][:RW_HEAD] + s_add[p][RW_HEAD:]

    ys = [y[:C] + y[C:] for y in ys]
    mus = [_mm(y, ones_bd) * (1.0 / RW_HEAD) for y in ys]
    ds = [y - mu for y, mu in zip(ys, mus)]
    var = [_mm(d * d, ones_bd) * (1.0 / RW_HEAD) for d in ds]
    bonus = [_mm(x[0] * x[1] * rk_ref[:, sl], ones_bd) * x[2] for x, sl in zip(ins, sls)]
    for p in pairs:
        sl = sls[p]
        y_rw = ds[p] * lax.rsqrt(var[p] + GN_EPS) * lnw_ref[:, sl] + lnb_ref[:, sl] + bonus[p]
        y_ref[0, :, sl] = (y_rw * _silu(gr_ref[0, :, sl])).astype(y_ref.dtype)

    @pl.when(c == pl.num_programs(1) - 1)
    def _():
        for p in range(N_PAIRS):
            s = sp_ref[p]
            s_ref[0, 2 * p] = s[:, :RW_HEAD]
            s_ref[0, 2 * p + 1] = s[:, RW_HEAD:]


def _wkv_chunked(r, lw, k, v, kk, a, proj, r_k, ln_w, ln_b, s0):
    B, T, W = r.shape
    seq = pl.BlockSpec((1, CHUNK, W), lambda b, c: (b, c, 0))
    gate = pl.BlockSpec((1, CHUNK, W), lambda b, c: (b, c, P_GR // W))
    par = pl.BlockSpec((1, W), lambda b, c: (0, 0))
    state = pl.BlockSpec((1, RW_HEADS, RW_HEAD, RW_HEAD), lambda b, c: (b, 0, 0, 0))
    return pl.pallas_call(
        _wkv_chunk_kernel,
        grid=(B, T // CHUNK),
        in_specs=[seq] * 6 + [gate] + [par] * 3 + [state],
        out_specs=[seq, state],
        out_shape=[
            jax.ShapeDtypeStruct((B, T, W), bf16),
            jax.ShapeDtypeStruct((B, RW_HEADS, RW_HEAD, RW_HEAD), f32),
        ],
        scratch_shapes=[pltpu.VMEM((N_PAIRS, RW_HEAD, PAIR), f32)],
        compiler_params=_cparams(("parallel", "arbitrary")),
        name="wkv_chunks",
    )(r, lw, k, v, kk, a, proj, r_k.reshape(1, W), ln_w.reshape(1, W), ln_b.reshape(1, W), s0)


def _post_kernel(final, gated, att_ref, y_ref, *refs):
    if gated:
        y = y_ref[0]
    else:
        y = (y_ref[0] * _silu(refs[0][0])).astype(bf16)
        refs = refs[1:]
    x_ref, gate_ref, w_ref, fg_ref, o_ref = refs
    cat = jnp.concatenate([att_ref[0], y], axis=1)
    out = jnp.dot(cat, w_ref[...], preferred_element_type=f32)
    x = x_ref[0] + gate_ref[0] * out
    if final:
        ms = jnp.mean(x * x, axis=-1, keepdims=True)
        x = x * lax.rsqrt(ms + NORM_EPS) * fg_ref[...]
    o_ref[0] = x


def _post(att, y_rw, proj, x, gate, w_out_bf, final_g, final, tm):
    G, R, _ = x.shape
    W = RWKV_WIDTH
    gated = proj is None
    gate_in = [] if gated else [pl.BlockSpec((1, tm, W), lambda g, m: (g, m, P_GR // W))]
    gate_arg = [] if gated else [proj]
    return pl.pallas_call(
        functools.partial(_post_kernel, final, gated),
        grid=(G, R // tm),
        in_specs=[
            pl.BlockSpec((1, tm, ATT_WIDTH), lambda g, m: (g, m, 0)),
            pl.BlockSpec((1, tm, W), lambda g, m: (g, m, 0)),
            *gate_in,
            pl.BlockSpec((1, tm, D_MODEL), lambda g, m: (g, m, 0)),
            _mod_spec(gate, tm),
            pl.BlockSpec((D_MODEL, D_MODEL), lambda g, m: (0, 0)),
            pl.BlockSpec((1, D_MODEL), lambda g, m: (0, 0)),
        ],
        out_specs=pl.BlockSpec((1, tm, D_MODEL), lambda g, m: (g, m, 0)),
        out_shape=jax.ShapeDtypeStruct((G, R, D_MODEL), f32),
        compiler_params=_cparams(("parallel", "parallel")),
        name="post_proj",
    )(att, y_rw, *gate_arg, x, gate, w_out_bf, final_g.reshape(1, D_MODEL))


def _arrange_w_in(w):
    pad = jnp.zeros((w.shape[0], P_WIDTH - IN_WIDTH), w.dtype)
    parts = [w[:, Q_OFF:KA_OFF], w[:, GA_OFF:GR_OFF], w[:, GR_OFF:IN_WIDTH], w[:, R_OFF:WD_OFF],
             w[:, KA_OFF:R_OFF], w[:, WD_OFF:GA_OFF], pad]
    return jnp.concatenate(parts, axis=1).astype(bf16)


def _shift_cols(t):
    return jnp.concatenate([t[..., P_R:P_KA], t[..., P_WA:P_WA + LORA_PAIR]], axis=-1)


def kernel(x_prompt, x_sample, cache_k, cache_v, state_wkv, state_shift, c_prompt, c_sample,
           norm_g, w_ada, b_ada, w_in, mu_shift, w0, w_decay, a0, w_iclr, k_k, k_a, r_k,
           ln_w, ln_b, sinks, w_out, final_g):
    Bp, Tp = x_prompt.shape[0], x_prompt.shape[1]
    Bd = x_sample.shape[0]
    W = RWKV_WIDTH

    n_c = Bp + Bd
    c_rows = -(-n_c // 8) * 8
    c_all = jnp.concatenate([c_prompt, c_sample, jnp.zeros((c_rows - n_c, D_MODEL), f32)], axis=0)
    mod = _ada(c_all, w_ada, b_ada)

    tab_p = _rope_tables(jnp.arange(Tp, dtype=jnp.int32))
    tab_s = _rope_tables(jnp.full((Bd,), PAST_LEN, jnp.int32))

    hp = x_prompt
    hs = x_sample.reshape(1, Bd, D_MODEL)
    s0_p = jnp.zeros((Bp, RW_HEADS, RW_HEAD, RW_HEAD), f32)
    outs = {k: [] for k in ("kp", "vp", "sp", "shp", "ks", "vs", "ss", "shs")}
    for l in range(DEPTH):
        final = l == DEPTH - 1
        w_bf = _arrange_w_in(w_in[l])
        w_out_bf = w_out[l].astype(bf16)
        mu_l = mu_shift[l]
        mu = [mu_l[0:W].reshape(1, W), mu_l[W:2 * W].reshape(1, W), mu_l[2 * W:3 * W].reshape(1, W),
              mu_l[3 * W:].reshape(1, LORA_PAIR)]
        vecs = [t[l].reshape(1, W) for t in (w0, a0, k_k, k_a)]
        wd_pad = jnp.concatenate([w_decay[l], jnp.zeros((ICLR_LORA, W), f32)], axis=0)
        wi_pad = jnp.concatenate([jnp.zeros((DECAY_LORA, W), f32), w_iclr[l]], axis=0)
        shift_p, scale_p, gate_p = (mod[l, :Bp, i * D_MODEL:(i + 1) * D_MODEL].reshape(Bp, 1, D_MODEL)
                                    for i in range(3))
        shift_s, scale_s, gate_s = (mod[l, Bp:n_c, i * D_MODEL:(i + 1) * D_MODEL].reshape(1, Bd, D_MODEL)
                                    for i in range(3))

        proj = _norm_proj(hp, norm_g[l], scale_p, shift_p, w_bf, tab_p, tm=1024)
        att = _attn_prompt(proj, sinks[l])
        prep = _rwkv_prep(proj, None, mu, *vecs, wd_pad, wi_pad, tm=256)
        y_rw, s_t = _wkv_chunked(*prep, proj, r_k[l], ln_w[l], ln_b[l], s0_p)
        hp = _post(att, y_rw, None, hp, gate_p, w_out_bf, final_g, final, tm=256)
        tail = proj[:, Tp - WINDOW:]
        outs["kp"].append(tail[..., P_KA:P_KA + KV_WIDTH].reshape(Bp, WINDOW, N_KV_HEADS, HEAD_DIM))
        outs["vp"].append(tail[..., P_VA:P_VA + KV_WIDTH].reshape(Bp, WINDOW, N_KV_HEADS, HEAD_DIM))
        outs["sp"].append(s_t)
        outs["shp"].append(_shift_cols(proj[:, Tp - 1]))

        proj = _norm_proj(hs, norm_g[l], scale_s, shift_s, w_bf, tab_s, tm=Bd)
        rows = proj[0]
        att, nk, nv = _attn_sample(
            rows[:, None, P_Q:P_Q + ATT_WIDTH], rows[:, None, P_KA:P_KA + KV_WIDTH],
            rows[:, None, P_VA:P_VA + KV_WIDTH], rows[:, None, P_GA:P_GA + ATT_WIDTH],
            cache_k[l].reshape(Bd, WINDOW, KV_WIDTH), cache_v[l].reshape(Bd, WINDOW, KV_WIDTH),
            sinks[l])
        sh = state_shift[l]
        prev = [sh[None, :, 0:W], sh[None, :, W:2 * W], sh[None, :, 2 * W:3 * W], sh[None, :, 3 * W:]]
        prep = _rwkv_prep(proj, prev, mu, *vecs, wd_pad, wi_pad, tm=Bd)
        prep = [t.reshape(Bd, 1, W) for t in prep]
        y_rw, s_t = _wkv(*prep, r_k[l], ln_w[l], ln_b[l], state_wkv[l], tc=1)
        hs = _post(att.reshape(1, Bd, ATT_WIDTH), y_rw.reshape(1, Bd, W), proj, hs, gate_s, w_out_bf,
                   final_g, final, tm=Bd)
        outs["ks"].append(nk.reshape(Bd, WINDOW, N_KV_HEADS, HEAD_DIM))
        outs["vs"].append(nv.reshape(Bd, WINDOW, N_KV_HEADS, HEAD_DIM))
        outs["ss"].append(s_t)
        outs["shs"].append(_shift_cols(rows))

    st = lambda k: jnp.stack(outs[k])
    return (hp, hs.reshape(Bd, 1, D_MODEL), st("kp"), st("vp"), st("sp"), st("shp"),
            st("ks"), st("vs"), st("ss"), st("shs"))
```

```python
import functools

import jax
import jax.numpy as jnp
from jax import lax
from jax.experimental import pallas as pl
from jax.experimental.pallas import tpu as pltpu

f32 = jnp.float32
bf16 = jnp.bfloat16

D_MODEL = 2048
DEPTH = 2
PAST_LEN = 16384
ATT_WIDTH = 1024
RWKV_WIDTH = 1024
HEAD_DIM = 64
N_Q_HEADS = 16
N_KV_HEADS = 4
GQA_GROUP = 4
KV_WIDTH = 256
WINDOW = 128
ROT_DIM = 16
ROPE_THETA = 500000.0
RW_HEAD = 64
RW_HEADS = 16
DECAY_LORA = 64
ICLR_LORA = 64
LORA_PAIR = DECAY_LORA + ICLR_LORA
NORM_EPS = 1e-5
GN_EPS = 64e-5
NEG_BIG = -1e30

Q_OFF = 0
KA_OFF = Q_OFF + ATT_WIDTH
VA_OFF = KA_OFF + KV_WIDTH
R_OFF = VA_OFF + KV_WIDTH
KR_OFF = R_OFF + RWKV_WIDTH
VR_OFF = KR_OFF + RWKV_WIDTH
WD_OFF = VR_OFF + RWKV_WIDTH
AD_OFF = WD_OFF + DECAY_LORA
GA_OFF = AD_OFF + ICLR_LORA
GR_OFF = GA_OFF + ATT_WIDTH
IN_WIDTH = GR_OFF + RWKV_WIDTH
SHIFT_DIM = GA_OFF - R_OFF

LANES = 128
P_Q = 0
P_GA = 1024
P_GR = 2048
P_R = 3072
P_KR = 4096
P_VR = 5120
P_KA = 6144
P_VA = 6400
P_WA = 6656
P_WIDTH = 7168
PROJ_TN = 1024

VMEM_LIMIT = 56 * 1024 * 1024


def _silu(x):
    return x * jax.nn.sigmoid(x)


def _cparams(sem):
    return pltpu.CompilerParams(dimension_semantics=sem, vmem_limit_bytes=VMEM_LIMIT)


def _ada_kernel(c_ref, w_ref, b_ref, o_ref):
    c = c_ref[...]
    o_ref[0] = jnp.dot(_silu(c), w_ref[0], precision=lax.Precision.HIGHEST,
                       preferred_element_type=f32) + b_ref[0]


def _ada(c_all, w_ada, b_ada):
    rows = c_all.shape[0]
    tn = 768
    n_out = w_ada.shape[2]
    return pl.pallas_call(
        _ada_kernel,
        grid=(DEPTH, n_out // tn),
        in_specs=[
            pl.BlockSpec((rows, D_MODEL), lambda l, n: (0, 0)),
            pl.BlockSpec((1, D_MODEL, tn), lambda l, n: (l, 0, n)),
            pl.BlockSpec((1, 1, tn), lambda l, n: (l, 0, n)),
        ],
        out_specs=pl.BlockSpec((1, rows, tn), lambda l, n: (l, 0, n)),
        out_shape=jax.ShapeDtypeStruct((DEPTH, rows, n_out), f32),
        compiler_params=_cparams(("parallel", "parallel")),
        name="ada_mod",
    )(c_all, w_ada, b_ada.reshape(DEPTH, 1, n_out))


def _rope(x, tab):
    w = x.shape[1]
    reps = w // LANES
    cosf, up, dn = (jnp.concatenate([tab[i]] * reps, axis=1) for i in range(3))
    half = ROT_DIM // 2
    return x * cosf + pltpu.roll(x, w - half, 1) * up + pltpu.roll(x, half, 1) * dn


def _norm_proj_kernel(x_ref, g_ref, scale_ref, shift_ref, w_ref, tab_ref, o_ref, h_ref):
    n = pl.program_id(2)

    @pl.when(n == 0)
    def _():
        x = x_ref[0]
        ms = jnp.mean(x * x, axis=-1, keepdims=True)
        y = x * lax.rsqrt(ms + NORM_EPS) * g_ref[...]
        h_ref[...] = (y * (1.0 + scale_ref[0]) + shift_ref[0]).astype(bf16)

    q_tile = P_Q // PROJ_TN
    k_tile = P_KA // PROJ_TN
    tm = h_ref.shape[0]
    rc = min(tm, 256)

    def rows(i):
        rs = slice(i * rc, (i + 1) * rc)
        return rs, jnp.dot(h_ref[rs, :], w_ref[...], preferred_element_type=f32)

    @pl.when(n == q_tile)
    def _():
        for i in range(tm // rc):
            rs, res = rows(i)
            o_ref[0, rs, :] = _rope(res, tab_ref[:, rs, :]) * (HEAD_DIM ** -0.5)

    @pl.when(n == k_tile)
    def _():
        for i in range(tm // rc):
            rs, res = rows(i)
            o_ref[0, rs, :] = jnp.concatenate(
                [_rope(res[:, :KV_WIDTH], tab_ref[:, rs, :]), res[:, KV_WIDTH:]], axis=1)

    @pl.when((n != q_tile) & (n != k_tile))
    def _():
        o_ref[0] = jnp.dot(h_ref[...], w_ref[...], preferred_element_type=f32)


def _mod_spec(mod, tm):
    if mod.shape[1] == 1:
        return pl.BlockSpec((1, 1, D_MODEL), lambda g, m, *_: (g, 0, 0))
    return pl.BlockSpec((1, tm, D_MODEL), lambda g, m, *_: (g, m, 0))


def _norm_proj(x, norm_g, scale, shift, w_bf, tab, tm):
    G, R, _ = x.shape
    return pl.pallas_call(
        _norm_proj_kernel,
        grid=(G, R // tm, P_WIDTH // PROJ_TN),
        in_specs=[
            pl.BlockSpec((1, tm, D_MODEL), lambda g, m, n: (g, m, 0)),
            pl.BlockSpec((1, D_MODEL), lambda g, m, n: (0, 0)),
            _mod_spec(scale, tm),
            _mod_spec(shift, tm),
            pl.BlockSpec((D_MODEL, PROJ_TN), lambda g, m, n: (0, n)),
            pl.BlockSpec((3, tm, LANES), lambda g, m, n: (0, m, 0)),
        ],
        out_specs=pl.BlockSpec((1, tm, PROJ_TN), lambda g, m, n: (g, m, n)),
        out_shape=jax.ShapeDtypeStruct((G, R, P_WIDTH), f32),
        scratch_shapes=[pltpu.VMEM((tm, D_MODEL), bf16)],
        compiler_params=_cparams(("parallel", "parallel", "arbitrary")),
        name="norm_proj",
    )(x, norm_g.reshape(1, D_MODEL), scale, shift, w_bf, tab)


def _rope_tables(pos):
    half = ROT_DIM // 2
    inv_freq = ROPE_THETA ** (-jnp.arange(half, dtype=f32) * (2.0 / ROT_DIM))
    ang = pos.astype(f32)[:, None] * inv_freq[None, :]
    cos, sin = jnp.cos(ang), jnp.sin(ang)
    t = pos.shape[0]
    z8 = jnp.zeros((t, half), f32)
    rest = HEAD_DIM - ROT_DIM
    cos64 = jnp.concatenate([cos, cos, jnp.ones((t, rest), f32)], axis=1)
    up64 = jnp.concatenate([-sin, z8, jnp.zeros((t, rest), f32)], axis=1)
    dn64 = jnp.concatenate([z8, sin, jnp.zeros((t, rest), f32)], axis=1)
    rep = LANES // HEAD_DIM
    return jnp.stack([jnp.tile(a, (1, rep)) for a in (cos64, up64, dn64)])


def _attn_prompt_kernel(sinks_ref, q_ref, kc_ref, kp_ref, vc_ref, vp_ref, ga_ref, o_ref):
    n = pl.program_id(1)
    wn = WINDOW
    half = HEAD_DIM
    q = q_ref[0].astype(bf16)
    k_t = jnp.concatenate([kp_ref[0], kc_ref[0]], axis=0).T.astype(bf16)
    vcat = jnp.concatenate([vp_ref[0], vc_ref[0]], axis=0)
    ga = ga_ref[0]

    qi = lax.broadcasted_iota(jnp.int32, (2 * wn, 2 * wn), 0) & (wn - 1)
    kj = lax.broadcasted_iota(jnp.int32, (2 * wn, 2 * wn), 1)
    rel = wn + qi - kj
    mask = (rel >= 0) & (rel <= wn) & ((kj >= wn) | (n > 0))
    top = lax.broadcasted_iota(jnp.int32, (2 * wn, 1), 0) < wn
    lo = lax.broadcasted_iota(jnp.int32, (2 * wn, LANES), 1) < half
    zeros_k = jnp.zeros((half, 2 * wn), bf16)
    ones_lo = jnp.where(lo, 1.0, 0.0).astype(bf16)
    ones_hi = jnp.where(lo, 0.0, 1.0).astype(bf16)

    for j in range(N_KV_HEADS // 2):
        vblk = vcat[:, j * LANES:(j + 1) * LANES]
        vswap = pltpu.roll(vblk, half, 1)
        for g in (2 * j, 2 * j + 1):
            own, other = (vblk, vswap) if g % 2 == 0 else (vswap, vblk)
            v_lo = jnp.where(lo, own, 0.0).astype(bf16)
            v_hi = jnp.where(lo, 0.0, other).astype(bf16)
            rhs_pv = jnp.concatenate([jnp.concatenate([v_lo, ones_lo], axis=1),
                                      jnp.concatenate([v_hi, ones_hi], axis=1)], axis=0)
            kg = k_t[g * half:(g + 1) * half, :]
            rhs_qk = jnp.concatenate([jnp.concatenate([kg, zeros_k], axis=0),
                                      jnp.concatenate([zeros_k, kg], axis=0)], axis=1)
            b0, b1 = 2 * g, 2 * g + 1
            qg = jnp.concatenate([q[:, b0 * LANES:(b0 + 1) * LANES], q[:, b1 * LANES:(b1 + 1) * LANES]],
                                 axis=0)
            s_all = jnp.dot(qg, rhs_qk, preferred_element_type=f32)
            ps, es = [], []
            for hh in range(2):
                s = jnp.where(mask, s_all[:, hh * 2 * wn:(hh + 1) * 2 * wn], NEG_BIG)
                sink = jnp.where(top, sinks_ref[2 * b0 + hh], sinks_ref[2 * b1 + hh])
                m = jnp.maximum(jnp.max(s, axis=-1, keepdims=True), sink)
                ps.append(jnp.exp(s - m).astype(bf16))
                es.append(jnp.exp(sink - m))
            res = jnp.dot(jnp.concatenate(ps, axis=1), rhs_pv, preferred_element_type=f32)
            out = res[:, :LANES] / (res[:, LANES:] + jnp.where(lo, es[0], es[1]))
            for i, blk in enumerate((b0, b1)):
                sl = slice(blk * LANES, (blk + 1) * LANES)
                o_ref[0, :, sl] = (out[i * wn:(i + 1) * wn] * _silu(ga[:, sl])).astype(o_ref.dtype)


def _attn_prompt(proj, sinks):
    B, T, _ = proj.shape
    nb = T // WINDOW
    kvb = KV_WIDTH
    prev = lambda b, n: jnp.maximum(n - 1, 0)
    return pl.pallas_call(
        _attn_prompt_kernel,
        grid=(B, nb),
        in_specs=[
            pl.BlockSpec(memory_space=pltpu.SMEM),
            pl.BlockSpec((1, WINDOW, ATT_WIDTH), lambda b, n: (b, n, P_Q // ATT_WIDTH)),
            pl.BlockSpec((1, WINDOW, kvb), lambda b, n: (b, n, P_KA // kvb)),
            pl.BlockSpec((1, WINDOW, kvb), lambda b, n: (b, prev(b, n), P_KA // kvb)),
            pl.BlockSpec((1, WINDOW, kvb), lambda b, n: (b, n, P_VA // kvb)),
            pl.BlockSpec((1, WINDOW, kvb), lambda b, n: (b, prev(b, n), P_VA // kvb)),
            pl.BlockSpec((1, WINDOW, ATT_WIDTH), lambda b, n: (b, n, P_GA // ATT_WIDTH)),
        ],
        out_specs=pl.BlockSpec((1, WINDOW, ATT_WIDTH), lambda b, n: (b, n, 0)),
        out_shape=jax.ShapeDtypeStruct((B, T, ATT_WIDTH), bf16),
        compiler_params=_cparams(("parallel", "arbitrary")),
        name="attn_prompt",
    )(sinks, proj, proj, proj, proj, proj, proj)


def _attn_sample_kernel(sinks_ref, q_ref, kn_ref, vn_ref, ga_ref, ck_ref, cv_ref,
                        o_ref, nk_ref, nv_ref):
    q = q_ref[0]
    kn = kn_ref[0]
    vn = vn_ref[0]
    ck = ck_ref[0]
    cv = cv_ref[0]
    ckb = ck.astype(bf16)
    cvb = cv.astype(bf16)
    outs = []
    for g in range(N_KV_HEADS):
        lo, hi = g * HEAD_DIM, (g + 1) * HEAD_DIM
        qg = jnp.concatenate(
            [q[:, (g * GQA_GROUP + i) * HEAD_DIM:(g * GQA_GROUP + i + 1) * HEAD_DIM]
             for i in range(GQA_GROUP)], axis=0)
        s = lax.dot_general(qg.astype(bf16), ckb[:, lo:hi], (((1,), (1,)), ((), ())),
                            preferred_element_type=f32)
        s_new = jnp.sum(qg * kn[:, lo:hi], axis=-1, keepdims=True)
        sink = sinks_ref[g * GQA_GROUP:(g + 1) * GQA_GROUP, :]
        m = jnp.maximum(jnp.maximum(jnp.max(s, axis=-1, keepdims=True), s_new), sink)
        p = jnp.exp(s - m)
        p_new = jnp.exp(s_new - m)
        den = jnp.sum(p, axis=-1, keepdims=True) + p_new + jnp.exp(sink - m)
        o = jnp.dot(p.astype(bf16), cvb[:, lo:hi], preferred_element_type=f32) + p_new * vn[:, lo:hi]
        o = o / den
        outs.extend(o[i:i + 1, :] for i in range(GQA_GROUP))
    ga = ga_ref[0]
    o_ref[0] = (jnp.concatenate(outs, axis=1) * _silu(ga)).astype(o_ref.dtype)
    row = lax.broadcasted_iota(jnp.int32, (WINDOW, KV_WIDTH), 0)
    last = row == WINDOW - 1
    nk_ref[0] = jnp.where(last, kn, pltpu.roll(ck, WINDOW - 1, 0))
    nv_ref[0] = jnp.where(last, vn, pltpu.roll(cv, WINDOW - 1, 0))


def _attn_sample(q, kn, vn, ga, ck, cv, sinks):
    Bd = q.shape[0]
    row = lambda w: pl.BlockSpec((1, 1, w), lambda b: (b, 0, 0))
    cache = pl.BlockSpec((1, WINDOW, KV_WIDTH), lambda b: (b, 0, 0))
    return pl.pallas_call(
        _attn_sample_kernel,
        grid=(Bd,),
        in_specs=[
            pl.BlockSpec((N_Q_HEADS, 1), lambda b: (0, 0)),
            row(ATT_WIDTH), row(KV_WIDTH), row(KV_WIDTH), row(ATT_WIDTH), cache, cache,
        ],
        out_specs=[row(ATT_WIDTH), cache, cache],
        out_shape=[
            jax.ShapeDtypeStruct((Bd, 1, ATT_WIDTH), bf16),
            jax.ShapeDtypeStruct((Bd, WINDOW, KV_WIDTH), f32),
            jax.ShapeDtypeStruct((Bd, WINDOW, KV_WIDTH), f32),
        ],
        compiler_params=_cparams(("parallel",)),
        name="attn_sample",
    )(sinks.reshape(N_Q_HEADS, 1), q, kn, vn, ga, ck, cv)


def _rwkv_prep_kernel(shift_rows, r_ref, kr_ref, vr_ref, wa_ref, pr_ref, pkr_ref, pvr_ref, pwa_ref,
                      mu_r_ref, mu_kr_ref, mu_vr_ref, mu_wa_ref, w0_ref, a0_ref, kk_ref, ka_ref,
                      wd_ref, wi_ref, ro_ref, wo_ref, ko_ref, vo_ref, kko_ref, ao_ref):
    m = pl.program_id(1)

    def mixed(cur_ref, prev_ref, mu_ref):
        cur = cur_ref[0]
        if shift_rows:
            above = jnp.where(m == 0, 0.0, prev_ref[0][7:8, :])
            rowi = lax.broadcasted_iota(jnp.int32, cur.shape, 0)
            prev = jnp.where(rowi == 0, above, pltpu.roll(cur, 1, 0))
        else:
            prev = prev_ref[0]
        return cur + (prev - cur) * mu_ref[...]

    r = mixed(r_ref, pr_ref, mu_r_ref)
    kr = mixed(kr_ref, pkr_ref, mu_kr_ref)
    vr = mixed(vr_ref, pvr_ref, mu_vr_ref)
    wa = mixed(wa_ref, pwa_ref, mu_wa_ref)
    hi = lax.Precision.HIGHEST
    z = w0_ref[...] + jnp.dot(jnp.tanh(wa), wd_ref[...], precision=hi, preferred_element_type=f32)
    nz = -z
    softplus = jnp.maximum(nz, 0.0) + jnp.log1p(jnp.exp(-jnp.abs(nz)))
    w_log = -softplus - 0.5
    a = jax.nn.sigmoid(a0_ref[...] + jnp.dot(wa, wi_ref[...], precision=hi, preferred_element_type=f32))
    ro_ref[0] = r
    wo_ref[0] = -jnp.exp(w_log)
    ko_ref[0] = kr * (1.0 + (a - 1.0) * ka_ref[...])
    vo_ref[0] = vr
    kko_ref[0] = kr * kk_ref[...]
    ao_ref[0] = a


def _rwkv_prep(proj, prev, mu, w0, a0, k_k, k_a, wd_pad, wi_pad, tm):
    G, R, _ = proj.shape
    W = RWKV_WIDTH
    shift_rows = prev is None
    cur_specs = [
        pl.BlockSpec((1, tm, W), lambda g, m: (g, m, P_R // W)),
        pl.BlockSpec((1, tm, W), lambda g, m: (g, m, P_KR // W)),
        pl.BlockSpec((1, tm, W), lambda g, m: (g, m, P_VR // W)),
        pl.BlockSpec((1, tm, LANES), lambda g, m: (g, m, P_WA // LANES)),
    ]
    if shift_rows:
        sub = 8
        above = lambda m: jnp.maximum(m * (tm // sub) - 1, 0)
        prev_specs = [
            pl.BlockSpec((1, sub, W), lambda g, m: (g, above(m), P_R // W)),
            pl.BlockSpec((1, sub, W), lambda g, m: (g, above(m), P_KR // W)),
            pl.BlockSpec((1, sub, W), lambda g, m: (g, above(m), P_VR // W)),
            pl.BlockSpec((1, sub, LANES), lambda g, m: (g, above(m), P_WA // LANES)),
        ]
        prev_args = [proj] * 4
    else:
        prev_specs = [
            pl.BlockSpec((1, tm, W), lambda g, m: (g, m, 0)),
            pl.BlockSpec((1, tm, W), lambda g, m: (g, m, 0)),
            pl.BlockSpec((1, tm, W), lambda g, m: (g, m, 0)),
            pl.BlockSpec((1, tm, LANES), lambda g, m: (g, m, 0)),
        ]
        prev_args = list(prev)
    vec = lambda w: pl.BlockSpec((1, w), lambda g, m: (0, 0))
    lora = pl.BlockSpec((LORA_PAIR, W), lambda g, m: (0, 0))
    out_spec = pl.BlockSpec((1, tm, W), lambda g, m: (g, m, 0))
    return pl.pallas_call(
        functools.partial(_rwkv_prep_kernel, shift_rows),
        grid=(G, R // tm),
        in_specs=cur_specs + prev_specs + [vec(W), vec(W), vec(W), vec(LANES),
                                           vec(W), vec(W), vec(W), vec(W), lora, lora],
        out_specs=[out_spec] * 6,
        out_shape=[jax.ShapeDtypeStruct((G, R, W), f32)] * 6,
        compiler_params=_cparams(("parallel", "arbitrary")),
        name="rwkv_prep",
    )(proj, proj, proj, proj, *prev_args, *mu, w0, a0, k_k, k_a, wd_pad, wi_pad)


def _wkv_kernel(tc, r_ref, w_ref, k_ref, v_ref, kk_ref, a_ref, rk_ref, lnw_ref, lnb_ref, s0_ref,
                y_ref, s_ref):
    @pl.when(pl.program_id(1) == 0)
    def _():
        s_ref[...] = s0_ref[...]

    eye = (lax.broadcasted_iota(jnp.int32, (RW_HEAD, RW_HEAD), 0)
           == lax.broadcasted_iota(jnp.int32, (RW_HEAD, RW_HEAD), 1)).astype(f32)
    rk = rk_ref[...]
    lnw = lnw_ref[...]
    lnb = lnb_ref[...]

    def step(t, carry):
        r, lw, k, v, kkr, a = (ref[0, t] for ref in (r_ref, w_ref, k_ref, v_ref, kk_ref, a_ref))
        w = jnp.exp(lw)
        norm = jnp.sqrt(jnp.sum(kkr * kkr, axis=-1, keepdims=True))
        kk = kkr / jnp.maximum(norm, 1e-12)
        b = kk * a
        bonus = jnp.sum(r * k * rk, axis=-1, keepdims=True) * v
        rows = []
        for h in range(RW_HEADS):
            hs = slice(h, h + 1)
            S = s_ref[0, h]
            sa = jnp.sum(S * (-kk[hs]), axis=-1, keepdims=True)
            v_col = jnp.sum(eye * v[hs], axis=-1, keepdims=True)
            S = S * w[hs] + sa * b[hs] + v_col * k[hs]
            s_ref[0, h] = S
            y = jnp.sum(S * r[hs], axis=-1, keepdims=True)
            mu = jnp.mean(y, axis=0, keepdims=True)
            var = jnp.mean(jnp.square(y - mu), axis=0, keepdims=True)
            yn = (y - mu) * lax.rsqrt(var + GN_EPS)
            rows.append(jnp.sum(yn * eye, axis=0, keepdims=True))
        y_ref[0, t] = jnp.concatenate(rows, axis=0) * lnw + lnb + bonus
        return carry

    lax.fori_loop(0, tc, step, 0)


def _wkv(r, w, k, v, kk, a, r_k, ln_w, ln_b, s0, tc):
    B, T = r.shape[0], r.shape[1]
    hd = (RW_HEADS, RW_HEAD)
    heads = lambda t: t.reshape(B, T, *hd)
    seq = pl.BlockSpec((1, tc, *hd), lambda b, c: (b, c, 0, 0))
    par = pl.BlockSpec(hd, lambda b, c: (0, 0))
    state = pl.BlockSpec((1, RW_HEADS, RW_HEAD, RW_HEAD), lambda b, c: (b, 0, 0, 0))
    y, s_t = pl.pallas_call(
        functools.partial(_wkv_kernel, tc),
        grid=(B, T // tc),
        in_specs=[seq] * 6 + [par] * 3 + [state],
        out_specs=[seq, state],
        out_shape=[
            jax.ShapeDtypeStruct((B, T, *hd), f32),
            jax.ShapeDtypeStruct((B, RW_HEADS, RW_HEAD, RW_HEAD), f32),
        ],
        compiler_params=_cparams(("parallel", "arbitrary")),
        name="wkv_steps",
    )(heads(r), heads(w), heads(k), heads(v), heads(kk), heads(a),
      r_k, ln_w.reshape(hd), ln_b.reshape(hd), s0)
    return y.reshape(B, T, RWKV_WIDTH), s_t


CHUNK = 64
PAIR = 2 * RW_HEAD
N_PAIRS = RW_HEADS // 2
_NN = (((1,), (0,)), ((), ()))
_NT = (((1,), (1,)), ((), ()))
_TN = (((0,), (0,)), ((), ()))


def _mmb(a, b, dims=_NN):
    return lax.dot_general(a.astype(bf16), b.astype(bf16), dims, preferred_element_type=f32)


def _wkv_chunk_kernel(r_ref, lw_ref, k_ref, v_ref, kk_ref, a_ref, gr_ref, rk_ref, lnw_ref, lnb_ref,
                      s0_ref, y_ref, s_ref, sp_ref):
    C = CHUNK
    c = pl.program_id(1)

    @pl.when(c == 0)
    def _():
        for p in range(N_PAIRS):
            sp_ref[p] = jnp.concatenate([s0_ref[0, 2 * p], s0_ref[0, 2 * p + 1]], axis=1)

    row = lax.broadcasted_iota(jnp.int32, (PAIR, PAIR), 0)
    col = lax.broadcasted_iota(jnp.int32, (PAIR, PAIR), 1)
    tril = row >= col
    stril = row > col
    eye = (row == col).astype(f32)
    lane_lo = lax.broadcasted_iota(jnp.int32, (C, PAIR), 1) < RW_HEAD

    def bd(x):
        return jnp.concatenate([jnp.where(lane_lo, x, 0.0), jnp.where(lane_lo, 0.0, x)], axis=0)

    def head_sums(x):
        lo_sum = jnp.sum(jnp.where(lane_lo, x, 0.0), axis=-1, keepdims=True)
        hi_sum = jnp.sum(jnp.where(lane_lo, 0.0, x), axis=-1, keepdims=True)
        return jnp.where(lane_lo, lo_sum, hi_sum)

    lw = lw_ref[0]
    width = lw.shape[1]
    lw_a = lw.astype(bf16)
    rest = lw - lw_a.astype(f32)
    lw_b = rest.astype(bf16)
    lw_c = (rest - lw_b.astype(f32)).astype(bf16)
    g3 = jnp.dot(tril[:C, :C].astype(bf16), jnp.concatenate([lw_a, lw_b, lw_c], axis=1),
                 preferred_element_type=f32)
    g = g3[:, :width] + g3[:, width:2 * width] + g3[:, 2 * width:]
    e_g = jnp.exp(g)
    e_ng = jnp.exp(-g)
    e_gm = jnp.exp(g - lw)
    e_hat = jnp.exp(g[C - 1:C, :] - g)
    e_end = e_g[C - 1:C, :]

    pairs = range(N_PAIRS)
    sls = [slice(p * PAIR, (p + 1) * PAIR) for p in pairs]
    ins = [[ref[0, :, sl] for ref in (r_ref, k_ref, v_ref, kk_ref, a_ref)] for sl in sls]
    norms = [jnp.sqrt(head_sums(x[3] * x[3])) for x in ins]
    at, rt, bt, kt, bh, kh, vb = ([] for _ in range(7))
    for (r, k, v, kkr, icl), norm, sl in zip(ins, norms, sls):
        kk = kkr / jnp.maximum(norm, 1e-12)
        b = kk * icl
        at.append(bd(-kk * e_gm[:, sl]).astype(bf16))
        rt.append(bd(r * e_g[:, sl]).astype(bf16))
        bt.append(bd(b * e_ng[:, sl]).astype(bf16))
        kt.append(bd(k * e_ng[:, sl]).astype(bf16))
        bh.append(bd(b * e_hat[:, sl]).astype(bf16))
        kh.append(bd(k * e_hat[:, sl]).astype(bf16))
        vb.append(bd(v).astype(bf16))

    tril2 = jnp.concatenate([tril, tril], axis=1)
    gram = [_mmb(jnp.concatenate([at[p], rt[p]], axis=0), jnp.concatenate([bt[p], kt[p]], axis=0), _NT)
            for p in pairs]
    lmat = [jnp.where(stril, gm[:PAIR, :PAIR], 0.0) for gm in gram]
    mv = [_mmb(jnp.where(stril, gram[p][:PAIR, PAIR:], 0.0), vb[p]) for p in pairs]
    lower = [jnp.where(tril2, gm[PAIR:, :], 0.0).astype(bf16) for gm in gram]

    tinv = [eye + lm for lm in lmat]
    pw = [_mmb(lm, lm) for lm in lmat]
    for _ in range(4):
        z = [_mmb(jnp.concatenate([x.astype(bf16), t.astype(bf16)], axis=0), x) for t, x in zip(tinv, pw)]
        pw = [zz[:PAIR] for zz in z]
        tinv = [t + zz[PAIR:] for t, zz in zip(tinv, z)]
    tinv = [t + _mmb(t, x) for t, x in zip(tinv, pw)]

    wx = [_mmb(tinv[p], jnp.concatenate([at[p], mv[p].astype(bf16)], axis=1)) for p in pairs]
    s_old = [sp_ref[p] for p in pairs]
    uy0 = [_mmb(jnp.concatenate([wx[p][:, :PAIR].astype(bf16), rt[p]], axis=0), bd(s_old[p]), _NT)
           for p in pairs]
    uv = [jnp.concatenate([(uy0[p][:PAIR] + wx[p][:, PAIR:]).astype(bf16), vb[p]], axis=0) for p in pairs]
    ys = [uy0[p][PAIR:] + _mmb(lower[p], uv[p]) for p in pairs]
    s_add = [_mmb(uv[p], jnp.concatenate([bh[p], kh[p]], axis=0), _TN) for p in pairs]
    for p in pairs:
        sp_ref[p] = s_old[p] * e_end[:, sls[p]] + s_add[p][:RW_HEAD] + s_add[p][RW_HEAD:]

    ys = [y[:C] + y[C:] for y in ys]
    mus = [head_sums(y) * (1.0 / RW_HEAD) for y in ys]
    ds = [y - mu for y, mu in zip(ys, mus)]
    var = [head_sums(d * d) * (1.0 / RW_HEAD) for d in ds]
    bonus = [head_sums(x[0] * x[1] * rk_ref[:, sl]) * x[2] for x, sl in zip(ins, sls)]
    for p in pairs:
        sl = sls[p]
        y_rw = ds[p] * lax.rsqrt(var[p] + GN_EPS) * lnw_ref[:, sl] + lnb_ref[:, sl] + bonus[p]
        y_ref[0, :, sl] = (y_rw * _silu(gr_ref[0, :, sl])).astype(y_ref.dtype)

    @pl.when(c == pl.num_programs(1) - 1)
    def _():
        for p in range(N_PAIRS):
            s = sp_ref[p]
            s_ref[0, 2 * p] = s[:, :RW_HEAD]
            s_ref[0, 2 * p + 1] = s[:, RW_HEAD:]


def _wkv_chunked(r, lw, k, v, kk, a, proj, r_k, ln_w, ln_b, s0):
    B, T, W = r.shape
    seq = pl.BlockSpec((1, CHUNK, W), lambda b, c: (b, c, 0))
    gate = pl.BlockSpec((1, CHUNK, W), lambda b, c: (b, c, P_GR // W))
    par = pl.BlockSpec((1, W), lambda b, c: (0, 0))
    state = pl.BlockSpec((1, RW_HEADS, RW_HEAD, RW_HEAD), lambda b, c: (b, 0, 0, 0))
    return pl.pallas_call(
        _wkv_chunk_kernel,
        grid=(B, T // CHUNK),
        in_specs=[seq] * 6 + [gate] + [par] * 3 + [state],
        out_specs=[seq, state],
        out_shape=[
            jax.ShapeDtypeStruct((B, T, W), bf16),
            jax.ShapeDtypeStruct((B, RW_HEADS, RW_HEAD, RW_HEAD), f32),
        ],
        scratch_shapes=[pltpu.VMEM((N_PAIRS, RW_HEAD, PAIR), f32)],
        compiler_params=_cparams(("parallel", "arbitrary")),
        name="wkv_chunks",
    )(r, lw, k, v, kk, a, proj, r_k.reshape(1, W), ln_w.reshape(1, W), ln_b.reshape(1, W), s0)


def _post_kernel(final, gated, att_ref, y_ref, *refs):
    if gated:
        y = y_ref[0]
    else:
        y = (y_ref[0] * _silu(refs[0][0])).astype(bf16)
        refs = refs[1:]
    x_ref, gate_ref, w_ref, fg_ref, o_ref = refs
    cat = jnp.concatenate([att_ref[0], y], axis=1)
    out = jnp.dot(cat, w_ref[...], preferred_element_type=f32)
    x = x_ref[0] + gate_ref[0] * out
    if final:
        ms = jnp.mean(x * x, axis=-1, keepdims=True)
        x = x * lax.rsqrt(ms + NORM_EPS) * fg_ref[...]
    o_ref[0] = x


def _post(att, y_rw, proj, x, gate, w_out_bf, final_g, final, tm):
    G, R, _ = x.shape
    W = RWKV_WIDTH
    gated = proj is None
    gate_in = [] if gated else [pl.BlockSpec((1, tm, W), lambda g, m: (g, m, P_GR // W))]
    gate_arg = [] if gated else [proj]
    return pl.pallas_call(
        functools.partial(_post_kernel, final, gated),
        grid=(G, R // tm),
        in_specs=[
            pl.BlockSpec((1, tm, ATT_WIDTH), lambda g, m: (g, m, 0)),
            pl.BlockSpec((1, tm, W), lambda g, m: (g, m, 0)),
            *gate_in,
            pl.BlockSpec((1, tm, D_MODEL), lambda g, m: (g, m, 0)),
            _mod_spec(gate, tm),
            pl.BlockSpec((D_MODEL, D_MODEL), lambda g, m: (0, 0)),
            pl.BlockSpec((1, D_MODEL), lambda g, m: (0, 0)),
        ],
        out_specs=pl.BlockSpec((1, tm, D_MODEL), lambda g, m: (g, m, 0)),
        out_shape=jax.ShapeDtypeStruct((G, R, D_MODEL), f32),
        compiler_params=_cparams(("parallel", "parallel")),
        name="post_proj",
    )(att, y_rw, *gate_arg, x, gate, w_out_bf, final_g.reshape(1, D_MODEL))


def _arrange_w_in(w):
    pad = jnp.zeros((w.shape[0], P_WIDTH - IN_WIDTH), w.dtype)
    parts = [w[:, Q_OFF:KA_OFF], w[:, GA_OFF:GR_OFF], w[:, GR_OFF:IN_WIDTH], w[:, R_OFF:WD_OFF],
             w[:, KA_OFF:R_OFF], w[:, WD_OFF:GA_OFF], pad]
    return jnp.concatenate(parts, axis=1).astype(bf16)


def _shift_cols(t):
    return jnp.concatenate([t[..., P_R:P_KA], t[..., P_WA:P_WA + LORA_PAIR]], axis=-1)


def kernel(x_prompt, x_sample, cache_k, cache_v, state_wkv, state_shift, c_prompt, c_sample,
           norm_g, w_ada, b_ada, w_in, mu_shift, w0, w_decay, a0, w_iclr, k_k, k_a, r_k,
           ln_w, ln_b, sinks, w_out, final_g):
    Bp, Tp = x_prompt.shape[0], x_prompt.shape[1]
    Bd = x_sample.shape[0]
    W = RWKV_WIDTH

    n_c = Bp + Bd
    c_rows = -(-n_c // 8) * 8
    c_all = jnp.concatenate([c_prompt, c_sample, jnp.zeros((c_rows - n_c, D_MODEL), f32)], axis=0)
    mod = _ada(c_all, w_ada, b_ada)

    tab_p = _rope_tables(jnp.arange(Tp, dtype=jnp.int32))
    tab_s = _rope_tables(jnp.full((Bd,), PAST_LEN, jnp.int32))

    hp = x_prompt
    hs = x_sample.reshape(1, Bd, D_MODEL)
    s0_p = jnp.zeros((Bp, RW_HEADS, RW_HEAD, RW_HEAD), f32)
    outs = {k: [] for k in ("kp", "vp", "sp", "shp", "ks", "vs", "ss", "shs")}
    for l in range(DEPTH):
        final = l == DEPTH - 1
        w_bf = _arrange_w_in(w_in[l])
        w_out_bf = w_out[l].astype(bf16)
        mu_l = mu_shift[l]
        mu = [mu_l[0:W].reshape(1, W), mu_l[W:2 * W].reshape(1, W), mu_l[2 * W:3 * W].reshape(1, W),
              mu_l[3 * W:].reshape(1, LORA_PAIR)]
        vecs = [t[l].reshape(1, W) for t in (w0, a0, k_k, k_a)]
        wd_pad = jnp.concatenate([w_decay[l], jnp.zeros((ICLR_LORA, W), f32)], axis=0)
        wi_pad = jnp.concatenate([jnp.zeros((DECAY_LORA, W), f32), w_iclr[l]], axis=0)
        shift_p, scale_p, gate_p = (mod[l, :Bp, i * D_MODEL:(i + 1) * D_MODEL].reshape(Bp, 1, D_MODEL)
                                    for i in range(3))
        shift_s, scale_s, gate_s = (mod[l, Bp:n_c, i * D_MODEL:(i + 1) * D_MODEL].reshape(1, Bd, D_MODEL)
                                    for i in range(3))

        proj = _norm_proj(hp, norm_g[l], scale_p, shift_p, w_bf, tab_p, tm=1024)
        att = _attn_prompt(proj, sinks[l])
        prep = _rwkv_prep(proj, None, mu, *vecs, wd_pad, wi_pad, tm=256)
        y_rw, s_t = _wkv_chunked(*prep, proj, r_k[l], ln_w[l], ln_b[l], s0_p)
        hp = _post(att, y_rw, None, hp, gate_p, w_out_bf, final_g, final, tm=256)
        tail = proj[:, Tp - WINDOW:]
        outs["kp"].append(tail[..., P_KA:P_KA + KV_WIDTH].reshape(Bp, WINDOW, N_KV_HEADS, HEAD_DIM))
        outs["vp"].append(tail[..., P_VA:P_VA + KV_WIDTH].reshape(Bp, WINDOW, N_KV_HEADS, HEAD_DIM))
        outs["sp"].append(s_t)
        outs["shp"].append(_shift_cols(proj[:, Tp - 1]))

        proj = _norm_proj(hs, norm_g[l], scale_s, shift_s, w_bf, tab_s, tm=Bd)
        rows = proj[0]
        att, nk, nv = _attn_sample(
            rows[:, None, P_Q:P_Q + ATT_WIDTH], rows[:, None, P_KA:P_KA + KV_WIDTH],
            rows[:, None, P_VA:P_VA + KV_WIDTH], rows[:, None, P_GA:P_GA + ATT_WIDTH],
            cache_k[l].reshape(Bd, WINDOW, KV_WIDTH), cache_v[l].reshape(Bd, WINDOW, KV_WIDTH),
            sinks[l])
        sh = state_shift[l]
        prev = [sh[None, :, 0:W], sh[None, :, W:2 * W], sh[None, :, 2 * W:3 * W], sh[None, :, 3 * W:]]
        prep = _rwkv_prep(proj, prev, mu, *vecs, wd_pad, wi_pad, tm=Bd)
        prep = [t.reshape(Bd, 1, W) for t in prep]
        y_rw, s_t = _wkv(*prep, r_k[l], ln_w[l], ln_b[l], state_wkv[l], tc=1)
        hs = _post(att.reshape(1, Bd, ATT_WIDTH), y_rw.reshape(1, Bd, W), proj, hs, gate_s, w_out_bf,
                   final_g, final, tm=Bd)
        outs["ks"].append(nk.reshape(Bd, WINDOW, N_KV_HEADS, HEAD_DIM))
        outs["vs"].append(nv.reshape(Bd, WINDOW, N_KV_HEADS, HEAD_DIM))
        outs["ss"].append(s_t)
        outs["shs"].append(_shift_cols(rows))

    st = lambda k: jnp.stack(outs[k])
    return (hp, hs.reshape(Bd, 1, D_MODEL), st("kp"), st("vp"), st("sp"), st("shp"),
            st("ks"), st("vs"), st("ss"), st("shs"))
```

```python
import functools

import jax
import jax.numpy as jnp
from jax import lax
from jax.experimental import pallas as pl
from jax.experimental.pallas import tpu as pltpu

f32 = jnp.float32
bf16 = jnp.bfloat16

D_MODEL = 2048
DEPTH = 2
PAST_LEN = 16384
ATT_WIDTH = 1024
RWKV_WIDTH = 1024
HEAD_DIM = 64
N_Q_HEADS = 16
N_KV_HEADS = 4
GQA_GROUP = 4
KV_WIDTH = 256
WINDOW = 128
ROT_DIM = 16
ROPE_THETA = 500000.0
RW_HEAD = 64
RW_HEADS = 16
DECAY_LORA = 64
ICLR_LORA = 64
LORA_PAIR = DECAY_LORA + ICLR_LORA
NORM_EPS = 1e-5
GN_EPS = 64e-5
NEG_BIG = -1e30

Q_OFF = 0
KA_OFF = Q_OFF + ATT_WIDTH
VA_OFF = KA_OFF + KV_WIDTH
R_OFF = VA_OFF + KV_WIDTH
KR_OFF = R_OFF + RWKV_WIDTH
VR_OFF = KR_OFF + RWKV_WIDTH
WD_OFF = VR_OFF + RWKV_WIDTH
AD_OFF = WD_OFF + DECAY_LORA
GA_OFF = AD_OFF + ICLR_LORA
GR_OFF = GA_OFF + ATT_WIDTH
IN_WIDTH = GR_OFF + RWKV_WIDTH
SHIFT_DIM = GA_OFF - R_OFF

LANES = 128
P_Q = 0
P_GA = 1024
P_GR = 2048
P_R = 3072
P_KR = 4096
P_VR = 5120
P_KA = 6144
P_VA = 6400
P_WA = 6656
P_WIDTH = 7168
PROJ_TN = 1024

VMEM_LIMIT = 56 * 1024 * 1024


def _silu(x):
    return x * jax.nn.sigmoid(x)


def _cparams(sem):
    return pltpu.CompilerParams(dimension_semantics=sem, vmem_limit_bytes=VMEM_LIMIT)


def _ada_kernel(c_ref, w_ref, b_ref, o_ref):
    c = c_ref[...]
    o_ref[0] = jnp.dot(_silu(c), w_ref[0], precision=lax.Precision.HIGHEST,
                       preferred_element_type=f32) + b_ref[0]


def _ada(c_all, w_ada, b_ada):
    rows = c_all.shape[0]
    tn = 768
    n_out = w_ada.shape[2]
    return pl.pallas_call(
        _ada_kernel,
        grid=(DEPTH, n_out // tn),
        in_specs=[
            pl.BlockSpec((rows, D_MODEL), lambda l, n: (0, 0)),
            pl.BlockSpec((1, D_MODEL, tn), lambda l, n: (l, 0, n)),
            pl.BlockSpec((1, 1, tn), lambda l, n: (l, 0, n)),
        ],
        out_specs=pl.BlockSpec((1, rows, tn), lambda l, n: (l, 0, n)),
        out_shape=jax.ShapeDtypeStruct((DEPTH, rows, n_out), f32),
        compiler_params=_cparams(("parallel", "parallel")),
        name="ada_mod",
    )(c_all, w_ada, b_ada.reshape(DEPTH, 1, n_out))


def _rope(x, tab):
    w = x.shape[1]
    reps = w // LANES
    cosf, up, dn = (jnp.concatenate([tab[i]] * reps, axis=1) for i in range(3))
    half = ROT_DIM // 2
    return x * cosf + pltpu.roll(x, w - half, 1) * up + pltpu.roll(x, half, 1) * dn


def _norm_proj_kernel(x_ref, g_ref, scale_ref, shift_ref, w_ref, tab_ref, o_ref, h_ref):
    n = pl.program_id(2)

    @pl.when(n == 0)
    def _():
        x = x_ref[0]
        ms = jnp.mean(x * x, axis=-1, keepdims=True)
        y = x * lax.rsqrt(ms + NORM_EPS) * g_ref[...]
        h_ref[...] = (y * (1.0 + scale_ref[0]) + shift_ref[0]).astype(bf16)

    q_tile = P_Q // PROJ_TN
    k_tile = P_KA // PROJ_TN
    tm = h_ref.shape[0]
    rc = min(tm, 256)

    def rows(i):
        rs = slice(i * rc, (i + 1) * rc)
        return rs, jnp.dot(h_ref[rs, :], w_ref[...], preferred_element_type=f32)

    @pl.when(n == q_tile)
    def _():
        for i in range(tm // rc):
            rs, res = rows(i)
            o_ref[0, rs, :] = _rope(res, tab_ref[:, rs, :]) * (HEAD_DIM ** -0.5)

    @pl.when(n == k_tile)
    def _():
        for i in range(tm // rc):
            rs, res = rows(i)
            o_ref[0, rs, :] = jnp.concatenate(
                [_rope(res[:, :KV_WIDTH], tab_ref[:, rs, :]), res[:, KV_WIDTH:]], axis=1)

    @pl.when((n != q_tile) & (n != k_tile))
    def _():
        o_ref[0] = jnp.dot(h_ref[...], w_ref[...], preferred_element_type=f32)


def _mod_spec(mod, tm):
    if mod.shape[1] == 1:
        return pl.BlockSpec((1, 1, D_MODEL), lambda g, m, *_: (g, 0, 0))
    return pl.BlockSpec((1, tm, D_MODEL), lambda g, m, *_: (g, m, 0))


def _norm_proj(x, norm_g, scale, shift, w_bf, tab, tm):
    G, R, _ = x.shape
    return pl.pallas_call(
        _norm_proj_kernel,
        grid=(G, R // tm, P_WIDTH // PROJ_TN),
        in_specs=[
            pl.BlockSpec((1, tm, D_MODEL), lambda g, m, n: (g, m, 0)),
            pl.BlockSpec((1, D_MODEL), lambda g, m, n: (0, 0)),
            _mod_spec(scale, tm),
            _mod_spec(shift, tm),
            pl.BlockSpec((D_MODEL, PROJ_TN), lambda g, m, n: (0, n)),
            pl.BlockSpec((3, tm, LANES), lambda g, m, n: (0, m, 0)),
        ],
        out_specs=pl.BlockSpec((1, tm, PROJ_TN), lambda g, m, n: (g, m, n)),
        out_shape=jax.ShapeDtypeStruct((G, R, P_WIDTH), f32),
        scratch_shapes=[pltpu.VMEM((tm, D_MODEL), bf16)],
        compiler_params=_cparams(("parallel", "parallel", "arbitrary")),
        name="norm_proj",
    )(x, norm_g.reshape(1, D_MODEL), scale, shift, w_bf, tab)


def _rope_tables(pos):
    half = ROT_DIM // 2
    inv_freq = ROPE_THETA ** (-jnp.arange(half, dtype=f32) * (2.0 / ROT_DIM))
    ang = pos.astype(f32)[:, None] * inv_freq[None, :]
    cos, sin = jnp.cos(ang), jnp.sin(ang)
    t = pos.shape[0]
    z8 = jnp.zeros((t, half), f32)
    rest = HEAD_DIM - ROT_DIM
    cos64 = jnp.concatenate([cos, cos, jnp.ones((t, rest), f32)], axis=1)
    up64 = jnp.concatenate([-sin, z8, jnp.zeros((t, rest), f32)], axis=1)
    dn64 = jnp.concatenate([z8, sin, jnp.zeros((t, rest), f32)], axis=1)
    rep = LANES // HEAD_DIM
    return jnp.stack([jnp.tile(a, (1, rep)) for a in (cos64, up64, dn64)])


def _attn_prompt_kernel(sinks_ref, q_ref, kc_ref, kp_ref, vc_ref, vp_ref, ga_ref, o_ref):
    n = pl.program_id(1)
    wn = WINDOW
    half = HEAD_DIM
    q = q_ref[0].astype(bf16)
    k_t = jnp.concatenate([kp_ref[0], kc_ref[0]], axis=0).T.astype(bf16)
    vcat = jnp.concatenate([vp_ref[0], vc_ref[0]], axis=0)
    ga = ga_ref[0]

    qi = lax.broadcasted_iota(jnp.int32, (2 * wn, 2 * wn), 0) & (wn - 1)
    kj = lax.broadcasted_iota(jnp.int32, (2 * wn, 2 * wn), 1)
    rel = wn + qi - kj
    mask = (rel >= 0) & (rel <= wn) & ((kj >= wn) | (n > 0))
    top = lax.broadcasted_iota(jnp.int32, (2 * wn, 1), 0) < wn
    lo = lax.broadcasted_iota(jnp.int32, (2 * wn, LANES), 1) < half
    zeros_k = jnp.zeros((half, 2 * wn), bf16)
    ones_lo = jnp.where(lo, 1.0, 0.0).astype(bf16)
    ones_hi = jnp.where(lo, 0.0, 1.0).astype(bf16)

    for j in range(N_KV_HEADS // 2):
        vblk = vcat[:, j * LANES:(j + 1) * LANES]
        vswap = pltpu.roll(vblk, half, 1)
        for g in (2 * j, 2 * j + 1):
            own, other = (vblk, vswap) if g % 2 == 0 else (vswap, vblk)
            v_lo = jnp.where(lo, own, 0.0).astype(bf16)
            v_hi = jnp.where(lo, 0.0, other).astype(bf16)
            rhs_pv = jnp.concatenate([jnp.concatenate([v_lo, ones_lo], axis=1),
                                      jnp.concatenate([v_hi, ones_hi], axis=1)], axis=0)
            kg = k_t[g * half:(g + 1) * half, :]
            rhs_qk = jnp.concatenate([jnp.concatenate([kg, zeros_k], axis=0),
                                      jnp.concatenate([zeros_k, kg], axis=0)], axis=1)
            b0, b1 = 2 * g, 2 * g + 1
            qg = jnp.concatenate([q[:, b0 * LANES:(b0 + 1) * LANES], q[:, b1 * LANES:(b1 + 1) * LANES]],
                                 axis=0)
            s_all = jnp.dot(qg, rhs_qk, preferred_element_type=f32)
            ps, es = [], []
            for hh in range(2):
                s = jnp.where(mask, s_all[:, hh * 2 * wn:(hh + 1) * 2 * wn], NEG_BIG)
                sink = jnp.where(top, sinks_ref[2 * b0 + hh], sinks_ref[2 * b1 + hh])
                m = jnp.maximum(jnp.max(s, axis=-1, keepdims=True), sink)
                ps.append(jnp.exp(s - m).astype(bf16))
                es.append(jnp.exp(sink - m))
            res = jnp.dot(jnp.concatenate(ps, axis=1), rhs_pv, preferred_element_type=f32)
            out = res[:, :LANES] / (res[:, LANES:] + jnp.where(lo, es[0], es[1]))
            for i, blk in enumerate((b0, b1)):
                sl = slice(blk * LANES, (blk + 1) * LANES)
                o_ref[0, :, sl] = (out[i * wn:(i + 1) * wn] * _silu(ga[:, sl])).astype(o_ref.dtype)


def _attn_prompt(proj, sinks):
    B, T, _ = proj.shape
    nb = T // WINDOW
    kvb = KV_WIDTH
    prev = lambda b, n: jnp.maximum(n - 1, 0)
    return pl.pallas_call(
        _attn_prompt_kernel,
        grid=(B, nb),
        in_specs=[
            pl.BlockSpec(memory_space=pltpu.SMEM),
            pl.BlockSpec((1, WINDOW, ATT_WIDTH), lambda b, n: (b, n, P_Q // ATT_WIDTH)),
            pl.BlockSpec((1, WINDOW, kvb), lambda b, n: (b, n, P_KA // kvb)),
            pl.BlockSpec((1, WINDOW, kvb), lambda b, n: (b, prev(b, n), P_KA // kvb)),
            pl.BlockSpec((1, WINDOW, kvb), lambda b, n: (b, n, P_VA // kvb)),
            pl.BlockSpec((1, WINDOW, kvb), lambda b, n: (b, prev(b, n), P_VA // kvb)),
            pl.BlockSpec((1, WINDOW, ATT_WIDTH), lambda b, n: (b, n, P_GA // ATT_WIDTH)),
        ],
        out_specs=pl.BlockSpec((1, WINDOW, ATT_WIDTH), lambda b, n: (b, n, 0)),
        out_shape=jax.ShapeDtypeStruct((B, T, ATT_WIDTH), bf16),
        compiler_params=_cparams(("parallel", "arbitrary")),
        name="attn_prompt",
    )(sinks, proj, proj, proj, proj, proj, proj)


def _attn_sample_kernel(sinks_ref, q_ref, kn_ref, vn_ref, ga_ref, ck_ref, cv_ref,
                        o_ref, nk_ref, nv_ref):
    q = q_ref[0]
    kn = kn_ref[0]
    vn = vn_ref[0]
    ck = ck_ref[0]
    cv = cv_ref[0]
    ckb = ck.astype(bf16)
    cvb = cv.astype(bf16)
    outs = []
    for g in range(N_KV_HEADS):
        lo, hi = g * HEAD_DIM, (g + 1) * HEAD_DIM
        qg = jnp.concatenate(
            [q[:, (g * GQA_GROUP + i) * HEAD_DIM:(g * GQA_GROUP + i + 1) * HEAD_DIM]
             for i in range(GQA_GROUP)], axis=0)
        s = lax.dot_general(qg.astype(bf16), ckb[:, lo:hi], (((1,), (1,)), ((), ())),
                            preferred_element_type=f32)
        s_new = jnp.sum(qg * kn[:, lo:hi], axis=-1, keepdims=True)
        sink = sinks_ref[g * GQA_GROUP:(g + 1) * GQA_GROUP, :]
        m = jnp.maximum(jnp.maximum(jnp.max(s, axis=-1, keepdims=True), s_new), sink)
        p = jnp.exp(s - m)
        p_new = jnp.exp(s_new - m)
        den = jnp.sum(p, axis=-1, keepdims=True) + p_new + jnp.exp(sink - m)
        o = jnp.dot(p.astype(bf16), cvb[:, lo:hi], preferred_element_type=f32) + p_new * vn[:, lo:hi]
        o = o / den
        outs.extend(o[i:i + 1, :] for i in range(GQA_GROUP))
    ga = ga_ref[0]
    o_ref[0] = (jnp.concatenate(outs, axis=1) * _silu(ga)).astype(o_ref.dtype)
    row = lax.broadcasted_iota(jnp.int32, (WINDOW, KV_WIDTH), 0)
    last = row == WINDOW - 1
    nk_ref[0] = jnp.where(last, kn, pltpu.roll(ck, WINDOW - 1, 0))
    nv_ref[0] = jnp.where(last, vn, pltpu.roll(cv, WINDOW - 1, 0))


def _attn_sample(q, kn, vn, ga, ck, cv, sinks):
    Bd = q.shape[0]
    row = lambda w: pl.BlockSpec((1, 1, w), lambda b: (b, 0, 0))
    cache = pl.BlockSpec((1, WINDOW, KV_WIDTH), lambda b: (b, 0, 0))
    return pl.pallas_call(
        _attn_sample_kernel,
        grid=(Bd,),
        in_specs=[
            pl.BlockSpec((N_Q_HEADS, 1), lambda b: (0, 0)),
            row(ATT_WIDTH), row(KV_WIDTH), row(KV_WIDTH), row(ATT_WIDTH), cache, cache,
        ],
        out_specs=[row(ATT_WIDTH), cache, cache],
        out_shape=[
            jax.ShapeDtypeStruct((Bd, 1, ATT_WIDTH), bf16),
            jax.ShapeDtypeStruct((Bd, WINDOW, KV_WIDTH), f32),
            jax.ShapeDtypeStruct((Bd, WINDOW, KV_WIDTH), f32),
        ],
        compiler_params=_cparams(("parallel",)),
        name="attn_sample",
    )(sinks.reshape(N_Q_HEADS, 1), q, kn, vn, ga, ck, cv)


def _rwkv_prep_kernel(r_ref, kr_ref, vr_ref, wa_ref, pr_ref, pkr_ref, pvr_ref, pwa_ref,
                      mu_r_ref, mu_kr_ref, mu_vr_ref, mu_wa_ref, w0_ref, a0_ref, kk_ref, ka_ref,
                      wd_ref, wi_ref, ro_ref, wo_ref, ko_ref, vo_ref, kko_ref, ao_ref):
    def mixed(cur_ref, prev_ref, mu_ref):
        cur = cur_ref[0]
        return cur + (prev_ref[0] - cur) * mu_ref[...]

    r = mixed(r_ref, pr_ref, mu_r_ref)
    kr = mixed(kr_ref, pkr_ref, mu_kr_ref)
    vr = mixed(vr_ref, pvr_ref, mu_vr_ref)
    wa = mixed(wa_ref, pwa_ref, mu_wa_ref)
    hi = lax.Precision.HIGHEST
    z = w0_ref[...] + jnp.dot(jnp.tanh(wa), wd_ref[...], precision=hi, preferred_element_type=f32)
    nz = -z
    softplus = jnp.maximum(nz, 0.0) + jnp.log1p(jnp.exp(-jnp.abs(nz)))
    w_log = -softplus - 0.5
    a = jax.nn.sigmoid(a0_ref[...] + jnp.dot(wa, wi_ref[...], precision=hi, preferred_element_type=f32))
    ro_ref[0] = r
    wo_ref[0] = -jnp.exp(w_log)
    ko_ref[0] = kr * (1.0 + (a - 1.0) * ka_ref[...])
    vo_ref[0] = vr
    kko_ref[0] = kr * kk_ref[...]
    ao_ref[0] = a


def _rwkv_prep(proj, prev, mu, w0, a0, k_k, k_a, wd_pad, wi_pad, tm):
    G, R, _ = proj.shape
    W = RWKV_WIDTH
    cur_specs = [
        pl.BlockSpec((1, tm, W), lambda g, m: (g, m, P_R // W)),
        pl.BlockSpec((1, tm, W), lambda g, m: (g, m, P_KR // W)),
        pl.BlockSpec((1, tm, W), lambda g, m: (g, m, P_VR // W)),
        pl.BlockSpec((1, tm, LANES), lambda g, m: (g, m, P_WA // LANES)),
    ]
    prev_specs = [
        pl.BlockSpec((1, tm, W), lambda g, m: (g, m, 0)),
        pl.BlockSpec((1, tm, W), lambda g, m: (g, m, 0)),
        pl.BlockSpec((1, tm, W), lambda g, m: (g, m, 0)),
        pl.BlockSpec((1, tm, LANES), lambda g, m: (g, m, 0)),
    ]
    prev_args = list(prev)
    vec = lambda w: pl.BlockSpec((1, w), lambda g, m: (0, 0))
    lora = pl.BlockSpec((LORA_PAIR, W), lambda g, m: (0, 0))
    out_spec = pl.BlockSpec((1, tm, W), lambda g, m: (g, m, 0))
    return pl.pallas_call(
        _rwkv_prep_kernel,
        grid=(G, R // tm),
        in_specs=cur_specs + prev_specs + [vec(W), vec(W), vec(W), vec(LANES),
                                           vec(W), vec(W), vec(W), vec(W), lora, lora],
        out_specs=[out_spec] * 6,
        out_shape=[jax.ShapeDtypeStruct((G, R, W), f32)] * 6,
        compiler_params=_cparams(("parallel", "arbitrary")),
        name="rwkv_prep",
    )(proj, proj, proj, proj, *prev_args, *mu, w0, a0, k_k, k_a, wd_pad, wi_pad)


def _wkv_kernel(tc, r_ref, w_ref, k_ref, v_ref, kk_ref, a_ref, rk_ref, lnw_ref, lnb_ref, s0_ref,
                y_ref, s_ref):
    @pl.when(pl.program_id(1) == 0)
    def _():
        s_ref[...] = s0_ref[...]

    eye = (lax.broadcasted_iota(jnp.int32, (RW_HEAD, RW_HEAD), 0)
           == lax.broadcasted_iota(jnp.int32, (RW_HEAD, RW_HEAD), 1)).astype(f32)
    rk = rk_ref[...]
    lnw = lnw_ref[...]
    lnb = lnb_ref[...]

    def step(t, carry):
        r, lw, k, v, kkr, a = (ref[0, t] for ref in (r_ref, w_ref, k_ref, v_ref, kk_ref, a_ref))
        w = jnp.exp(lw)
        norm = jnp.sqrt(jnp.sum(kkr * kkr, axis=-1, keepdims=True))
        kk = kkr / jnp.maximum(norm, 1e-12)
        b = kk * a
        bonus = jnp.sum(r * k * rk, axis=-1, keepdims=True) * v
        rows = []
        for h in range(RW_HEADS):
            hs = slice(h, h + 1)
            S = s_ref[0, h]
            sa = jnp.sum(S * (-kk[hs]), axis=-1, keepdims=True)
            v_col = jnp.sum(eye * v[hs], axis=-1, keepdims=True)
            S = S * w[hs] + sa * b[hs] + v_col * k[hs]
            s_ref[0, h] = S
            y = jnp.sum(S * r[hs], axis=-1, keepdims=True)
            mu = jnp.mean(y, axis=0, keepdims=True)
            var = jnp.mean(jnp.square(y - mu), axis=0, keepdims=True)
            yn = (y - mu) * lax.rsqrt(var + GN_EPS)
            rows.append(jnp.sum(yn * eye, axis=0, keepdims=True))
        y_ref[0, t] = jnp.concatenate(rows, axis=0) * lnw + lnb + bonus
        return carry

    lax.fori_loop(0, tc, step, 0)


def _wkv(r, w, k, v, kk, a, r_k, ln_w, ln_b, s0, tc):
    B, T = r.shape[0], r.shape[1]
    hd = (RW_HEADS, RW_HEAD)
    heads = lambda t: t.reshape(B, T, *hd)
    seq = pl.BlockSpec((1, tc, *hd), lambda b, c: (b, c, 0, 0))
    par = pl.BlockSpec(hd, lambda b, c: (0, 0))
    state = pl.BlockSpec((1, RW_HEADS, RW_HEAD, RW_HEAD), lambda b, c: (b, 0, 0, 0))
    y, s_t = pl.pallas_call(
        functools.partial(_wkv_kernel, tc),
        grid=(B, T // tc),
        in_specs=[seq] * 6 + [par] * 3 + [state],
        out_specs=[seq, state],
        out_shape=[
            jax.ShapeDtypeStruct((B, T, *hd), f32),
            jax.ShapeDtypeStruct((B, RW_HEADS, RW_HEAD, RW_HEAD), f32),
        ],
        compiler_params=_cparams(("parallel", "arbitrary")),
        name="wkv_steps",
    )(heads(r), heads(w), heads(k), heads(v), heads(kk), heads(a),
      r_k, ln_w.reshape(hd), ln_b.reshape(hd), s0)
    return y.reshape(B, T, RWKV_WIDTH), s_t


CHUNK = 64
PAIR = 2 * RW_HEAD
N_PAIRS = RW_HEADS // 2
_NN = (((1,), (0,)), ((), ()))
_NT = (((1,), (1,)), ((), ()))
_TN = (((0,), (0,)), ((), ()))


def _mmb(a, b, dims=_NN):
    return lax.dot_general(a.astype(bf16), b.astype(bf16), dims, preferred_element_type=f32)


def _split_bf16(x):
    hi = x.astype(bf16)
    return hi, (x - hi.astype(f32)).astype(bf16)


def _lora_dot(x, wh_ref, wl_ref):
    xh, xl = _split_bf16(x)
    rows = x.shape[0]
    both = jnp.dot(jnp.concatenate([xh, xl], axis=0), wh_ref[...], preferred_element_type=f32)
    return both[:rows] + both[rows:] + jnp.dot(xh, wl_ref[...], preferred_element_type=f32)


def _wkv_chunk_kernel(r_ref, kr_ref, vr_ref, wa_ref, gr_ref, pr_ref, pkr_ref, pvr_ref, pwa_ref,
                      mu_r_ref, mu_kr_ref, mu_vr_ref, mu_wa_ref, w0_ref, a0_ref, kk_ref, ka_ref,
                      wdh_ref, wdl_ref, wih_ref, wil_ref, rk_ref, lnw_ref, lnb_ref, s0_ref,
                      y_ref, s_ref, sp_ref, cr_ref, ckr_ref, cvr_ref, cwa_ref):
    C = CHUNK
    c = pl.program_id(1)

    @pl.when(c == 0)
    def _():
        for p in range(N_PAIRS):
            sp_ref[p] = jnp.concatenate([s0_ref[0, 2 * p], s0_ref[0, 2 * p + 1]], axis=1)
        for carry, first in ((cr_ref, pr_ref), (ckr_ref, pkr_ref), (cvr_ref, pvr_ref), (cwa_ref, pwa_ref)):
            carry[...] = first[0]

    def mixed(cur_ref, carry_ref, mu_ref):
        cur = cur_ref[0]
        first = lax.broadcasted_iota(jnp.int32, cur.shape, 0) == 0
        prev = jnp.where(first, carry_ref[...], pltpu.roll(cur, 1, 0))
        carry_ref[...] = cur[C - 1:C, :]
        return cur + (prev - cur) * mu_ref[...]

    r_all = mixed(r_ref, cr_ref, mu_r_ref)
    kr_all = mixed(kr_ref, ckr_ref, mu_kr_ref)
    v_all = mixed(vr_ref, cvr_ref, mu_vr_ref)
    wa = mixed(wa_ref, cwa_ref, mu_wa_ref)
    nz = -(w0_ref[...] + _lora_dot(jnp.tanh(wa), wdh_ref, wdl_ref))
    softplus = jnp.maximum(nz, 0.0) + jnp.log1p(jnp.exp(-jnp.abs(nz)))
    lw = -jnp.exp(-softplus - 0.5)
    icl_all = jax.nn.sigmoid(a0_ref[...] + _lora_dot(wa, wih_ref, wil_ref))
    k_all = kr_all * (1.0 + (icl_all - 1.0) * ka_ref[...])
    kkr_all = kr_all * kk_ref[...]

    row = lax.broadcasted_iota(jnp.int32, (PAIR, PAIR), 0)
    col = lax.broadcasted_iota(jnp.int32, (PAIR, PAIR), 1)
    tril = row >= col
    stril = row > col
    eye = (row == col).astype(f32)
    lane_lo = lax.broadcasted_iota(jnp.int32, (C, PAIR), 1) < RW_HEAD

    def bd(x):
        return jnp.concatenate([jnp.where(lane_lo, x, 0.0), jnp.where(lane_lo, 0.0, x)], axis=0)

    def head_sums(x):
        lo_sum = jnp.sum(jnp.where(lane_lo, x, 0.0), axis=-1, keepdims=True)
        hi_sum = jnp.sum(jnp.where(lane_lo, 0.0, x), axis=-1, keepdims=True)
        return jnp.where(lane_lo, lo_sum, hi_sum)

    width = lw.shape[1]
    lw_a = lw.astype(bf16)
    rest = lw - lw_a.astype(f32)
    lw_b = rest.astype(bf16)
    lw_c = (rest - lw_b.astype(f32)).astype(bf16)
    g3 = jnp.dot(tril[:C, :C].astype(bf16), jnp.concatenate([lw_a, lw_b, lw_c], axis=1),
                 preferred_element_type=f32)
    g = g3[:, :width] + g3[:, width:2 * width] + g3[:, 2 * width:]
    e_g = jnp.exp(g)
    e_ng = jnp.exp(-g)
    e_gm = jnp.exp(g - lw)
    e_hat = jnp.exp(g[C - 1:C, :] - g)
    e_end = e_g[C - 1:C, :]

    pairs = range(N_PAIRS)
    sls = [slice(p * PAIR, (p + 1) * PAIR) for p in pairs]
    ins = [[t[:, sl] for t in (r_all, k_all, v_all, kkr_all, icl_all)] for sl in sls]
    norms = [jnp.sqrt(head_sums(x[3] * x[3])) for x in ins]
    at, rt, bt, kt, bh, kh, vb = ([] for _ in range(7))
    for (r, k, v, kkr, icl), norm, sl in zip(ins, norms, sls):
        kk = kkr / jnp.maximum(norm, 1e-12)
        b = kk * icl
        at.append(bd(-kk * e_gm[:, sl]).astype(bf16))
        rt.append(bd(r * e_g[:, sl]).astype(bf16))
        bt.append(bd(b * e_ng[:, sl]).astype(bf16))
        kt.append(bd(k * e_ng[:, sl]).astype(bf16))
        bh.append(bd(b * e_hat[:, sl]).astype(bf16))
        kh.append(bd(k * e_hat[:, sl]).astype(bf16))
        vb.append(bd(v).astype(bf16))

    tril2 = jnp.concatenate([tril, tril], axis=1)
    gram = [_mmb(jnp.concatenate([at[p], rt[p]], axis=0), jnp.concatenate([bt[p], kt[p]], axis=0), _NT)
            for p in pairs]
    lmat = [jnp.where(stril, gm[:PAIR, :PAIR], 0.0) for gm in gram]
    mv = [_mmb(jnp.where(stril, gram[p][:PAIR, PAIR:], 0.0), vb[p]) for p in pairs]
    lower = [jnp.where(tril2, gm[PAIR:, :], 0.0).astype(bf16) for gm in gram]

    tinv = [eye + lm for lm in lmat]
    pw = [_mmb(lm, lm) for lm in lmat]
    for _ in range(4):
        z = [_mmb(jnp.concatenate([x.astype(bf16), t.astype(bf16)], axis=0), x) for t, x in zip(tinv, pw)]
        pw = [zz[:PAIR] for zz in z]
        tinv = [t + zz[PAIR:] for t, zz in zip(tinv, z)]
    tinv = [t + _mmb(t, x) for t, x in zip(tinv, pw)]

    wx = [_mmb(tinv[p], jnp.concatenate([at[p], mv[p].astype(bf16)], axis=1)) for p in pairs]
    s_old = [sp_ref[p] for p in pairs]
    uy0 = [_mmb(jnp.concatenate([wx[p][:, :PAIR].astype(bf16), rt[p]], axis=0), bd(s_old[p]), _NT)
           for p in pairs]
    uv = [jnp.concatenate([(uy0[p][:PAIR] + wx[p][:, PAIR:]).astype(bf16), vb[p]], axis=0) for p in pairs]
    ys = [uy0[p][PAIR:] + _mmb(lower[p], uv[p]) for p in pairs]
    s_add = [_mmb(uv[p], jnp.concatenate([bh[p], kh[p]], axis=0), _TN) for p in pairs]
    for p in pairs:
        sp_ref[p] = s_old[p] * e_end[:, sls[p]] + s_add[p][:RW_HEAD] + s_add[p][RW_HEAD:]

    ys = [y[:C] + y[C:] for y in ys]
    mus = [head_sums(y) * (1.0 / RW_HEAD) for y in ys]
    ds = [y - mu for y, mu in zip(ys, mus)]
    var = [head_sums(d * d) * (1.0 / RW_HEAD) for d in ds]
    bonus = [head_sums(x[0] * x[1] * rk_ref[:, sl]) * x[2] for x, sl in zip(ins, sls)]
    for p in pairs:
        sl = sls[p]
        y_rw = ds[p] * lax.rsqrt(var[p] + GN_EPS) * lnw_ref[:, sl] + lnb_ref[:, sl] + bonus[p]
        y_ref[0, :, sl] = (y_rw * _silu(gr_ref[0, :, sl])).astype(y_ref.dtype)

    @pl.when(c == pl.num_programs(1) - 1)
    def _():
        for p in range(N_PAIRS):
            s = sp_ref[p]
            s_ref[0, 2 * p] = s[:, :RW_HEAD]
            s_ref[0, 2 * p + 1] = s[:, RW_HEAD:]


def _wkv_chunked(proj, prev, mu, w0, a0, k_k, k_a, lora, r_k, ln_w, ln_b, s0):
    B, T, _ = proj.shape
    W = RWKV_WIDTH
    col = lambda w, off: pl.BlockSpec((1, CHUNK, w), lambda b, c: (b, c, off // w))
    first = lambda w: pl.BlockSpec((1, 1, w), lambda b, c: (b, 0, 0))
    vec = lambda w: pl.BlockSpec((1, w), lambda b, c: (0, 0))
    lora_spec = pl.BlockSpec((LORA_PAIR, W), lambda b, c: (0, 0))
    state = pl.BlockSpec((1, RW_HEADS, RW_HEAD, RW_HEAD), lambda b, c: (b, 0, 0, 0))
    return pl.pallas_call(
        _wkv_chunk_kernel,
        grid=(B, T // CHUNK),
        in_specs=[col(W, P_R), col(W, P_KR), col(W, P_VR), col(LANES, P_WA), col(W, P_GR),
                  first(W), first(W), first(W), first(LANES),
                  vec(W), vec(W), vec(W), vec(LANES), vec(W), vec(W), vec(W), vec(W),
                  lora_spec, lora_spec, lora_spec, lora_spec, vec(W), vec(W), vec(W), state],
        out_specs=[pl.BlockSpec((1, CHUNK, W), lambda b, c: (b, c, 0)), state],
        out_shape=[
            jax.ShapeDtypeStruct((B, T, W), bf16),
            jax.ShapeDtypeStruct((B, RW_HEADS, RW_HEAD, RW_HEAD), f32),
        ],
        scratch_shapes=[pltpu.VMEM((N_PAIRS, RW_HEAD, PAIR), f32),
                        pltpu.VMEM((1, W), f32), pltpu.VMEM((1, W), f32), pltpu.VMEM((1, W), f32),
                        pltpu.VMEM((1, LANES), f32)],
        compiler_params=_cparams(("parallel", "arbitrary")),
        name="wkv_chunks",
    )(proj, proj, proj, proj, proj, *prev, *mu, w0, a0, k_k, k_a, *lora,
      r_k.reshape(1, W), ln_w.reshape(1, W), ln_b.reshape(1, W), s0)


def _post_kernel(final, gated, att_ref, y_ref, *refs):
    if gated:
        y = y_ref[0]
    else:
        y = (y_ref[0] * _silu(refs[0][0])).astype(bf16)
        refs = refs[1:]
    x_ref, gate_ref, w_ref, fg_ref, o_ref = refs
    cat = jnp.concatenate([att_ref[0], y], axis=1)
    out = jnp.dot(cat, w_ref[...], preferred_element_type=f32)
    x = x_ref[0] + gate_ref[0] * out
    if final:
        ms = jnp.mean(x * x, axis=-1, keepdims=True)
        x = x * lax.rsqrt(ms + NORM_EPS) * fg_ref[...]
    o_ref[0] = x


def _post(att, y_rw, proj, x, gate, w_out_bf, final_g, final, tm):
    G, R, _ = x.shape
    W = RWKV_WIDTH
    gated = proj is None
    gate_in = [] if gated else [pl.BlockSpec((1, tm, W), lambda g, m: (g, m, P_GR // W))]
    gate_arg = [] if gated else [proj]
    return pl.pallas_call(
        functools.partial(_post_kernel, final, gated),
        grid=(G, R // tm),
        in_specs=[
            pl.BlockSpec((1, tm, ATT_WIDTH), lambda g, m: (g, m, 0)),
            pl.BlockSpec((1, tm, W), lambda g, m: (g, m, 0)),
            *gate_in,
            pl.BlockSpec((1, tm, D_MODEL), lambda g, m: (g, m, 0)),
            _mod_spec(gate, tm),
            pl.BlockSpec((D_MODEL, D_MODEL), lambda g, m: (0, 0)),
            pl.BlockSpec((1, D_MODEL), lambda g, m: (0, 0)),
        ],
        out_specs=pl.BlockSpec((1, tm, D_MODEL), lambda g, m: (g, m, 0)),
        out_shape=jax.ShapeDtypeStruct((G, R, D_MODEL), f32),
        compiler_params=_cparams(("parallel", "parallel")),
        name="post_proj",
    )(att, y_rw, *gate_arg, x, gate, w_out_bf, final_g.reshape(1, D_MODEL))


def _arrange_w_in(w):
    pad = jnp.zeros((w.shape[0], P_WIDTH - IN_WIDTH), w.dtype)
    parts = [w[:, Q_OFF:KA_OFF], w[:, GA_OFF:GR_OFF], w[:, GR_OFF:IN_WIDTH], w[:, R_OFF:WD_OFF],
             w[:, KA_OFF:R_OFF], w[:, WD_OFF:GA_OFF], pad]
    return jnp.concatenate(parts, axis=1).astype(bf16)


def _shift_cols(t):
    return jnp.concatenate([t[..., P_R:P_KA], t[..., P_WA:P_WA + LORA_PAIR]], axis=-1)


def kernel(x_prompt, x_sample, cache_k, cache_v, state_wkv, state_shift, c_prompt, c_sample,
           norm_g, w_ada, b_ada, w_in, mu_shift, w0, w_decay, a0, w_iclr, k_k, k_a, r_k,
           ln_w, ln_b, sinks, w_out, final_g):
    Bp, Tp = x_prompt.shape[0], x_prompt.shape[1]
    Bd = x_sample.shape[0]
    W = RWKV_WIDTH

    n_c = Bp + Bd
    c_rows = -(-n_c // 8) * 8
    c_all = jnp.concatenate([c_prompt, c_sample, jnp.zeros((c_rows - n_c, D_MODEL), f32)], axis=0)
    mod = _ada(c_all, w_ada, b_ada)

    tab_p = _rope_tables(jnp.arange(Tp, dtype=jnp.int32))
    tab_s = _rope_tables(jnp.full((Bd,), PAST_LEN, jnp.int32))

    hp = x_prompt
    hs = x_sample.reshape(1, Bd, D_MODEL)
    s0_p = jnp.zeros((Bp, RW_HEADS, RW_HEAD, RW_HEAD), f32)
    shift0_p = [jnp.zeros((Bp, 1, w), f32) for w in (W, W, W, LORA_PAIR)]
    outs = {k: [] for k in ("kp", "vp", "sp", "shp", "ks", "vs", "ss", "shs")}
    for l in range(DEPTH):
        final = l == DEPTH - 1
        w_bf = _arrange_w_in(w_in[l])
        w_out_bf = w_out[l].astype(bf16)
        mu_l = mu_shift[l]
        mu = [mu_l[0:W].reshape(1, W), mu_l[W:2 * W].reshape(1, W), mu_l[2 * W:3 * W].reshape(1, W),
              mu_l[3 * W:].reshape(1, LORA_PAIR)]
        vecs = [t[l].reshape(1, W) for t in (w0, a0, k_k, k_a)]
        wd_pad = jnp.concatenate([w_decay[l], jnp.zeros((ICLR_LORA, W), f32)], axis=0)
        wi_pad = jnp.concatenate([jnp.zeros((DECAY_LORA, W), f32), w_iclr[l]], axis=0)
        shift_p, scale_p, gate_p = (mod[l, :Bp, i * D_MODEL:(i + 1) * D_MODEL].reshape(Bp, 1, D_MODEL)
                                    for i in range(3))
        shift_s, scale_s, gate_s = (mod[l, Bp:n_c, i * D_MODEL:(i + 1) * D_MODEL].reshape(1, Bd, D_MODEL)
                                    for i in range(3))

        proj = _norm_proj(hp, norm_g[l], scale_p, shift_p, w_bf, tab_p, tm=1024)
        att = _attn_prompt(proj, sinks[l])
        lora = [piece for wp in (wd_pad, wi_pad) for piece in _split_bf16(wp)]
        y_rw, s_t = _wkv_chunked(proj, shift0_p, mu, *vecs, lora, r_k[l], ln_w[l], ln_b[l], s0_p)
        hp = _post(att, y_rw, None, hp, gate_p, w_out_bf, final_g, final, tm=256)
        tail = proj[:, Tp - WINDOW:]
        outs["kp"].append(tail[..., P_KA:P_KA + KV_WIDTH].reshape(Bp, WINDOW, N_KV_HEADS, HEAD_DIM))
        outs["vp"].append(tail[..., P_VA:P_VA + KV_WIDTH].reshape(Bp, WINDOW, N_KV_HEADS, HEAD_DIM))
        outs["sp"].append(s_t)
        outs["shp"].append(_shift_cols(proj[:, Tp - 1]))

        proj = _norm_proj(hs, norm_g[l], scale_s, shift_s, w_bf, tab_s, tm=Bd)
        rows = proj[0]
        att, nk, nv = _attn_sample(
            rows[:, None, P_Q:P_Q + ATT_WIDTH], rows[:, None, P_KA:P_KA + KV_WIDTH],
            rows[:, None, P_VA:P_VA + KV_WIDTH], rows[:, None, P_GA:P_GA + ATT_WIDTH],
            cache_k[l].reshape(Bd, WINDOW, KV_WIDTH), cache_v[l].reshape(Bd, WINDOW, KV_WIDTH),
            sinks[l])
        sh = state_shift[l]
        prev = [sh[None, :, 0:W], sh[None, :, W:2 * W], sh[None, :, 2 * W:3 * W], sh[None, :, 3 * W:]]
        prep = _rwkv_prep(proj, prev, mu, *vecs, wd_pad, wi_pad, tm=Bd)
        prep = [t.reshape(Bd, 1, W) for t in prep]
        y_rw, s_t = _wkv(*prep, r_k[l], ln_w[l], ln_b[l], state_wkv[l], tc=1)
        hs = _post(att.reshape(1, Bd, ATT_WIDTH), y_rw.reshape(1, Bd, W), proj, hs, gate_s, w_out_bf,
                   final_g, final, tm=Bd)
        outs["ks"].append(nk.reshape(Bd, WINDOW, N_KV_HEADS, HEAD_DIM))
        outs["vs"].append(nv.reshape(Bd, WINDOW, N_KV_HEADS, HEAD_DIM))
        outs["ss"].append(s_t)
        outs["shs"].append(_shift_cols(rows))

    st = lambda k: jnp.stack(outs[k])
    return (hp, hs.reshape(Bd, 1, D_MODEL), st("kp"), st("vp"), st("sp"), st("shp"),
            st("ks"), st("vs"), st("ss"), st("shs"))
```

```python
import functools

import jax
import jax.numpy as jnp
from jax import lax
from jax.experimental import pallas as pl
from jax.experimental.pallas import tpu as pltpu

f32 = jnp.float32
bf16 = jnp.bfloat16

D_MODEL = 2048
DEPTH = 2
PAST_LEN = 16384
ATT_WIDTH = 1024
RWKV_WIDTH = 1024
HEAD_DIM = 64
N_Q_HEADS = 16
N_KV_HEADS = 4
GQA_GROUP = 4
KV_WIDTH = 256
WINDOW = 128
ROT_DIM = 16
ROPE_THETA = 500000.0
RW_HEAD = 64
RW_HEADS = 16
DECAY_LORA = 64
ICLR_LORA = 64
LORA_PAIR = DECAY_LORA + ICLR_LORA
NORM_EPS = 1e-5
GN_EPS = 64e-5
NEG_BIG = -1e30

Q_OFF = 0
KA_OFF = Q_OFF + ATT_WIDTH
VA_OFF = KA_OFF + KV_WIDTH
R_OFF = VA_OFF + KV_WIDTH
KR_OFF = R_OFF + RWKV_WIDTH
VR_OFF = KR_OFF + RWKV_WIDTH
WD_OFF = VR_OFF + RWKV_WIDTH
AD_OFF = WD_OFF + DECAY_LORA
GA_OFF = AD_OFF + ICLR_LORA
GR_OFF = GA_OFF + ATT_WIDTH
IN_WIDTH = GR_OFF + RWKV_WIDTH
SHIFT_DIM = GA_OFF - R_OFF

LANES = 128
P_Q = 0
P_GA = 1024
P_GR = 2048
P_R = 3072
P_KR = 4096
P_VR = 5120
P_KA = 6144
P_VA = 6400
P_WA = 6656
P_WIDTH = 7168
PROJ_TN = 1024

VMEM_LIMIT = 56 * 1024 * 1024


_NN = (((1,), (0,)), ((), ()))
_NT = (((1,), (1,)), ((), ()))
_TN = (((0,), (0,)), ((), ()))


def _silu(x):
    return x * jax.nn.sigmoid(x)


def _cparams(sem):
    return pltpu.CompilerParams(dimension_semantics=sem, vmem_limit_bytes=VMEM_LIMIT)


def _ada_kernel(c_ref, w_ref, b_ref, o_ref):
    c = c_ref[...]
    o_ref[0] = jnp.dot(_silu(c), w_ref[0], precision=lax.Precision.HIGHEST,
                       preferred_element_type=f32) + b_ref[0]


def _ada(c_all, w_ada, b_ada):
    rows = c_all.shape[0]
    tn = 768
    n_out = w_ada.shape[2]
    return pl.pallas_call(
        _ada_kernel,
        grid=(DEPTH, n_out // tn),
        in_specs=[
            pl.BlockSpec((rows, D_MODEL), lambda l, n: (0, 0)),
            pl.BlockSpec((1, D_MODEL, tn), lambda l, n: (l, 0, n)),
            pl.BlockSpec((1, 1, tn), lambda l, n: (l, 0, n)),
        ],
        out_specs=pl.BlockSpec((1, rows, tn), lambda l, n: (l, 0, n)),
        out_shape=jax.ShapeDtypeStruct((DEPTH, rows, n_out), f32),
        compiler_params=_cparams(("parallel", "parallel")),
        name="ada_mod",
    )(c_all, w_ada, b_ada.reshape(DEPTH, 1, n_out))


def _rope(x, tab):
    w = x.shape[1]
    reps = w // LANES
    cosf, up, dn = (jnp.concatenate([tab[i]] * reps, axis=1) for i in range(3))
    half = ROT_DIM // 2
    return x * cosf + pltpu.roll(x, w - half, 1) * up + pltpu.roll(x, half, 1) * dn


def _norm_proj_kernel(x_ref, g_ref, scale_ref, shift_ref, w_ref, tab_ref, o_ref, h_ref):
    n = pl.program_id(2)

    @pl.when(n == 0)
    def _():
        x = x_ref[0]
        ms = jnp.mean(x * x, axis=-1, keepdims=True)
        y = x * lax.rsqrt(ms + NORM_EPS) * g_ref[...]
        h_ref[...] = (y * (1.0 + scale_ref[0]) + shift_ref[0]).astype(bf16)

    q_tile = P_Q // PROJ_TN
    k_tile = P_KA // PROJ_TN
    tm = h_ref.shape[0]
    rc = min(tm, 256)

    def rows(i):
        rs = slice(i * rc, (i + 1) * rc)
        return rs, jnp.dot(h_ref[rs, :], w_ref[...], preferred_element_type=f32)

    @pl.when(n == q_tile)
    def _():
        for i in range(tm // rc):
            rs, res = rows(i)
            o_ref[0, rs, :] = _rope(res, tab_ref[:, rs, :]) * (HEAD_DIM ** -0.5)

    @pl.when(n == k_tile)
    def _():
        for i in range(tm // rc):
            rs, res = rows(i)
            o_ref[0, rs, :] = jnp.concatenate(
                [_rope(res[:, :KV_WIDTH], tab_ref[:, rs, :]), res[:, KV_WIDTH:]], axis=1)

    @pl.when((n != q_tile) & (n != k_tile))
    def _():
        o_ref[0] = jnp.dot(h_ref[...], w_ref[...], preferred_element_type=f32)


def _mod_spec(mod, tm):
    if mod.shape[1] == 1:
        return pl.BlockSpec((1, 1, D_MODEL), lambda g, m, *_: (g, 0, 0))
    return pl.BlockSpec((1, tm, D_MODEL), lambda g, m, *_: (g, m, 0))


def _norm_proj(x, norm_g, scale, shift, w_bf, tab, tm):
    G, R, _ = x.shape
    return pl.pallas_call(
        _norm_proj_kernel,
        grid=(G, R // tm, P_WIDTH // PROJ_TN),
        in_specs=[
            pl.BlockSpec((1, tm, D_MODEL), lambda g, m, n: (g, m, 0)),
            pl.BlockSpec((1, D_MODEL), lambda g, m, n: (0, 0)),
            _mod_spec(scale, tm),
            _mod_spec(shift, tm),
            pl.BlockSpec((D_MODEL, PROJ_TN), lambda g, m, n: (0, n)),
            pl.BlockSpec((3, tm, LANES), lambda g, m, n: (0, m, 0)),
        ],
        out_specs=pl.BlockSpec((1, tm, PROJ_TN), lambda g, m, n: (g, m, n)),
        out_shape=jax.ShapeDtypeStruct((G, R, P_WIDTH), f32),
        scratch_shapes=[pltpu.VMEM((tm, D_MODEL), bf16)],
        compiler_params=_cparams(("parallel", "parallel", "arbitrary")),
        name="norm_proj",
    )(x, norm_g.reshape(1, D_MODEL), scale, shift, w_bf, tab)


def _rope_tables(pos):
    half = ROT_DIM // 2
    inv_freq = ROPE_THETA ** (-jnp.arange(half, dtype=f32) * (2.0 / ROT_DIM))
    ang = pos.astype(f32)[:, None] * inv_freq[None, :]
    cos, sin = jnp.cos(ang), jnp.sin(ang)
    t = pos.shape[0]
    z8 = jnp.zeros((t, half), f32)
    rest = HEAD_DIM - ROT_DIM
    cos64 = jnp.concatenate([cos, cos, jnp.ones((t, rest), f32)], axis=1)
    up64 = jnp.concatenate([-sin, z8, jnp.zeros((t, rest), f32)], axis=1)
    dn64 = jnp.concatenate([z8, sin, jnp.zeros((t, rest), f32)], axis=1)
    rep = LANES // HEAD_DIM
    return jnp.stack([jnp.tile(a, (1, rep)) for a in (cos64, up64, dn64)])


def _attn_prompt_kernel(sinks_ref, q_ref, kc_ref, kp_ref, vc_ref, vp_ref, ga_ref, o_ref):
    n = pl.program_id(1)
    wn = WINDOW
    half = HEAD_DIM
    q = q_ref[0].astype(bf16)
    k_t = jnp.concatenate([kp_ref[0], kc_ref[0]], axis=0).T.astype(bf16)
    vcat = jnp.concatenate([vp_ref[0], vc_ref[0]], axis=0)
    ga = ga_ref[0]

    qi = lax.broadcasted_iota(jnp.int32, (2 * wn, 2 * wn), 0) & (wn - 1)
    kj = lax.broadcasted_iota(jnp.int32, (2 * wn, 2 * wn), 1)
    rel = wn + qi - kj
    mask = (rel >= 0) & (rel <= wn) & ((kj >= wn) | (n > 0))
    top = lax.broadcasted_iota(jnp.int32, (2 * wn, 1), 0) < wn
    lo = lax.broadcasted_iota(jnp.int32, (2 * wn, LANES), 1) < half
    zeros_k = jnp.zeros((half, 2 * wn), bf16)
    ones_lo = jnp.where(lo, 1.0, 0.0).astype(bf16)
    ones_hi = jnp.where(lo, 0.0, 1.0).astype(bf16)

    for j in range(N_KV_HEADS // 2):
        vblk = vcat[:, j * LANES:(j + 1) * LANES]
        vswap = pltpu.roll(vblk, half, 1)
        for g in (2 * j, 2 * j + 1):
            own, other = (vblk, vswap) if g % 2 == 0 else (vswap, vblk)
            v_lo = jnp.where(lo, own, 0.0).astype(bf16)
            v_hi = jnp.where(lo, 0.0, other).astype(bf16)
            rhs_pv = jnp.concatenate([jnp.concatenate([v_lo, ones_lo], axis=1),
                                      jnp.concatenate([v_hi, ones_hi], axis=1)], axis=0)
            kg = k_t[g * half:(g + 1) * half, :]
            rhs_qk = jnp.concatenate([jnp.concatenate([kg, zeros_k], axis=0),
                                      jnp.concatenate([zeros_k, kg], axis=0)], axis=1)
            b0, b1 = 2 * g, 2 * g + 1
            qg = jnp.concatenate([q[:, b0 * LANES:(b0 + 1) * LANES], q[:, b1 * LANES:(b1 + 1) * LANES]],
                                 axis=0)
            s_all = jnp.dot(qg, rhs_qk, preferred_element_type=f32)
            ps, es = [], []
            for hh in range(2):
                s = jnp.where(mask, s_all[:, hh * 2 * wn:(hh + 1) * 2 * wn], NEG_BIG)
                sink = jnp.where(top, sinks_ref[2 * b0 + hh], sinks_ref[2 * b1 + hh])
                m = jnp.maximum(jnp.max(s, axis=-1, keepdims=True), sink)
                ps.append(jnp.exp(s - m).astype(bf16))
                es.append(jnp.exp(sink - m))
            res = jnp.dot(jnp.concatenate(ps, axis=1), rhs_pv, preferred_element_type=f32)
            out = res[:, :LANES] / (res[:, LANES:] + jnp.where(lo, es[0], es[1]))
            for i, blk in enumerate((b0, b1)):
                sl = slice(blk * LANES, (blk + 1) * LANES)
                o_ref[0, :, sl] = (out[i * wn:(i + 1) * wn] * _silu(ga[:, sl])).astype(o_ref.dtype)


def _attn_prompt(proj, sinks):
    B, T, _ = proj.shape
    nb = T // WINDOW
    kvb = KV_WIDTH
    prev = lambda b, n: jnp.maximum(n - 1, 0)
    return pl.pallas_call(
        _attn_prompt_kernel,
        grid=(B, nb),
        in_specs=[
            pl.BlockSpec(memory_space=pltpu.SMEM),
            pl.BlockSpec((1, WINDOW, ATT_WIDTH), lambda b, n: (b, n, P_Q // ATT_WIDTH)),
            pl.BlockSpec((1, WINDOW, kvb), lambda b, n: (b, n, P_KA // kvb)),
            pl.BlockSpec((1, WINDOW, kvb), lambda b, n: (b, prev(b, n), P_KA // kvb)),
            pl.BlockSpec((1, WINDOW, kvb), lambda b, n: (b, n, P_VA // kvb)),
            pl.BlockSpec((1, WINDOW, kvb), lambda b, n: (b, prev(b, n), P_VA // kvb)),
            pl.BlockSpec((1, WINDOW, ATT_WIDTH), lambda b, n: (b, n, P_GA // ATT_WIDTH)),
        ],
        out_specs=pl.BlockSpec((1, WINDOW, ATT_WIDTH), lambda b, n: (b, n, 0)),
        out_shape=jax.ShapeDtypeStruct((B, T, ATT_WIDTH), bf16),
        compiler_params=_cparams(("parallel", "arbitrary")),
        name="attn_prompt",
    )(sinks, proj, proj, proj, proj, proj, proj)


SAMPLE_ROWS = 16


def _attn_sample_kernel(sinks_ref, q_ref, kn_ref, vn_ref, ga_ref, ck_ref, cv_ref, o_ref, nk_ref, nv_ref):
    rb = q_ref.shape[1]
    q_all, kn_all, vn_all = q_ref[0], kn_ref[0], vn_ref[0]
    work = [(r, g) for r in range(rb) for g in range(N_KV_HEADS)]
    span = lambda g: slice(g * HEAD_DIM, (g + 1) * HEAD_DIM)
    ckb = [ck_ref[r].astype(bf16) for r in range(rb)]
    cvb = [cv_ref[r].astype(bf16) for r in range(rb)]
    qg = {(r, g): jnp.concatenate(
        [q_all[r:r + 1, (g * GQA_GROUP + i) * HEAD_DIM:(g * GQA_GROUP + i + 1) * HEAD_DIM]
         for i in range(GQA_GROUP)], axis=0) for r, g in work}
    s = {rg: lax.dot_general(qg[rg].astype(bf16), ckb[rg[0]][:, span(rg[1])], _NT,
                             preferred_element_type=f32) for rg in work}
    outs = {}
    for r, g in work:
        s_new = jnp.sum(qg[r, g] * kn_all[r:r + 1, span(g)], axis=-1, keepdims=True)
        sink = sinks_ref[g * GQA_GROUP:(g + 1) * GQA_GROUP, :]
        m = jnp.maximum(jnp.maximum(jnp.max(s[r, g], axis=-1, keepdims=True), s_new), sink)
        p = jnp.exp(s[r, g] - m)
        p_new = jnp.exp(s_new - m)
        den = jnp.sum(p, axis=-1, keepdims=True) + p_new + jnp.exp(sink - m)
        o = (jnp.dot(p.astype(bf16), cvb[r][:, span(g)], preferred_element_type=f32)
             + p_new * vn_all[r:r + 1, span(g)]) / den
        outs[r, g] = [o[i:i + 1, :] for i in range(GQA_GROUP)]
    att = jnp.concatenate(
        [jnp.concatenate([h for g in range(N_KV_HEADS) for h in outs[r, g]], axis=1) for r in range(rb)],
        axis=0)
    o_ref[0] = (att * _silu(ga_ref[0])).astype(o_ref.dtype)
    last = lax.broadcasted_iota(jnp.int32, (WINDOW, KV_WIDTH), 0) == WINDOW - 1
    for r in range(rb):
        nk_ref[r] = jnp.where(last, kn_all[r:r + 1], pltpu.roll(ck_ref[r], WINDOW - 1, 0))
        nv_ref[r] = jnp.where(last, vn_all[r:r + 1], pltpu.roll(cv_ref[r], WINDOW - 1, 0))


def _attn_sample(proj, ck, cv, sinks):
    Bd = proj.shape[1]
    rb = SAMPLE_ROWS
    col = lambda w, off: pl.BlockSpec((1, rb, w), lambda i: (0, i, off // w))
    cache = pl.BlockSpec((rb, WINDOW, KV_WIDTH), lambda i: (i, 0, 0))
    return pl.pallas_call(
        _attn_sample_kernel,
        grid=(Bd // rb,),
        in_specs=[
            pl.BlockSpec((N_Q_HEADS, 1), lambda i: (0, 0)),
            col(ATT_WIDTH, P_Q), col(KV_WIDTH, P_KA), col(KV_WIDTH, P_VA), col(ATT_WIDTH, P_GA),
            cache, cache,
        ],
        out_specs=[pl.BlockSpec((1, rb, ATT_WIDTH), lambda i: (0, i, 0)), cache, cache],
        out_shape=[
            jax.ShapeDtypeStruct((1, Bd, ATT_WIDTH), bf16),
            jax.ShapeDtypeStruct((Bd, WINDOW, KV_WIDTH), f32),
            jax.ShapeDtypeStruct((Bd, WINDOW, KV_WIDTH), f32),
        ],
        compiler_params=_cparams(("parallel",)),
        name="attn_sample",
    )(sinks.reshape(N_Q_HEADS, 1), proj, proj, proj, proj, ck, cv)


def _rwkv_prep_kernel(r_ref, kr_ref, vr_ref, wa_ref, pr_ref, pkr_ref, pvr_ref, pwa_ref,
                      mu_r_ref, mu_kr_ref, mu_vr_ref, mu_wa_ref, w0_ref, a0_ref, kk_ref, ka_ref,
                      wd_ref, wi_ref, ro_ref, wo_ref, ko_ref, vo_ref, kko_ref, ao_ref):
    def mixed(cur_ref, prev_ref, mu_ref):
        cur = cur_ref[0]
        return cur + (prev_ref[0] - cur) * mu_ref[...]

    r = mixed(r_ref, pr_ref, mu_r_ref)
    kr = mixed(kr_ref, pkr_ref, mu_kr_ref)
    vr = mixed(vr_ref, pvr_ref, mu_vr_ref)
    wa = mixed(wa_ref, pwa_ref, mu_wa_ref)
    hi = lax.Precision.HIGHEST
    z = w0_ref[...] + jnp.dot(jnp.tanh(wa), wd_ref[...], precision=hi, preferred_element_type=f32)
    nz = -z
    softplus = jnp.maximum(nz, 0.0) + jnp.log1p(jnp.exp(-jnp.abs(nz)))
    w_log = -softplus - 0.5
    a = jax.nn.sigmoid(a0_ref[...] + jnp.dot(wa, wi_ref[...], precision=hi, preferred_element_type=f32))
    ro_ref[0] = r
    wo_ref[0] = -jnp.exp(w_log)
    ko_ref[0] = kr * (1.0 + (a - 1.0) * ka_ref[...])
    vo_ref[0] = vr
    kko_ref[0] = kr * kk_ref[...]
    ao_ref[0] = a


def _rwkv_prep(proj, prev, mu, w0, a0, k_k, k_a, wd_pad, wi_pad, tm):
    G, R, _ = proj.shape
    W = RWKV_WIDTH
    cur_specs = [
        pl.BlockSpec((1, tm, W), lambda g, m: (g, m, P_R // W)),
        pl.BlockSpec((1, tm, W), lambda g, m: (g, m, P_KR // W)),
        pl.BlockSpec((1, tm, W), lambda g, m: (g, m, P_VR // W)),
        pl.BlockSpec((1, tm, LANES), lambda g, m: (g, m, P_WA // LANES)),
    ]
    prev_specs = [
        pl.BlockSpec((1, tm, W), lambda g, m: (g, m, 0)),
        pl.BlockSpec((1, tm, W), lambda g, m: (g, m, 0)),
        pl.BlockSpec((1, tm, W), lambda g, m: (g, m, 0)),
        pl.BlockSpec((1, tm, LANES), lambda g, m: (g, m, 0)),
    ]
    prev_args = list(prev)
    vec = lambda w: pl.BlockSpec((1, w), lambda g, m: (0, 0))
    lora = pl.BlockSpec((LORA_PAIR, W), lambda g, m: (0, 0))
    out_spec = pl.BlockSpec((1, tm, W), lambda g, m: (g, m, 0))
    return pl.pallas_call(
        _rwkv_prep_kernel,
        grid=(G, R // tm),
        in_specs=cur_specs + prev_specs + [vec(W), vec(W), vec(W), vec(LANES),
                                           vec(W), vec(W), vec(W), vec(W), lora, lora],
        out_specs=[out_spec] * 6,
        out_shape=[jax.ShapeDtypeStruct((G, R, W), f32)] * 6,
        compiler_params=_cparams(("parallel", "arbitrary")),
        name="rwkv_prep",
    )(proj, proj, proj, proj, *prev_args, *mu, w0, a0, k_k, k_a, wd_pad, wi_pad)


def _wkv_kernel(tc, r_ref, w_ref, k_ref, v_ref, kk_ref, a_ref, rk_ref, lnw_ref, lnb_ref, s0_ref,
                y_ref, s_ref):
    @pl.when(pl.program_id(1) == 0)
    def _():
        s_ref[...] = s0_ref[...]

    eye = (lax.broadcasted_iota(jnp.int32, (RW_HEAD, RW_HEAD), 0)
           == lax.broadcasted_iota(jnp.int32, (RW_HEAD, RW_HEAD), 1)).astype(f32)
    rk = rk_ref[...]
    lnw = lnw_ref[...]
    lnb = lnb_ref[...]

    def step(t, carry):
        r, lw, k, v, kkr, a = (ref[0, t] for ref in (r_ref, w_ref, k_ref, v_ref, kk_ref, a_ref))
        w = jnp.exp(lw)
        norm = jnp.sqrt(jnp.sum(kkr * kkr, axis=-1, keepdims=True))
        kk = kkr / jnp.maximum(norm, 1e-12)
        b = kk * a
        bonus = jnp.sum(r * k * rk, axis=-1, keepdims=True) * v
        rows = []
        for h in range(RW_HEADS):
            hs = slice(h, h + 1)
            S = s_ref[0, h]
            sa = jnp.sum(S * (-kk[hs]), axis=-1, keepdims=True)
            v_col = jnp.sum(eye * v[hs], axis=-1, keepdims=True)
            S = S * w[hs] + sa * b[hs] + v_col * k[hs]
            s_ref[0, h] = S
            y = jnp.sum(S * r[hs], axis=-1, keepdims=True)
            mu = jnp.mean(y, axis=0, keepdims=True)
            var = jnp.mean(jnp.square(y - mu), axis=0, keepdims=True)
            yn = (y - mu) * lax.rsqrt(var + GN_EPS)
            rows.append(jnp.sum(yn * eye, axis=0, keepdims=True))
        y_ref[0, t] = jnp.concatenate(rows, axis=0) * lnw + lnb + bonus
        return carry

    lax.fori_loop(0, tc, step, 0)


def _wkv(r, w, k, v, kk, a, r_k, ln_w, ln_b, s0, tc):
    B, T = r.shape[0], r.shape[1]
    hd = (RW_HEADS, RW_HEAD)
    heads = lambda t: t.reshape(B, T, *hd)
    seq = pl.BlockSpec((1, tc, *hd), lambda b, c: (b, c, 0, 0))
    par = pl.BlockSpec(hd, lambda b, c: (0, 0))
    state = pl.BlockSpec((1, RW_HEADS, RW_HEAD, RW_HEAD), lambda b, c: (b, 0, 0, 0))
    y, s_t = pl.pallas_call(
        functools.partial(_wkv_kernel, tc),
        grid=(B, T // tc),
        in_specs=[seq] * 6 + [par] * 3 + [state],
        out_specs=[seq, state],
        out_shape=[
            jax.ShapeDtypeStruct((B, T, *hd), f32),
            jax.ShapeDtypeStruct((B, RW_HEADS, RW_HEAD, RW_HEAD), f32),
        ],
        compiler_params=_cparams(("parallel", "arbitrary")),
        name="wkv_steps",
    )(heads(r), heads(w), heads(k), heads(v), heads(kk), heads(a),
      r_k, ln_w.reshape(hd), ln_b.reshape(hd), s0)
    return y.reshape(B, T, RWKV_WIDTH), s_t


CHUNK = 64
PAIR = 2 * RW_HEAD
N_PAIRS = RW_HEADS // 2


def _mmb(a, b, dims=_NN):
    return lax.dot_general(a.astype(bf16), b.astype(bf16), dims, preferred_element_type=f32)


def _split_bf16(x):
    hi = x.astype(bf16)
    return hi, (x - hi.astype(f32)).astype(bf16)


def _lora_dot(x, wh_ref, wl_ref):
    xh, xl = _split_bf16(x)
    rows = x.shape[0]
    both = jnp.dot(jnp.concatenate([xh, xl], axis=0), wh_ref[...], preferred_element_type=f32)
    return both[:rows] + both[rows:] + jnp.dot(xh, wl_ref[...], preferred_element_type=f32)


def _wkv_chunk_kernel(r_ref, kr_ref, vr_ref, wa_ref, gr_ref, pr_ref, pkr_ref, pvr_ref, pwa_ref,
                      mu_r_ref, mu_kr_ref, mu_vr_ref, mu_wa_ref, w0_ref, a0_ref, kk_ref, ka_ref,
                      wdh_ref, wdl_ref, wih_ref, wil_ref, rk_ref, lnw_ref, lnb_ref, s0_ref,
                      y_ref, s_ref, sp_ref, cr_ref, ckr_ref, cvr_ref, cwa_ref):
    C = CHUNK
    c = pl.program_id(1)

    @pl.when(c == 0)
    def _():
        for p in range(N_PAIRS):
            sp_ref[p] = jnp.concatenate([s0_ref[0, 2 * p], s0_ref[0, 2 * p + 1]], axis=1)
        for carry, first in ((cr_ref, pr_ref), (ckr_ref, pkr_ref), (cvr_ref, pvr_ref), (cwa_ref, pwa_ref)):
            carry[...] = first[0]

    def mixed(cur_ref, carry_ref, mu_ref):
        cur = cur_ref[0]
        first = lax.broadcasted_iota(jnp.int32, cur.shape, 0) == 0
        prev = jnp.where(first, carry_ref[...], pltpu.roll(cur, 1, 0))
        carry_ref[...] = cur[C - 1:C, :]
        return cur + (prev - cur) * mu_ref[...]

    r_all = mixed(r_ref, cr_ref, mu_r_ref)
    kr_all = mixed(kr_ref, ckr_ref, mu_kr_ref)
    v_all = mixed(vr_ref, cvr_ref, mu_vr_ref)
    wa = mixed(wa_ref, cwa_ref, mu_wa_ref)
    nz = -(w0_ref[...] + _lora_dot(jnp.tanh(wa), wdh_ref, wdl_ref))
    softplus = jnp.maximum(nz, 0.0) + jnp.log1p(jnp.exp(-jnp.abs(nz)))
    lw = -jnp.exp(-softplus - 0.5)
    icl_all = jax.nn.sigmoid(a0_ref[...] + _lora_dot(wa, wih_ref, wil_ref))
    k_all = kr_all * (1.0 + (icl_all - 1.0) * ka_ref[...])
    kkr_all = kr_all * kk_ref[...]

    row = lax.broadcasted_iota(jnp.int32, (PAIR, PAIR), 0)
    col = lax.broadcasted_iota(jnp.int32, (PAIR, PAIR), 1)
    tril = row >= col
    stril = row > col
    eye = (row == col).astype(f32)
    lane_lo = lax.broadcasted_iota(jnp.int32, (C, PAIR), 1) < RW_HEAD

    def bd(x):
        zero = jnp.zeros_like(x)
        return jnp.concatenate([jnp.where(lane_lo, x, zero), jnp.where(lane_lo, zero, x)], axis=0)

    def head_sums(x):
        lo_sum = jnp.sum(jnp.where(lane_lo, x, 0.0), axis=-1, keepdims=True)
        hi_sum = jnp.sum(jnp.where(lane_lo, 0.0, x), axis=-1, keepdims=True)
        return jnp.where(lane_lo, lo_sum, hi_sum)

    width = lw.shape[1]
    lw_a = lw.astype(bf16)
    rest = lw - lw_a.astype(f32)
    lw_b = rest.astype(bf16)
    lw_c = (rest - lw_b.astype(f32)).astype(bf16)
    g3 = jnp.dot(tril[:C, :C].astype(bf16), jnp.concatenate([lw_a, lw_b, lw_c], axis=1),
                 preferred_element_type=f32)
    g = g3[:, :width] + g3[:, width:2 * width] + g3[:, 2 * width:]
    e_g = jnp.exp(g)
    e_ng = jnp.exp(-g)
    e_gm = jnp.exp(g - lw)
    e_end = e_g[C - 1:C, :]

    pairs = range(N_PAIRS)
    sls = [slice(p * PAIR, (p + 1) * PAIR) for p in pairs]
    ins = [[t[:, sl] for t in (r_all, k_all, v_all, kkr_all, icl_all)] for sl in sls]
    norms = [jnp.sqrt(head_sums(x[3] * x[3])) for x in ins]
    at, rt, bt, kt, vb = ([] for _ in range(5))
    for (r, k, v, kkr, icl), norm, sl in zip(ins, norms, sls):
        kk = kkr / jnp.maximum(norm, 1e-12)
        b = kk * icl
        at.append(bd((-kk * e_gm[:, sl]).astype(bf16)))
        rt.append(bd((r * e_g[:, sl]).astype(bf16)))
        bt.append(bd((b * e_ng[:, sl]).astype(bf16)))
        kt.append(bd((k * e_ng[:, sl]).astype(bf16)))
        vb.append(bd(v.astype(bf16)))

    tril2 = jnp.concatenate([tril, tril], axis=1)
    gram = [_mmb(jnp.concatenate([at[p], rt[p]], axis=0), jnp.concatenate([bt[p], kt[p]], axis=0), _NT)
            for p in pairs]
    lmat = [jnp.where(stril, gm[:PAIR, :PAIR], 0.0) for gm in gram]
    mv = [_mmb(jnp.where(stril, gram[p][:PAIR, PAIR:], 0.0), vb[p]) for p in pairs]
    lower = [jnp.where(tril2, gm[PAIR:, :], 0.0).astype(bf16) for gm in gram]

    tinv = [eye + lm for lm in lmat]
    pw = [_mmb(lm, lm) for lm in lmat]
    for _ in range(4):
        z = [_mmb(jnp.concatenate([x.astype(bf16), t.astype(bf16)], axis=0), x) for t, x in zip(tinv, pw)]
        pw = [zz[:PAIR] for zz in z]
        tinv = [t + zz[PAIR:] for t, zz in zip(tinv, z)]
    tinv = [t + _mmb(t, x) for t, x in zip(tinv, pw)]

    wx = [_mmb(tinv[p], jnp.concatenate([at[p], mv[p].astype(bf16)], axis=1)) for p in pairs]
    s_old = [sp_ref[p] for p in pairs]
    uy0 = [_mmb(jnp.concatenate([wx[p][:, :PAIR].astype(bf16), rt[p]], axis=0),
                bd(s_old[p].astype(bf16)), _NT) for p in pairs]
    uv = [jnp.concatenate([(uy0[p][:PAIR] + wx[p][:, PAIR:]).astype(bf16), vb[p]], axis=0) for p in pairs]
    ys = [uy0[p][PAIR:] + _mmb(lower[p], uv[p]) for p in pairs]
    s_add = [_mmb(uv[p], jnp.concatenate([bt[p], kt[p]], axis=0), _TN) for p in pairs]
    for p in pairs:
        sp_ref[p] = (s_old[p] + s_add[p][:RW_HEAD] + s_add[p][RW_HEAD:]) * e_end[:, sls[p]]

    ys = [y[:C] + y[C:] for y in ys]
    mus = [head_sums(y) * (1.0 / RW_HEAD) for y in ys]
    ds = [y - mu for y, mu in zip(ys, mus)]
    var = [head_sums(d * d) * (1.0 / RW_HEAD) for d in ds]
    bonus = [head_sums(x[0] * x[1] * rk_ref[:, sl]) * x[2] for x, sl in zip(ins, sls)]
    for p in pairs:
        sl = sls[p]
        y_rw = ds[p] * lax.rsqrt(var[p] + GN_EPS) * lnw_ref[:, sl] + lnb_ref[:, sl] + bonus[p]
        y_ref[0, :, sl] = (y_rw * _silu(gr_ref[0, :, sl])).astype(y_ref.dtype)

    @pl.when(c == pl.num_programs(1) - 1)
    def _():
        for p in range(N_PAIRS):
            s = sp_ref[p]
            s_ref[0, 2 * p] = s[:, :RW_HEAD]
            s_ref[0, 2 * p + 1] = s[:, RW_HEAD:]


def _wkv_chunked(proj, prev, mu, w0, a0, k_k, k_a, lora, r_k, ln_w, ln_b, s0):
    B, T, _ = proj.shape
    W = RWKV_WIDTH
    col = lambda w, off: pl.BlockSpec((1, CHUNK, w), lambda b, c: (b, c, off // w))
    first = lambda w: pl.BlockSpec((1, 1, w), lambda b, c: (b, 0, 0))
    vec = lambda w: pl.BlockSpec((1, w), lambda b, c: (0, 0))
    lora_spec = pl.BlockSpec((LORA_PAIR, W), lambda b, c: (0, 0))
    state = pl.BlockSpec((1, RW_HEADS, RW_HEAD, RW_HEAD), lambda b, c: (b, 0, 0, 0))
    return pl.pallas_call(
        _wkv_chunk_kernel,
        grid=(B, T // CHUNK),
        in_specs=[col(W, P_R), col(W, P_KR), col(W, P_VR), col(LANES, P_WA), col(W, P_GR),
                  first(W), first(W), first(W), first(LANES),
                  vec(W), vec(W), vec(W), vec(LANES), vec(W), vec(W), vec(W), vec(W),
                  lora_spec, lora_spec, lora_spec, lora_spec, vec(W), vec(W), vec(W), state],
        out_specs=[pl.BlockSpec((1, CHUNK, W), lambda b, c: (b, c, 0)), state],
        out_shape=[
            jax.ShapeDtypeStruct((B, T, W), bf16),
            jax.ShapeDtypeStruct((B, RW_HEADS, RW_HEAD, RW_HEAD), f32),
        ],
        scratch_shapes=[pltpu.VMEM((N_PAIRS, RW_HEAD, PAIR), f32),
                        pltpu.VMEM((1, W), f32), pltpu.VMEM((1, W), f32), pltpu.VMEM((1, W), f32),
                        pltpu.VMEM((1, LANES), f32)],
        compiler_params=_cparams(("parallel", "arbitrary")),
        name="wkv_chunks",
    )(proj, proj, proj, proj, proj, *prev, *mu, w0, a0, k_k, k_a, *lora,
      r_k.reshape(1, W), ln_w.reshape(1, W), ln_b.reshape(1, W), s0)


def _post_kernel(final, gated, att_ref, y_ref, *refs):
    if gated:
        y = y_ref[0]
    else:
        y = (y_ref[0] * _silu(refs[0][0])).astype(bf16)
        refs = refs[1:]
    x_ref, gate_ref, w_ref, fg_ref, o_ref = refs
    cat = jnp.concatenate([att_ref[0], y], axis=1)
    out = jnp.dot(cat, w_ref[...], preferred_element_type=f32)
    x = x_ref[0] + gate_ref[0] * out
    if final:
        ms = jnp.mean(x * x, axis=-1, keepdims=True)
        x = x * lax.rsqrt(ms + NORM_EPS) * fg_ref[...]
    o_ref[0] = x


def _post(att, y_rw, proj, x, gate, w_out_bf, final_g, final, tm):
    G, R, _ = x.shape
    W = RWKV_WIDTH
    gated = proj is None
    gate_in = [] if gated else [pl.BlockSpec((1, tm, W), lambda g, m: (g, m, P_GR // W))]
    gate_arg = [] if gated else [proj]
    return pl.pallas_call(
        functools.partial(_post_kernel, final, gated),
        grid=(G, R // tm),
        in_specs=[
            pl.BlockSpec((1, tm, ATT_WIDTH), lambda g, m: (g, m, 0)),
            pl.BlockSpec((1, tm, W), lambda g, m: (g, m, 0)),
            *gate_in,
            pl.BlockSpec((1, tm, D_MODEL), lambda g, m: (g, m, 0)),
            _mod_spec(gate, tm),
            pl.BlockSpec((D_MODEL, D_MODEL), lambda g, m: (0, 0)),
            pl.BlockSpec((1, D_MODEL), lambda g, m: (0, 0)),
        ],
        out_specs=pl.BlockSpec((1, tm, D_MODEL), lambda g, m: (g, m, 0)),
        out_shape=jax.ShapeDtypeStruct((G, R, D_MODEL), f32),
        compiler_params=_cparams(("parallel", "parallel")),
        name="post_proj",
    )(att, y_rw, *gate_arg, x, gate, w_out_bf, final_g.reshape(1, D_MODEL))


def _arrange_w_in(w):
    pad = jnp.zeros((w.shape[0], P_WIDTH - IN_WIDTH), w.dtype)
    parts = [w[:, Q_OFF:KA_OFF], w[:, GA_OFF:GR_OFF], w[:, GR_OFF:IN_WIDTH], w[:, R_OFF:WD_OFF],
             w[:, KA_OFF:R_OFF], w[:, WD_OFF:GA_OFF], pad]
    return jnp.concatenate(parts, axis=1).astype(bf16)


def _shift_cols(t):
    return jnp.concatenate([t[..., P_R:P_KA], t[..., P_WA:P_WA + LORA_PAIR]], axis=-1)


def kernel(x_prompt, x_sample, cache_k, cache_v, state_wkv, state_shift, c_prompt, c_sample,
           norm_g, w_ada, b_ada, w_in, mu_shift, w0, w_decay, a0, w_iclr, k_k, k_a, r_k,
           ln_w, ln_b, sinks, w_out, final_g):
    Bp, Tp = x_prompt.shape[0], x_prompt.shape[1]
    Bd = x_sample.shape[0]
    W = RWKV_WIDTH

    n_c = Bp + Bd
    c_rows = -(-n_c // 8) * 8
    c_all = jnp.concatenate([c_prompt, c_sample, jnp.zeros((c_rows - n_c, D_MODEL), f32)], axis=0)
    mod = _ada(c_all, w_ada, b_ada)

    tab_p = _rope_tables(jnp.arange(Tp, dtype=jnp.int32))
    tab_s = _rope_tables(jnp.full((Bd,), PAST_LEN, jnp.int32))

    hp = x_prompt
    hs = x_sample.reshape(1, Bd, D_MODEL)
    s0_p = jnp.zeros((Bp, RW_HEADS, RW_HEAD, RW_HEAD), f32)
    shift0_p = [jnp.zeros((Bp, 1, w), f32) for w in (W, W, W, LORA_PAIR)]
    outs = {k: [] for k in ("kp", "vp", "sp", "shp", "ks", "vs", "ss", "shs")}
    for l in range(DEPTH):
        final = l == DEPTH - 1
        w_bf = _arrange_w_in(w_in[l])
        w_out_bf = w_out[l].astype(bf16)
        mu_l = mu_shift[l]
        mu = [mu_l[0:W].reshape(1, W), mu_l[W:2 * W].reshape(1, W), mu_l[2 * W:3 * W].reshape(1, W),
              mu_l[3 * W:].reshape(1, LORA_PAIR)]
        vecs = [t[l].reshape(1, W) for t in (w0, a0, k_k, k_a)]
        wd_pad = jnp.concatenate([w_decay[l], jnp.zeros((ICLR_LORA, W), f32)], axis=0)
        wi_pad = jnp.concatenate([jnp.zeros((DECAY_LORA, W), f32), w_iclr[l]], axis=0)
        shift_p, scale_p, gate_p = (mod[l, :Bp, i * D_MODEL:(i + 1) * D_MODEL].reshape(Bp, 1, D_MODEL)
                                    for i in range(3))
        shift_s, scale_s, gate_s = (mod[l, Bp:n_c, i * D_MODEL:(i + 1) * D_MODEL].reshape(1, Bd, D_MODEL)
                                    for i in range(3))

        proj = _norm_proj(hp, norm_g[l], scale_p, shift_p, w_bf, tab_p, tm=1024)
        att = _attn_prompt(proj, sinks[l])
        lora = [piece for wp in (wd_pad, wi_pad) for piece in _split_bf16(wp)]
        y_rw, s_t = _wkv_chunked(proj, shift0_p, mu, *vecs, lora, r_k[l], ln_w[l], ln_b[l], s0_p)
        hp = _post(att, y_rw, None, hp, gate_p, w_out_bf, final_g, final, tm=256)
        tail = proj[:, Tp - WINDOW:]
        outs["kp"].append(tail[..., P_KA:P_KA + KV_WIDTH].reshape(Bp, WINDOW, N_KV_HEADS, HEAD_DIM))
        outs["vp"].append(tail[..., P_VA:P_VA + KV_WIDTH].reshape(Bp, WINDOW, N_KV_HEADS, HEAD_DIM))
        outs["sp"].append(s_t)
        outs["shp"].append(_shift_cols(proj[:, Tp - 1]))

        proj = _norm_proj(hs, norm_g[l], scale_s, shift_s, w_bf, tab_s, tm=Bd)
        att, nk, nv = _attn_sample(proj, cache_k[l].reshape(Bd, WINDOW, KV_WIDTH),
                                   cache_v[l].reshape(Bd, WINDOW, KV_WIDTH), sinks[l])
        sh = state_shift[l]
        prev = [sh[None, :, 0:W], sh[None, :, W:2 * W], sh[None, :, 2 * W:3 * W], sh[None, :, 3 * W:]]
        prep = _rwkv_prep(proj, prev, mu, *vecs, wd_pad, wi_pad, tm=Bd)
        prep = [t.reshape(Bd, 1, W) for t in prep]
        y_rw, s_t = _wkv(*prep, r_k[l], ln_w[l], ln_b[l], state_wkv[l], tc=1)
        hs = _post(att, y_rw.reshape(1, Bd, W), proj, hs, gate_s, w_out_bf, final_g, final, tm=Bd)
        outs["ks"].append(nk.reshape(Bd, WINDOW, N_KV_HEADS, HEAD_DIM))
        outs["vs"].append(nv.reshape(Bd, WINDOW, N_KV_HEADS, HEAD_DIM))
        outs["ss"].append(s_t)
        outs["shs"].append(_shift_cols(proj[0]))

    st = lambda k: jnp.stack(outs[k])
    return (hp, hs.reshape(Bd, 1, D_MODEL), st("kp"), st("vp"), st("sp"), st("shp"),
            st("ks"), st("vs"), st("ss"), st("shs"))
```

```python
import functools

import jax
import jax.numpy as jnp
from jax import lax
from jax.experimental import pallas as pl
from jax.experimental.pallas import tpu as pltpu

f32 = jnp.float32
bf16 = jnp.bfloat16

D_MODEL = 2048
DEPTH = 2
PAST_LEN = 16384
ATT_WIDTH = 1024
RWKV_WIDTH = 1024
HEAD_DIM = 64
N_Q_HEADS = 16
N_KV_HEADS = 4
GQA_GROUP = 4
KV_WIDTH = 256
WINDOW = 128
ROT_DIM = 16
ROPE_THETA = 500000.0
RW_HEAD = 64
RW_HEADS = 16
DECAY_LORA = 64
ICLR_LORA = 64
LORA_PAIR = DECAY_LORA + ICLR_LORA
NORM_EPS = 1e-5
GN_EPS = 64e-5
NEG_BIG = -1e30

Q_OFF = 0
KA_OFF = Q_OFF + ATT_WIDTH
VA_OFF = KA_OFF + KV_WIDTH
R_OFF = VA_OFF + KV_WIDTH
KR_OFF = R_OFF + RWKV_WIDTH
VR_OFF = KR_OFF + RWKV_WIDTH
WD_OFF = VR_OFF + RWKV_WIDTH
AD_OFF = WD_OFF + DECAY_LORA
GA_OFF = AD_OFF + ICLR_LORA
GR_OFF = GA_OFF + ATT_WIDTH
IN_WIDTH = GR_OFF + RWKV_WIDTH
SHIFT_DIM = GA_OFF - R_OFF

LANES = 128
P_Q = 0
P_GA = 1024
P_GR = 2048
P_R = 3072
P_KR = 4096
P_VR = 5120
P_KA = 6144
P_VA = 6400
P_WA = 6656
P_WIDTH = 7168
PROJ_TN = 1024

VMEM_LIMIT = 56 * 1024 * 1024


_NN = (((1,), (0,)), ((), ()))
_NT = (((1,), (1,)), ((), ()))
_TN = (((0,), (0,)), ((), ()))


def _silu(x):
    return x * jax.nn.sigmoid(x)


def _cparams(sem):
    return pltpu.CompilerParams(dimension_semantics=sem, vmem_limit_bytes=VMEM_LIMIT)


def _split_bf16(x):
    hi = x.astype(bf16)
    return hi, (x - hi.astype(f32)).astype(bf16)


def _ada_kernel(c_ref, w_ref, b_ref, o_ref):
    ch, cl = _split_bf16(_silu(c_ref[...]))
    wh, wl = _split_bf16(w_ref[0])
    rows = ch.shape[0]
    both = jnp.dot(jnp.concatenate([ch, cl], axis=0), wh, preferred_element_type=f32)
    o_ref[0] = both[:rows] + both[rows:] + jnp.dot(ch, wl, preferred_element_type=f32) + b_ref[0]


def _ada(c_all, w_ada, b_ada):
    rows = c_all.shape[0]
    tn = 768
    n_out = w_ada.shape[2]
    return pl.pallas_call(
        _ada_kernel,
        grid=(DEPTH, n_out // tn),
        in_specs=[
            pl.BlockSpec((rows, D_MODEL), lambda l, n: (0, 0)),
            pl.BlockSpec((1, D_MODEL, tn), lambda l, n: (l, 0, n)),
            pl.BlockSpec((1, 1, tn), lambda l, n: (l, 0, n)),
        ],
        out_specs=pl.BlockSpec((1, rows, tn), lambda l, n: (l, 0, n)),
        out_shape=jax.ShapeDtypeStruct((DEPTH, rows, n_out), f32),
        compiler_params=_cparams(("parallel", "parallel")),
        name="ada_mod",
    )(c_all, w_ada, b_ada.reshape(DEPTH, 1, n_out))


def _rope(x, tab):
    w = x.shape[1]
    reps = w // LANES
    cosf, up, dn = (jnp.concatenate([tab[i]] * reps, axis=1) for i in range(3))
    half = ROT_DIM // 2
    return x * cosf + pltpu.roll(x, w - half, 1) * up + pltpu.roll(x, half, 1) * dn


def _norm_proj_kernel(x_ref, g_ref, scale_ref, shift_ref, w_ref, tab_ref, o_ref, h_ref):
    n = pl.program_id(2)

    @pl.when(n == 0)
    def _():
        x = x_ref[0]
        ms = jnp.mean(x * x, axis=-1, keepdims=True)
        y = x * lax.rsqrt(ms + NORM_EPS) * g_ref[...]
        h_ref[...] = (y * (1.0 + scale_ref[0]) + shift_ref[0]).astype(bf16)

    q_tile = P_Q // PROJ_TN
    k_tile = P_KA // PROJ_TN
    tm = h_ref.shape[0]
    rc = min(tm, 256)

    def rows(i):
        rs = slice(i * rc, (i + 1) * rc)
        return rs, jnp.dot(h_ref[rs, :], w_ref[...], preferred_element_type=f32)

    @pl.when(n == q_tile)
    def _():
        for i in range(tm // rc):
            rs, res = rows(i)
            o_ref[0, rs, :] = _rope(res, tab_ref[:, rs, :]) * (HEAD_DIM ** -0.5)

    @pl.when(n == k_tile)
    def _():
        for i in range(tm // rc):
            rs, res = rows(i)
            o_ref[0, rs, :] = jnp.concatenate(
                [_rope(res[:, :KV_WIDTH], tab_ref[:, rs, :]), res[:, KV_WIDTH:]], axis=1)

    @pl.when((n != q_tile) & (n != k_tile))
    def _():
        o_ref[0] = jnp.dot(h_ref[...], w_ref[...], preferred_element_type=f32)


def _mod_spec(mod, tm):
    if mod.shape[1] == 1:
        return pl.BlockSpec((1, 1, D_MODEL), lambda g, m, *_: (g, 0, 0))
    return pl.BlockSpec((1, tm, D_MODEL), lambda g, m, *_: (g, m, 0))


def _norm_proj(x, norm_g, scale, shift, w_bf, tab, tm):
    G, R, _ = x.shape
    return pl.pallas_call(
        _norm_proj_kernel,
        grid=(G, R // tm, P_WIDTH // PROJ_TN),
        in_specs=[
            pl.BlockSpec((1, tm, D_MODEL), lambda g, m, n: (g, m, 0)),
            pl.BlockSpec((1, D_MODEL), lambda g, m, n: (0, 0)),
            _mod_spec(scale, tm),
            _mod_spec(shift, tm),
            pl.BlockSpec((D_MODEL, PROJ_TN), lambda g, m, n: (0, n)),
            pl.BlockSpec((3, tm, LANES), lambda g, m, n: (0, m, 0)),
        ],
        out_specs=pl.BlockSpec((1, tm, PROJ_TN), lambda g, m, n: (g, m, n)),
        out_shape=jax.ShapeDtypeStruct((G, R, P_WIDTH), f32),
        scratch_shapes=[pltpu.VMEM((tm, D_MODEL), bf16)],
        compiler_params=_cparams(("parallel", "parallel", "arbitrary")),
        name="norm_proj",
    )(x, norm_g.reshape(1, D_MODEL), scale, shift, w_bf, tab)


def _rope_tables(pos):
    half = ROT_DIM // 2
    inv_freq = ROPE_THETA ** (-jnp.arange(half, dtype=f32) * (2.0 / ROT_DIM))
    ang = pos.astype(f32)[:, None] * inv_freq[None, :]
    cos, sin = jnp.cos(ang), jnp.sin(ang)
    t = pos.shape[0]
    z8 = jnp.zeros((t, half), f32)
    rest = HEAD_DIM - ROT_DIM
    cos64 = jnp.concatenate([cos, cos, jnp.ones((t, rest), f32)], axis=1)
    up64 = jnp.concatenate([-sin, z8, jnp.zeros((t, rest), f32)], axis=1)
    dn64 = jnp.concatenate([z8, sin, jnp.zeros((t, rest), f32)], axis=1)
    rep = LANES // HEAD_DIM
    return jnp.stack([jnp.tile(a, (1, rep)) for a in (cos64, up64, dn64)])


def _attn_prompt_kernel(sinks_ref, q_ref, kc_ref, kp_ref, vc_ref, vp_ref, ga_ref, o_ref):
    n = pl.program_id(1)
    wn = WINDOW
    half = HEAD_DIM
    q = q_ref[0].astype(bf16)
    k_t = jnp.concatenate([kp_ref[0], kc_ref[0]], axis=0).T.astype(bf16)
    vcat = jnp.concatenate([vp_ref[0], vc_ref[0]], axis=0)
    ga = ga_ref[0]

    qi = lax.broadcasted_iota(jnp.int32, (2 * wn, 2 * wn), 0) & (wn - 1)
    kj = lax.broadcasted_iota(jnp.int32, (2 * wn, 2 * wn), 1)
    rel = wn + qi - kj
    mask = (rel >= 0) & (rel <= wn) & ((kj >= wn) | (n > 0))
    top = lax.broadcasted_iota(jnp.int32, (2 * wn, 1), 0) < wn
    lo = lax.broadcasted_iota(jnp.int32, (2 * wn, LANES), 1) < half
    zeros_k = jnp.zeros((half, 2 * wn), bf16)
    ones_lo = jnp.where(lo, 1.0, 0.0).astype(bf16)
    ones_hi = jnp.where(lo, 0.0, 1.0).astype(bf16)

    for j in range(N_KV_HEADS // 2):
        vblk = vcat[:, j * LANES:(j + 1) * LANES]
        vswap = pltpu.roll(vblk, half, 1)
        for g in (2 * j, 2 * j + 1):
            own, other = (vblk, vswap) if g % 2 == 0 else (vswap, vblk)
            v_lo = jnp.where(lo, own, 0.0).astype(bf16)
            v_hi = jnp.where(lo, 0.0, other).astype(bf16)
            rhs_pv = jnp.concatenate([jnp.concatenate([v_lo, ones_lo], axis=1),
                                      jnp.concatenate([v_hi, ones_hi], axis=1)], axis=0)
            kg = k_t[g * half:(g + 1) * half, :]
            rhs_qk = jnp.concatenate([jnp.concatenate([kg, zeros_k], axis=0),
                                      jnp.concatenate([zeros_k, kg], axis=0)], axis=1)
            b0, b1 = 2 * g, 2 * g + 1
            qg = jnp.concatenate([q[:, b0 * LANES:(b0 + 1) * LANES], q[:, b1 * LANES:(b1 + 1) * LANES]],
                                 axis=0)
            s_all = jnp.dot(qg, rhs_qk, preferred_element_type=f32)
            ps, es = [], []
            for hh in range(2):
                s = jnp.where(mask, s_all[:, hh * 2 * wn:(hh + 1) * 2 * wn], NEG_BIG)
                sink = jnp.where(top, sinks_ref[2 * b0 + hh], sinks_ref[2 * b1 + hh])
                m = jnp.maximum(jnp.max(s, axis=-1, keepdims=True), sink)
                ps.append(jnp.exp(s - m).astype(bf16))
                es.append(jnp.exp(sink - m))
            res = jnp.dot(jnp.concatenate(ps, axis=1), rhs_pv, preferred_element_type=f32)
            out = res[:, :LANES] / (res[:, LANES:] + jnp.where(lo, es[0], es[1]))
            for i, blk in enumerate((b0, b1)):
                sl = slice(blk * LANES, (blk + 1) * LANES)
                o_ref[0, :, sl] = (out[i * wn:(i + 1) * wn] * _silu(ga[:, sl])).astype(o_ref.dtype)


def _attn_prompt(proj, sinks):
    B, T, _ = proj.shape
    nb = T // WINDOW
    kvb = KV_WIDTH
    prev = lambda b, n: jnp.maximum(n - 1, 0)
    return pl.pallas_call(
        _attn_prompt_kernel,
        grid=(B, nb),
        in_specs=[
            pl.BlockSpec(memory_space=pltpu.SMEM),
            pl.BlockSpec((1, WINDOW, ATT_WIDTH), lambda b, n: (b, n, P_Q // ATT_WIDTH)),
            pl.BlockSpec((1, WINDOW, kvb), lambda b, n: (b, n, P_KA // kvb)),
            pl.BlockSpec((1, WINDOW, kvb), lambda b, n: (b, prev(b, n), P_KA // kvb)),
            pl.BlockSpec((1, WINDOW, kvb), lambda b, n: (b, n, P_VA // kvb)),
            pl.BlockSpec((1, WINDOW, kvb), lambda b, n: (b, prev(b, n), P_VA // kvb)),
            pl.BlockSpec((1, WINDOW, ATT_WIDTH), lambda b, n: (b, n, P_GA // ATT_WIDTH)),
        ],
        out_specs=pl.BlockSpec((1, WINDOW, ATT_WIDTH), lambda b, n: (b, n, 0)),
        out_shape=jax.ShapeDtypeStruct((B, T, ATT_WIDTH), bf16),
        compiler_params=_cparams(("parallel", "arbitrary")),
        name="attn_prompt",
    )(sinks, proj, proj, proj, proj, proj, proj)


SAMPLE_ROWS = 16


def _attn_sample_kernel(sinks_ref, q_ref, kn_ref, vn_ref, ga_ref, ck_ref, cv_ref, *refs):
    o_ref, nk_ref, nv_ref = refs[-3:]
    rb = q_ref.shape[1]
    q_all, kn_all, vn_all = q_ref[0], kn_ref[0], vn_ref[0]
    work = [(r, g) for r in range(rb) for g in range(N_KV_HEADS)]
    span = lambda g: slice(g * HEAD_DIM, (g + 1) * HEAD_DIM)
    ckb = [ck_ref[0, r].astype(bf16) for r in range(rb)]
    cvb = [cv_ref[0, r].astype(bf16) for r in range(rb)]
    qg = {(r, g): jnp.concatenate(
        [q_all[r:r + 1, (g * GQA_GROUP + i) * HEAD_DIM:(g * GQA_GROUP + i + 1) * HEAD_DIM]
         for i in range(GQA_GROUP)], axis=0) for r, g in work}
    s = {rg: lax.dot_general(qg[rg].astype(bf16), ckb[rg[0]][:, span(rg[1])], _NT,
                             preferred_element_type=f32) for rg in work}
    outs = {}
    for r, g in work:
        s_new = jnp.sum(qg[r, g] * kn_all[r:r + 1, span(g)], axis=-1, keepdims=True)
        sink = sinks_ref[g * GQA_GROUP:(g + 1) * GQA_GROUP, :]
        m = jnp.maximum(jnp.maximum(jnp.max(s[r, g], axis=-1, keepdims=True), s_new), sink)
        p = jnp.exp(s[r, g] - m)
        p_new = jnp.exp(s_new - m)
        den = jnp.sum(p, axis=-1, keepdims=True) + p_new + jnp.exp(sink - m)
        o = (jnp.dot(p.astype(bf16), cvb[r][:, span(g)], preferred_element_type=f32)
             + p_new * vn_all[r:r + 1, span(g)]) / den
        outs[r, g] = [o[i:i + 1, :] for i in range(GQA_GROUP)]
    att = jnp.concatenate(
        [jnp.concatenate([h for g in range(N_KV_HEADS) for h in outs[r, g]], axis=1) for r in range(rb)],
        axis=0)
    o_ref[0] = (att * _silu(ga_ref[0])).astype(o_ref.dtype)
    last = lax.broadcasted_iota(jnp.int32, (WINDOW, KV_WIDTH), 0) == WINDOW - 1
    for r in range(rb):
        nk_ref[0, r] = jnp.where(last, kn_all[r:r + 1], pltpu.roll(ck_ref[0, r], WINDOW - 1, 0))
        nv_ref[0, r] = jnp.where(last, vn_all[r:r + 1], pltpu.roll(cv_ref[0, r], WINDOW - 1, 0))


def _attn_sample(proj, ck_all, cv_all, layer, stacked, sinks):
    Bd = proj.shape[1]
    rb = SAMPLE_ROWS
    col = lambda w, off: pl.BlockSpec((1, rb, w), lambda i: (0, i, off // w))
    cache = pl.BlockSpec((1, rb, WINDOW, KV_WIDTH), lambda i: (layer, i, 0, 0))
    args = [sinks.reshape(N_Q_HEADS, 1), proj, proj, proj, proj, ck_all, cv_all]
    in_specs = [
        pl.BlockSpec((N_Q_HEADS, 1), lambda i: (0, 0)),
        col(ATT_WIDTH, P_Q), col(KV_WIDTH, P_KA), col(KV_WIDTH, P_VA), col(ATT_WIDTH, P_GA),
        cache, cache,
    ]
    aliases = {}
    if stacked is not None:
        aliases = {len(args): 1, len(args) + 1: 2}
        args.extend(stacked)
        in_specs.extend([pl.BlockSpec(memory_space=pl.ANY)] * 2)
    return pl.pallas_call(
        _attn_sample_kernel,
        grid=(Bd // rb,),
        in_specs=in_specs,
        out_specs=[pl.BlockSpec((1, rb, ATT_WIDTH), lambda i: (0, i, 0)), cache, cache],
        out_shape=[
            jax.ShapeDtypeStruct((1, Bd, ATT_WIDTH), bf16),
            jax.ShapeDtypeStruct(ck_all.shape, f32),
            jax.ShapeDtypeStruct(cv_all.shape, f32),
        ],
        input_output_aliases=aliases,
        compiler_params=_cparams(("parallel",)),
        name="attn_sample",
    )(*args)


def _rwkv_prep_kernel(r_ref, kr_ref, vr_ref, wa_ref, pr_ref, pkr_ref, pvr_ref, pwa_ref,
                      mu_r_ref, mu_kr_ref, mu_vr_ref, mu_wa_ref, w0_ref, a0_ref, kk_ref, ka_ref,
                      wd_ref, wi_ref, ro_ref, wo_ref, ko_ref, vo_ref, kko_ref, ao_ref):
    def mixed(cur_ref, prev_ref, mu_ref):
        cur = cur_ref[0]
        return cur + (prev_ref[0] - cur) * mu_ref[...]

    r = mixed(r_ref, pr_ref, mu_r_ref)
    kr = mixed(kr_ref, pkr_ref, mu_kr_ref)
    vr = mixed(vr_ref, pvr_ref, mu_vr_ref)
    wa = mixed(wa_ref, pwa_ref, mu_wa_ref)
    hi = lax.Precision.HIGHEST
    z = w0_ref[...] + jnp.dot(jnp.tanh(wa), wd_ref[...], precision=hi, preferred_element_type=f32)
    nz = -z
    softplus = jnp.maximum(nz, 0.0) + jnp.log1p(jnp.exp(-jnp.abs(nz)))
    w_log = -softplus - 0.5
    a = jax.nn.sigmoid(a0_ref[...] + jnp.dot(wa, wi_ref[...], precision=hi, preferred_element_type=f32))
    ro_ref[0] = r
    wo_ref[0] = -jnp.exp(w_log)
    ko_ref[0] = kr * (1.0 + (a - 1.0) * ka_ref[...])
    vo_ref[0] = vr
    kko_ref[0] = kr * kk_ref[...]
    ao_ref[0] = a


def _rwkv_prep(proj, prev, mu, w0, a0, k_k, k_a, wd_pad, wi_pad, tm):
    G, R, _ = proj.shape
    W = RWKV_WIDTH
    cur_specs = [
        pl.BlockSpec((1, tm, W), lambda g, m: (g, m, P_R // W)),
        pl.BlockSpec((1, tm, W), lambda g, m: (g, m, P_KR // W)),
        pl.BlockSpec((1, tm, W), lambda g, m: (g, m, P_VR // W)),
        pl.BlockSpec((1, tm, LANES), lambda g, m: (g, m, P_WA // LANES)),
    ]
    prev_specs = [
        pl.BlockSpec((1, tm, W), lambda g, m: (g, m, 0)),
        pl.BlockSpec((1, tm, W), lambda g, m: (g, m, 0)),
        pl.BlockSpec((1, tm, W), lambda g, m: (g, m, 0)),
        pl.BlockSpec((1, tm, LANES), lambda g, m: (g, m, 0)),
    ]
    prev_args = list(prev)
    vec = lambda w: pl.BlockSpec((1, w), lambda g, m: (0, 0))
    lora = pl.BlockSpec((LORA_PAIR, W), lambda g, m: (0, 0))
    out_spec = pl.BlockSpec((1, tm, W), lambda g, m: (g, m, 0))
    return pl.pallas_call(
        _rwkv_prep_kernel,
        grid=(G, R // tm),
        in_specs=cur_specs + prev_specs + [vec(W), vec(W), vec(W), vec(LANES),
                                           vec(W), vec(W), vec(W), vec(W), lora, lora],
        out_specs=[out_spec] * 6,
        out_shape=[jax.ShapeDtypeStruct((G, R, W), f32)] * 6,
        compiler_params=_cparams(("parallel", "arbitrary")),
        name="rwkv_prep",
    )(proj, proj, proj, proj, *prev_args, *mu, w0, a0, k_k, k_a, wd_pad, wi_pad)


def _wkv_kernel(tc, r_ref, w_ref, k_ref, v_ref, kk_ref, a_ref, rk_ref, lnw_ref, lnb_ref, s0_ref,
                *refs):
    y_ref, s_ref = refs[-2:]

    @pl.when(pl.program_id(1) == 0)
    def _():
        s_ref[...] = s0_ref[...]

    eye = (lax.broadcasted_iota(jnp.int32, (RW_HEAD, RW_HEAD), 0)
           == lax.broadcasted_iota(jnp.int32, (RW_HEAD, RW_HEAD), 1)).astype(f32)
    rk = rk_ref[...]
    lnw = lnw_ref[...]
    lnb = lnb_ref[...]

    def step(t, carry):
        r, lw, k, v, kkr, a = (ref[0, t] for ref in (r_ref, w_ref, k_ref, v_ref, kk_ref, a_ref))
        w = jnp.exp(lw)
        norm = jnp.sqrt(jnp.sum(kkr * kkr, axis=-1, keepdims=True))
        kk = kkr / jnp.maximum(norm, 1e-12)
        b = kk * a
        bonus = jnp.sum(r * k * rk, axis=-1, keepdims=True) * v
        rows = []
        for h in range(RW_HEADS):
            hs = slice(h, h + 1)
            S = s_ref[0, 0, h]
            sa = jnp.sum(S * (-kk[hs]), axis=-1, keepdims=True)
            v_col = jnp.sum(eye * v[hs], axis=-1, keepdims=True)
            S = S * w[hs] + sa * b[hs] + v_col * k[hs]
            s_ref[0, 0, h] = S
            y = jnp.sum(S * r[hs], axis=-1, keepdims=True)
            mu = jnp.mean(y, axis=0, keepdims=True)
            var = jnp.mean(jnp.square(y - mu), axis=0, keepdims=True)
            yn = (y - mu) * lax.rsqrt(var + GN_EPS)
            rows.append(jnp.sum(yn * eye, axis=0, keepdims=True))
        y_ref[0, t] = jnp.concatenate(rows, axis=0) * lnw + lnb + bonus
        return carry

    lax.fori_loop(0, tc, step, 0)


def _wkv(r, w, k, v, kk, a, r_k, ln_w, ln_b, s0_all, layer, stacked, tc):
    B, T = r.shape[0], r.shape[1]
    hd = (RW_HEADS, RW_HEAD)
    heads = lambda t: t.reshape(B, T, *hd)
    seq = pl.BlockSpec((1, tc, *hd), lambda b, c: (b, c, 0, 0))
    par = pl.BlockSpec(hd, lambda b, c: (0, 0))
    state = pl.BlockSpec((1, 1, RW_HEADS, RW_HEAD, RW_HEAD), lambda b, c: (layer, b, 0, 0, 0))
    args = [heads(r), heads(w), heads(k), heads(v), heads(kk), heads(a),
            r_k, ln_w.reshape(hd), ln_b.reshape(hd), s0_all]
    in_specs = [seq] * 6 + [par] * 3 + [state]
    aliases = {}
    if stacked is not None:
        aliases = {len(args): 1}
        args.append(stacked)
        in_specs.append(pl.BlockSpec(memory_space=pl.ANY))
    y, s_new = pl.pallas_call(
        functools.partial(_wkv_kernel, tc),
        grid=(B, T // tc),
        in_specs=in_specs,
        out_specs=[seq, state],
        out_shape=[
            jax.ShapeDtypeStruct((B, T, *hd), f32),
            jax.ShapeDtypeStruct(s0_all.shape, f32),
        ],
        input_output_aliases=aliases,
        compiler_params=_cparams(("parallel", "arbitrary")),
        name="wkv_steps",
    )(*args)
    return y.reshape(B, T, RWKV_WIDTH), s_new


CHUNK = 64
PAIR = 2 * RW_HEAD
N_PAIRS = RW_HEADS // 2


def _mmb(a, b, dims=_NN):
    return lax.dot_general(a.astype(bf16), b.astype(bf16), dims, preferred_element_type=f32)


def _lora_dot(x, wh_ref, wl_ref):
    xh, xl = _split_bf16(x)
    rows = x.shape[0]
    both = jnp.dot(jnp.concatenate([xh, xl], axis=0), wh_ref[...], preferred_element_type=f32)
    return both[:rows] + both[rows:] + jnp.dot(xh, wl_ref[...], preferred_element_type=f32)


def _wkv_chunk_kernel(r_ref, kr_ref, vr_ref, wa_ref, gr_ref, pr_ref, pkr_ref, pvr_ref, pwa_ref,
                      mu_r_ref, mu_kr_ref, mu_vr_ref, mu_wa_ref, w0_ref, a0_ref, kk_ref, ka_ref,
                      wdh_ref, wdl_ref, wih_ref, wil_ref, rk_ref, lnw_ref, lnb_ref, s0_ref,
                      y_ref, s_ref, sp_ref, cr_ref, ckr_ref, cvr_ref, cwa_ref):
    C = CHUNK
    c = pl.program_id(1)

    @pl.when(c == 0)
    def _():
        for p in range(N_PAIRS):
            sp_ref[p] = jnp.concatenate([s0_ref[0, 2 * p], s0_ref[0, 2 * p + 1]], axis=1)
        for carry, first in ((cr_ref, pr_ref), (ckr_ref, pkr_ref), (cvr_ref, pvr_ref), (cwa_ref, pwa_ref)):
            carry[...] = first[0]

    def mixed(cur_ref, carry_ref, mu_ref):
        cur = cur_ref[0]
        first = lax.broadcasted_iota(jnp.int32, cur.shape, 0) == 0
        prev = jnp.where(first, carry_ref[...], pltpu.roll(cur, 1, 0))
        carry_ref[...] = cur[C - 1:C, :]
        return cur + (prev - cur) * mu_ref[...]

    r_all = mixed(r_ref, cr_ref, mu_r_ref)
    kr_all = mixed(kr_ref, ckr_ref, mu_kr_ref)
    v_all = mixed(vr_ref, cvr_ref, mu_vr_ref)
    wa = mixed(wa_ref, cwa_ref, mu_wa_ref)
    nz = -(w0_ref[...] + _lora_dot(jnp.tanh(wa), wdh_ref, wdl_ref))
    softplus = jnp.maximum(nz, 0.0) + jnp.log1p(jnp.exp(-jnp.abs(nz)))
    lw = -jnp.exp(-softplus - 0.5)
    icl_all = jax.nn.sigmoid(a0_ref[...] + _lora_dot(wa, wih_ref, wil_ref))
    k_all = kr_all * (1.0 + (icl_all - 1.0) * ka_ref[...])
    kkr_all = kr_all * kk_ref[...]

    row = lax.broadcasted_iota(jnp.int32, (PAIR, PAIR), 0)
    col = lax.broadcasted_iota(jnp.int32, (PAIR, PAIR), 1)
    tril = row >= col
    stril = row > col
    eye = (row == col).astype(f32)
    lane_lo = lax.broadcasted_iota(jnp.int32, (C, PAIR), 1) < RW_HEAD

    def bd(x):
        zero = jnp.zeros_like(x)
        return jnp.concatenate([jnp.where(lane_lo, x, zero), jnp.where(lane_lo, zero, x)], axis=0)

    def head_sums(x):
        lo_sum = jnp.sum(jnp.where(lane_lo, x, 0.0), axis=-1, keepdims=True)
        hi_sum = jnp.sum(jnp.where(lane_lo, 0.0, x), axis=-1, keepdims=True)
        return jnp.where(lane_lo, lo_sum, hi_sum)

    width = lw.shape[1]
    lw_a = lw.astype(bf16)
    rest = lw - lw_a.astype(f32)
    lw_b = rest.astype(bf16)
    lw_c = (rest - lw_b.astype(f32)).astype(bf16)
    g3 = jnp.dot(tril[:C, :C].astype(bf16), jnp.concatenate([lw_a, lw_b, lw_c], axis=1),
                 preferred_element_type=f32)
    g = g3[:, :width] + g3[:, width:2 * width] + g3[:, 2 * width:]
    e_g = jnp.exp(g)
    e_ng = jnp.exp(-g)
    e_gm = jnp.exp(g - lw)
    e_end = e_g[C - 1:C, :]

    pairs = range(N_PAIRS)
    sls = [slice(p * PAIR, (p + 1) * PAIR) for p in pairs]
    ins = [[t[:, sl] for t in (r_all, k_all, v_all, kkr_all, icl_all)] for sl in sls]
    norms = [jnp.sqrt(head_sums(x[3] * x[3])) for x in ins]
    at, rt, bt, kt, vb = ([] for _ in range(5))
    for (r, k, v, kkr, icl), norm, sl in zip(ins, norms, sls):
        kk = kkr / jnp.maximum(norm, 1e-12)
        b = kk * icl
        at.append(bd((-kk * e_gm[:, sl]).astype(bf16)))
        rt.append(bd((r * e_g[:, sl]).astype(bf16)))
        bt.append(bd((b * e_ng[:, sl]).astype(bf16)))
        kt.append(bd((k * e_ng[:, sl]).astype(bf16)))
        vb.append(bd(v.astype(bf16)))

    tril2 = jnp.concatenate([tril, tril], axis=1)
    gram = [_mmb(jnp.concatenate([at[p], rt[p]], axis=0), jnp.concatenate([bt[p], kt[p]], axis=0), _NT)
            for p in pairs]
    lmat = [jnp.where(stril, gm[:PAIR, :PAIR], 0.0) for gm in gram]
    mv = [_mmb(jnp.where(stril, gram[p][:PAIR, PAIR:], 0.0), vb[p]) for p in pairs]
    lower = [jnp.where(tril2, gm[PAIR:, :], 0.0).astype(bf16) for gm in gram]

    tinv = [eye + lm for lm in lmat]
    pw = [_mmb(lm, lm) for lm in lmat]
    for _ in range(4):
        z = [_mmb(jnp.concatenate([x.astype(bf16), t.astype(bf16)], axis=0), x) for t, x in zip(tinv, pw)]
        pw = [zz[:PAIR] for zz in z]
        tinv = [t + zz[PAIR:] for t, zz in zip(tinv, z)]
    tinv = [t + _mmb(t, x) for t, x in zip(tinv, pw)]

    wx = [_mmb(tinv[p], jnp.concatenate([at[p], mv[p].astype(bf16)], axis=1)) for p in pairs]
    s_old = [sp_ref[p] for p in pairs]
    uy0 = [_mmb(jnp.concatenate([wx[p][:, :PAIR].astype(bf16), rt[p]], axis=0),
                bd(s_old[p].astype(bf16)), _NT) for p in pairs]
    uv = [jnp.concatenate([(uy0[p][:PAIR] + wx[p][:, PAIR:]).astype(bf16), vb[p]], axis=0) for p in pairs]
    ys = [uy0[p][PAIR:] + _mmb(lower[p], uv[p]) for p in pairs]
    s_add = [_mmb(uv[p], jnp.concatenate([bt[p], kt[p]], axis=0), _TN) for p in pairs]
    for p in pairs:
        sp_ref[p] = (s_old[p] + s_add[p][:RW_HEAD] + s_add[p][RW_HEAD:]) * e_end[:, sls[p]]

    ys = [y[:C] + y[C:] for y in ys]
    mus = [head_sums(y) * (1.0 / RW_HEAD) for y in ys]
    ds = [y - mu for y, mu in zip(ys, mus)]
    var = [head_sums(d * d) * (1.0 / RW_HEAD) for d in ds]
    bonus = [head_sums(x[0] * x[1] * rk_ref[:, sl]) * x[2] for x, sl in zip(ins, sls)]
    for p in pairs:
        sl = sls[p]
        y_rw = ds[p] * lax.rsqrt(var[p] + GN_EPS) * lnw_ref[:, sl] + lnb_ref[:, sl] + bonus[p]
        y_ref[0, :, sl] = (y_rw * _silu(gr_ref[0, :, sl])).astype(y_ref.dtype)

    @pl.when(c == pl.num_programs(1) - 1)
    def _():
        for p in range(N_PAIRS):
            s = sp_ref[p]
            s_ref[0, 2 * p] = s[:, :RW_HEAD]
            s_ref[0, 2 * p + 1] = s[:, RW_HEAD:]


def _wkv_chunked(proj, prev, mu, w0, a0, k_k, k_a, lora, r_k, ln_w, ln_b, s0):
    B, T, _ = proj.shape
    W = RWKV_WIDTH
    col = lambda w, off: pl.BlockSpec((1, CHUNK, w), lambda b, c: (b, c, off // w))
    first = lambda w: pl.BlockSpec((1, 1, w), lambda b, c: (b, 0, 0))
    vec = lambda w: pl.BlockSpec((1, w), lambda b, c: (0, 0))
    lora_spec = pl.BlockSpec((LORA_PAIR, W), lambda b, c: (0, 0))
    state = pl.BlockSpec((1, RW_HEADS, RW_HEAD, RW_HEAD), lambda b, c: (b, 0, 0, 0))
    return pl.pallas_call(
        _wkv_chunk_kernel,
        grid=(B, T // CHUNK),
        in_specs=[col(W, P_R), col(W, P_KR), col(W, P_VR), col(LANES, P_WA), col(W, P_GR),
                  first(W), first(W), first(W), first(LANES),
                  vec(W), vec(W), vec(W), vec(LANES), vec(W), vec(W), vec(W), vec(W),
                  lora_spec, lora_spec, lora_spec, lora_spec, vec(W), vec(W), vec(W), state],
        out_specs=[pl.BlockSpec((1, CHUNK, W), lambda b, c: (b, c, 0)), state],
        out_shape=[
            jax.ShapeDtypeStruct((B, T, W), bf16),
            jax.ShapeDtypeStruct((B, RW_HEADS, RW_HEAD, RW_HEAD), f32),
        ],
        scratch_shapes=[pltpu.VMEM((N_PAIRS, RW_HEAD, PAIR), f32),
                        pltpu.VMEM((1, W), f32), pltpu.VMEM((1, W), f32), pltpu.VMEM((1, W), f32),
                        pltpu.VMEM((1, LANES), f32)],
        compiler_params=_cparams(("parallel", "arbitrary")),
        name="wkv_chunks",
    )(proj, proj, proj, proj, proj, *prev, *mu, w0, a0, k_k, k_a, *lora,
      r_k.reshape(1, W), ln_w.reshape(1, W), ln_b.reshape(1, W), s0)


def _post_kernel(final, gated, att_ref, y_ref, *refs):
    if gated:
        y = y_ref[0]
    else:
        y = (y_ref[0] * _silu(refs[0][0])).astype(bf16)
        refs = refs[1:]
    x_ref, gate_ref, w_ref, fg_ref, o_ref = refs
    cat = jnp.concatenate([att_ref[0], y], axis=1)
    out = jnp.dot(cat, w_ref[...], preferred_element_type=f32)
    x = x_ref[0] + gate_ref[0] * out
    if final:
        ms = jnp.mean(x * x, axis=-1, keepdims=True)
        x = x * lax.rsqrt(ms + NORM_EPS) * fg_ref[...]
    o_ref[0] = x


def _post(att, y_rw, proj, x, gate, w_out_bf, final_g, final, tm):
    G, R, _ = x.shape
    W = RWKV_WIDTH
    gated = proj is None
    gate_in = [] if gated else [pl.BlockSpec((1, tm, W), lambda g, m: (g, m, P_GR // W))]
    gate_arg = [] if gated else [proj]
    return pl.pallas_call(
        functools.partial(_post_kernel, final, gated),
        grid=(G, R // tm),
        in_specs=[
            pl.BlockSpec((1, tm, ATT_WIDTH), lambda g, m: (g, m, 0)),
            pl.BlockSpec((1, tm, W), lambda g, m: (g, m, 0)),
            *gate_in,
            pl.BlockSpec((1, tm, D_MODEL), lambda g, m: (g, m, 0)),
            _mod_spec(gate, tm),
            pl.BlockSpec((D_MODEL, D_MODEL), lambda g, m: (0, 0)),
            pl.BlockSpec((1, D_MODEL), lambda g, m: (0, 0)),
        ],
        out_specs=pl.BlockSpec((1, tm, D_MODEL), lambda g, m: (g, m, 0)),
        out_shape=jax.ShapeDtypeStruct((G, R, D_MODEL), f32),
        compiler_params=_cparams(("parallel", "parallel")),
        name="post_proj",
    )(att, y_rw, *gate_arg, x, gate, w_out_bf, final_g.reshape(1, D_MODEL))


def _arrange_w_in(w):
    pad = jnp.zeros((w.shape[0], P_WIDTH - IN_WIDTH), w.dtype)
    parts = [w[:, Q_OFF:KA_OFF], w[:, GA_OFF:GR_OFF], w[:, GR_OFF:IN_WIDTH], w[:, R_OFF:WD_OFF],
             w[:, KA_OFF:R_OFF], w[:, WD_OFF:GA_OFF], pad]
    return jnp.concatenate(parts, axis=1).astype(bf16)


def _shift_cols(t):
    return jnp.concatenate([t[..., P_R:P_KA], t[..., P_WA:P_WA + LORA_PAIR]], axis=-1)


def kernel(x_prompt, x_sample, cache_k, cache_v, state_wkv, state_shift, c_prompt, c_sample,
           norm_g, w_ada, b_ada, w_in, mu_shift, w0, w_decay, a0, w_iclr, k_k, k_a, r_k,
           ln_w, ln_b, sinks, w_out, final_g):
    Bp, Tp = x_prompt.shape[0], x_prompt.shape[1]
    Bd = x_sample.shape[0]
    W = RWKV_WIDTH

    n_c = Bp + Bd
    c_rows = -(-n_c // 16) * 16
    c_all = jnp.concatenate([c_prompt, c_sample, jnp.zeros((c_rows - n_c, D_MODEL), f32)], axis=0)
    mod = _ada(c_all, w_ada, b_ada)

    tab_p = _rope_tables(jnp.arange(Tp, dtype=jnp.int32))
    tab_s = _rope_tables(jnp.full((Bd,), PAST_LEN, jnp.int32))

    hp = x_prompt
    hs = x_sample.reshape(1, Bd, D_MODEL)
    s0_p = jnp.zeros((Bp, RW_HEADS, RW_HEAD, RW_HEAD), f32)
    shift0_p = [jnp.zeros((Bp, 1, w), f32) for w in (W, W, W, LORA_PAIR)]
    new_state_s = new_cache_s = None
    ck_all = cache_k.reshape(DEPTH, Bd, WINDOW, KV_WIDTH)
    cv_all = cache_v.reshape(DEPTH, Bd, WINDOW, KV_WIDTH)
    outs = {k: [] for k in ("kp", "vp", "sp", "shp", "shs")}
    for l in range(DEPTH):
        final = l == DEPTH - 1
        w_bf = _arrange_w_in(w_in[l])
        w_out_bf = w_out[l].astype(bf16)
        mu_l = mu_shift[l]
        mu = [mu_l[0:W].reshape(1, W), mu_l[W:2 * W].reshape(1, W), mu_l[2 * W:3 * W].reshape(1, W),
              mu_l[3 * W:].reshape(1, LORA_PAIR)]
        vecs = [t[l].reshape(1, W) for t in (w0, a0, k_k, k_a)]
        wd_pad = jnp.concatenate([w_decay[l], jnp.zeros((ICLR_LORA, W), f32)], axis=0)
        wi_pad = jnp.concatenate([jnp.zeros((DECAY_LORA, W), f32), w_iclr[l]], axis=0)
        shift_p, scale_p, gate_p = (mod[l, :Bp, i * D_MODEL:(i + 1) * D_MODEL].reshape(Bp, 1, D_MODEL)
                                    for i in range(3))
        shift_s, scale_s, gate_s = (mod[l, Bp:n_c, i * D_MODEL:(i + 1) * D_MODEL].reshape(1, Bd, D_MODEL)
                                    for i in range(3))

        proj = _norm_proj(hp, norm_g[l], scale_p, shift_p, w_bf, tab_p, tm=1024)
        att = _attn_prompt(proj, sinks[l])
        lora = [piece for wp in (wd_pad, wi_pad) for piece in _split_bf16(wp)]
        y_rw, s_t = _wkv_chunked(proj, shift0_p, mu, *vecs, lora, r_k[l], ln_w[l], ln_b[l], s0_p)
        hp = _post(att, y_rw, None, hp, gate_p, w_out_bf, final_g, final, tm=512)
        tail = proj[:, Tp - WINDOW:]
        outs["kp"].append(tail[..., P_KA:P_KA + KV_WIDTH].reshape(Bp, WINDOW, N_KV_HEADS, HEAD_DIM))
        outs["vp"].append(tail[..., P_VA:P_VA + KV_WIDTH].reshape(Bp, WINDOW, N_KV_HEADS, HEAD_DIM))
        outs["sp"].append(s_t)
        outs["shp"].append(_shift_cols(proj[:, Tp - 1]))

        proj = _norm_proj(hs, norm_g[l], scale_s, shift_s, w_bf, tab_s, tm=Bd)
        att, *new_cache_s = _attn_sample(proj, ck_all, cv_all, l, new_cache_s, sinks[l])
        sh = state_shift[l]
        prev = [sh[None, :, 0:W], sh[None, :, W:2 * W], sh[None, :, 2 * W:3 * W], sh[None, :, 3 * W:]]
        prep = _rwkv_prep(proj, prev, mu, *vecs, wd_pad, wi_pad, tm=Bd)
        prep = [t.reshape(Bd, 1, W) for t in prep]
        y_rw, new_state_s = _wkv(*prep, r_k[l], ln_w[l], ln_b[l], state_wkv, l, new_state_s, tc=1)
        hs = _post(att, y_rw.reshape(1, Bd, W), proj, hs, gate_s, w_out_bf, final_g, final, tm=Bd)
        outs["shs"].append(_shift_cols(proj[0]))

    st = lambda k: jnp.stack(outs[k])
    return (hp, hs.reshape(Bd, 1, D_MODEL), st("kp"), st("vp"), st("sp"), st("shp"),
            *(t.reshape(cache_k.shape) for t in new_cache_s), new_state_s, st("shs"))
```

```python
import functools

import jax
import jax.numpy as jnp
from jax import lax
from jax.experimental import pallas as pl
from jax.experimental.pallas import tpu as pltpu

f32 = jnp.float32
bf16 = jnp.bfloat16

D_MODEL = 2048
DEPTH = 2
PAST_LEN = 16384
ATT_WIDTH = 1024
RWKV_WIDTH = 1024
HEAD_DIM = 64
N_Q_HEADS = 16
N_KV_HEADS = 4
GQA_GROUP = 4
KV_WIDTH = 256
WINDOW = 128
ROT_DIM = 16
ROPE_THETA = 500000.0
RW_HEAD = 64
RW_HEADS = 16
DECAY_LORA = 64
ICLR_LORA = 64
LORA_PAIR = DECAY_LORA + ICLR_LORA
NORM_EPS = 1e-5
GN_EPS = 64e-5
NEG_BIG = -1e30

Q_OFF = 0
KA_OFF = Q_OFF + ATT_WIDTH
VA_OFF = KA_OFF + KV_WIDTH
R_OFF = VA_OFF + KV_WIDTH
KR_OFF = R_OFF + RWKV_WIDTH
VR_OFF = KR_OFF + RWKV_WIDTH
WD_OFF = VR_OFF + RWKV_WIDTH
AD_OFF = WD_OFF + DECAY_LORA
GA_OFF = AD_OFF + ICLR_LORA
GR_OFF = GA_OFF + ATT_WIDTH
IN_WIDTH = GR_OFF + RWKV_WIDTH
SHIFT_DIM = GA_OFF - R_OFF

LANES = 128
P_Q = 0
P_GA = 1024
P_GR = 2048
P_R = 3072
P_KR = 4096
P_VR = 5120
P_KA = 6144
P_VA = 6400
P_WA = 6656
P_WIDTH = 7168
PROJ_TN = 1024

VMEM_LIMIT = 56 * 1024 * 1024


_NN = (((1,), (0,)), ((), ()))
_NT = (((1,), (1,)), ((), ()))
_TN = (((0,), (0,)), ((), ()))


def _silu(x):
    return x * jax.nn.sigmoid(x)


def _cparams(sem):
    return pltpu.CompilerParams(dimension_semantics=sem, vmem_limit_bytes=VMEM_LIMIT)


def _split_bf16(x):
    hi = x.astype(bf16)
    return hi, (x - hi.astype(f32)).astype(bf16)


def _ada_kernel(c_ref, w_ref, b_ref, o_ref):
    ch, cl = _split_bf16(_silu(c_ref[...]))
    wh, wl = _split_bf16(w_ref[0])
    rows = ch.shape[0]
    both = jnp.dot(jnp.concatenate([ch, cl], axis=0), wh, preferred_element_type=f32)
    o_ref[0] = both[:rows] + both[rows:] + jnp.dot(ch, wl, preferred_element_type=f32) + b_ref[0]


def _ada(c_all, w_ada, b_ada):
    rows = c_all.shape[0]
    tn = 768
    n_out = w_ada.shape[2]
    return pl.pallas_call(
        _ada_kernel,
        grid=(DEPTH, n_out // tn),
        in_specs=[
            pl.BlockSpec((rows, D_MODEL), lambda l, n: (0, 0)),
            pl.BlockSpec((1, D_MODEL, tn), lambda l, n: (l, 0, n)),
            pl.BlockSpec((1, 1, tn), lambda l, n: (l, 0, n)),
        ],
        out_specs=pl.BlockSpec((1, rows, tn), lambda l, n: (l, 0, n)),
        out_shape=jax.ShapeDtypeStruct((DEPTH, rows, n_out), f32),
        compiler_params=_cparams(("parallel", "parallel")),
        name="ada_mod",
    )(c_all, w_ada, b_ada.reshape(DEPTH, 1, n_out))


def _rope(x, tab):
    w = x.shape[1]
    reps = w // LANES
    cosf, up, dn = (jnp.concatenate([tab[i]] * reps, axis=1) for i in range(3))
    half = ROT_DIM // 2
    return x * cosf + pltpu.roll(x, w - half, 1) * up + pltpu.roll(x, half, 1) * dn


def _norm_proj_kernel(x_ref, g_ref, scale_ref, shift_ref, w_ref, tab_ref, o_ref, h_ref):
    n = pl.program_id(2)

    @pl.when(n == 0)
    def _():
        x = x_ref[0]
        ms = jnp.mean(x * x, axis=-1, keepdims=True)
        y = x * lax.rsqrt(ms + NORM_EPS) * g_ref[...]
        h_ref[...] = (y * (1.0 + scale_ref[0]) + shift_ref[0]).astype(bf16)

    q_tile = P_Q // PROJ_TN
    k_tile = P_KA // PROJ_TN
    tm = h_ref.shape[0]
    rc = min(tm, 256)

    def rows(i):
        rs = slice(i * rc, (i + 1) * rc)
        return rs, jnp.dot(h_ref[rs, :], w_ref[...], preferred_element_type=f32)

    @pl.when(n == q_tile)
    def _():
        for i in range(tm // rc):
            rs, res = rows(i)
            o_ref[0, rs, :] = _rope(res, tab_ref[:, rs, :]) * (HEAD_DIM ** -0.5)

    @pl.when(n == k_tile)
    def _():
        for i in range(tm // rc):
            rs, res = rows(i)
            o_ref[0, rs, :] = jnp.concatenate(
                [_rope(res[:, :KV_WIDTH], tab_ref[:, rs, :]), res[:, KV_WIDTH:]], axis=1)

    @pl.when((n != q_tile) & (n != k_tile))
    def _():
        o_ref[0] = jnp.dot(h_ref[...], w_ref[...], preferred_element_type=f32)


def _mod_spec(mod, tm):
    if mod.shape[1] == 1:
        return pl.BlockSpec((1, 1, D_MODEL), lambda g, m, *_: (g, 0, 0))
    return pl.BlockSpec((1, tm, D_MODEL), lambda g, m, *_: (g, m, 0))


def _norm_proj(x, norm_g, scale, shift, w_bf, tab, tm):
    G, R, _ = x.shape
    return pl.pallas_call(
        _norm_proj_kernel,
        grid=(G, R // tm, P_WIDTH // PROJ_TN),
        in_specs=[
            pl.BlockSpec((1, tm, D_MODEL), lambda g, m, n: (g, m, 0)),
            pl.BlockSpec((1, D_MODEL), lambda g, m, n: (0, 0)),
            _mod_spec(scale, tm),
            _mod_spec(shift, tm),
            pl.BlockSpec((D_MODEL, PROJ_TN), lambda g, m, n: (0, n)),
            pl.BlockSpec((3, tm, LANES), lambda g, m, n: (0, m, 0)),
        ],
        out_specs=pl.BlockSpec((1, tm, PROJ_TN), lambda g, m, n: (g, m, n)),
        out_shape=jax.ShapeDtypeStruct((G, R, P_WIDTH), f32),
        scratch_shapes=[pltpu.VMEM((tm, D_MODEL), bf16)],
        compiler_params=_cparams(("parallel", "parallel", "arbitrary")),
        name="norm_proj",
    )(x, norm_g.reshape(1, D_MODEL), scale, shift, w_bf, tab)


def _rope_tables(pos):
    half = ROT_DIM // 2
    inv_freq = ROPE_THETA ** (-jnp.arange(half, dtype=f32) * (2.0 / ROT_DIM))
    ang = pos.astype(f32)[:, None] * inv_freq[None, :]
    cos, sin = jnp.cos(ang), jnp.sin(ang)
    t = pos.shape[0]
    z8 = jnp.zeros((t, half), f32)
    rest = HEAD_DIM - ROT_DIM
    cos64 = jnp.concatenate([cos, cos, jnp.ones((t, rest), f32)], axis=1)
    up64 = jnp.concatenate([-sin, z8, jnp.zeros((t, rest), f32)], axis=1)
    dn64 = jnp.concatenate([z8, sin, jnp.zeros((t, rest), f32)], axis=1)
    rep = LANES // HEAD_DIM
    return jnp.stack([jnp.tile(a, (1, rep)) for a in (cos64, up64, dn64)])


def _attn_prompt_kernel(sinks_ref, q_ref, kc_ref, kp_ref, vc_ref, vp_ref, ga_ref, o_ref):
    n = pl.program_id(1)
    wn = WINDOW
    half = HEAD_DIM
    q = q_ref[0].astype(bf16)
    k_t = jnp.concatenate([kp_ref[0], kc_ref[0]], axis=0).T.astype(bf16)
    vcat = jnp.concatenate([vp_ref[0], vc_ref[0]], axis=0)
    ga = ga_ref[0]

    qi = lax.broadcasted_iota(jnp.int32, (2 * wn, 2 * wn), 0) & (wn - 1)
    kj = lax.broadcasted_iota(jnp.int32, (2 * wn, 2 * wn), 1)
    rel = wn + qi - kj
    mask = (rel >= 0) & (rel <= wn) & ((kj >= wn) | (n > 0))
    top = lax.broadcasted_iota(jnp.int32, (2 * wn, 1), 0) < wn
    lo = lax.broadcasted_iota(jnp.int32, (2 * wn, LANES), 1) < half
    zeros_k = jnp.zeros((half, 2 * wn), bf16)
    ones_lo = jnp.where(lo, 1.0, 0.0).astype(bf16)
    ones_hi = jnp.where(lo, 0.0, 1.0).astype(bf16)

    for j in range(N_KV_HEADS // 2):
        vblk = vcat[:, j * LANES:(j + 1) * LANES]
        vswap = pltpu.roll(vblk, half, 1)
        for g in (2 * j, 2 * j + 1):
            own, other = (vblk, vswap) if g % 2 == 0 else (vswap, vblk)
            v_lo = jnp.where(lo, own, 0.0).astype(bf16)
            v_hi = jnp.where(lo, 0.0, other).astype(bf16)
            rhs_pv = jnp.concatenate([jnp.concatenate([v_lo, ones_lo], axis=1),
                                      jnp.concatenate([v_hi, ones_hi], axis=1)], axis=0)
            kg = k_t[g * half:(g + 1) * half, :]
            rhs_qk = jnp.concatenate([jnp.concatenate([kg, zeros_k], axis=0),
                                      jnp.concatenate([zeros_k, kg], axis=0)], axis=1)
            b0, b1 = 2 * g, 2 * g + 1
            qg = jnp.concatenate([q[:, b0 * LANES:(b0 + 1) * LANES], q[:, b1 * LANES:(b1 + 1) * LANES]],
                                 axis=0)
            s_all = jnp.dot(qg, rhs_qk, preferred_element_type=f32)
            ps, es = [], []
            for hh in range(2):
                s = jnp.where(mask, s_all[:, hh * 2 * wn:(hh + 1) * 2 * wn], NEG_BIG)
                sink = jnp.where(top, sinks_ref[2 * b0 + hh], sinks_ref[2 * b1 + hh])
                m = jnp.maximum(jnp.max(s, axis=-1, keepdims=True), sink)
                ps.append(jnp.exp(s - m).astype(bf16))
                es.append(jnp.exp(sink - m))
            res = jnp.dot(jnp.concatenate(ps, axis=1), rhs_pv, preferred_element_type=f32)
            out = res[:, :LANES] / (res[:, LANES:] + jnp.where(lo, es[0], es[1]))
            for i, blk in enumerate((b0, b1)):
                sl = slice(blk * LANES, (blk + 1) * LANES)
                o_ref[0, :, sl] = (out[i * wn:(i + 1) * wn] * _silu(ga[:, sl])).astype(o_ref.dtype)


def _attn_prompt(proj, sinks):
    B, T, _ = proj.shape
    nb = T // WINDOW
    kvb = KV_WIDTH
    prev = lambda b, n: jnp.maximum(n - 1, 0)
    return pl.pallas_call(
        _attn_prompt_kernel,
        grid=(B, nb),
        in_specs=[
            pl.BlockSpec(memory_space=pltpu.SMEM),
            pl.BlockSpec((1, WINDOW, ATT_WIDTH), lambda b, n: (b, n, P_Q // ATT_WIDTH)),
            pl.BlockSpec((1, WINDOW, kvb), lambda b, n: (b, n, P_KA // kvb)),
            pl.BlockSpec((1, WINDOW, kvb), lambda b, n: (b, prev(b, n), P_KA // kvb)),
            pl.BlockSpec((1, WINDOW, kvb), lambda b, n: (b, n, P_VA // kvb)),
            pl.BlockSpec((1, WINDOW, kvb), lambda b, n: (b, prev(b, n), P_VA // kvb)),
            pl.BlockSpec((1, WINDOW, ATT_WIDTH), lambda b, n: (b, n, P_GA // ATT_WIDTH)),
        ],
        out_specs=pl.BlockSpec((1, WINDOW, ATT_WIDTH), lambda b, n: (b, n, 0)),
        out_shape=jax.ShapeDtypeStruct((B, T, ATT_WIDTH), bf16),
        compiler_params=_cparams(("parallel", "arbitrary")),
        name="attn_prompt",
    )(sinks, proj, proj, proj, proj, proj, proj)


SAMPLE_ROWS = 16


def _attn_sample_kernel(sinks_ref, q_ref, kn_ref, vn_ref, ga_ref, ck_ref, cv_ref, *refs):
    o_ref, nk_ref, nv_ref = refs[-3:]
    rb = q_ref.shape[1]
    q_all, kn_all, vn_all = q_ref[0], kn_ref[0], vn_ref[0]
    work = [(r, g) for r in range(rb) for g in range(N_KV_HEADS)]
    span = lambda g: slice(g * HEAD_DIM, (g + 1) * HEAD_DIM)
    ckb = [ck_ref[0, r].astype(bf16) for r in range(rb)]
    cvb = [cv_ref[0, r].astype(bf16) for r in range(rb)]
    qg = {(r, g): jnp.concatenate(
        [q_all[r:r + 1, (g * GQA_GROUP + i) * HEAD_DIM:(g * GQA_GROUP + i + 1) * HEAD_DIM]
         for i in range(GQA_GROUP)], axis=0) for r, g in work}
    s = {rg: lax.dot_general(qg[rg].astype(bf16), ckb[rg[0]][:, span(rg[1])], _NT,
                             preferred_element_type=f32) for rg in work}
    outs = {}
    for r, g in work:
        s_new = jnp.sum(qg[r, g] * kn_all[r:r + 1, span(g)], axis=-1, keepdims=True)
        sink = sinks_ref[g * GQA_GROUP:(g + 1) * GQA_GROUP, :]
        m = jnp.maximum(jnp.maximum(jnp.max(s[r, g], axis=-1, keepdims=True), s_new), sink)
        p = jnp.exp(s[r, g] - m)
        p_new = jnp.exp(s_new - m)
        den = jnp.sum(p, axis=-1, keepdims=True) + p_new + jnp.exp(sink - m)
        o = (jnp.dot(p.astype(bf16), cvb[r][:, span(g)], preferred_element_type=f32)
             + p_new * vn_all[r:r + 1, span(g)]) / den
        outs[r, g] = [o[i:i + 1, :] for i in range(GQA_GROUP)]
    att = jnp.concatenate(
        [jnp.concatenate([h for g in range(N_KV_HEADS) for h in outs[r, g]], axis=1) for r in range(rb)],
        axis=0)
    o_ref[0] = (att * _silu(ga_ref[0])).astype(o_ref.dtype)
    last = lax.broadcasted_iota(jnp.int32, (WINDOW, KV_WIDTH), 0) == WINDOW - 1
    for r in range(rb):
        nk_ref[0, r] = jnp.where(last, kn_all[r:r + 1], pltpu.roll(ck_ref[0, r], WINDOW - 1, 0))
        nv_ref[0, r] = jnp.where(last, vn_all[r:r + 1], pltpu.roll(cv_ref[0, r], WINDOW - 1, 0))


def _attn_sample(proj, ck_all, cv_all, layer, stacked, sinks):
    Bd = proj.shape[1]
    rb = SAMPLE_ROWS
    col = lambda w, off: pl.BlockSpec((1, rb, w), lambda i: (0, i, off // w))
    cache = pl.BlockSpec((1, rb, WINDOW, KV_WIDTH), lambda i: (layer, i, 0, 0))
    args = [sinks.reshape(N_Q_HEADS, 1), proj, proj, proj, proj, ck_all, cv_all]
    in_specs = [
        pl.BlockSpec((N_Q_HEADS, 1), lambda i: (0, 0)),
        col(ATT_WIDTH, P_Q), col(KV_WIDTH, P_KA), col(KV_WIDTH, P_VA), col(ATT_WIDTH, P_GA),
        cache, cache,
    ]
    aliases = {}
    if stacked is not None:
        aliases = {len(args): 1, len(args) + 1: 2}
        args.extend(stacked)
        in_specs.extend([pl.BlockSpec(memory_space=pl.ANY)] * 2)
    return pl.pallas_call(
        _attn_sample_kernel,
        grid=(Bd // rb,),
        in_specs=in_specs,
        out_specs=[pl.BlockSpec((1, rb, ATT_WIDTH), lambda i: (0, i, 0)), cache, cache],
        out_shape=[
            jax.ShapeDtypeStruct((1, Bd, ATT_WIDTH), bf16),
            jax.ShapeDtypeStruct(ck_all.shape, f32),
            jax.ShapeDtypeStruct(cv_all.shape, f32),
        ],
        input_output_aliases=aliases,
        compiler_params=_cparams(("parallel",)),
        name="attn_sample",
    )(*args)


def _rwkv_prep_kernel(r_ref, kr_ref, vr_ref, wa_ref, pr_ref, pkr_ref, pvr_ref, pwa_ref,
                      mu_r_ref, mu_kr_ref, mu_vr_ref, mu_wa_ref, w0_ref, a0_ref, kk_ref, ka_ref,
                      wd_ref, wi_ref, ro_ref, wo_ref, ko_ref, vo_ref, kko_ref, ao_ref):
    def mixed(cur_ref, prev_ref, mu_ref):
        cur = cur_ref[0]
        return cur + (prev_ref[0] - cur) * mu_ref[...]

    r = mixed(r_ref, pr_ref, mu_r_ref)
    kr = mixed(kr_ref, pkr_ref, mu_kr_ref)
    vr = mixed(vr_ref, pvr_ref, mu_vr_ref)
    wa = mixed(wa_ref, pwa_ref, mu_wa_ref)
    hi = lax.Precision.HIGHEST
    z = w0_ref[...] + jnp.dot(jnp.tanh(wa), wd_ref[...], precision=hi, preferred_element_type=f32)
    nz = -z
    softplus = jnp.maximum(nz, 0.0) + jnp.log1p(jnp.exp(-jnp.abs(nz)))
    w_log = -softplus - 0.5
    a = jax.nn.sigmoid(a0_ref[...] + jnp.dot(wa, wi_ref[...], precision=hi, preferred_element_type=f32))
    ro_ref[0] = r
    wo_ref[0] = -jnp.exp(w_log)
    ko_ref[0] = kr * (1.0 + (a - 1.0) * ka_ref[...])
    vo_ref[0] = vr
    kko_ref[0] = kr * kk_ref[...]
    ao_ref[0] = a


def _rwkv_prep(proj, prev, mu, w0, a0, k_k, k_a, wd_pad, wi_pad, tm):
    G, R, _ = proj.shape
    W = RWKV_WIDTH
    cur_specs = [
        pl.BlockSpec((1, tm, W), lambda g, m: (g, m, P_R // W)),
        pl.BlockSpec((1, tm, W), lambda g, m: (g, m, P_KR // W)),
        pl.BlockSpec((1, tm, W), lambda g, m: (g, m, P_VR // W)),
        pl.BlockSpec((1, tm, LANES), lambda g, m: (g, m, P_WA // LANES)),
    ]
    prev_specs = [
        pl.BlockSpec((1, tm, W), lambda g, m: (g, m, 0)),
        pl.BlockSpec((1, tm, W), lambda g, m: (g, m, 0)),
        pl.BlockSpec((1, tm, W), lambda g, m: (g, m, 0)),
        pl.BlockSpec((1, tm, LANES), lambda g, m: (g, m, 0)),
    ]
    prev_args = list(prev)
    vec = lambda w: pl.BlockSpec((1, w), lambda g, m: (0, 0))
    lora = pl.BlockSpec((LORA_PAIR, W), lambda g, m: (0, 0))
    out_spec = pl.BlockSpec((1, tm, W), lambda g, m: (g, m, 0))
    return pl.pallas_call(
        _rwkv_prep_kernel,
        grid=(G, R // tm),
        in_specs=cur_specs + prev_specs + [vec(W), vec(W), vec(W), vec(LANES),
                                           vec(W), vec(W), vec(W), vec(W), lora, lora],
        out_specs=[out_spec] * 6,
        out_shape=[jax.ShapeDtypeStruct((G, R, W), f32)] * 6,
        compiler_params=_cparams(("parallel", "arbitrary")),
        name="rwkv_prep",
    )(proj, proj, proj, proj, *prev_args, *mu, w0, a0, k_k, k_a, wd_pad, wi_pad)


def _wkv_kernel(tc, r_ref, w_ref, k_ref, v_ref, kk_ref, a_ref, rk_ref, lnw_ref, lnb_ref, s0_ref,
                *refs):
    y_ref, s_ref = refs[-2:]

    @pl.when(pl.program_id(1) == 0)
    def _():
        s_ref[...] = s0_ref[...]

    eye = (lax.broadcasted_iota(jnp.int32, (RW_HEAD, RW_HEAD), 0)
           == lax.broadcasted_iota(jnp.int32, (RW_HEAD, RW_HEAD), 1)).astype(f32)
    rk = rk_ref[...]
    lnw = lnw_ref[...]
    lnb = lnb_ref[...]

    def step(t, carry):
        r, lw, k, v, kkr, a = (ref[0, t] for ref in (r_ref, w_ref, k_ref, v_ref, kk_ref, a_ref))
        w = jnp.exp(lw)
        norm = jnp.sqrt(jnp.sum(kkr * kkr, axis=-1, keepdims=True))
        kk = kkr / jnp.maximum(norm, 1e-12)
        b = kk * a
        bonus = jnp.sum(r * k * rk, axis=-1, keepdims=True) * v
        rows = []
        for h in range(RW_HEADS):
            hs = slice(h, h + 1)
            S = s_ref[0, 0, h]
            sa = jnp.sum(S * (-kk[hs]), axis=-1, keepdims=True)
            v_col = jnp.sum(eye * v[hs], axis=-1, keepdims=True)
            S = S * w[hs] + sa * b[hs] + v_col * k[hs]
            s_ref[0, 0, h] = S
            y = jnp.sum(S * r[hs], axis=-1, keepdims=True)
            mu = jnp.mean(y, axis=0, keepdims=True)
            var = jnp.mean(jnp.square(y - mu), axis=0, keepdims=True)
            yn = (y - mu) * lax.rsqrt(var + GN_EPS)
            rows.append(jnp.sum(yn * eye, axis=0, keepdims=True))
        y_ref[0, t] = jnp.concatenate(rows, axis=0) * lnw + lnb + bonus
        return carry

    lax.fori_loop(0, tc, step, 0)


def _wkv(r, w, k, v, kk, a, r_k, ln_w, ln_b, s0_all, layer, stacked, tc):
    B, T = r.shape[0], r.shape[1]
    hd = (RW_HEADS, RW_HEAD)
    heads = lambda t: t.reshape(B, T, *hd)
    seq = pl.BlockSpec((1, tc, *hd), lambda b, c: (b, c, 0, 0))
    par = pl.BlockSpec(hd, lambda b, c: (0, 0))
    state = pl.BlockSpec((1, 1, RW_HEADS, RW_HEAD, RW_HEAD), lambda b, c: (layer, b, 0, 0, 0))
    args = [heads(r), heads(w), heads(k), heads(v), heads(kk), heads(a),
            r_k, ln_w.reshape(hd), ln_b.reshape(hd), s0_all]
    in_specs = [seq] * 6 + [par] * 3 + [state]
    aliases = {}
    if stacked is not None:
        aliases = {len(args): 1}
        args.append(stacked)
        in_specs.append(pl.BlockSpec(memory_space=pl.ANY))
    y, s_new = pl.pallas_call(
        functools.partial(_wkv_kernel, tc),
        grid=(B, T // tc),
        in_specs=in_specs,
        out_specs=[seq, state],
        out_shape=[
            jax.ShapeDtypeStruct((B, T, *hd), f32),
            jax.ShapeDtypeStruct(s0_all.shape, f32),
        ],
        input_output_aliases=aliases,
        compiler_params=_cparams(("parallel", "arbitrary")),
        name="wkv_steps",
    )(*args)
    return y.reshape(B, T, RWKV_WIDTH), s_new


CHUNK = 64
PAIR = 2 * RW_HEAD
N_PAIRS = RW_HEADS // 2


def _mmb(a, b, dims=_NN):
    return lax.dot_general(a.astype(bf16), b.astype(bf16), dims, preferred_element_type=f32)


def _lora_dot(x, wh_ref, wl_ref):
    xh, xl = _split_bf16(x)
    rows = x.shape[0]
    both = jnp.dot(jnp.concatenate([xh, xl], axis=0), wh_ref[...], preferred_element_type=f32)
    return both[:rows] + both[rows:] + jnp.dot(xh, wl_ref[...], preferred_element_type=f32)


CHUNKS_PER_STEP = 4


def _wkv_chunk_kernel(r_ref, kr_ref, vr_ref, wa_ref, gr_ref, pr_ref, pkr_ref, pvr_ref, pwa_ref,
                      mu_r_ref, mu_kr_ref, mu_vr_ref, mu_wa_ref, w0_ref, a0_ref, kk_ref, ka_ref,
                      wdh_ref, wdl_ref, wih_ref, wil_ref, rk_ref, lnw_ref, lnb_ref, s0_ref,
                      y_ref, s_ref, sp_ref, cr_ref, ckr_ref, cvr_ref, cwa_ref):
    C = CHUNK
    rows_blk = r_ref.shape[1]
    c = pl.program_id(1)

    @pl.when(c == 0)
    def _():
        for p in range(N_PAIRS):
            sp_ref[p] = jnp.concatenate([s0_ref[0, 2 * p], s0_ref[0, 2 * p + 1]], axis=1)
        for carry, first in ((cr_ref, pr_ref), (ckr_ref, pkr_ref), (cvr_ref, pvr_ref), (cwa_ref, pwa_ref)):
            carry[...] = first[0]

    def mixed(cur_ref, carry_ref, mu_ref):
        cur = cur_ref[0]
        first = lax.broadcasted_iota(jnp.int32, cur.shape, 0) == 0
        prev = jnp.where(first, carry_ref[...], pltpu.roll(cur, 1, 0))
        carry_ref[...] = cur[rows_blk - 1:rows_blk, :]
        return cur + (prev - cur) * mu_ref[...]

    r_blk = mixed(r_ref, cr_ref, mu_r_ref)
    kr_blk = mixed(kr_ref, ckr_ref, mu_kr_ref)
    v_blk = mixed(vr_ref, cvr_ref, mu_vr_ref)
    wa = mixed(wa_ref, cwa_ref, mu_wa_ref)
    nz = -(w0_ref[...] + _lora_dot(jnp.tanh(wa), wdh_ref, wdl_ref))
    softplus = jnp.maximum(nz, 0.0) + jnp.log1p(jnp.exp(-jnp.abs(nz)))
    lw_blk = -jnp.exp(-softplus - 0.5)
    icl_blk = jax.nn.sigmoid(a0_ref[...] + _lora_dot(wa, wih_ref, wil_ref))
    k_blk = kr_blk * (1.0 + (icl_blk - 1.0) * ka_ref[...])
    kkr_blk = kr_blk * kk_ref[...]

    row = lax.broadcasted_iota(jnp.int32, (PAIR, PAIR), 0)
    col = lax.broadcasted_iota(jnp.int32, (PAIR, PAIR), 1)
    tril = row >= col
    stril = row > col
    tril2 = jnp.concatenate([tril, tril], axis=1)
    eye = (row == col).astype(f32)
    lane_lo = lax.broadcasted_iota(jnp.int32, (C, PAIR), 1) < RW_HEAD
    pairs = range(N_PAIRS)
    sls = [slice(p * PAIR, (p + 1) * PAIR) for p in pairs]

    def bd(x):
        zero = jnp.zeros_like(x)
        return jnp.concatenate([jnp.where(lane_lo, x, zero), jnp.where(lane_lo, zero, x)], axis=0)

    def head_sums(x):
        lo_sum = jnp.sum(jnp.where(lane_lo, x, 0.0), axis=-1, keepdims=True)
        hi_sum = jnp.sum(jnp.where(lane_lo, 0.0, x), axis=-1, keepdims=True)
        return jnp.where(lane_lo, lo_sum, hi_sum)

    state = [sp_ref[p] for p in pairs]
    for j in range(rows_blk // C):
        rs = slice(j * C, (j + 1) * C)
        lw = lw_blk[rs]
        width = lw.shape[1]
        lw_a = lw.astype(bf16)
        rest = lw - lw_a.astype(f32)
        lw_b = rest.astype(bf16)
        lw_c = (rest - lw_b.astype(f32)).astype(bf16)
        g3 = jnp.dot(tril[:C, :C].astype(bf16), jnp.concatenate([lw_a, lw_b, lw_c], axis=1),
                     preferred_element_type=f32)
        g = g3[:, :width] + g3[:, width:2 * width] + g3[:, 2 * width:]
        e_g = jnp.exp(g)
        e_ng = jnp.exp(-g)
        e_gm = jnp.exp(g - lw)
        e_end = e_g[C - 1:C, :]

        ins = [[t[rs, sl] for t in (r_blk, k_blk, v_blk, kkr_blk, icl_blk)] for sl in sls]
        norms = [jnp.sqrt(head_sums(x[3] * x[3])) for x in ins]
        at, rt, bt, kt, vb = ([] for _ in range(5))
        for (r, k, v, kkr, icl), norm, sl in zip(ins, norms, sls):
            kk = kkr / jnp.maximum(norm, 1e-12)
            b = kk * icl
            at.append(bd((-kk * e_gm[:, sl]).astype(bf16)))
            rt.append(bd((r * e_g[:, sl]).astype(bf16)))
            bt.append(bd((b * e_ng[:, sl]).astype(bf16)))
            kt.append(bd((k * e_ng[:, sl]).astype(bf16)))
            vb.append(bd(v.astype(bf16)))

        gram = [_mmb(jnp.concatenate([at[p], rt[p]], axis=0), jnp.concatenate([bt[p], kt[p]], axis=0), _NT)
                for p in pairs]
        lmat = [jnp.where(stril, gm[:PAIR, :PAIR], 0.0) for gm in gram]
        mv = [_mmb(jnp.where(stril, gram[p][:PAIR, PAIR:], 0.0), vb[p]) for p in pairs]
        lower = [jnp.where(tril2, gm[PAIR:, :], 0.0).astype(bf16) for gm in gram]

        tinv = [eye + lm for lm in lmat]
        pw = [_mmb(lm, lm) for lm in lmat]
        for _ in range(4):
            z = [_mmb(jnp.concatenate([x.astype(bf16), t.astype(bf16)], axis=0), x)
                 for t, x in zip(tinv, pw)]
            pw = [zz[:PAIR] for zz in z]
            tinv = [t + zz[PAIR:] for t, zz in zip(tinv, z)]
        tinv = [t + _mmb(t, x) for t, x in zip(tinv, pw)]

        wx = [_mmb(tinv[p], jnp.concatenate([at[p], mv[p].astype(bf16)], axis=1)) for p in pairs]
        uy0 = [_mmb(jnp.concatenate([wx[p][:, :PAIR].astype(bf16), rt[p]], axis=0),
                    bd(state[p].astype(bf16)), _NT) for p in pairs]
        uv = [jnp.concatenate([(uy0[p][:PAIR] + wx[p][:, PAIR:]).astype(bf16), vb[p]], axis=0)
              for p in pairs]
        ys = [uy0[p][PAIR:] + _mmb(lower[p], uv[p]) for p in pairs]
        s_add = [_mmb(uv[p], jnp.concatenate([bt[p], kt[p]], axis=0), _TN) for p in pairs]
        state = [(state[p] + s_add[p][:RW_HEAD] + s_add[p][RW_HEAD:]) * e_end[:, sls[p]] for p in pairs]

        ys = [y[:C] + y[C:] for y in ys]
        mus = [head_sums(y) * (1.0 / RW_HEAD) for y in ys]
        ds = [y - mu for y, mu in zip(ys, mus)]
        var = [head_sums(d * d) * (1.0 / RW_HEAD) for d in ds]
        bonus = [head_sums(x[0] * x[1] * rk_ref[:, sl]) * x[2] for x, sl in zip(ins, sls)]
        for p in pairs:
            sl = sls[p]
            y_rw = ds[p] * lax.rsqrt(var[p] + GN_EPS) * lnw_ref[:, sl] + lnb_ref[:, sl] + bonus[p]
            y_ref[0, rs, sl] = (y_rw * _silu(gr_ref[0, rs, sl])).astype(y_ref.dtype)

    for p in pairs:
        sp_ref[p] = state[p]

    @pl.when(c == pl.num_programs(1) - 1)
    def _():
        for p in range(N_PAIRS):
            s_ref[0, 2 * p] = state[p][:, :RW_HEAD]
            s_ref[0, 2 * p + 1] = state[p][:, RW_HEAD:]


def _wkv_chunked(proj, prev, mu, w0, a0, k_k, k_a, lora, r_k, ln_w, ln_b, s0):
    B, T, _ = proj.shape
    W = RWKV_WIDTH
    rows = CHUNKS_PER_STEP * CHUNK
    col = lambda w, off: pl.BlockSpec((1, rows, w), lambda b, c: (b, c, off // w))
    first = lambda w: pl.BlockSpec((1, 1, w), lambda b, c: (b, 0, 0))
    vec = lambda w: pl.BlockSpec((1, w), lambda b, c: (0, 0))
    lora_spec = pl.BlockSpec((LORA_PAIR, W), lambda b, c: (0, 0))
    state = pl.BlockSpec((1, RW_HEADS, RW_HEAD, RW_HEAD), lambda b, c: (b, 0, 0, 0))
    return pl.pallas_call(
        _wkv_chunk_kernel,
        grid=(B, T // rows),
        in_specs=[col(W, P_R), col(W, P_KR), col(W, P_VR), col(LANES, P_WA), col(W, P_GR),
                  first(W), first(W), first(W), first(LANES),
                  vec(W), vec(W), vec(W), vec(LANES), vec(W), vec(W), vec(W), vec(W),
                  lora_spec, lora_spec, lora_spec, lora_spec, vec(W), vec(W), vec(W), state],
        out_specs=[col(W, 0), state],
        out_shape=[
            jax.ShapeDtypeStruct((B, T, W), bf16),
            jax.ShapeDtypeStruct((B, RW_HEADS, RW_HEAD, RW_HEAD), f32),
        ],
        scratch_shapes=[pltpu.VMEM((N_PAIRS, RW_HEAD, PAIR), f32),
                        pltpu.VMEM((1, W), f32), pltpu.VMEM((1, W), f32), pltpu.VMEM((1, W), f32),
                        pltpu.VMEM((1, LANES), f32)],
        compiler_params=_cparams(("parallel", "arbitrary")),
        name="wkv_chunks",
    )(proj, proj, proj, proj, proj, *prev, *mu, w0, a0, k_k, k_a, *lora,
      r_k.reshape(1, W), ln_w.reshape(1, W), ln_b.reshape(1, W), s0)


def _post_kernel(final, gated, att_ref, y_ref, *refs):
    if gated:
        y = y_ref[0]
    else:
        y = (y_ref[0] * _silu(refs[0][0])).astype(bf16)
        refs = refs[1:]
    x_ref, gate_ref, w_ref, fg_ref, o_ref = refs
    cat = jnp.concatenate([att_ref[0], y], axis=1)
    out = jnp.dot(cat, w_ref[...], preferred_element_type=f32)
    x = x_ref[0] + gate_ref[0] * out
    if final:
        ms = jnp.mean(x * x, axis=-1, keepdims=True)
        x = x * lax.rsqrt(ms + NORM_EPS) * fg_ref[...]
    o_ref[0] = x


def _post(att, y_rw, proj, x, gate, w_out_bf, final_g, final, tm):
    G, R, _ = x.shape
    W = RWKV_WIDTH
    gated = proj is None
    gate_in = [] if gated else [pl.BlockSpec((1, tm, W), lambda g, m: (g, m, P_GR // W))]
    gate_arg = [] if gated else [proj]
    return pl.pallas_call(
        functools.partial(_post_kernel, final, gated),
        grid=(G, R // tm),
        in_specs=[
            pl.BlockSpec((1, tm, ATT_WIDTH), lambda g, m: (g, m, 0)),
            pl.BlockSpec((1, tm, W), lambda g, m: (g, m, 0)),
            *gate_in,
            pl.BlockSpec((1, tm, D_MODEL), lambda g, m: (g, m, 0)),
            _mod_spec(gate, tm),
            pl.BlockSpec((D_MODEL, D_MODEL), lambda g, m: (0, 0)),
            pl.BlockSpec((1, D_MODEL), lambda g, m: (0, 0)),
        ],
        out_specs=pl.BlockSpec((1, tm, D_MODEL), lambda g, m: (g, m, 0)),
        out_shape=jax.ShapeDtypeStruct((G, R, D_MODEL), f32),
        compiler_params=_cparams(("parallel", "parallel")),
        name="post_proj",
    )(att, y_rw, *gate_arg, x, gate, w_out_bf, final_g.reshape(1, D_MODEL))


def _arrange_w_in(w):
    pad = jnp.zeros((w.shape[0], P_WIDTH - IN_WIDTH), w.dtype)
    parts = [w[:, Q_OFF:KA_OFF], w[:, GA_OFF:GR_OFF], w[:, GR_OFF:IN_WIDTH], w[:, R_OFF:WD_OFF],
             w[:, KA_OFF:R_OFF], w[:, WD_OFF:GA_OFF], pad]
    return jnp.concatenate(parts, axis=1).astype(bf16)


def _shift_cols(t):
    return jnp.concatenate([t[..., P_R:P_KA], t[..., P_WA:P_WA + LORA_PAIR]], axis=-1)


def kernel(x_prompt, x_sample, cache_k, cache_v, state_wkv, state_shift, c_prompt, c_sample,
           norm_g, w_ada, b_ada, w_in, mu_shift, w0, w_decay, a0, w_iclr, k_k, k_a, r_k,
           ln_w, ln_b, sinks, w_out, final_g):
    Bp, Tp = x_prompt.shape[0], x_prompt.shape[1]
    Bd = x_sample.shape[0]
    W = RWKV_WIDTH

    n_c = Bp + Bd
    c_rows = -(-n_c // 16) * 16
    c_all = jnp.concatenate([c_prompt, c_sample, jnp.zeros((c_rows - n_c, D_MODEL), f32)], axis=0)
    mod = _ada(c_all, w_ada, b_ada)

    tab_p = _rope_tables(jnp.arange(Tp, dtype=jnp.int32))
    tab_s = _rope_tables(jnp.full((Bd,), PAST_LEN, jnp.int32))

    hp = x_prompt
    hs = x_sample.reshape(1, Bd, D_MODEL)
    s0_p = jnp.zeros((Bp, RW_HEADS, RW_HEAD, RW_HEAD), f32)
    shift0_p = [jnp.zeros((Bp, 1, w), f32) for w in (W, W, W, LORA_PAIR)]
    new_state_s = new_cache_s = None
    ck_all = cache_k.reshape(DEPTH, Bd, WINDOW, KV_WIDTH)
    cv_all = cache_v.reshape(DEPTH, Bd, WINDOW, KV_WIDTH)
    outs = {k: [] for k in ("kp", "vp", "sp", "shp", "shs")}
    for l in range(DEPTH):
        final = l == DEPTH - 1
        w_bf = _arrange_w_in(w_in[l])
        w_out_bf = w_out[l].astype(bf16)
        mu_l = mu_shift[l]
        mu = [mu_l[0:W].reshape(1, W), mu_l[W:2 * W].reshape(1, W), mu_l[2 * W:3 * W].reshape(1, W),
              mu_l[3 * W:].reshape(1, LORA_PAIR)]
        vecs = [t[l].reshape(1, W) for t in (w0, a0, k_k, k_a)]
        wd_pad = jnp.concatenate([w_decay[l], jnp.zeros((ICLR_LORA, W), f32)], axis=0)
        wi_pad = jnp.concatenate([jnp.zeros((DECAY_LORA, W), f32), w_iclr[l]], axis=0)
        shift_p, scale_p, gate_p = (mod[l, :Bp, i * D_MODEL:(i + 1) * D_MODEL].reshape(Bp, 1, D_MODEL)
                                    for i in range(3))
        shift_s, scale_s, gate_s = (mod[l, Bp:n_c, i * D_MODEL:(i + 1) * D_MODEL].reshape(1, Bd, D_MODEL)
                                    for i in range(3))

        proj = _norm_proj(hp, norm_g[l], scale_p, shift_p, w_bf, tab_p, tm=1024)
        att = _attn_prompt(proj, sinks[l])
        lora = [piece for wp in (wd_pad, wi_pad) for piece in _split_bf16(wp)]
        y_rw, s_t = _wkv_chunked(proj, shift0_p, mu, *vecs, lora, r_k[l], ln_w[l], ln_b[l], s0_p)
        hp = _post(att, y_rw, None, hp, gate_p, w_out_bf, final_g, final, tm=512)
        tail = proj[:, Tp - WINDOW:]
        outs["kp"].append(tail[..., P_KA:P_KA + KV_WIDTH].reshape(Bp, WINDOW, N_KV_HEADS, HEAD_DIM))
        outs["vp"].append(tail[..., P_VA:P_VA + KV_WIDTH].reshape(Bp, WINDOW, N_KV_HEADS, HEAD_DIM))
        outs["sp"].append(s_t)
        outs["shp"].append(_shift_cols(proj[:, Tp - 1]))

        proj = _norm_proj(hs, norm_g[l], scale_s, shift_s, w_bf, tab_s, tm=Bd)
        att, *new_cache_s = _attn_sample(proj, ck_all, cv_all, l, new_cache_s, sinks[l])
        sh = state_shift[l]
        prev = [sh[None, :, 0:W], sh[None, :, W:2 * W], sh[None, :, 2 * W:3 * W], sh[None, :, 3 * W:]]
        prep = _rwkv_prep(proj, prev, mu, *vecs, wd_pad, wi_pad, tm=Bd)
        prep = [t.reshape(Bd, 1, W) for t in prep]
        y_rw, new_state_s = _wkv(*prep, r_k[l], ln_w[l], ln_b[l], state_wkv, l, new_state_s, tc=1)
        hs = _post(att, y_rw.reshape(1, Bd, W), proj, hs, gate_s, w_out_bf, final_g, final, tm=Bd)
        outs["shs"].append(_shift_cols(proj[0]))

    st = lambda k: jnp.stack(outs[k])
    return (hp, hs.reshape(Bd, 1, D_MODEL), st("kp"), st("vp"), st("sp"), st("shp"),
            *(t.reshape(cache_k.shape) for t in new_cache_s), new_state_s, st("shs"))
```

```python
import functools

import jax
import jax.numpy as jnp
from jax import lax
from jax.experimental import pallas as pl
from jax.experimental.pallas import tpu as pltpu

f32 = jnp.float32
bf16 = jnp.bfloat16

D_MODEL = 2048
DEPTH = 2
PAST_LEN = 16384
ATT_WIDTH = 1024
RWKV_WIDTH = 1024
HEAD_DIM = 64
N_Q_HEADS = 16
N_KV_HEADS = 4
GQA_GROUP = 4
KV_WIDTH = 256
WINDOW = 128
ROT_DIM = 16
ROPE_THETA = 500000.0
RW_HEAD = 64
RW_HEADS = 16
DECAY_LORA = 64
ICLR_LORA = 64
LORA_PAIR = DECAY_LORA + ICLR_LORA
NORM_EPS = 1e-5
GN_EPS = 64e-5
NEG_BIG = -1e30

Q_OFF = 0
KA_OFF = Q_OFF + ATT_WIDTH
VA_OFF = KA_OFF + KV_WIDTH
R_OFF = VA_OFF + KV_WIDTH
KR_OFF = R_OFF + RWKV_WIDTH
VR_OFF = KR_OFF + RWKV_WIDTH
WD_OFF = VR_OFF + RWKV_WIDTH
AD_OFF = WD_OFF + DECAY_LORA
GA_OFF = AD_OFF + ICLR_LORA
GR_OFF = GA_OFF + ATT_WIDTH
IN_WIDTH = GR_OFF + RWKV_WIDTH
SHIFT_DIM = GA_OFF - R_OFF

LANES = 128
P_Q = 0
P_GA = 1024
P_GR = 2048
P_R = 3072
P_KR = 4096
P_VR = 5120
P_KA = 6144
P_VA = 6400
P_WA = 6656
P_WIDTH = 7168
PROJ_TN = 1024

VMEM_LIMIT = 56 * 1024 * 1024


_NN = (((1,), (0,)), ((), ()))
_NT = (((1,), (1,)), ((), ()))
_TN = (((0,), (0,)), ((), ()))


def _silu(x):
    return x * jax.nn.sigmoid(x)


def _cparams(sem):
    return pltpu.CompilerParams(dimension_semantics=sem, vmem_limit_bytes=VMEM_LIMIT)


def _split_bf16(x):
    hi = x.astype(bf16)
    return hi, (x - hi.astype(f32)).astype(bf16)


def _ada_kernel(c_ref, w_ref, b_ref, o_ref):
    ch, cl = _split_bf16(_silu(c_ref[...]))
    wh, wl = _split_bf16(w_ref[0])
    rows = ch.shape[0]
    both = jnp.dot(jnp.concatenate([ch, cl], axis=0), wh, preferred_element_type=f32)
    o_ref[0] = both[:rows] + both[rows:] + jnp.dot(ch, wl, preferred_element_type=f32) + b_ref[0]


def _ada(c_all, w_ada, b_ada):
    rows = c_all.shape[0]
    tn = 768
    n_out = w_ada.shape[2]
    return pl.pallas_call(
        _ada_kernel,
        grid=(DEPTH, n_out // tn),
        in_specs=[
            pl.BlockSpec((rows, D_MODEL), lambda l, n: (0, 0)),
            pl.BlockSpec((1, D_MODEL, tn), lambda l, n: (l, 0, n)),
            pl.BlockSpec((1, 1, tn), lambda l, n: (l, 0, n)),
        ],
        out_specs=pl.BlockSpec((1, rows, tn), lambda l, n: (l, 0, n)),
        out_shape=jax.ShapeDtypeStruct((DEPTH, rows, n_out), f32),
        compiler_params=_cparams(("parallel", "parallel")),
        name="ada_mod",
    )(c_all, w_ada, b_ada.reshape(DEPTH, 1, n_out))


def _rope(x, tab):
    w = x.shape[1]
    reps = w // LANES
    cosf, up, dn = (jnp.concatenate([tab[i]] * reps, axis=1) for i in range(3))
    half = ROT_DIM // 2
    return x * cosf + pltpu.roll(x, w - half, 1) * up + pltpu.roll(x, half, 1) * dn


def _norm_proj_kernel(x_ref, g_ref, scale_ref, shift_ref, w_ref, tab_ref, o_ref, h_ref):
    n = pl.program_id(2)

    @pl.when(n == 0)
    def _():
        x = x_ref[0]
        ms = jnp.mean(x * x, axis=-1, keepdims=True)
        y = x * lax.rsqrt(ms + NORM_EPS) * g_ref[...]
        h_ref[...] = (y * (1.0 + scale_ref[0]) + shift_ref[0]).astype(bf16)

    q_tile = P_Q // PROJ_TN
    k_tile = P_KA // PROJ_TN
    tm = h_ref.shape[0]
    rc = min(tm, 256)

    def rows(i):
        rs = slice(i * rc, (i + 1) * rc)
        return rs, jnp.dot(h_ref[rs, :], w_ref[...], preferred_element_type=f32)

    @pl.when(n == q_tile)
    def _():
        for i in range(tm // rc):
            rs, res = rows(i)
            o_ref[0, rs, :] = _rope(res, tab_ref[:, rs, :]) * (HEAD_DIM ** -0.5)

    @pl.when(n == k_tile)
    def _():
        for i in range(tm // rc):
            rs, res = rows(i)
            o_ref[0, rs, :] = jnp.concatenate(
                [_rope(res[:, :KV_WIDTH], tab_ref[:, rs, :]), res[:, KV_WIDTH:]], axis=1)

    @pl.when((n != q_tile) & (n != k_tile))
    def _():
        o_ref[0] = jnp.dot(h_ref[...], w_ref[...], preferred_element_type=f32)


def _mod_spec(mod, tm):
    if mod.shape[1] == 1:
        return pl.BlockSpec((1, 1, D_MODEL), lambda g, m, *_: (g, 0, 0))
    return pl.BlockSpec((1, tm, D_MODEL), lambda g, m, *_: (g, m, 0))


def _norm_proj(x, norm_g, scale, shift, w_bf, tab, tm):
    G, R, _ = x.shape
    return pl.pallas_call(
        _norm_proj_kernel,
        grid=(G, R // tm, P_WIDTH // PROJ_TN),
        in_specs=[
            pl.BlockSpec((1, tm, D_MODEL), lambda g, m, n: (g, m, 0)),
            pl.BlockSpec((1, D_MODEL), lambda g, m, n: (0, 0)),
            _mod_spec(scale, tm),
            _mod_spec(shift, tm),
            pl.BlockSpec((D_MODEL, PROJ_TN), lambda g, m, n: (0, n)),
            pl.BlockSpec((3, tm, LANES), lambda g, m, n: (0, m, 0)),
        ],
        out_specs=pl.BlockSpec((1, tm, PROJ_TN), lambda g, m, n: (g, m, n)),
        out_shape=jax.ShapeDtypeStruct((G, R, P_WIDTH), f32),
        scratch_shapes=[pltpu.VMEM((tm, D_MODEL), bf16)],
        compiler_params=_cparams(("parallel", "parallel", "arbitrary")),
        name="norm_proj",
    )(x, norm_g.reshape(1, D_MODEL), scale, shift, w_bf, tab)


def _rope_tables(pos):
    half = ROT_DIM // 2
    inv_freq = ROPE_THETA ** (-jnp.arange(half, dtype=f32) * (2.0 / ROT_DIM))
    ang = pos.astype(f32)[:, None] * inv_freq[None, :]
    cos, sin = jnp.cos(ang), jnp.sin(ang)
    t = pos.shape[0]
    z8 = jnp.zeros((t, half), f32)
    rest = HEAD_DIM - ROT_DIM
    cos64 = jnp.concatenate([cos, cos, jnp.ones((t, rest), f32)], axis=1)
    up64 = jnp.concatenate([-sin, z8, jnp.zeros((t, rest), f32)], axis=1)
    dn64 = jnp.concatenate([z8, sin, jnp.zeros((t, rest), f32)], axis=1)
    rep = LANES // HEAD_DIM
    return jnp.stack([jnp.tile(a, (1, rep)) for a in (cos64, up64, dn64)])


def _attn_prompt_kernel(sinks_ref, q_ref, kc_ref, kp_ref, vc_ref, vp_ref, ga_ref, o_ref):
    n = pl.program_id(1)
    wn = WINDOW
    half = HEAD_DIM
    q = q_ref[0].astype(bf16)
    k_t = jnp.concatenate([kp_ref[0], kc_ref[0]], axis=0).T.astype(bf16)
    vcat = jnp.concatenate([vp_ref[0], vc_ref[0]], axis=0)
    ga = ga_ref[0]

    qi = lax.broadcasted_iota(jnp.int32, (2 * wn, 2 * wn), 0) & (wn - 1)
    kj = lax.broadcasted_iota(jnp.int32, (2 * wn, 2 * wn), 1)
    rel = wn + qi - kj
    mask = (rel >= 0) & (rel <= wn) & ((kj >= wn) | (n > 0))
    top = lax.broadcasted_iota(jnp.int32, (2 * wn, 1), 0) < wn
    lo = lax.broadcasted_iota(jnp.int32, (2 * wn, LANES), 1) < half
    zeros_k = jnp.zeros((half, 2 * wn), bf16)
    ones_lo = jnp.where(lo, 1.0, 0.0).astype(bf16)
    ones_hi = jnp.where(lo, 0.0, 1.0).astype(bf16)

    for j in range(N_KV_HEADS // 2):
        vblk = vcat[:, j * LANES:(j + 1) * LANES]
        vswap = pltpu.roll(vblk, half, 1)
        for g in (2 * j, 2 * j + 1):
            own, other = (vblk, vswap) if g % 2 == 0 else (vswap, vblk)
            v_lo = jnp.where(lo, own, 0.0).astype(bf16)
            v_hi = jnp.where(lo, 0.0, other).astype(bf16)
            rhs_pv = jnp.concatenate([jnp.concatenate([v_lo, ones_lo], axis=1),
                                      jnp.concatenate([v_hi, ones_hi], axis=1)], axis=0)
            kg = k_t[g * half:(g + 1) * half, :]
            rhs_qk = jnp.concatenate([jnp.concatenate([kg, zeros_k], axis=0),
                                      jnp.concatenate([zeros_k, kg], axis=0)], axis=1)
            b0, b1 = 2 * g, 2 * g + 1
            qg = jnp.concatenate([q[:, b0 * LANES:(b0 + 1) * LANES], q[:, b1 * LANES:(b1 + 1) * LANES]],
                                 axis=0)
            s_all = jnp.dot(qg, rhs_qk, preferred_element_type=f32)
            ps, es = [], []
            for hh in range(2):
                s = jnp.where(mask, s_all[:, hh * 2 * wn:(hh + 1) * 2 * wn], NEG_BIG)
                sink = jnp.where(top, sinks_ref[2 * b0 + hh], sinks_ref[2 * b1 + hh])
                m = jnp.maximum(jnp.max(s, axis=-1, keepdims=True), sink)
                ps.append(jnp.exp(s - m).astype(bf16))
                es.append(jnp.exp(sink - m))
            res = jnp.dot(jnp.concatenate(ps, axis=1), rhs_pv, preferred_element_type=f32)
            out = res[:, :LANES] / (res[:, LANES:] + jnp.where(lo, es[0], es[1]))
            for i, blk in enumerate((b0, b1)):
                sl = slice(blk * LANES, (blk + 1) * LANES)
                o_ref[0, :, sl] = (out[i * wn:(i + 1) * wn] * _silu(ga[:, sl])).astype(o_ref.dtype)


def _attn_prompt(proj, sinks):
    B, T, _ = proj.shape
    nb = T // WINDOW
    kvb = KV_WIDTH
    prev = lambda b, n: jnp.maximum(n - 1, 0)
    return pl.pallas_call(
        _attn_prompt_kernel,
        grid=(B, nb),
        in_specs=[
            pl.BlockSpec(memory_space=pltpu.SMEM),
            pl.BlockSpec((1, WINDOW, ATT_WIDTH), lambda b, n: (b, n, P_Q // ATT_WIDTH)),
            pl.BlockSpec((1, WINDOW, kvb), lambda b, n: (b, n, P_KA // kvb)),
            pl.BlockSpec((1, WINDOW, kvb), lambda b, n: (b, prev(b, n), P_KA // kvb)),
            pl.BlockSpec((1, WINDOW, kvb), lambda b, n: (b, n, P_VA // kvb)),
            pl.BlockSpec((1, WINDOW, kvb), lambda b, n: (b, prev(b, n), P_VA // kvb)),
            pl.BlockSpec((1, WINDOW, ATT_WIDTH), lambda b, n: (b, n, P_GA // ATT_WIDTH)),
        ],
        out_specs=pl.BlockSpec((1, WINDOW, ATT_WIDTH), lambda b, n: (b, n, 0)),
        out_shape=jax.ShapeDtypeStruct((B, T, ATT_WIDTH), bf16),
        compiler_params=_cparams(("parallel", "arbitrary")),
        name="attn_prompt",
    )(sinks, proj, proj, proj, proj, proj, proj)


SAMPLE_ROWS = 16


def _attn_sample_kernel(sinks_ref, q_ref, kn_ref, vn_ref, ga_ref, ck_ref, cv_ref, *refs):
    o_ref, nk_ref, nv_ref = refs[-3:]
    rb = q_ref.shape[1]
    q_all, kn_all, vn_all = q_ref[0], kn_ref[0], vn_ref[0]
    work = [(r, g) for r in range(rb) for g in range(N_KV_HEADS)]
    span = lambda g: slice(g * HEAD_DIM, (g + 1) * HEAD_DIM)
    ckb = [ck_ref[0, r].astype(bf16) for r in range(rb)]
    cvb = [cv_ref[0, r].astype(bf16) for r in range(rb)]
    qg = {(r, g): jnp.concatenate(
        [q_all[r:r + 1, (g * GQA_GROUP + i) * HEAD_DIM:(g * GQA_GROUP + i + 1) * HEAD_DIM]
         for i in range(GQA_GROUP)], axis=0) for r, g in work}
    s = {rg: lax.dot_general(qg[rg].astype(bf16), ckb[rg[0]][:, span(rg[1])], _NT,
                             preferred_element_type=f32) for rg in work}
    outs = {}
    for r, g in work:
        s_new = jnp.sum(qg[r, g] * kn_all[r:r + 1, span(g)], axis=-1, keepdims=True)
        sink = sinks_ref[g * GQA_GROUP:(g + 1) * GQA_GROUP, :]
        m = jnp.maximum(jnp.maximum(jnp.max(s[r, g], axis=-1, keepdims=True), s_new), sink)
        p = jnp.exp(s[r, g] - m)
        p_new = jnp.exp(s_new - m)
        den = jnp.sum(p, axis=-1, keepdims=True) + p_new + jnp.exp(sink - m)
        o = (jnp.dot(p.astype(bf16), cvb[r][:, span(g)], preferred_element_type=f32)
             + p_new * vn_all[r:r + 1, span(g)]) / den
        outs[r, g] = [o[i:i + 1, :] for i in range(GQA_GROUP)]
    att = jnp.concatenate(
        [jnp.concatenate([h for g in range(N_KV_HEADS) for h in outs[r, g]], axis=1) for r in range(rb)],
        axis=0)
    o_ref[0] = (att * _silu(ga_ref[0])).astype(o_ref.dtype)
    last = lax.broadcasted_iota(jnp.int32, (WINDOW, KV_WIDTH), 0) == WINDOW - 1
    for r in range(rb):
        nk_ref[0, r] = jnp.where(last, kn_all[r:r + 1], pltpu.roll(ck_ref[0, r], WINDOW - 1, 0))
        nv_ref[0, r] = jnp.where(last, vn_all[r:r + 1], pltpu.roll(cv_ref[0, r], WINDOW - 1, 0))


def _attn_sample(proj, ck_all, cv_all, layer, stacked, sinks):
    Bd = proj.shape[1]
    rb = SAMPLE_ROWS
    col = lambda w, off: pl.BlockSpec((1, rb, w), lambda i: (0, i, off // w))
    cache = pl.BlockSpec((1, rb, WINDOW, KV_WIDTH), lambda i: (layer, i, 0, 0))
    args = [sinks.reshape(N_Q_HEADS, 1), proj, proj, proj, proj, ck_all, cv_all, *stacked]
    hbm = pl.BlockSpec(memory_space=pl.ANY)
    return pl.pallas_call(
        _attn_sample_kernel,
        grid=(Bd // rb,),
        in_specs=[
            pl.BlockSpec((N_Q_HEADS, 1), lambda i: (0, 0)),
            col(ATT_WIDTH, P_Q), col(KV_WIDTH, P_KA), col(KV_WIDTH, P_VA), col(ATT_WIDTH, P_GA),
            cache, cache, hbm, hbm,
        ],
        out_specs=[pl.BlockSpec((1, rb, ATT_WIDTH), lambda i: (0, i, 0)), cache, cache],
        out_shape=[
            jax.ShapeDtypeStruct((1, Bd, ATT_WIDTH), bf16),
            jax.ShapeDtypeStruct(ck_all.shape, f32),
            jax.ShapeDtypeStruct(cv_all.shape, f32),
        ],
        input_output_aliases={len(args) - 2: 1, len(args) - 1: 2},
        compiler_params=_cparams(("parallel",)),
        name="attn_sample",
    )(*args)


def _rwkv_prep_kernel(r_ref, kr_ref, vr_ref, wa_ref, pr_ref, pkr_ref, pvr_ref, pwa_ref,
                      mu_r_ref, mu_kr_ref, mu_vr_ref, mu_wa_ref, w0_ref, a0_ref, kk_ref, ka_ref,
                      wd_ref, wi_ref, ro_ref, wo_ref, ko_ref, vo_ref, kko_ref, ao_ref):
    def mixed(cur_ref, prev_ref, mu_ref):
        cur = cur_ref[0]
        return cur + (prev_ref[0] - cur) * mu_ref[...]

    r = mixed(r_ref, pr_ref, mu_r_ref)
    kr = mixed(kr_ref, pkr_ref, mu_kr_ref)
    vr = mixed(vr_ref, pvr_ref, mu_vr_ref)
    wa = mixed(wa_ref, pwa_ref, mu_wa_ref)
    hi = lax.Precision.HIGHEST
    z = w0_ref[...] + jnp.dot(jnp.tanh(wa), wd_ref[...], precision=hi, preferred_element_type=f32)
    nz = -z
    softplus = jnp.maximum(nz, 0.0) + jnp.log1p(jnp.exp(-jnp.abs(nz)))
    w_log = -softplus - 0.5
    a = jax.nn.sigmoid(a0_ref[...] + jnp.dot(wa, wi_ref[...], precision=hi, preferred_element_type=f32))
    ro_ref[0] = r
    wo_ref[0] = -jnp.exp(w_log)
    ko_ref[0] = kr * (1.0 + (a - 1.0) * ka_ref[...])
    vo_ref[0] = vr
    kko_ref[0] = kr * kk_ref[...]
    ao_ref[0] = a


def _rwkv_prep(proj, prev, mu, w0, a0, k_k, k_a, wd_pad, wi_pad, tm):
    G, R, _ = proj.shape
    W = RWKV_WIDTH
    cur_specs = [
        pl.BlockSpec((1, tm, W), lambda g, m: (g, m, P_R // W)),
        pl.BlockSpec((1, tm, W), lambda g, m: (g, m, P_KR // W)),
        pl.BlockSpec((1, tm, W), lambda g, m: (g, m, P_VR // W)),
        pl.BlockSpec((1, tm, LANES), lambda g, m: (g, m, P_WA // LANES)),
    ]
    prev_specs = [
        pl.BlockSpec((1, tm, W), lambda g, m: (g, m, 0)),
        pl.BlockSpec((1, tm, W), lambda g, m: (g, m, 0)),
        pl.BlockSpec((1, tm, W), lambda g, m: (g, m, 0)),
        pl.BlockSpec((1, tm, LANES), lambda g, m: (g, m, 0)),
    ]
    prev_args = list(prev)
    vec = lambda w: pl.BlockSpec((1, w), lambda g, m: (0, 0))
    lora = pl.BlockSpec((LORA_PAIR, W), lambda g, m: (0, 0))
    out_spec = pl.BlockSpec((1, tm, W), lambda g, m: (g, m, 0))
    return pl.pallas_call(
        _rwkv_prep_kernel,
        grid=(G, R // tm),
        in_specs=cur_specs + prev_specs + [vec(W), vec(W), vec(W), vec(LANES),
                                           vec(W), vec(W), vec(W), vec(W), lora, lora],
        out_specs=[out_spec] * 6,
        out_shape=[jax.ShapeDtypeStruct((G, R, W), f32)] * 6,
        compiler_params=_cparams(("parallel", "arbitrary")),
        name="rwkv_prep",
    )(proj, proj, proj, proj, *prev_args, *mu, w0, a0, k_k, k_a, wd_pad, wi_pad)


def _wkv_kernel(tc, r_ref, w_ref, k_ref, v_ref, kk_ref, a_ref, rk_ref, lnw_ref, lnb_ref, s0_ref,
                *refs):
    y_ref, s_ref = refs[-2:]

    @pl.when(pl.program_id(1) == 0)
    def _():
        s_ref[...] = s0_ref[...]

    eye = (lax.broadcasted_iota(jnp.int32, (RW_HEAD, RW_HEAD), 0)
           == lax.broadcasted_iota(jnp.int32, (RW_HEAD, RW_HEAD), 1)).astype(f32)
    rk = rk_ref[...]
    lnw = lnw_ref[...]
    lnb = lnb_ref[...]

    def step(t, carry):
        r, lw, k, v, kkr, a = (ref[0, t] for ref in (r_ref, w_ref, k_ref, v_ref, kk_ref, a_ref))
        w = jnp.exp(lw)
        norm = jnp.sqrt(jnp.sum(kkr * kkr, axis=-1, keepdims=True))
        kk = kkr / jnp.maximum(norm, 1e-12)
        b = kk * a
        bonus = jnp.sum(r * k * rk, axis=-1, keepdims=True) * v
        rows = []
        for h in range(RW_HEADS):
            hs = slice(h, h + 1)
            S = s_ref[0, 0, h]
            sa = jnp.sum(S * (-kk[hs]), axis=-1, keepdims=True)
            v_col = jnp.sum(eye * v[hs], axis=-1, keepdims=True)
            S = S * w[hs] + sa * b[hs] + v_col * k[hs]
            s_ref[0, 0, h] = S
            y = jnp.sum(S * r[hs], axis=-1, keepdims=True)
            mu = jnp.mean(y, axis=0, keepdims=True)
            var = jnp.mean(jnp.square(y - mu), axis=0, keepdims=True)
            yn = (y - mu) * lax.rsqrt(var + GN_EPS)
            rows.append(jnp.sum(yn * eye, axis=0, keepdims=True))
        y_ref[0, t] = jnp.concatenate(rows, axis=0) * lnw + lnb + bonus
        return carry

    lax.fori_loop(0, tc, step, 0)


def _wkv(r, w, k, v, kk, a, r_k, ln_w, ln_b, s0_all, layer, stacked, tc):
    B, T = r.shape[0], r.shape[1]
    hd = (RW_HEADS, RW_HEAD)
    heads = lambda t: t.reshape(B, T, *hd)
    seq = pl.BlockSpec((1, tc, *hd), lambda b, c: (b, c, 0, 0))
    par = pl.BlockSpec(hd, lambda b, c: (0, 0))
    state = pl.BlockSpec((1, 1, RW_HEADS, RW_HEAD, RW_HEAD), lambda b, c: (layer, b, 0, 0, 0))
    args = [heads(r), heads(w), heads(k), heads(v), heads(kk), heads(a),
            r_k, ln_w.reshape(hd), ln_b.reshape(hd), s0_all, stacked]
    y, s_new = pl.pallas_call(
        functools.partial(_wkv_kernel, tc),
        grid=(B, T // tc),
        in_specs=[seq] * 6 + [par] * 3 + [state, pl.BlockSpec(memory_space=pl.ANY)],
        out_specs=[seq, state],
        out_shape=[
            jax.ShapeDtypeStruct((B, T, *hd), f32),
            jax.ShapeDtypeStruct(s0_all.shape, f32),
        ],
        input_output_aliases={len(args) - 1: 1},
        compiler_params=_cparams(("parallel", "arbitrary")),
        name="wkv_steps",
    )(*args)
    return y.reshape(B, T, RWKV_WIDTH), s_new


CHUNK = 64
PAIR = 2 * RW_HEAD
N_PAIRS = RW_HEADS // 2


def _mmb(a, b, dims=_NN):
    return lax.dot_general(a.astype(bf16), b.astype(bf16), dims, preferred_element_type=f32)


def _lora_dot(x, wh_ref, wl_ref):
    xh, xl = _split_bf16(x)
    rows = x.shape[0]
    both = jnp.dot(jnp.concatenate([xh, xl], axis=0), wh_ref[...], preferred_element_type=f32)
    return both[:rows] + both[rows:] + jnp.dot(xh, wl_ref[...], preferred_element_type=f32)


CHUNKS_PER_STEP = 4


def _wkv_chunk_kernel(r_ref, kr_ref, vr_ref, wa_ref, gr_ref, pr_ref, pkr_ref, pvr_ref, pwa_ref,
                      mu_r_ref, mu_kr_ref, mu_vr_ref, mu_wa_ref, w0_ref, a0_ref, kk_ref, ka_ref,
                      wdh_ref, wdl_ref, wih_ref, wil_ref, rk_ref, lnw_ref, lnb_ref, s0_ref,
                      y_ref, s_ref, sp_ref, cr_ref, ckr_ref, cvr_ref, cwa_ref):
    C = CHUNK
    rows_blk = r_ref.shape[1]
    c = pl.program_id(1)

    @pl.when(c == 0)
    def _():
        for p in range(N_PAIRS):
            sp_ref[p] = jnp.concatenate([s0_ref[0, 2 * p], s0_ref[0, 2 * p + 1]], axis=1)
        for carry, first in ((cr_ref, pr_ref), (ckr_ref, pkr_ref), (cvr_ref, pvr_ref), (cwa_ref, pwa_ref)):
            carry[...] = first[0]

    def mixed(cur_ref, carry_ref, mu_ref):
        cur = cur_ref[0]
        first = lax.broadcasted_iota(jnp.int32, cur.shape, 0) == 0
        prev = jnp.where(first, carry_ref[...], pltpu.roll(cur, 1, 0))
        carry_ref[...] = cur[rows_blk - 1:rows_blk, :]
        return cur + (prev - cur) * mu_ref[...]

    r_blk = mixed(r_ref, cr_ref, mu_r_ref)
    kr_blk = mixed(kr_ref, ckr_ref, mu_kr_ref)
    v_blk = mixed(vr_ref, cvr_ref, mu_vr_ref)
    wa = mixed(wa_ref, cwa_ref, mu_wa_ref)
    nz = -(w0_ref[...] + _lora_dot(jnp.tanh(wa), wdh_ref, wdl_ref))
    softplus = jnp.maximum(nz, 0.0) + jnp.log1p(jnp.exp(-jnp.abs(nz)))
    lw_blk = -jnp.exp(-softplus - 0.5)
    icl_blk = jax.nn.sigmoid(a0_ref[...] + _lora_dot(wa, wih_ref, wil_ref))
    k_blk = kr_blk * (1.0 + (icl_blk - 1.0) * ka_ref[...])
    kkr_blk = kr_blk * kk_ref[...]

    row = lax.broadcasted_iota(jnp.int32, (PAIR, PAIR), 0)
    col = lax.broadcasted_iota(jnp.int32, (PAIR, PAIR), 1)
    tril = row >= col
    stril = row > col
    tril2 = jnp.concatenate([tril, tril], axis=1)
    eye = (row == col).astype(f32)
    lane_lo = lax.broadcasted_iota(jnp.int32, (C, PAIR), 1) < RW_HEAD
    pairs = range(N_PAIRS)
    sls = [slice(p * PAIR, (p + 1) * PAIR) for p in pairs]

    def bd(x):
        zero = jnp.zeros_like(x)
        return jnp.concatenate([jnp.where(lane_lo, x, zero), jnp.where(lane_lo, zero, x)], axis=0)

    def head_sums(x):
        lo_sum = jnp.sum(jnp.where(lane_lo, x, 0.0), axis=-1, keepdims=True)
        hi_sum = jnp.sum(jnp.where(lane_lo, 0.0, x), axis=-1, keepdims=True)
        return jnp.where(lane_lo, lo_sum, hi_sum)

    state = [sp_ref[p] for p in pairs]
    for j in range(rows_blk // C):
        rs = slice(j * C, (j + 1) * C)
        lw = lw_blk[rs]
        width = lw.shape[1]
        lw_a = lw.astype(bf16)
        rest = lw - lw_a.astype(f32)
        lw_b = rest.astype(bf16)
        lw_c = (rest - lw_b.astype(f32)).astype(bf16)
        g3 = jnp.dot(tril[:C, :C].astype(bf16), jnp.concatenate([lw_a, lw_b, lw_c], axis=1),
                     preferred_element_type=f32)
        g = g3[:, :width] + g3[:, width:2 * width] + g3[:, 2 * width:]
        e_g = jnp.exp(g)
        e_ng = jnp.exp(-g)
        e_gm = jnp.exp(g - lw)
        e_end = e_g[C - 1:C, :]

        ins = [[t[rs, sl] for t in (r_blk, k_blk, v_blk, kkr_blk, icl_blk)] for sl in sls]
        norms = [jnp.sqrt(head_sums(x[3] * x[3])) for x in ins]
        at, rt, bt, kt, vb = ([] for _ in range(5))
        for (r, k, v, kkr, icl), norm, sl in zip(ins, norms, sls):
            kk = kkr / jnp.maximum(norm, 1e-12)
            b = kk * icl
            at.append(bd((-kk * e_gm[:, sl]).astype(bf16)))
            rt.append(bd((r * e_g[:, sl]).astype(bf16)))
            bt.append(bd((b * e_ng[:, sl]).astype(bf16)))
            kt.append(bd((k * e_ng[:, sl]).astype(bf16)))
            vb.append(bd(v.astype(bf16)))

        gram = [_mmb(jnp.concatenate([at[p], rt[p]], axis=0), jnp.concatenate([bt[p], kt[p]], axis=0), _NT)
                for p in pairs]
        lmat = [jnp.where(stril, gm[:PAIR, :PAIR], 0.0) for gm in gram]
        mv = [_mmb(jnp.where(stril, gram[p][:PAIR, PAIR:], 0.0), vb[p]) for p in pairs]
        lower = [jnp.where(tril2, gm[PAIR:, :], 0.0).astype(bf16) for gm in gram]

        tinv = [eye + lm for lm in lmat]
        pw = [_mmb(lm, lm) for lm in lmat]
        for _ in range(4):
            z = [_mmb(jnp.concatenate([x.astype(bf16), t.astype(bf16)], axis=0), x)
                 for t, x in zip(tinv, pw)]
            pw = [zz[:PAIR] for zz in z]
            tinv = [t + zz[PAIR:] for t, zz in zip(tinv, z)]
        tinv = [t + _mmb(t, x) for t, x in zip(tinv, pw)]

        wx = [_mmb(tinv[p], jnp.concatenate([at[p], mv[p].astype(bf16)], axis=1)) for p in pairs]
        uy0 = [_mmb(jnp.concatenate([wx[p][:, :PAIR].astype(bf16), rt[p]], axis=0),
                    bd(state[p].astype(bf16)), _NT) for p in pairs]
        uv = [jnp.concatenate([(uy0[p][:PAIR] + wx[p][:, PAIR:]).astype(bf16), vb[p]], axis=0)
              for p in pairs]
        ys = [uy0[p][PAIR:] + _mmb(lower[p], uv[p]) for p in pairs]
        s_add = [_mmb(uv[p], jnp.concatenate([bt[p], kt[p]], axis=0), _TN) for p in pairs]
        state = [(state[p] + s_add[p][:RW_HEAD] + s_add[p][RW_HEAD:]) * e_end[:, sls[p]] for p in pairs]

        ys = [y[:C] + y[C:] for y in ys]
        mus = [head_sums(y) * (1.0 / RW_HEAD) for y in ys]
        ds = [y - mu for y, mu in zip(ys, mus)]
        var = [head_sums(d * d) * (1.0 / RW_HEAD) for d in ds]
        bonus = [head_sums(x[0] * x[1] * rk_ref[:, sl]) * x[2] for x, sl in zip(ins, sls)]
        for p in pairs:
            sl = sls[p]
            y_rw = ds[p] * lax.rsqrt(var[p] + GN_EPS) * lnw_ref[:, sl] + lnb_ref[:, sl] + bonus[p]
            y_ref[0, rs, sl] = (y_rw * _silu(gr_ref[0, rs, sl])).astype(y_ref.dtype)

    for p in pairs:
        sp_ref[p] = state[p]

    @pl.when(c == pl.num_programs(1) - 1)
    def _():
        for p in range(N_PAIRS):
            s_ref[0, 2 * p] = state[p][:, :RW_HEAD]
            s_ref[0, 2 * p + 1] = state[p][:, RW_HEAD:]


def _wkv_chunked(proj, prev, mu, w0, a0, k_k, k_a, lora, r_k, ln_w, ln_b, s0):
    B, T, _ = proj.shape
    W = RWKV_WIDTH
    rows = CHUNKS_PER_STEP * CHUNK
    col = lambda w, off: pl.BlockSpec((1, rows, w), lambda b, c: (b, c, off // w))
    first = lambda w: pl.BlockSpec((1, 1, w), lambda b, c: (b, 0, 0))
    vec = lambda w: pl.BlockSpec((1, w), lambda b, c: (0, 0))
    lora_spec = pl.BlockSpec((LORA_PAIR, W), lambda b, c: (0, 0))
    state = pl.BlockSpec((1, RW_HEADS, RW_HEAD, RW_HEAD), lambda b, c: (b, 0, 0, 0))
    return pl.pallas_call(
        _wkv_chunk_kernel,
        grid=(B, T // rows),
        in_specs=[col(W, P_R), col(W, P_KR), col(W, P_VR), col(LANES, P_WA), col(W, P_GR),
                  first(W), first(W), first(W), first(LANES),
                  vec(W), vec(W), vec(W), vec(LANES), vec(W), vec(W), vec(W), vec(W),
                  lora_spec, lora_spec, lora_spec, lora_spec, vec(W), vec(W), vec(W), state],
        out_specs=[col(W, 0), state],
        out_shape=[
            jax.ShapeDtypeStruct((B, T, W), bf16),
            jax.ShapeDtypeStruct((B, RW_HEADS, RW_HEAD, RW_HEAD), f32),
        ],
        scratch_shapes=[pltpu.VMEM((N_PAIRS, RW_HEAD, PAIR), f32),
                        pltpu.VMEM((1, W), f32), pltpu.VMEM((1, W), f32), pltpu.VMEM((1, W), f32),
                        pltpu.VMEM((1, LANES), f32)],
        compiler_params=_cparams(("parallel", "arbitrary")),
        name="wkv_chunks",
    )(proj, proj, proj, proj, proj, *prev, *mu, w0, a0, k_k, k_a, *lora,
      r_k.reshape(1, W), ln_w.reshape(1, W), ln_b.reshape(1, W), s0)


def _post_kernel(final, gated, att_ref, y_ref, *refs):
    if gated:
        y = y_ref[0]
    else:
        y = (y_ref[0] * _silu(refs[0][0])).astype(bf16)
        refs = refs[1:]
    x_ref, gate_ref, w_ref, fg_ref, o_ref = refs
    cat = jnp.concatenate([att_ref[0], y], axis=1)
    out = jnp.dot(cat, w_ref[...], preferred_element_type=f32)
    x = x_ref[0] + gate_ref[0] * out
    if final:
        ms = jnp.mean(x * x, axis=-1, keepdims=True)
        x = x * lax.rsqrt(ms + NORM_EPS) * fg_ref[...]
    o_ref[0] = x


def _post(att, y_rw, proj, x, gate, w_out_bf, final_g, final, tm):
    G, R, _ = x.shape
    W = RWKV_WIDTH
    gated = proj is None
    gate_in = [] if gated else [pl.BlockSpec((1, tm, W), lambda g, m: (g, m, P_GR // W))]
    gate_arg = [] if gated else [proj]
    return pl.pallas_call(
        functools.partial(_post_kernel, final, gated),
        grid=(G, R // tm),
        in_specs=[
            pl.BlockSpec((1, tm, ATT_WIDTH), lambda g, m: (g, m, 0)),
            pl.BlockSpec((1, tm, W), lambda g, m: (g, m, 0)),
            *gate_in,
            pl.BlockSpec((1, tm, D_MODEL), lambda g, m: (g, m, 0)),
            _mod_spec(gate, tm),
            pl.BlockSpec((D_MODEL, D_MODEL), lambda g, m: (0, 0)),
            pl.BlockSpec((1, D_MODEL), lambda g, m: (0, 0)),
        ],
        out_specs=pl.BlockSpec((1, tm, D_MODEL), lambda g, m: (g, m, 0)),
        out_shape=jax.ShapeDtypeStruct((G, R, D_MODEL), f32),
        compiler_params=_cparams(("parallel", "parallel")),
        name="post_proj",
    )(att, y_rw, *gate_arg, x, gate, w_out_bf, final_g.reshape(1, D_MODEL))


def _arrange_w_in(w):
    pad = jnp.zeros((w.shape[0], P_WIDTH - IN_WIDTH), w.dtype)
    parts = [w[:, Q_OFF:KA_OFF], w[:, GA_OFF:GR_OFF], w[:, GR_OFF:IN_WIDTH], w[:, R_OFF:WD_OFF],
             w[:, KA_OFF:R_OFF], w[:, WD_OFF:GA_OFF], pad]
    return jnp.concatenate(parts, axis=1).astype(bf16)


def _shift_cols(t):
    return jnp.concatenate([t[..., P_R:P_KA], t[..., P_WA:P_WA + LORA_PAIR]], axis=-1)


def kernel(x_prompt, x_sample, cache_k, cache_v, state_wkv, state_shift, c_prompt, c_sample,
           norm_g, w_ada, b_ada, w_in, mu_shift, w0, w_decay, a0, w_iclr, k_k, k_a, r_k,
           ln_w, ln_b, sinks, w_out, final_g):
    Bp, Tp = x_prompt.shape[0], x_prompt.shape[1]
    Bd = x_sample.shape[0]
    W = RWKV_WIDTH

    n_c = Bp + Bd
    c_rows = -(-n_c // 16) * 16
    c_all = jnp.concatenate([c_prompt, c_sample, jnp.zeros((c_rows - n_c, D_MODEL), f32)], axis=0)
    mod = _ada(c_all, w_ada, b_ada)

    tab_p = _rope_tables(jnp.arange(Tp, dtype=jnp.int32))
    tab_s = _rope_tables(jnp.full((Bd,), PAST_LEN, jnp.int32))

    hp = x_prompt
    hs = x_sample.reshape(1, Bd, D_MODEL)
    s0_p = jnp.zeros((Bp, RW_HEADS, RW_HEAD, RW_HEAD), f32)
    shift0_p = [jnp.zeros((Bp, 1, w), f32) for w in (W, W, W, LORA_PAIR)]
    ck_all = cache_k.reshape(DEPTH, Bd, WINDOW, KV_WIDTH)
    cv_all = cache_v.reshape(DEPTH, Bd, WINDOW, KV_WIDTH)
    new_state_s = jnp.zeros(state_wkv.shape, f32)
    new_cache_s = [jnp.zeros(ck_all.shape, f32), jnp.zeros(cv_all.shape, f32)]
    outs = {k: [] for k in ("kp", "vp", "sp", "shp", "shs")}
    for l in range(DEPTH):
        final = l == DEPTH - 1
        w_bf = _arrange_w_in(w_in[l])
        w_out_bf = w_out[l].astype(bf16)
        mu_l = mu_shift[l]
        mu = [mu_l[0:W].reshape(1, W), mu_l[W:2 * W].reshape(1, W), mu_l[2 * W:3 * W].reshape(1, W),
              mu_l[3 * W:].reshape(1, LORA_PAIR)]
        vecs = [t[l].reshape(1, W) for t in (w0, a0, k_k, k_a)]
        wd_pad = jnp.concatenate([w_decay[l], jnp.zeros((ICLR_LORA, W), f32)], axis=0)
        wi_pad = jnp.concatenate([jnp.zeros((DECAY_LORA, W), f32), w_iclr[l]], axis=0)
        shift_p, scale_p, gate_p = (mod[l, :Bp, i * D_MODEL:(i + 1) * D_MODEL].reshape(Bp, 1, D_MODEL)
                                    for i in range(3))
        shift_s, scale_s, gate_s = (mod[l, Bp:n_c, i * D_MODEL:(i + 1) * D_MODEL].reshape(1, Bd, D_MODEL)
                                    for i in range(3))

        proj = _norm_proj(hp, norm_g[l], scale_p, shift_p, w_bf, tab_p, tm=1024)
        att = _attn_prompt(proj, sinks[l])
        lora = [piece for wp in (wd_pad, wi_pad) for piece in _split_bf16(wp)]
        y_rw, s_t = _wkv_chunked(proj, shift0_p, mu, *vecs, lora, r_k[l], ln_w[l], ln_b[l], s0_p)
        hp = _post(att, y_rw, None, hp, gate_p, w_out_bf, final_g, final, tm=512)
        tail = proj[:, Tp - WINDOW:]
        outs["kp"].append(tail[..., P_KA:P_KA + KV_WIDTH].reshape(Bp, WINDOW, N_KV_HEADS, HEAD_DIM))
        outs["vp"].append(tail[..., P_VA:P_VA + KV_WIDTH].reshape(Bp, WINDOW, N_KV_HEADS, HEAD_DIM))
        outs["sp"].append(s_t)
        outs["shp"].append(_shift_cols(proj[:, Tp - 1]))

        proj = _norm_proj(hs, norm_g[l], scale_s, shift_s, w_bf, tab_s, tm=Bd)
        att, *new_cache_s = _attn_sample(proj, ck_all, cv_all, l, new_cache_s, sinks[l])
        sh = state_shift[l]
        prev = [sh[None, :, 0:W], sh[None, :, W:2 * W], sh[None, :, 2 * W:3 * W], sh[None, :, 3 * W:]]
        prep = _rwkv_prep(proj, prev, mu, *vecs, wd_pad, wi_pad, tm=Bd)
        prep = [t.reshape(Bd, 1, W) for t in prep]
        y_rw, new_state_s = _wkv(*prep, r_k[l], ln_w[l], ln_b[l], state_wkv, l, new_state_s, tc=1)
        hs = _post(att, y_rw.reshape(1, Bd, W), proj, hs, gate_s, w_out_bf, final_g, final, tm=Bd)
        outs["shs"].append(_shift_cols(proj[0]))

    st = lambda k: jnp.stack(outs[k])
    return (hp, hs.reshape(Bd, 1, D_MODEL), st("kp"), st("vp"), st("sp"), st("shp"),
            *(t.reshape(cache_k.shape) for t in new_cache_s), new_state_s, st("shs"))
```

```python
import functools

import jax
import jax.numpy as jnp
from jax import lax
from jax.experimental import pallas as pl
from jax.experimental.pallas import tpu as pltpu

f32 = jnp.float32
bf16 = jnp.bfloat16

D_MODEL = 2048
DEPTH = 2
PAST_LEN = 16384
ATT_WIDTH = 1024
RWKV_WIDTH = 1024
HEAD_DIM = 64
N_Q_HEADS = 16
N_KV_HEADS = 4
GQA_GROUP = 4
KV_WIDTH = 256
WINDOW = 128
ROT_DIM = 16
ROPE_THETA = 500000.0
RW_HEAD = 64
RW_HEADS = 16
DECAY_LORA = 64
ICLR_LORA = 64
LORA_PAIR = DECAY_LORA + ICLR_LORA
NORM_EPS = 1e-5
GN_EPS = 64e-5
NEG_BIG = -1e30

Q_OFF = 0
KA_OFF = Q_OFF + ATT_WIDTH
VA_OFF = KA_OFF + KV_WIDTH
R_OFF = VA_OFF + KV_WIDTH
KR_OFF = R_OFF + RWKV_WIDTH
VR_OFF = KR_OFF + RWKV_WIDTH
WD_OFF = VR_OFF + RWKV_WIDTH
AD_OFF = WD_OFF + DECAY_LORA
GA_OFF = AD_OFF + ICLR_LORA
GR_OFF = GA_OFF + ATT_WIDTH
IN_WIDTH = GR_OFF + RWKV_WIDTH
SHIFT_DIM = GA_OFF - R_OFF

LANES = 128
P_Q = 0
P_GA = 1024
P_GR = 2048
P_R = 3072
P_KR = 4096
P_VR = 5120
P_KA = 6144
P_VA = 6400
P_WA = 6656
P_WIDTH = 7168
PROJ_TN = 1024

VMEM_LIMIT = 56 * 1024 * 1024


_NN = (((1,), (0,)), ((), ()))
_NT = (((1,), (1,)), ((), ()))
_TN = (((0,), (0,)), ((), ()))


def _silu(x):
    return x * jax.nn.sigmoid(x)


def _cparams(sem):
    return pltpu.CompilerParams(dimension_semantics=sem, vmem_limit_bytes=VMEM_LIMIT)


def _split_bf16(x):
    hi = x.astype(bf16)
    return hi, (x - hi.astype(f32)).astype(bf16)


def _ada_kernel(c_ref, w_ref, b_ref, o_ref):
    ch, cl = _split_bf16(_silu(c_ref[...]))
    wh, wl = _split_bf16(w_ref[0])
    rows = ch.shape[0]
    both = jnp.dot(jnp.concatenate([ch, cl], axis=0), wh, preferred_element_type=f32)
    o_ref[0] = both[:rows] + both[rows:] + jnp.dot(ch, wl, preferred_element_type=f32) + b_ref[0]


def _ada(c_all, w_ada, b_ada):
    rows = c_all.shape[0]
    tn = 1536
    n_out = w_ada.shape[2]
    return pl.pallas_call(
        _ada_kernel,
        grid=(DEPTH, n_out // tn),
        in_specs=[
            pl.BlockSpec((rows, D_MODEL), lambda l, n: (0, 0)),
            pl.BlockSpec((1, D_MODEL, tn), lambda l, n: (l, 0, n)),
            pl.BlockSpec((1, 1, tn), lambda l, n: (l, 0, n)),
        ],
        out_specs=pl.BlockSpec((1, rows, tn), lambda l, n: (l, 0, n)),
        out_shape=jax.ShapeDtypeStruct((DEPTH, rows, n_out), f32),
        compiler_params=_cparams(("parallel", "parallel")),
        name="ada_mod",
    )(c_all, w_ada, b_ada.reshape(DEPTH, 1, n_out))


def _rope(x, tab):
    w = x.shape[1]
    reps = w // LANES
    cosf, up, dn = (jnp.concatenate([tab[i]] * reps, axis=1) for i in range(3))
    half = ROT_DIM // 2
    return x * cosf + pltpu.roll(x, w - half, 1) * up + pltpu.roll(x, half, 1) * dn


def _norm_proj_kernel(x_ref, g_ref, scale_ref, shift_ref, w_ref, tab_ref, o_ref, h_ref):
    n = pl.program_id(2)

    @pl.when(n == 0)
    def _():
        x = x_ref[0]
        ms = jnp.mean(x * x, axis=-1, keepdims=True)
        y = x * lax.rsqrt(ms + NORM_EPS) * g_ref[...]
        h_ref[...] = (y * (1.0 + scale_ref[0]) + shift_ref[0]).astype(bf16)

    q_tile = P_Q // PROJ_TN
    k_tile = P_KA // PROJ_TN
    tm = h_ref.shape[0]
    rc = min(tm, 256)

    def rows(i):
        rs = slice(i * rc, (i + 1) * rc)
        return rs, jnp.dot(h_ref[rs, :], w_ref[...], preferred_element_type=f32)

    @pl.when(n == q_tile)
    def _():
        for i in range(tm // rc):
            rs, res = rows(i)
            o_ref[0, rs, :] = _rope(res, tab_ref[:, rs, :]) * (HEAD_DIM ** -0.5)

    @pl.when(n == k_tile)
    def _():
        for i in range(tm // rc):
            rs, res = rows(i)
            o_ref[0, rs, :] = jnp.concatenate(
                [_rope(res[:, :KV_WIDTH], tab_ref[:, rs, :]), res[:, KV_WIDTH:]], axis=1)

    @pl.when((n != q_tile) & (n != k_tile))
    def _():
        o_ref[0] = jnp.dot(h_ref[...], w_ref[...], preferred_element_type=f32)


def _mod_spec(mod, tm):
    if mod.shape[1] == 1:
        return pl.BlockSpec((1, 1, D_MODEL), lambda g, m, *_: (g, 0, 0))
    return pl.BlockSpec((1, tm, D_MODEL), lambda g, m, *_: (g, m, 0))


def _norm_proj(x, norm_g, scale, shift, w_bf, tab, tm):
    G, R, _ = x.shape
    return pl.pallas_call(
        _norm_proj_kernel,
        grid=(G, R // tm, P_WIDTH // PROJ_TN),
        in_specs=[
            pl.BlockSpec((1, tm, D_MODEL), lambda g, m, n: (g, m, 0)),
            pl.BlockSpec((1, D_MODEL), lambda g, m, n: (0, 0)),
            _mod_spec(scale, tm),
            _mod_spec(shift, tm),
            pl.BlockSpec((D_MODEL, PROJ_TN), lambda g, m, n: (0, n)),
            pl.BlockSpec((3, tm, LANES), lambda g, m, n: (0, m, 0)),
        ],
        out_specs=pl.BlockSpec((1, tm, PROJ_TN), lambda g, m, n: (g, m, n)),
        out_shape=jax.ShapeDtypeStruct((G, R, P_WIDTH), f32),
        scratch_shapes=[pltpu.VMEM((tm, D_MODEL), bf16)],
        compiler_params=_cparams(("parallel", "parallel", "arbitrary")),
        name="norm_proj",
    )(x, norm_g.reshape(1, D_MODEL), scale, shift, w_bf, tab)


def _rope_tables(pos):
    half = ROT_DIM // 2
    inv_freq = ROPE_THETA ** (-jnp.arange(half, dtype=f32) * (2.0 / ROT_DIM))
    ang = pos.astype(f32)[:, None] * inv_freq[None, :]
    cos, sin = jnp.cos(ang), jnp.sin(ang)
    t = pos.shape[0]
    z8 = jnp.zeros((t, half), f32)
    rest = HEAD_DIM - ROT_DIM
    cos64 = jnp.concatenate([cos, cos, jnp.ones((t, rest), f32)], axis=1)
    up64 = jnp.concatenate([-sin, z8, jnp.zeros((t, rest), f32)], axis=1)
    dn64 = jnp.concatenate([z8, sin, jnp.zeros((t, rest), f32)], axis=1)
    rep = LANES // HEAD_DIM
    return jnp.stack([jnp.tile(a, (1, rep)) for a in (cos64, up64, dn64)])


ATTN_BLOCKS = 4


def _attn_prompt_kernel(sinks_ref, q_ref, kc_ref, kp_ref, vc_ref, vp_ref, ga_ref, o_ref):
    n = pl.program_id(1)
    wn = WINDOW
    half = HEAD_DIM
    k_t_all = jnp.concatenate([kp_ref[0], kc_ref[0]], axis=0).T.astype(bf16)
    v_all = jnp.concatenate([vp_ref[0], vc_ref[0]], axis=0)

    qi = lax.broadcasted_iota(jnp.int32, (2 * wn, 2 * wn), 0) & (wn - 1)
    kj = lax.broadcasted_iota(jnp.int32, (2 * wn, 2 * wn), 1)
    rel = wn + qi - kj
    band = (rel >= 0) & (rel <= wn)
    top = lax.broadcasted_iota(jnp.int32, (2 * wn, 1), 0) < wn
    lo = lax.broadcasted_iota(jnp.int32, (2 * wn, LANES), 1) < half
    zeros_k = jnp.zeros((half, 2 * wn), bf16)
    ones_lo = jnp.where(lo, 1.0, 0.0).astype(bf16)
    ones_hi = jnp.where(lo, 0.0, 1.0).astype(bf16)

    for sb in range(q_ref.shape[1] // wn):
        rows = slice(sb * wn, (sb + 1) * wn)
        q = q_ref[0, rows, :].astype(bf16)
        ga = ga_ref[0, rows, :]
        k_t = k_t_all[:, sb * wn:(sb + 2) * wn]
        vcat = v_all[sb * wn:(sb + 2) * wn]
        mask = band & ((kj >= wn) | (n > 0)) if sb == 0 else band
        for j in range(N_KV_HEADS // 2):
            vblk = vcat[:, j * LANES:(j + 1) * LANES]
            vswap = pltpu.roll(vblk, half, 1)
            for g in (2 * j, 2 * j + 1):
                own, other = (vblk, vswap) if g % 2 == 0 else (vswap, vblk)
                v_lo = jnp.where(lo, own, 0.0).astype(bf16)
                v_hi = jnp.where(lo, 0.0, other).astype(bf16)
                rhs_pv = jnp.concatenate([jnp.concatenate([v_lo, ones_lo], axis=1),
                                          jnp.concatenate([v_hi, ones_hi], axis=1)], axis=0)
                kg = k_t[g * half:(g + 1) * half, :]
                rhs_qk = jnp.concatenate([jnp.concatenate([kg, zeros_k], axis=0),
                                          jnp.concatenate([zeros_k, kg], axis=0)], axis=1)
                b0, b1 = 2 * g, 2 * g + 1
                qg = jnp.concatenate([q[:, b0 * LANES:(b0 + 1) * LANES], q[:, b1 * LANES:(b1 + 1) * LANES]],
                                     axis=0)
                s_all = jnp.dot(qg, rhs_qk, preferred_element_type=f32)
                ps, es = [], []
                for hh in range(2):
                    s = jnp.where(mask, s_all[:, hh * 2 * wn:(hh + 1) * 2 * wn], NEG_BIG)
                    sink = jnp.where(top, sinks_ref[2 * b0 + hh], sinks_ref[2 * b1 + hh])
                    m = jnp.maximum(jnp.max(s, axis=-1, keepdims=True), sink)
                    ps.append(jnp.exp(s - m).astype(bf16))
                    es.append(jnp.exp(sink - m))
                res = jnp.dot(jnp.concatenate(ps, axis=1), rhs_pv, preferred_element_type=f32)
                out = res[:, :LANES] / (res[:, LANES:] + jnp.where(lo, es[0], es[1]))
                for i, blk in enumerate((b0, b1)):
                    sl = slice(blk * LANES, (blk + 1) * LANES)
                    o_ref[0, rows, sl] = (out[i * wn:(i + 1) * wn] * _silu(ga[:, sl])).astype(o_ref.dtype)


def _attn_prompt(proj, sinks):
    B, T, _ = proj.shape
    rows = ATTN_BLOCKS * WINDOW
    kvb = KV_WIDTH
    prev = lambda b, n: jnp.maximum(n * ATTN_BLOCKS - 1, 0)
    return pl.pallas_call(
        _attn_prompt_kernel,
        grid=(B, T // rows),
        in_specs=[
            pl.BlockSpec(memory_space=pltpu.SMEM),
            pl.BlockSpec((1, rows, ATT_WIDTH), lambda b, n: (b, n, P_Q // ATT_WIDTH)),
            pl.BlockSpec((1, rows, kvb), lambda b, n: (b, n, P_KA // kvb)),
            pl.BlockSpec((1, WINDOW, kvb), lambda b, n: (b, prev(b, n), P_KA // kvb)),
            pl.BlockSpec((1, rows, kvb), lambda b, n: (b, n, P_VA // kvb)),
            pl.BlockSpec((1, WINDOW, kvb), lambda b, n: (b, prev(b, n), P_VA // kvb)),
            pl.BlockSpec((1, rows, ATT_WIDTH), lambda b, n: (b, n, P_GA // ATT_WIDTH)),
        ],
        out_specs=pl.BlockSpec((1, rows, ATT_WIDTH), lambda b, n: (b, n, 0)),
        out_shape=jax.ShapeDtypeStruct((B, T, ATT_WIDTH), bf16),
        compiler_params=_cparams(("parallel", "arbitrary")),
        name="attn_prompt",
    )(sinks, proj, proj, proj, proj, proj, proj)


SAMPLE_ROWS = 16


def _attn_sample_kernel(sinks_ref, q_ref, kn_ref, vn_ref, ga_ref, ck_ref, cv_ref, *refs):
    o_ref, nk_ref, nv_ref = refs[-3:]
    rb = q_ref.shape[1]
    q_all, kn_all, vn_all = q_ref[0], kn_ref[0], vn_ref[0]
    work = [(r, g) for r in range(rb) for g in range(N_KV_HEADS)]
    span = lambda g: slice(g * HEAD_DIM, (g + 1) * HEAD_DIM)
    ckb = [ck_ref[0, r].astype(bf16) for r in range(rb)]
    cvb = [cv_ref[0, r].astype(bf16) for r in range(rb)]
    qg = {(r, g): jnp.concatenate(
        [q_all[r:r + 1, (g * GQA_GROUP + i) * HEAD_DIM:(g * GQA_GROUP + i + 1) * HEAD_DIM]
         for i in range(GQA_GROUP)], axis=0) for r, g in work}
    s = {rg: lax.dot_general(qg[rg].astype(bf16), ckb[rg[0]][:, span(rg[1])], _NT,
                             preferred_element_type=f32) for rg in work}
    outs = {}
    for r, g in work:
        s_new = jnp.sum(qg[r, g] * kn_all[r:r + 1, span(g)], axis=-1, keepdims=True)
        sink = sinks_ref[g * GQA_GROUP:(g + 1) * GQA_GROUP, :]
        m = jnp.maximum(jnp.maximum(jnp.max(s[r, g], axis=-1, keepdims=True), s_new), sink)
        p = jnp.exp(s[r, g] - m)
        p_new = jnp.exp(s_new - m)
        den = jnp.sum(p, axis=-1, keepdims=True) + p_new + jnp.exp(sink - m)
        o = (jnp.dot(p.astype(bf16), cvb[r][:, span(g)], preferred_element_type=f32)
             + p_new * vn_all[r:r + 1, span(g)]) / den
        outs[r, g] = [o[i:i + 1, :] for i in range(GQA_GROUP)]
    att = jnp.concatenate(
        [jnp.concatenate([h for g in range(N_KV_HEADS) for h in outs[r, g]], axis=1) for r in range(rb)],
        axis=0)
    o_ref[0] = (att * _silu(ga_ref[0])).astype(o_ref.dtype)
    last = lax.broadcasted_iota(jnp.int32, (WINDOW, KV_WIDTH), 0) == WINDOW - 1
    for r in range(rb):
        nk_ref[0, r] = jnp.where(last, kn_all[r:r + 1], pltpu.roll(ck_ref[0, r], WINDOW - 1, 0))
        nv_ref[0, r] = jnp.where(last, vn_all[r:r + 1], pltpu.roll(cv_ref[0, r], WINDOW - 1, 0))


def _attn_sample(proj, ck_all, cv_all, layer, stacked, sinks):
    Bd = proj.shape[1]
    rb = SAMPLE_ROWS
    col = lambda w, off: pl.BlockSpec((1, rb, w), lambda i: (0, i, off // w))
    cache = pl.BlockSpec((1, rb, WINDOW, KV_WIDTH), lambda i: (layer, i, 0, 0))
    args = [sinks.reshape(N_Q_HEADS, 1), proj, proj, proj, proj, ck_all, cv_all, *stacked]
    hbm = pl.BlockSpec(memory_space=pl.ANY)
    return pl.pallas_call(
        _attn_sample_kernel,
        grid=(Bd // rb,),
        in_specs=[
            pl.BlockSpec((N_Q_HEADS, 1), lambda i: (0, 0)),
            col(ATT_WIDTH, P_Q), col(KV_WIDTH, P_KA), col(KV_WIDTH, P_VA), col(ATT_WIDTH, P_GA),
            cache, cache, hbm, hbm,
        ],
        out_specs=[pl.BlockSpec((1, rb, ATT_WIDTH), lambda i: (0, i, 0)), cache, cache],
        out_shape=[
            jax.ShapeDtypeStruct((1, Bd, ATT_WIDTH), bf16),
            jax.ShapeDtypeStruct(ck_all.shape, f32),
            jax.ShapeDtypeStruct(cv_all.shape, f32),
        ],
        input_output_aliases={len(args) - 2: 1, len(args) - 1: 2},
        compiler_params=_cparams(("parallel",)),
        name="attn_sample",
    )(*args)


def _rwkv_prep_kernel(r_ref, kr_ref, vr_ref, wa_ref, pr_ref, pkr_ref, pvr_ref, pwa_ref,
                      mu_r_ref, mu_kr_ref, mu_vr_ref, mu_wa_ref, w0_ref, a0_ref, kk_ref, ka_ref,
                      wd_ref, wi_ref, ro_ref, wo_ref, ko_ref, vo_ref, kko_ref, ao_ref):
    def mixed(cur_ref, prev_ref, mu_ref):
        cur = cur_ref[0]
        return cur + (prev_ref[0] - cur) * mu_ref[...]

    r = mixed(r_ref, pr_ref, mu_r_ref)
    kr = mixed(kr_ref, pkr_ref, mu_kr_ref)
    vr = mixed(vr_ref, pvr_ref, mu_vr_ref)
    wa = mixed(wa_ref, pwa_ref, mu_wa_ref)
    hi = lax.Precision.HIGHEST
    z = w0_ref[...] + jnp.dot(jnp.tanh(wa), wd_ref[...], precision=hi, preferred_element_type=f32)
    nz = -z
    softplus = jnp.maximum(nz, 0.0) + jnp.log1p(jnp.exp(-jnp.abs(nz)))
    w_log = -softplus - 0.5
    a = jax.nn.sigmoid(a0_ref[...] + jnp.dot(wa, wi_ref[...], precision=hi, preferred_element_type=f32))
    ro_ref[0] = r
    wo_ref[0] = -jnp.exp(w_log)
    ko_ref[0] = kr * (1.0 + (a - 1.0) * ka_ref[...])
    vo_ref[0] = vr
    kko_ref[0] = kr * kk_ref[...]
    ao_ref[0] = a


def _rwkv_prep(proj, prev, mu, w0, a0, k_k, k_a, wd_pad, wi_pad, tm):
    G, R, _ = proj.shape
    W = RWKV_WIDTH
    cur_specs = [
        pl.BlockSpec((1, tm, W), lambda g, m: (g, m, P_R // W)),
        pl.BlockSpec((1, tm, W), lambda g, m: (g, m, P_KR // W)),
        pl.BlockSpec((1, tm, W), lambda g, m: (g, m, P_VR // W)),
        pl.BlockSpec((1, tm, LANES), lambda g, m: (g, m, P_WA // LANES)),
    ]
    prev_specs = [
        pl.BlockSpec((1, tm, W), lambda g, m: (g, m, 0)),
        pl.BlockSpec((1, tm, W), lambda g, m: (g, m, 0)),
        pl.BlockSpec((1, tm, W), lambda g, m: (g, m, 0)),
        pl.BlockSpec((1, tm, LANES), lambda g, m: (g, m, 0)),
    ]
    prev_args = list(prev)
    vec = lambda w: pl.BlockSpec((1, w), lambda g, m: (0, 0))
    lora = pl.BlockSpec((LORA_PAIR, W), lambda g, m: (0, 0))
    out_spec = pl.BlockSpec((1, tm, W), lambda g, m: (g, m, 0))
    return pl.pallas_call(
        _rwkv_prep_kernel,
        grid=(G, R // tm),
        in_specs=cur_specs + prev_specs + [vec(W), vec(W), vec(W), vec(LANES),
                                           vec(W), vec(W), vec(W), vec(W), lora, lora],
        out_specs=[out_spec] * 6,
        out_shape=[jax.ShapeDtypeStruct((G, R, W), f32)] * 6,
        compiler_params=_cparams(("parallel", "arbitrary")),
        name="rwkv_prep",
    )(proj, proj, proj, proj, *prev_args, *mu, w0, a0, k_k, k_a, wd_pad, wi_pad)


def _wkv_kernel(tc, r_ref, w_ref, k_ref, v_ref, kk_ref, a_ref, rk_ref, lnw_ref, lnb_ref, s0_ref,
                *refs):
    y_ref, s_ref = refs[-2:]

    @pl.when(pl.program_id(1) == 0)
    def _():
        s_ref[...] = s0_ref[...]

    eye = (lax.broadcasted_iota(jnp.int32, (RW_HEAD, RW_HEAD), 0)
           == lax.broadcasted_iota(jnp.int32, (RW_HEAD, RW_HEAD), 1)).astype(f32)
    rk = rk_ref[...]
    lnw = lnw_ref[...]
    lnb = lnb_ref[...]

    def step(t, carry):
        r, lw, k, v, kkr, a = (ref[0, t] for ref in (r_ref, w_ref, k_ref, v_ref, kk_ref, a_ref))
        w = jnp.exp(lw)
        norm = jnp.sqrt(jnp.sum(kkr * kkr, axis=-1, keepdims=True))
        kk = kkr / jnp.maximum(norm, 1e-12)
        b = kk * a
        bonus = jnp.sum(r * k * rk, axis=-1, keepdims=True) * v
        rows = []
        for h in range(RW_HEADS):
            hs = slice(h, h + 1)
            S = s_ref[0, 0, h]
            sa = jnp.sum(S * (-kk[hs]), axis=-1, keepdims=True)
            v_col = jnp.sum(eye * v[hs], axis=-1, keepdims=True)
            S = S * w[hs] + sa * b[hs] + v_col * k[hs]
            s_ref[0, 0, h] = S
            y = jnp.sum(S * r[hs], axis=-1, keepdims=True)
            mu = jnp.mean(y, axis=0, keepdims=True)
            var = jnp.mean(jnp.square(y - mu), axis=0, keepdims=True)
            yn = (y - mu) * lax.rsqrt(var + GN_EPS)
            rows.append(jnp.sum(yn * eye, axis=0, keepdims=True))
        y_ref[0, t] = jnp.concatenate(rows, axis=0) * lnw + lnb + bonus
        return carry

    lax.fori_loop(0, tc, step, 0)


def _wkv(r, w, k, v, kk, a, r_k, ln_w, ln_b, s0_all, layer, stacked, tc):
    B, T = r.shape[0], r.shape[1]
    hd = (RW_HEADS, RW_HEAD)
    heads = lambda t: t.reshape(B, T, *hd)
    seq = pl.BlockSpec((1, tc, *hd), lambda b, c: (b, c, 0, 0))
    par = pl.BlockSpec(hd, lambda b, c: (0, 0))
    state = pl.BlockSpec((1, 1, RW_HEADS, RW_HEAD, RW_HEAD), lambda b, c: (layer, b, 0, 0, 0))
    args = [heads(r), heads(w), heads(k), heads(v), heads(kk), heads(a),
            r_k, ln_w.reshape(hd), ln_b.reshape(hd), s0_all, stacked]
    y, s_new = pl.pallas_call(
        functools.partial(_wkv_kernel, tc),
        grid=(B, T // tc),
        in_specs=[seq] * 6 + [par] * 3 + [state, pl.BlockSpec(memory_space=pl.ANY)],
        out_specs=[seq, state],
        out_shape=[
            jax.ShapeDtypeStruct((B, T, *hd), f32),
            jax.ShapeDtypeStruct(s0_all.shape, f32),
        ],
        input_output_aliases={len(args) - 1: 1},
        compiler_params=_cparams(("parallel", "arbitrary")),
        name="wkv_steps",
    )(*args)
    return y.reshape(B, T, RWKV_WIDTH), s_new


CHUNK = 64
PAIR = 2 * RW_HEAD
N_PAIRS = RW_HEADS // 2


def _mmb(a, b, dims=_NN):
    return lax.dot_general(a.astype(bf16), b.astype(bf16), dims, preferred_element_type=f32)


def _lora_dot(x, wh_ref, wl_ref):
    xh, xl = _split_bf16(x)
    rows = x.shape[0]
    both = jnp.dot(jnp.concatenate([xh, xl], axis=0), wh_ref[...], preferred_element_type=f32)
    return both[:rows] + both[rows:] + jnp.dot(xh, wl_ref[...], preferred_element_type=f32)


CHUNKS_PER_STEP = 4


def _wkv_chunk_kernel(r_ref, kr_ref, vr_ref, wa_ref, gr_ref, pr_ref, pkr_ref, pvr_ref, pwa_ref,
                      mu_r_ref, mu_kr_ref, mu_vr_ref, mu_wa_ref, w0_ref, a0_ref, kk_ref, ka_ref,
                      wdh_ref, wdl_ref, wih_ref, wil_ref, rk_ref, lnw_ref, lnb_ref, s0_ref,
                      y_ref, s_ref, sp_ref, cr_ref, ckr_ref, cvr_ref, cwa_ref):
    C = CHUNK
    rows_blk = r_ref.shape[1]
    c = pl.program_id(1)

    @pl.when(c == 0)
    def _():
        for p in range(N_PAIRS):
            sp_ref[p] = jnp.concatenate([s0_ref[0, 2 * p], s0_ref[0, 2 * p + 1]], axis=1)
        for carry, first in ((cr_ref, pr_ref), (ckr_ref, pkr_ref), (cvr_ref, pvr_ref), (cwa_ref, pwa_ref)):
            carry[...] = first[0]

    def mixed(cur_ref, carry_ref, mu_ref):
        cur = cur_ref[0]
        first = lax.broadcasted_iota(jnp.int32, cur.shape, 0) == 0
        prev = jnp.where(first, carry_ref[...], pltpu.roll(cur, 1, 0))
        carry_ref[...] = cur[rows_blk - 1:rows_blk, :]
        return cur + (prev - cur) * mu_ref[...]

    r_blk = mixed(r_ref, cr_ref, mu_r_ref)
    kr_blk = mixed(kr_ref, ckr_ref, mu_kr_ref)
    v_blk = mixed(vr_ref, cvr_ref, mu_vr_ref)
    wa = mixed(wa_ref, cwa_ref, mu_wa_ref)
    nz = -(w0_ref[...] + _lora_dot(jnp.tanh(wa), wdh_ref, wdl_ref))
    softplus = jnp.maximum(nz, 0.0) + jnp.log1p(jnp.exp(-jnp.abs(nz)))
    lw_blk = -jnp.exp(-softplus - 0.5)
    icl_blk = jax.nn.sigmoid(a0_ref[...] + _lora_dot(wa, wih_ref, wil_ref))
    k_blk = kr_blk * (1.0 + (icl_blk - 1.0) * ka_ref[...])
    kkr_blk = kr_blk * kk_ref[...]

    row = lax.broadcasted_iota(jnp.int32, (PAIR, PAIR), 0)
    col = lax.broadcasted_iota(jnp.int32, (PAIR, PAIR), 1)
    tril = row >= col
    stril = row > col
    tril2 = jnp.concatenate([tril, tril], axis=1)
    eye = (row == col).astype(f32)
    lane_lo = lax.broadcasted_iota(jnp.int32, (C, PAIR), 1) < RW_HEAD
    pairs = range(N_PAIRS)
    sls = [slice(p * PAIR, (p + 1) * PAIR) for p in pairs]

    def bd(x):
        zero = jnp.zeros_like(x)
        return jnp.concatenate([jnp.where(lane_lo, x, zero), jnp.where(lane_lo, zero, x)], axis=0)

    def head_sums(x):
        lo_sum = jnp.sum(jnp.where(lane_lo, x, 0.0), axis=-1, keepdims=True)
        hi_sum = jnp.sum(jnp.where(lane_lo, 0.0, x), axis=-1, keepdims=True)
        return jnp.where(lane_lo, lo_sum, hi_sum)

    state = [sp_ref[p] for p in pairs]
    for j in range(rows_blk // C):
        rs = slice(j * C, (j + 1) * C)
        lw = lw_blk[rs]
        width = lw.shape[1]
        lw_a = lw.astype(bf16)
        rest = lw - lw_a.astype(f32)
        lw_b = rest.astype(bf16)
        lw_c = (rest - lw_b.astype(f32)).astype(bf16)
        g3 = jnp.dot(tril[:C, :C].astype(bf16), jnp.concatenate([lw_a, lw_b, lw_c], axis=1),
                     preferred_element_type=f32)
        g = g3[:, :width] + g3[:, width:2 * width] + g3[:, 2 * width:]
        e_g = jnp.exp(g)
        e_ng = jnp.exp(-g)
        e_gm = jnp.exp(g - lw)
        e_end = e_g[C - 1:C, :]

        ins = [[t[rs, sl] for t in (r_blk, k_blk, v_blk, kkr_blk, icl_blk)] for sl in sls]
        norms = [jnp.sqrt(head_sums(x[3] * x[3])) for x in ins]
        at, rt, bt, kt, vb = ([] for _ in range(5))
        for (r, k, v, kkr, icl), norm, sl in zip(ins, norms, sls):
            kk = kkr / jnp.maximum(norm, 1e-12)
            b = kk * icl
            at.append(bd((-kk * e_gm[:, sl]).astype(bf16)))
            rt.append(bd((r * e_g[:, sl]).astype(bf16)))
            bt.append(bd((b * e_ng[:, sl]).astype(bf16)))
            kt.append(bd((k * e_ng[:, sl]).astype(bf16)))
            vb.append(bd(v.astype(bf16)))

        gram = [_mmb(jnp.concatenate([at[p], rt[p]], axis=0), jnp.concatenate([bt[p], kt[p]], axis=0), _NT)
                for p in pairs]
        lmat = [jnp.where(stril, gm[:PAIR, :PAIR], 0.0) for gm in gram]
        mv = [_mmb(jnp.where(stril, gram[p][:PAIR, PAIR:], 0.0), vb[p]) for p in pairs]
        lower = [jnp.where(tril2, gm[PAIR:, :], 0.0).astype(bf16) for gm in gram]

        tinv = [eye + lm for lm in lmat]
        pw = [_mmb(lm, lm) for lm in lmat]
        for _ in range(4):
            z = [_mmb(jnp.concatenate([x.astype(bf16), t.astype(bf16)], axis=0), x)
                 for t, x in zip(tinv, pw)]
            pw = [zz[:PAIR] for zz in z]
            tinv = [t + zz[PAIR:] for t, zz in zip(tinv, z)]
        tinv = [t + _mmb(t, x) for t, x in zip(tinv, pw)]

        wx = [_mmb(tinv[p], jnp.concatenate([at[p], mv[p].astype(bf16)], axis=1)) for p in pairs]
        uy0 = [_mmb(jnp.concatenate([wx[p][:, :PAIR].astype(bf16), rt[p]], axis=0),
                    bd(state[p].astype(bf16)), _NT) for p in pairs]
        uv = [jnp.concatenate([(uy0[p][:PAIR] + wx[p][:, PAIR:]).astype(bf16), vb[p]], axis=0)
              for p in pairs]
        ys = [uy0[p][PAIR:] + _mmb(lower[p], uv[p]) for p in pairs]
        s_add = [_mmb(uv[p], jnp.concatenate([bt[p], kt[p]], axis=0), _TN) for p in pairs]
        state = [(state[p] + s_add[p][:RW_HEAD] + s_add[p][RW_HEAD:]) * e_end[:, sls[p]] for p in pairs]

        ys = [y[:C] + y[C:] for y in ys]
        mus = [head_sums(y) * (1.0 / RW_HEAD) for y in ys]
        ds = [y - mu for y, mu in zip(ys, mus)]
        var = [head_sums(d * d) * (1.0 / RW_HEAD) for d in ds]
        bonus = [head_sums(x[0] * x[1] * rk_ref[:, sl]) * x[2] for x, sl in zip(ins, sls)]
        for p in pairs:
            sl = sls[p]
            y_rw = ds[p] * lax.rsqrt(var[p] + GN_EPS) * lnw_ref[:, sl] + lnb_ref[:, sl] + bonus[p]
            y_ref[0, rs, sl] = (y_rw * _silu(gr_ref[0, rs, sl])).astype(y_ref.dtype)

    for p in pairs:
        sp_ref[p] = state[p]

    @pl.when(c == pl.num_programs(1) - 1)
    def _():
        for p in range(N_PAIRS):
            s_ref[0, 2 * p] = state[p][:, :RW_HEAD]
            s_ref[0, 2 * p + 1] = state[p][:, RW_HEAD:]


def _wkv_chunked(proj, prev, mu, w0, a0, k_k, k_a, lora, r_k, ln_w, ln_b, s0):
    B, T, _ = proj.shape
    W = RWKV_WIDTH
    rows = CHUNKS_PER_STEP * CHUNK
    col = lambda w, off: pl.BlockSpec((1, rows, w), lambda b, c: (b, c, off // w))
    first = lambda w: pl.BlockSpec((1, 1, w), lambda b, c: (b, 0, 0))
    vec = lambda w: pl.BlockSpec((1, w), lambda b, c: (0, 0))
    lora_spec = pl.BlockSpec((LORA_PAIR, W), lambda b, c: (0, 0))
    state = pl.BlockSpec((1, RW_HEADS, RW_HEAD, RW_HEAD), lambda b, c: (b, 0, 0, 0))
    return pl.pallas_call(
        _wkv_chunk_kernel,
        grid=(B, T // rows),
        in_specs=[col(W, P_R), col(W, P_KR), col(W, P_VR), col(LANES, P_WA), col(W, P_GR),
                  first(W), first(W), first(W), first(LANES),
                  vec(W), vec(W), vec(W), vec(LANES), vec(W), vec(W), vec(W), vec(W),
                  lora_spec, lora_spec, lora_spec, lora_spec, vec(W), vec(W), vec(W), state],
        out_specs=[col(W, 0), state],
        out_shape=[
            jax.ShapeDtypeStruct((B, T, W), bf16),
            jax.ShapeDtypeStruct((B, RW_HEADS, RW_HEAD, RW_HEAD), f32),
        ],
        scratch_shapes=[pltpu.VMEM((N_PAIRS, RW_HEAD, PAIR), f32),
                        pltpu.VMEM((1, W), f32), pltpu.VMEM((1, W), f32), pltpu.VMEM((1, W), f32),
                        pltpu.VMEM((1, LANES), f32)],
        compiler_params=_cparams(("parallel", "arbitrary")),
        name="wkv_chunks",
    )(proj, proj, proj, proj, proj, *prev, *mu, w0, a0, k_k, k_a, *lora,
      r_k.reshape(1, W), ln_w.reshape(1, W), ln_b.reshape(1, W), s0)


def _post_kernel(final, gated, att_ref, y_ref, *refs):
    if gated:
        y = y_ref[0]
    else:
        y = (y_ref[0] * _silu(refs[0][0])).astype(bf16)
        refs = refs[1:]
    x_ref, gate_ref, w_ref, fg_ref, o_ref = refs
    cat = jnp.concatenate([att_ref[0], y], axis=1)
    out = jnp.dot(cat, w_ref[...], preferred_element_type=f32)
    x = x_ref[0] + gate_ref[0] * out
    if final:
        ms = jnp.mean(x * x, axis=-1, keepdims=True)
        x = x * lax.rsqrt(ms + NORM_EPS) * fg_ref[...]
    o_ref[0] = x


def _post(att, y_rw, proj, x, gate, w_out_bf, final_g, final, tm):
    G, R, _ = x.shape
    W = RWKV_WIDTH
    gated = proj is None
    gate_in = [] if gated else [pl.BlockSpec((1, tm, W), lambda g, m: (g, m, P_GR // W))]
    gate_arg = [] if gated else [proj]
    return pl.pallas_call(
        functools.partial(_post_kernel, final, gated),
        grid=(G, R // tm),
        in_specs=[
            pl.BlockSpec((1, tm, ATT_WIDTH), lambda g, m: (g, m, 0)),
            pl.BlockSpec((1, tm, W), lambda g, m: (g, m, 0)),
            *gate_in,
            pl.BlockSpec((1, tm, D_MODEL), lambda g, m: (g, m, 0)),
            _mod_spec(gate, tm),
            pl.BlockSpec((D_MODEL, D_MODEL), lambda g, m: (0, 0)),
            pl.BlockSpec((1, D_MODEL), lambda g, m: (0, 0)),
        ],
        out_specs=pl.BlockSpec((1, tm, D_MODEL), lambda g, m: (g, m, 0)),
        out_shape=jax.ShapeDtypeStruct((G, R, D_MODEL), f32),
        compiler_params=_cparams(("parallel", "parallel")),
        name="post_proj",
    )(att, y_rw, *gate_arg, x, gate, w_out_bf, final_g.reshape(1, D_MODEL))


def _arrange_w_in(w):
    pad = jnp.zeros((w.shape[0], P_WIDTH - IN_WIDTH), w.dtype)
    parts = [w[:, Q_OFF:KA_OFF], w[:, GA_OFF:GR_OFF], w[:, GR_OFF:IN_WIDTH], w[:, R_OFF:WD_OFF],
             w[:, KA_OFF:R_OFF], w[:, WD_OFF:GA_OFF], pad]
    return jnp.concatenate(parts, axis=1).astype(bf16)


def _shift_cols(t):
    return jnp.concatenate([t[..., P_R:P_KA], t[..., P_WA:P_WA + LORA_PAIR]], axis=-1)


def kernel(x_prompt, x_sample, cache_k, cache_v, state_wkv, state_shift, c_prompt, c_sample,
           norm_g, w_ada, b_ada, w_in, mu_shift, w0, w_decay, a0, w_iclr, k_k, k_a, r_k,
           ln_w, ln_b, sinks, w_out, final_g):
    Bp, Tp = x_prompt.shape[0], x_prompt.shape[1]
    Bd = x_sample.shape[0]
    W = RWKV_WIDTH

    n_c = Bp + Bd
    c_rows = -(-n_c // 16) * 16
    c_all = jnp.concatenate([c_prompt, c_sample, jnp.zeros((c_rows - n_c, D_MODEL), f32)], axis=0)
    mod = _ada(c_all, w_ada, b_ada)

    tab_p = _rope_tables(jnp.arange(Tp, dtype=jnp.int32))
    tab_s = _rope_tables(jnp.full((Bd,), PAST_LEN, jnp.int32))

    hp = x_prompt
    hs = x_sample.reshape(1, Bd, D_MODEL)
    s0_p = jnp.zeros((Bp, RW_HEADS, RW_HEAD, RW_HEAD), f32)
    shift0_p = [jnp.zeros((Bp, 1, w), f32) for w in (W, W, W, LORA_PAIR)]
    ck_all = cache_k.reshape(DEPTH, Bd, WINDOW, KV_WIDTH)
    cv_all = cache_v.reshape(DEPTH, Bd, WINDOW, KV_WIDTH)
    new_state_s = jnp.zeros(state_wkv.shape, f32)
    new_cache_s = [jnp.zeros(ck_all.shape, f32), jnp.zeros(cv_all.shape, f32)]
    outs = {k: [] for k in ("kp", "vp", "sp", "shp", "shs")}
    for l in range(DEPTH):
        final = l == DEPTH - 1
        w_bf = _arrange_w_in(w_in[l])
        w_out_bf = w_out[l].astype(bf16)
        mu_l = mu_shift[l]
        mu = [mu_l[0:W].reshape(1, W), mu_l[W:2 * W].reshape(1, W), mu_l[2 * W:3 * W].reshape(1, W),
              mu_l[3 * W:].reshape(1, LORA_PAIR)]
        vecs = [t[l].reshape(1, W) for t in (w0, a0, k_k, k_a)]
        wd_pad = jnp.concatenate([w_decay[l], jnp.zeros((ICLR_LORA, W), f32)], axis=0)
        wi_pad = jnp.concatenate([jnp.zeros((DECAY_LORA, W), f32), w_iclr[l]], axis=0)
        shift_p, scale_p, gate_p = (mod[l, :Bp, i * D_MODEL:(i + 1) * D_MODEL].reshape(Bp, 1, D_MODEL)
                                    for i in range(3))
        shift_s, scale_s, gate_s = (mod[l, Bp:n_c, i * D_MODEL:(i + 1) * D_MODEL].reshape(1, Bd, D_MODEL)
                                    for i in range(3))

        proj = _norm_proj(hp, norm_g[l], scale_p, shift_p, w_bf, tab_p, tm=1024)
        att = _attn_prompt(proj, sinks[l])
        lora = [piece for wp in (wd_pad, wi_pad) for piece in _split_bf16(wp)]
        y_rw, s_t = _wkv_chunked(proj, shift0_p, mu, *vecs, lora, r_k[l], ln_w[l], ln_b[l], s0_p)
        hp = _post(att, y_rw, None, hp, gate_p, w_out_bf, final_g, final, tm=512)
        tail = proj[:, Tp - WINDOW:]
        outs["kp"].append(tail[..., P_KA:P_KA + KV_WIDTH].reshape(Bp, WINDOW, N_KV_HEADS, HEAD_DIM))
        outs["vp"].append(tail[..., P_VA:P_VA + KV_WIDTH].reshape(Bp, WINDOW, N_KV_HEADS, HEAD_DIM))
        outs["sp"].append(s_t)
        outs["shp"].append(_shift_cols(proj[:, Tp - 1]))

        proj = _norm_proj(hs, norm_g[l], scale_s, shift_s, w_bf, tab_s, tm=Bd)
        att, *new_cache_s = _attn_sample(proj, ck_all, cv_all, l, new_cache_s, sinks[l])
        sh = state_shift[l]
        prev = [sh[None, :, 0:W], sh[None, :, W:2 * W], sh[None, :, 2 * W:3 * W], sh[None, :, 3 * W:]]
        prep = _rwkv_prep(proj, prev, mu, *vecs, wd_pad, wi_pad, tm=Bd)
        prep = [t.reshape(Bd, 1, W) for t in prep]
        y_rw, new_state_s = _wkv(*prep, r_k[l], ln_w[l], ln_b[l], state_wkv, l, new_state_s, tc=1)
        hs = _post(att, y_rw.reshape(1, Bd, W), proj, hs, gate_s, w_out_bf, final_g, final, tm=Bd)
        outs["shs"].append(_shift_cols(proj[0]))

    st = lambda k: jnp.stack(outs[k])
    return (hp, hs.reshape(Bd, 1, D_MODEL), st("kp"), st("vp"), st("sp"), st("shp"),
            *(t.reshape(cache_k.shape) for t in new_cache_s), new_state_s, st("shs"))
```

```python
import functools

import jax
import jax.numpy as jnp
from jax import lax
from jax.experimental import pallas as pl
from jax.experimental.pallas import tpu as pltpu

f32 = jnp.float32
bf16 = jnp.bfloat16

D_MODEL = 2048
DEPTH = 2
PAST_LEN = 16384
ATT_WIDTH = 1024
RWKV_WIDTH = 1024
HEAD_DIM = 64
N_Q_HEADS = 16
N_KV_HEADS = 4
GQA_GROUP = 4
KV_WIDTH = 256
WINDOW = 128
ROT_DIM = 16
ROPE_THETA = 500000.0
RW_HEAD = 64
RW_HEADS = 16
DECAY_LORA = 64
ICLR_LORA = 64
LORA_PAIR = DECAY_LORA + ICLR_LORA
NORM_EPS = 1e-5
GN_EPS = 64e-5
NEG_BIG = -1e30

Q_OFF = 0
KA_OFF = Q_OFF + ATT_WIDTH
VA_OFF = KA_OFF + KV_WIDTH
R_OFF = VA_OFF + KV_WIDTH
KR_OFF = R_OFF + RWKV_WIDTH
VR_OFF = KR_OFF + RWKV_WIDTH
WD_OFF = VR_OFF + RWKV_WIDTH
AD_OFF = WD_OFF + DECAY_LORA
GA_OFF = AD_OFF + ICLR_LORA
GR_OFF = GA_OFF + ATT_WIDTH
IN_WIDTH = GR_OFF + RWKV_WIDTH
SHIFT_DIM = GA_OFF - R_OFF

LANES = 128
P_Q = 0
P_GA = 1024
P_GR = 2048
P_R = 3072
P_KR = 4096
P_VR = 5120
P_KA = 6144
P_VA = 6400
P_WA = 6656
P_WIDTH = 7168
PROJ_TN = 1024

VMEM_LIMIT = 56 * 1024 * 1024


_NN = (((1,), (0,)), ((), ()))
_NT = (((1,), (1,)), ((), ()))
_TN = (((0,), (0,)), ((), ()))


def _silu(x):
    return x * jax.nn.sigmoid(x)


def _cparams(sem):
    return pltpu.CompilerParams(dimension_semantics=sem, vmem_limit_bytes=VMEM_LIMIT)


def _split_bf16(x):
    hi = x.astype(bf16)
    return hi, (x - hi.astype(f32)).astype(bf16)


def _ada_kernel(c_ref, w_ref, b_ref, o_ref):
    ch, cl = _split_bf16(_silu(c_ref[...]))
    wh, wl = _split_bf16(w_ref[0])
    rows = ch.shape[0]
    both = jnp.dot(jnp.concatenate([ch, cl], axis=0), wh, preferred_element_type=f32)
    o_ref[0] = both[:rows] + both[rows:] + jnp.dot(ch, wl, preferred_element_type=f32) + b_ref[0]


def _ada(c_all, w_ada, b_ada):
    rows = c_all.shape[0]
    tn = 1536
    n_out = w_ada.shape[2]
    return pl.pallas_call(
        _ada_kernel,
        grid=(DEPTH, n_out // tn),
        in_specs=[
            pl.BlockSpec((rows, D_MODEL), lambda l, n: (0, 0)),
            pl.BlockSpec((1, D_MODEL, tn), lambda l, n: (l, 0, n)),
            pl.BlockSpec((1, 1, tn), lambda l, n: (l, 0, n)),
        ],
        out_specs=pl.BlockSpec((1, rows, tn), lambda l, n: (l, 0, n)),
        out_shape=jax.ShapeDtypeStruct((DEPTH, rows, n_out), f32),
        compiler_params=_cparams(("parallel", "parallel")),
        name="ada_mod",
    )(c_all, w_ada, b_ada.reshape(DEPTH, 1, n_out))


def _rope(x, tab):
    w = x.shape[1]
    reps = w // LANES
    cosf, up, dn = (jnp.concatenate([tab[i]] * reps, axis=1) for i in range(3))
    half = ROT_DIM // 2
    return x * cosf + pltpu.roll(x, w - half, 1) * up + pltpu.roll(x, half, 1) * dn


def _norm_proj_kernel(x_ref, g_ref, scale_ref, shift_ref, w_ref, tab_ref, o_ref, h_ref):
    n = pl.program_id(2)

    @pl.when(n == 0)
    def _():
        x = x_ref[0]
        ms = jnp.mean(x * x, axis=-1, keepdims=True)
        y = x * lax.rsqrt(ms + NORM_EPS) * g_ref[...]
        h_ref[...] = (y * (1.0 + scale_ref[0]) + shift_ref[0]).astype(bf16)

    q_tile = P_Q // PROJ_TN
    k_tile = P_KA // PROJ_TN
    tm = h_ref.shape[0]
    rc = min(tm, 256)

    def rows(i):
        rs = slice(i * rc, (i + 1) * rc)
        return rs, jnp.dot(h_ref[rs, :], w_ref[...], preferred_element_type=f32)

    @pl.when(n == q_tile)
    def _():
        for i in range(tm // rc):
            rs, res = rows(i)
            o_ref[0, rs, :] = _rope(res, tab_ref[:, rs, :]) * (HEAD_DIM ** -0.5)

    @pl.when(n == k_tile)
    def _():
        for i in range(tm // rc):
            rs, res = rows(i)
            o_ref[0, rs, :] = jnp.concatenate(
                [_rope(res[:, :KV_WIDTH], tab_ref[:, rs, :]), res[:, KV_WIDTH:]], axis=1)

    @pl.when((n != q_tile) & (n != k_tile))
    def _():
        o_ref[0] = jnp.dot(h_ref[...], w_ref[...], preferred_element_type=f32)


def _mod_spec(mod, tm):
    if mod.shape[1] == 1:
        return pl.BlockSpec((1, 1, D_MODEL), lambda g, m, *_: (g, 0, 0))
    return pl.BlockSpec((1, tm, D_MODEL), lambda g, m, *_: (g, m, 0))


def _norm_proj(x, norm_g, scale, shift, w_bf, tab, tm):
    G, R, _ = x.shape
    return pl.pallas_call(
        _norm_proj_kernel,
        grid=(G, R // tm, P_WIDTH // PROJ_TN),
        in_specs=[
            pl.BlockSpec((1, tm, D_MODEL), lambda g, m, n: (g, m, 0)),
            pl.BlockSpec((1, D_MODEL), lambda g, m, n: (0, 0)),
            _mod_spec(scale, tm),
            _mod_spec(shift, tm),
            pl.BlockSpec((D_MODEL, PROJ_TN), lambda g, m, n: (0, n)),
            pl.BlockSpec((3, tm, LANES), lambda g, m, n: (0, m, 0)),
        ],
        out_specs=pl.BlockSpec((1, tm, PROJ_TN), lambda g, m, n: (g, m, n)),
        out_shape=jax.ShapeDtypeStruct((G, R, P_WIDTH), f32),
        scratch_shapes=[pltpu.VMEM((tm, D_MODEL), bf16)],
        compiler_params=_cparams(("parallel", "parallel", "arbitrary")),
        name="norm_proj",
    )(x, norm_g.reshape(1, D_MODEL), scale, shift, w_bf, tab)


def _rope_tables(pos):
    half = ROT_DIM // 2
    inv_freq = ROPE_THETA ** (-jnp.arange(half, dtype=f32) * (2.0 / ROT_DIM))
    ang = pos.astype(f32)[:, None] * inv_freq[None, :]
    cos, sin = jnp.cos(ang), jnp.sin(ang)
    t = pos.shape[0]
    z8 = jnp.zeros((t, half), f32)
    rest = HEAD_DIM - ROT_DIM
    cos64 = jnp.concatenate([cos, cos, jnp.ones((t, rest), f32)], axis=1)
    up64 = jnp.concatenate([-sin, z8, jnp.zeros((t, rest), f32)], axis=1)
    dn64 = jnp.concatenate([z8, sin, jnp.zeros((t, rest), f32)], axis=1)
    rep = LANES // HEAD_DIM
    return jnp.stack([jnp.tile(a, (1, rep)) for a in (cos64, up64, dn64)])


ATTN_BLOCKS = 4


def _attn_prompt_kernel(sinks_ref, q_ref, kc_ref, kp_ref, vc_ref, vp_ref, ga_ref, o_ref):
    n = pl.program_id(1)
    wn = WINDOW
    half = HEAD_DIM
    k_t_all = jnp.concatenate([kp_ref[0], kc_ref[0]], axis=0).T.astype(bf16)
    v_all = jnp.concatenate([vp_ref[0], vc_ref[0]], axis=0)

    qi = lax.broadcasted_iota(jnp.int32, (2 * wn, 2 * wn), 0) & (wn - 1)
    kj = lax.broadcasted_iota(jnp.int32, (2 * wn, 2 * wn), 1)
    rel = wn + qi - kj
    band = (rel >= 0) & (rel <= wn)
    top = lax.broadcasted_iota(jnp.int32, (2 * wn, 1), 0) < wn
    lo = lax.broadcasted_iota(jnp.int32, (2 * wn, LANES), 1) < half
    zeros_k = jnp.zeros((half, 2 * wn), bf16)
    ones_lo = jnp.where(lo, 1.0, 0.0).astype(bf16)
    ones_hi = jnp.where(lo, 0.0, 1.0).astype(bf16)

    for sb in range(q_ref.shape[1] // wn):
        rows = slice(sb * wn, (sb + 1) * wn)
        q = q_ref[0, rows, :].astype(bf16)
        ga = ga_ref[0, rows, :]
        k_t = k_t_all[:, sb * wn:(sb + 2) * wn]
        vcat = v_all[sb * wn:(sb + 2) * wn]
        mask = band & ((kj >= wn) | (n > 0)) if sb == 0 else band
        for j in range(N_KV_HEADS // 2):
            vblk = vcat[:, j * LANES:(j + 1) * LANES]
            vswap = pltpu.roll(vblk, half, 1)
            for g in (2 * j, 2 * j + 1):
                own, other = (vblk, vswap) if g % 2 == 0 else (vswap, vblk)
                v_lo = jnp.where(lo, own, 0.0).astype(bf16)
                v_hi = jnp.where(lo, 0.0, other).astype(bf16)
                rhs_pv = jnp.concatenate([jnp.concatenate([v_lo, ones_lo], axis=1),
                                          jnp.concatenate([v_hi, ones_hi], axis=1)], axis=0)
                kg = k_t[g * half:(g + 1) * half, :]
                rhs_qk = jnp.concatenate([jnp.concatenate([kg, zeros_k], axis=0),
                                          jnp.concatenate([zeros_k, kg], axis=0)], axis=1)
                b0, b1 = 2 * g, 2 * g + 1
                qg = jnp.concatenate([q[:, b0 * LANES:(b0 + 1) * LANES], q[:, b1 * LANES:(b1 + 1) * LANES]],
                                     axis=0)
                s_all = jnp.dot(qg, rhs_qk, preferred_element_type=f32)
                ps, es = [], []
                for hh in range(2):
                    s = jnp.where(mask, s_all[:, hh * 2 * wn:(hh + 1) * 2 * wn], NEG_BIG)
                    sink = jnp.where(top, sinks_ref[2 * b0 + hh], sinks_ref[2 * b1 + hh])
                    m = jnp.maximum(jnp.max(s, axis=-1, keepdims=True), sink)
                    ps.append(jnp.exp(s - m).astype(bf16))
                    es.append(jnp.exp(sink - m))
                res = jnp.dot(jnp.concatenate(ps, axis=1), rhs_pv, preferred_element_type=f32)
                out = res[:, :LANES] / (res[:, LANES:] + jnp.where(lo, es[0], es[1]))
                for i, blk in enumerate((b0, b1)):
                    sl = slice(blk * LANES, (blk + 1) * LANES)
                    o_ref[0, rows, sl] = (out[i * wn:(i + 1) * wn] * _silu(ga[:, sl])).astype(o_ref.dtype)


def _attn_prompt(proj, sinks):
    B, T, _ = proj.shape
    rows = ATTN_BLOCKS * WINDOW
    kvb = KV_WIDTH
    prev = lambda b, n: jnp.maximum(n * ATTN_BLOCKS - 1, 0)
    return pl.pallas_call(
        _attn_prompt_kernel,
        grid=(B, T // rows),
        in_specs=[
            pl.BlockSpec(memory_space=pltpu.SMEM),
            pl.BlockSpec((1, rows, ATT_WIDTH), lambda b, n: (b, n, P_Q // ATT_WIDTH)),
            pl.BlockSpec((1, rows, kvb), lambda b, n: (b, n, P_KA // kvb)),
            pl.BlockSpec((1, WINDOW, kvb), lambda b, n: (b, prev(b, n), P_KA // kvb)),
            pl.BlockSpec((1, rows, kvb), lambda b, n: (b, n, P_VA // kvb)),
            pl.BlockSpec((1, WINDOW, kvb), lambda b, n: (b, prev(b, n), P_VA // kvb)),
            pl.BlockSpec((1, rows, ATT_WIDTH), lambda b, n: (b, n, P_GA // ATT_WIDTH)),
        ],
        out_specs=pl.BlockSpec((1, rows, ATT_WIDTH), lambda b, n: (b, n, 0)),
        out_shape=jax.ShapeDtypeStruct((B, T, ATT_WIDTH), bf16),
        compiler_params=_cparams(("parallel", "arbitrary")),
        name="attn_prompt",
    )(sinks, proj, proj, proj, proj, proj, proj)


SAMPLE_ROWS = 16


def _attn_sample_kernel(sinks_ref, q_ref, kn_ref, vn_ref, ga_ref, ck_ref, cv_ref, *refs):
    o_ref, nk_ref, nv_ref = refs[-3:]
    rb = q_ref.shape[1]
    q_all, kn_all, vn_all = q_ref[0], kn_ref[0], vn_ref[0]
    work = [(r, g) for r in range(rb) for g in range(N_KV_HEADS)]
    span = lambda g: slice(g * HEAD_DIM, (g + 1) * HEAD_DIM)
    ckb = [ck_ref[0, r].astype(bf16) for r in range(rb)]
    cvb = [cv_ref[0, r].astype(bf16) for r in range(rb)]
    qg = {(r, g): jnp.concatenate(
        [q_all[r:r + 1, (g * GQA_GROUP + i) * HEAD_DIM:(g * GQA_GROUP + i + 1) * HEAD_DIM]
         for i in range(GQA_GROUP)], axis=0) for r, g in work}
    s = {rg: lax.dot_general(qg[rg].astype(bf16), ckb[rg[0]][:, span(rg[1])], _NT,
                             preferred_element_type=f32) for rg in work}
    outs = {}
    for r, g in work:
        s_new = jnp.sum(qg[r, g] * kn_all[r:r + 1, span(g)], axis=-1, keepdims=True)
        sink = sinks_ref[g * GQA_GROUP:(g + 1) * GQA_GROUP, :]
        m = jnp.maximum(jnp.maximum(jnp.max(s[r, g], axis=-1, keepdims=True), s_new), sink)
        p = jnp.exp(s[r, g] - m)
        p_new = jnp.exp(s_new - m)
        den = jnp.sum(p, axis=-1, keepdims=True) + p_new + jnp.exp(sink - m)
        o = (jnp.dot(p.astype(bf16), cvb[r][:, span(g)], preferred_element_type=f32)
             + p_new * vn_all[r:r + 1, span(g)]) / den
        outs[r, g] = [o[i:i + 1, :] for i in range(GQA_GROUP)]
    att = jnp.concatenate(
        [jnp.concatenate([h for g in range(N_KV_HEADS) for h in outs[r, g]], axis=1) for r in range(rb)],
        axis=0)
    o_ref[0] = (att * _silu(ga_ref[0])).astype(o_ref.dtype)
    last = lax.broadcasted_iota(jnp.int32, (WINDOW, KV_WIDTH), 0) == WINDOW - 1
    for r in range(rb):
        nk_ref[0, r] = jnp.where(last, kn_all[r:r + 1], pltpu.roll(ck_ref[0, r], WINDOW - 1, 0))
        nv_ref[0, r] = jnp.where(last, vn_all[r:r + 1], pltpu.roll(cv_ref[0, r], WINDOW - 1, 0))


def _attn_sample(proj, ck_all, cv_all, layer, stacked, sinks):
    Bd = proj.shape[1]
    rb = SAMPLE_ROWS
    col = lambda w, off: pl.BlockSpec((1, rb, w), lambda i: (0, i, off // w))
    cache = pl.BlockSpec((1, rb, WINDOW, KV_WIDTH), lambda i: (layer, i, 0, 0))
    args = [sinks.reshape(N_Q_HEADS, 1), proj, proj, proj, proj, ck_all, cv_all, *stacked]
    hbm = pl.BlockSpec(memory_space=pl.ANY)
    return pl.pallas_call(
        _attn_sample_kernel,
        grid=(Bd // rb,),
        in_specs=[
            pl.BlockSpec((N_Q_HEADS, 1), lambda i: (0, 0)),
            col(ATT_WIDTH, P_Q), col(KV_WIDTH, P_KA), col(KV_WIDTH, P_VA), col(ATT_WIDTH, P_GA),
            cache, cache, hbm, hbm,
        ],
        out_specs=[pl.BlockSpec((1, rb, ATT_WIDTH), lambda i: (0, i, 0)), cache, cache],
        out_shape=[
            jax.ShapeDtypeStruct((1, Bd, ATT_WIDTH), bf16),
            jax.ShapeDtypeStruct(ck_all.shape, f32),
            jax.ShapeDtypeStruct(cv_all.shape, f32),
        ],
        input_output_aliases={len(args) - 2: 1, len(args) - 1: 2},
        compiler_params=_cparams(("parallel",)),
        name="attn_sample",
    )(*args)


def _rwkv_prep_kernel(r_ref, kr_ref, vr_ref, wa_ref, pr_ref, pkr_ref, pvr_ref, pwa_ref,
                      mu_r_ref, mu_kr_ref, mu_vr_ref, mu_wa_ref, w0_ref, a0_ref, kk_ref, ka_ref,
                      wd_ref, wi_ref, ro_ref, wo_ref, ko_ref, vo_ref, kko_ref, ao_ref):
    def mixed(cur_ref, prev_ref, mu_ref):
        cur = cur_ref[0]
        return cur + (prev_ref[0] - cur) * mu_ref[...]

    r = mixed(r_ref, pr_ref, mu_r_ref)
    kr = mixed(kr_ref, pkr_ref, mu_kr_ref)
    vr = mixed(vr_ref, pvr_ref, mu_vr_ref)
    wa = mixed(wa_ref, pwa_ref, mu_wa_ref)
    hi = lax.Precision.HIGHEST
    z = w0_ref[...] + jnp.dot(jnp.tanh(wa), wd_ref[...], precision=hi, preferred_element_type=f32)
    nz = -z
    softplus = jnp.maximum(nz, 0.0) + jnp.log1p(jnp.exp(-jnp.abs(nz)))
    w_log = -softplus - 0.5
    a = jax.nn.sigmoid(a0_ref[...] + jnp.dot(wa, wi_ref[...], precision=hi, preferred_element_type=f32))
    outs = (r, -jnp.exp(w_log), kr * (1.0 + (a - 1.0) * ka_ref[...]), vr, kr * kk_ref[...], a)
    for o_ref, val in zip((ro_ref, wo_ref, ko_ref, vo_ref, kko_ref, ao_ref), outs):
        for h in range(RW_HEADS):
            o_ref[:, h, :] = val[:, h * RW_HEAD:(h + 1) * RW_HEAD]


def _rwkv_prep(proj, prev, mu, w0, a0, k_k, k_a, wd_pad, wi_pad, tm):
    G, R, _ = proj.shape
    assert G == 1
    W = RWKV_WIDTH
    cur_specs = [
        pl.BlockSpec((1, tm, W), lambda g, m: (g, m, P_R // W)),
        pl.BlockSpec((1, tm, W), lambda g, m: (g, m, P_KR // W)),
        pl.BlockSpec((1, tm, W), lambda g, m: (g, m, P_VR // W)),
        pl.BlockSpec((1, tm, LANES), lambda g, m: (g, m, P_WA // LANES)),
    ]
    prev_specs = [
        pl.BlockSpec((1, tm, W), lambda g, m: (g, m, 0)),
        pl.BlockSpec((1, tm, W), lambda g, m: (g, m, 0)),
        pl.BlockSpec((1, tm, W), lambda g, m: (g, m, 0)),
        pl.BlockSpec((1, tm, LANES), lambda g, m: (g, m, 0)),
    ]
    prev_args = list(prev)
    vec = lambda w: pl.BlockSpec((1, w), lambda g, m: (0, 0))
    lora = pl.BlockSpec((LORA_PAIR, W), lambda g, m: (0, 0))
    out_spec = pl.BlockSpec((tm, RW_HEADS, RW_HEAD), lambda g, m: (m, 0, 0))
    return pl.pallas_call(
        _rwkv_prep_kernel,
        grid=(G, R // tm),
        in_specs=cur_specs + prev_specs + [vec(W), vec(W), vec(W), vec(LANES),
                                           vec(W), vec(W), vec(W), vec(W), lora, lora],
        out_specs=[out_spec] * 6,
        out_shape=[jax.ShapeDtypeStruct((R, RW_HEADS, RW_HEAD), f32)] * 6,
        compiler_params=_cparams(("parallel", "arbitrary")),
        name="rwkv_prep",
    )(proj, proj, proj, proj, *prev_args, *mu, w0, a0, k_k, k_a, wd_pad, wi_pad)


def _wkv_kernel(tc, r_ref, w_ref, k_ref, v_ref, kk_ref, a_ref, rk_ref, lnw_ref, lnb_ref, s0_ref,
                *refs):
    y_ref, s_ref = refs[-2:]

    @pl.when(pl.program_id(1) == 0)
    def _():
        s_ref[...] = s0_ref[...]

    eye = (lax.broadcasted_iota(jnp.int32, (RW_HEAD, RW_HEAD), 0)
           == lax.broadcasted_iota(jnp.int32, (RW_HEAD, RW_HEAD), 1)).astype(f32)
    rk = rk_ref[...]
    lnw = lnw_ref[...]
    lnb = lnb_ref[...]

    def step(t, carry):
        r, lw, k, v, kkr, a = (ref[0, t] for ref in (r_ref, w_ref, k_ref, v_ref, kk_ref, a_ref))
        w = jnp.exp(lw)
        norm = jnp.sqrt(jnp.sum(kkr * kkr, axis=-1, keepdims=True))
        kk = kkr / jnp.maximum(norm, 1e-12)
        b = kk * a
        bonus = jnp.sum(r * k * rk, axis=-1, keepdims=True) * v
        rows = []
        for h in range(RW_HEADS):
            hs = slice(h, h + 1)
            S = s_ref[0, 0, h]
            sa = jnp.sum(S * (-kk[hs]), axis=-1, keepdims=True)
            v_col = jnp.sum(eye * v[hs], axis=-1, keepdims=True)
            S = S * w[hs] + sa * b[hs] + v_col * k[hs]
            s_ref[0, 0, h] = S
            y = jnp.sum(S * r[hs], axis=-1, keepdims=True)
            mu = jnp.mean(y, axis=0, keepdims=True)
            var = jnp.mean(jnp.square(y - mu), axis=0, keepdims=True)
            yn = (y - mu) * lax.rsqrt(var + GN_EPS)
            rows.append(jnp.sum(yn * eye, axis=0, keepdims=True))
        y_ref[0, t] = jnp.concatenate(rows, axis=0) * lnw + lnb + bonus
        return carry

    lax.fori_loop(0, tc, step, 0)


def _wkv(r, w, k, v, kk, a, r_k, ln_w, ln_b, s0_all, layer, stacked, tc):
    B, T = r.shape[0], r.shape[1]
    hd = (RW_HEADS, RW_HEAD)
    seq = pl.BlockSpec((1, tc, *hd), lambda b, c: (b, c, 0, 0))
    par = pl.BlockSpec(hd, lambda b, c: (0, 0))
    state = pl.BlockSpec((1, 1, RW_HEADS, RW_HEAD, RW_HEAD), lambda b, c: (layer, b, 0, 0, 0))
    args = [r, w, k, v, kk, a, r_k, ln_w.reshape(hd), ln_b.reshape(hd), s0_all, stacked]
    y, s_new = pl.pallas_call(
        functools.partial(_wkv_kernel, tc),
        grid=(B, T // tc),
        in_specs=[seq] * 6 + [par] * 3 + [state, pl.BlockSpec(memory_space=pl.ANY)],
        out_specs=[seq, state],
        out_shape=[
            jax.ShapeDtypeStruct((B, T, *hd), f32),
            jax.ShapeDtypeStruct(s0_all.shape, f32),
        ],
        input_output_aliases={len(args) - 1: 1},
        compiler_params=_cparams(("parallel", "arbitrary")),
        name="wkv_steps",
    )(*args)
    return y.reshape(B, T, RWKV_WIDTH), s_new


CHUNK = 64
PAIR = 2 * RW_HEAD
N_PAIRS = RW_HEADS // 2


def _mmb(a, b, dims=_NN):
    return lax.dot_general(a.astype(bf16), b.astype(bf16), dims, preferred_element_type=f32)


def _lora_dot(x, wh_ref, wl_ref):
    xh, xl = _split_bf16(x)
    rows = x.shape[0]
    both = jnp.dot(jnp.concatenate([xh, xl], axis=0), wh_ref[...], preferred_element_type=f32)
    return both[:rows] + both[rows:] + jnp.dot(xh, wl_ref[...], preferred_element_type=f32)


CHUNKS_PER_STEP = 4


def _wkv_chunk_kernel(r_ref, kr_ref, vr_ref, wa_ref, gr_ref, pr_ref, pkr_ref, pvr_ref, pwa_ref,
                      mu_r_ref, mu_kr_ref, mu_vr_ref, mu_wa_ref, w0_ref, a0_ref, kk_ref, ka_ref,
                      wdh_ref, wdl_ref, wih_ref, wil_ref, rk_ref, lnw_ref, lnb_ref, s0_ref,
                      y_ref, s_ref, sp_ref, cr_ref, ckr_ref, cvr_ref, cwa_ref):
    C = CHUNK
    rows_blk = r_ref.shape[1]
    c = pl.program_id(1)

    @pl.when(c == 0)
    def _():
        for p in range(N_PAIRS):
            sp_ref[p] = jnp.concatenate([s0_ref[0, 2 * p], s0_ref[0, 2 * p + 1]], axis=1)
        for carry, first in ((cr_ref, pr_ref), (ckr_ref, pkr_ref), (cvr_ref, pvr_ref), (cwa_ref, pwa_ref)):
            carry[...] = first[0]

    def mixed(cur_ref, carry_ref, mu_ref):
        cur = cur_ref[0]
        first = lax.broadcasted_iota(jnp.int32, cur.shape, 0) == 0
        prev = jnp.where(first, carry_ref[...], pltpu.roll(cur, 1, 0))
        carry_ref[...] = cur[rows_blk - 1:rows_blk, :]
        return cur + (prev - cur) * mu_ref[...]

    r_blk = mixed(r_ref, cr_ref, mu_r_ref)
    kr_blk = mixed(kr_ref, ckr_ref, mu_kr_ref)
    v_blk = mixed(vr_ref, cvr_ref, mu_vr_ref)
    wa = mixed(wa_ref, cwa_ref, mu_wa_ref)
    nz = -(w0_ref[...] + _lora_dot(jnp.tanh(wa), wdh_ref, wdl_ref))
    softplus = jnp.maximum(nz, 0.0) + jnp.log1p(jnp.exp(-jnp.abs(nz)))
    lw_blk = -jnp.exp(-softplus - 0.5)
    icl_blk = jax.nn.sigmoid(a0_ref[...] + _lora_dot(wa, wih_ref, wil_ref))
    k_blk = kr_blk * (1.0 + (icl_blk - 1.0) * ka_ref[...])
    kkr_blk = kr_blk * kk_ref[...]

    row = lax.broadcasted_iota(jnp.int32, (PAIR, PAIR), 0)
    col = lax.broadcasted_iota(jnp.int32, (PAIR, PAIR), 1)
    tril = row >= col
    stril = row > col
    tril2 = jnp.concatenate([tril, tril], axis=1)
    eye = (row == col).astype(f32)
    lane_lo = lax.broadcasted_iota(jnp.int32, (C, PAIR), 1) < RW_HEAD
    pairs = range(N_PAIRS)
    sls = [slice(p * PAIR, (p + 1) * PAIR) for p in pairs]

    def bd(x):
        zero = jnp.zeros_like(x)
        return jnp.concatenate([jnp.where(lane_lo, x, zero), jnp.where(lane_lo, zero, x)], axis=0)

    def head_sums(x):
        lo_sum = jnp.sum(jnp.where(lane_lo, x, 0.0), axis=-1, keepdims=True)
        hi_sum = jnp.sum(jnp.where(lane_lo, 0.0, x), axis=-1, keepdims=True)
        return jnp.where(lane_lo, lo_sum, hi_sum)

    state = [sp_ref[p] for p in pairs]
    for j in range(rows_blk // C):
        rs = slice(j * C, (j + 1) * C)
        lw = lw_blk[rs]
        width = lw.shape[1]
        lw_a = lw.astype(bf16)
        rest = lw - lw_a.astype(f32)
        lw_b = rest.astype(bf16)
        lw_c = (rest - lw_b.astype(f32)).astype(bf16)
        g3 = jnp.dot(tril[:C, :C].astype(bf16), jnp.concatenate([lw_a, lw_b, lw_c], axis=1),
                     preferred_element_type=f32)
        g = g3[:, :width] + g3[:, width:2 * width] + g3[:, 2 * width:]
        e_g = jnp.exp(g)
        e_ng = jnp.exp(-g)
        e_gm = jnp.exp(g - lw)
        e_end = e_g[C - 1:C, :]

        ins = [[t[rs, sl] for t in (r_blk, k_blk, v_blk, kkr_blk, icl_blk)] for sl in sls]
        norms = [jnp.sqrt(head_sums(x[3] * x[3])) for x in ins]
        at, rt, bt, kt, vb = ([] for _ in range(5))
        for (r, k, v, kkr, icl), norm, sl in zip(ins, norms, sls):
            kk = kkr / jnp.maximum(norm, 1e-12)
            b = kk * icl
            at.append(bd((-kk * e_gm[:, sl]).astype(bf16)))
            rt.append(bd((r * e_g[:, sl]).astype(bf16)))
            bt.append(bd((b * e_ng[:, sl]).astype(bf16)))
            kt.append(bd((k * e_ng[:, sl]).astype(bf16)))
            vb.append(bd(v.astype(bf16)))

        gram = [_mmb(jnp.concatenate([at[p], rt[p]], axis=0), jnp.concatenate([bt[p], kt[p]], axis=0), _NT)
                for p in pairs]
        lmat = [jnp.where(stril, gm[:PAIR, :PAIR], 0.0) for gm in gram]
        mv = [_mmb(jnp.where(stril, gram[p][:PAIR, PAIR:], 0.0), vb[p]) for p in pairs]
        lower = [jnp.where(tril2, gm[PAIR:, :], 0.0).astype(bf16) for gm in gram]

        tinv = [eye + lm for lm in lmat]
        pw = [_mmb(lm, lm) for lm in lmat]
        for _ in range(4):
            z = [_mmb(jnp.concatenate([x.astype(bf16), t.astype(bf16)], axis=0), x)
                 for t, x in zip(tinv, pw)]
            pw = [zz[:PAIR] for zz in z]
            tinv = [t + zz[PAIR:] for t, zz in zip(tinv, z)]
        tinv = [t + _mmb(t, x) for t, x in zip(tinv, pw)]

        wx = [_mmb(tinv[p], jnp.concatenate([at[p], mv[p].astype(bf16)], axis=1)) for p in pairs]
        uy0 = [_mmb(jnp.concatenate([wx[p][:, :PAIR].astype(bf16), rt[p]], axis=0),
                    bd(state[p].astype(bf16)), _NT) for p in pairs]
        uv = [jnp.concatenate([(uy0[p][:PAIR] + wx[p][:, PAIR:]).astype(bf16), vb[p]], axis=0)
              for p in pairs]
        ys = [uy0[p][PAIR:] + _mmb(lower[p], uv[p]) for p in pairs]
        s_add = [_mmb(uv[p], jnp.concatenate([bt[p], kt[p]], axis=0), _TN) for p in pairs]
        state = [(state[p] + s_add[p][:RW_HEAD] + s_add[p][RW_HEAD:]) * e_end[:, sls[p]] for p in pairs]

        ys = [y[:C] + y[C:] for y in ys]
        mus = [head_sums(y) * (1.0 / RW_HEAD) for y in ys]
        ds = [y - mu for y, mu in zip(ys, mus)]
        var = [head_sums(d * d) * (1.0 / RW_HEAD) for d in ds]
        bonus = [head_sums(x[0] * x[1] * rk_ref[:, sl]) * x[2] for x, sl in zip(ins, sls)]
        for p in pairs:
            sl = sls[p]
            y_rw = ds[p] * lax.rsqrt(var[p] + GN_EPS) * lnw_ref[:, sl] + lnb_ref[:, sl] + bonus[p]
            y_ref[0, rs, sl] = (y_rw * _silu(gr_ref[0, rs, sl])).astype(y_ref.dtype)

    for p in pairs:
        sp_ref[p] = state[p]

    @pl.when(c == pl.num_programs(1) - 1)
    def _():
        for p in range(N_PAIRS):
            s_ref[0, 2 * p] = state[p][:, :RW_HEAD]
            s_ref[0, 2 * p + 1] = state[p][:, RW_HEAD:]


def _wkv_chunked(proj, prev, mu, w0, a0, k_k, k_a, lora, r_k, ln_w, ln_b, s0):
    B, T, _ = proj.shape
    W = RWKV_WIDTH
    rows = CHUNKS_PER_STEP * CHUNK
    col = lambda w, off: pl.BlockSpec((1, rows, w), lambda b, c: (b, c, off // w))
    first = lambda w: pl.BlockSpec((1, 1, w), lambda b, c: (b, 0, 0))
    vec = lambda w: pl.BlockSpec((1, w), lambda b, c: (0, 0))
    lora_spec = pl.BlockSpec((LORA_PAIR, W), lambda b, c: (0, 0))
    state = pl.BlockSpec((1, RW_HEADS, RW_HEAD, RW_HEAD), lambda b, c: (b, 0, 0, 0))
    return pl.pallas_call(
        _wkv_chunk_kernel,
        grid=(B, T // rows),
        in_specs=[col(W, P_R), col(W, P_KR), col(W, P_VR), col(LANES, P_WA), col(W, P_GR),
                  first(W), first(W), first(W), first(LANES),
                  vec(W), vec(W), vec(W), vec(LANES), vec(W), vec(W), vec(W), vec(W),
                  lora_spec, lora_spec, lora_spec, lora_spec, vec(W), vec(W), vec(W), state],
        out_specs=[col(W, 0), state],
        out_shape=[
            jax.ShapeDtypeStruct((B, T, W), bf16),
            jax.ShapeDtypeStruct((B, RW_HEADS, RW_HEAD, RW_HEAD), f32),
        ],
        scratch_shapes=[pltpu.VMEM((N_PAIRS, RW_HEAD, PAIR), f32),
                        pltpu.VMEM((1, W), f32), pltpu.VMEM((1, W), f32), pltpu.VMEM((1, W), f32),
                        pltpu.VMEM((1, LANES), f32)],
        compiler_params=_cparams(("parallel", "arbitrary")),
        name="wkv_chunks",
    )(proj, proj, proj, proj, proj, *prev, *mu, w0, a0, k_k, k_a, *lora,
      r_k.reshape(1, W), ln_w.reshape(1, W), ln_b.reshape(1, W), s0)


def _post_kernel(final, gated, att_ref, y_ref, *refs):
    if gated:
        y = y_ref[0]
    else:
        y = (y_ref[0] * _silu(refs[0][0])).astype(bf16)
        refs = refs[1:]
    x_ref, gate_ref, w_ref, fg_ref, o_ref = refs
    cat = jnp.concatenate([att_ref[0], y], axis=1)
    out = jnp.dot(cat, w_ref[...], preferred_element_type=f32)
    x = x_ref[0] + gate_ref[0] * out
    if final:
        ms = jnp.mean(x * x, axis=-1, keepdims=True)
        x = x * lax.rsqrt(ms + NORM_EPS) * fg_ref[...]
    o_ref[0] = x


def _post(att, y_rw, proj, x, gate, w_out_bf, final_g, final, tm):
    G, R, _ = x.shape
    W = RWKV_WIDTH
    gated = proj is None
    gate_in = [] if gated else [pl.BlockSpec((1, tm, W), lambda g, m: (g, m, P_GR // W))]
    gate_arg = [] if gated else [proj]
    return pl.pallas_call(
        functools.partial(_post_kernel, final, gated),
        grid=(G, R // tm),
        in_specs=[
            pl.BlockSpec((1, tm, ATT_WIDTH), lambda g, m: (g, m, 0)),
            pl.BlockSpec((1, tm, W), lambda g, m: (g, m, 0)),
            *gate_in,
            pl.BlockSpec((1, tm, D_MODEL), lambda g, m: (g, m, 0)),
            _mod_spec(gate, tm),
            pl.BlockSpec((D_MODEL, D_MODEL), lambda g, m: (0, 0)),
            pl.BlockSpec((1, D_MODEL), lambda g, m: (0, 0)),
        ],
        out_specs=pl.BlockSpec((1, tm, D_MODEL), lambda g, m: (g, m, 0)),
        out_shape=jax.ShapeDtypeStruct((G, R, D_MODEL), f32),
        compiler_params=_cparams(("parallel", "parallel")),
        name="post_proj",
    )(att, y_rw, *gate_arg, x, gate, w_out_bf, final_g.reshape(1, D_MODEL))


def _arrange_w_in(w):
    pad = jnp.zeros((w.shape[0], P_WIDTH - IN_WIDTH), w.dtype)
    parts = [w[:, Q_OFF:KA_OFF], w[:, GA_OFF:GR_OFF], w[:, GR_OFF:IN_WIDTH], w[:, R_OFF:WD_OFF],
             w[:, KA_OFF:R_OFF], w[:, WD_OFF:GA_OFF], pad]
    return jnp.concatenate(parts, axis=1).astype(bf16)


def _shift_cols(t):
    return jnp.concatenate([t[..., P_R:P_KA], t[..., P_WA:P_WA + LORA_PAIR]], axis=-1)


def kernel(x_prompt, x_sample, cache_k, cache_v, state_wkv, state_shift, c_prompt, c_sample,
           norm_g, w_ada, b_ada, w_in, mu_shift, w0, w_decay, a0, w_iclr, k_k, k_a, r_k,
           ln_w, ln_b, sinks, w_out, final_g):
    Bp, Tp = x_prompt.shape[0], x_prompt.shape[1]
    Bd = x_sample.shape[0]
    W = RWKV_WIDTH

    n_c = Bp + Bd
    c_rows = -(-n_c // 16) * 16
    c_all = jnp.concatenate([c_prompt, c_sample, jnp.zeros((c_rows - n_c, D_MODEL), f32)], axis=0)
    mod = _ada(c_all, w_ada, b_ada)

    tab_p = _rope_tables(jnp.arange(Tp, dtype=jnp.int32))
    tab_s = _rope_tables(jnp.full((Bd,), PAST_LEN, jnp.int32))

    hp = x_prompt
    hs = x_sample.reshape(1, Bd, D_MODEL)
    s0_p = jnp.zeros((Bp, RW_HEADS, RW_HEAD, RW_HEAD), f32)
    shift0_p = [jnp.zeros((Bp, 1, w), f32) for w in (W, W, W, LORA_PAIR)]
    ck_all = cache_k.reshape(DEPTH, Bd, WINDOW, KV_WIDTH)
    cv_all = cache_v.reshape(DEPTH, Bd, WINDOW, KV_WIDTH)
    new_state_s = jnp.zeros(state_wkv.shape, f32)
    new_cache_s = [jnp.zeros(ck_all.shape, f32), jnp.zeros(cv_all.shape, f32)]
    outs = {k: [] for k in ("kp", "vp", "sp", "shp", "shs")}
    for l in range(DEPTH):
        final = l == DEPTH - 1
        w_bf = _arrange_w_in(w_in[l])
        w_out_bf = w_out[l].astype(bf16)
        mu_l = mu_shift[l]
        mu = [mu_l[0:W].reshape(1, W), mu_l[W:2 * W].reshape(1, W), mu_l[2 * W:3 * W].reshape(1, W),
              mu_l[3 * W:].reshape(1, LORA_PAIR)]
        vecs = [t[l].reshape(1, W) for t in (w0, a0, k_k, k_a)]
        wd_pad = jnp.concatenate([w_decay[l], jnp.zeros((ICLR_LORA, W), f32)], axis=0)
        wi_pad = jnp.concatenate([jnp.zeros((DECAY_LORA, W), f32), w_iclr[l]], axis=0)
        shift_p, scale_p, gate_p = (mod[l, :Bp, i * D_MODEL:(i + 1) * D_MODEL].reshape(Bp, 1, D_MODEL)
                                    for i in range(3))
        shift_s, scale_s, gate_s = (mod[l, Bp:n_c, i * D_MODEL:(i + 1) * D_MODEL].reshape(1, Bd, D_MODEL)
                                    for i in range(3))

        proj = _norm_proj(hp, norm_g[l], scale_p, shift_p, w_bf, tab_p, tm=1024)
        att = _attn_prompt(proj, sinks[l])
        lora = [piece for wp in (wd_pad, wi_pad) for piece in _split_bf16(wp)]
        y_rw, s_t = _wkv_chunked(proj, shift0_p, mu, *vecs, lora, r_k[l], ln_w[l], ln_b[l], s0_p)
        hp = _post(att, y_rw, None, hp, gate_p, w_out_bf, final_g, final, tm=512)
        tail = proj[:, Tp - WINDOW:]
        outs["kp"].append(tail[..., P_KA:P_KA + KV_WIDTH].reshape(Bp, WINDOW, N_KV_HEADS, HEAD_DIM))
        outs["vp"].append(tail[..., P_VA:P_VA + KV_WIDTH].reshape(Bp, WINDOW, N_KV_HEADS, HEAD_DIM))
        outs["sp"].append(s_t)
        outs["shp"].append(_shift_cols(proj[:, Tp - 1]))

        proj = _norm_proj(hs, norm_g[l], scale_s, shift_s, w_bf, tab_s, tm=Bd)
        att, *new_cache_s = _attn_sample(proj, ck_all, cv_all, l, new_cache_s, sinks[l])
        sh = state_shift[l]
        prev = [sh[None, :, 0:W], sh[None, :, W:2 * W], sh[None, :, 2 * W:3 * W], sh[None, :, 3 * W:]]
        prep = _rwkv_prep(proj, prev, mu, *vecs, wd_pad, wi_pad, tm=Bd)
        prep = [t[:, None] for t in prep]
        y_rw, new_state_s = _wkv(*prep, r_k[l], ln_w[l], ln_b[l], state_wkv, l, new_state_s, tc=1)
        hs = _post(att, y_rw.reshape(1, Bd, W), proj, hs, gate_s, w_out_bf, final_g, final, tm=Bd)
        outs["shs"].append(_shift_cols(proj[0]))

    st = lambda k: jnp.stack(outs[k])
    return (hp, hs.reshape(Bd, 1, D_MODEL), st("kp"), st("vp"), st("sp"), st("shp"),
            *(t.reshape(cache_k.shape) for t in new_cache_s), new_state_s, st("shs"))
```

```python
import functools

import jax
import jax.numpy as jnp
from jax import lax
from jax.experimental import pallas as pl
from jax.experimental.pallas import tpu as pltpu

f32 = jnp.float32
bf16 = jnp.bfloat16

D_MODEL = 2048
DEPTH = 2
PAST_LEN = 16384
ATT_WIDTH = 1024
RWKV_WIDTH = 1024
HEAD_DIM = 64
N_Q_HEADS = 16
N_KV_HEADS = 4
GQA_GROUP = 4
KV_WIDTH = 256
WINDOW = 128
ROT_DIM = 16
ROPE_THETA = 500000.0
RW_HEAD = 64
RW_HEADS = 16
DECAY_LORA = 64
ICLR_LORA = 64
LORA_PAIR = DECAY_LORA + ICLR_LORA
NORM_EPS = 1e-5
GN_EPS = 64e-5
NEG_BIG = -1e30

Q_OFF = 0
KA_OFF = Q_OFF + ATT_WIDTH
VA_OFF = KA_OFF + KV_WIDTH
R_OFF = VA_OFF + KV_WIDTH
KR_OFF = R_OFF + RWKV_WIDTH
VR_OFF = KR_OFF + RWKV_WIDTH
WD_OFF = VR_OFF + RWKV_WIDTH
AD_OFF = WD_OFF + DECAY_LORA
GA_OFF = AD_OFF + ICLR_LORA
GR_OFF = GA_OFF + ATT_WIDTH
IN_WIDTH = GR_OFF + RWKV_WIDTH
SHIFT_DIM = GA_OFF - R_OFF

LANES = 128
P_Q = 0
P_GA = 1024
P_GR = 2048
P_R = 3072
P_KR = 4096
P_VR = 5120
P_KA = 6144
P_VA = 6400
P_WA = 6656
P_WIDTH = 7168
PROJ_TN = 1792

VMEM_LIMIT = 56 * 1024 * 1024


_NN = (((1,), (0,)), ((), ()))
_NT = (((1,), (1,)), ((), ()))
_TN = (((0,), (0,)), ((), ()))


def _silu(x):
    return x * jax.nn.sigmoid(x)


def _cparams(sem):
    return pltpu.CompilerParams(dimension_semantics=sem, vmem_limit_bytes=VMEM_LIMIT)


def _split_bf16(x):
    hi = x.astype(bf16)
    return hi, (x - hi.astype(f32)).astype(bf16)


def _ada_kernel(c_ref, w_ref, b_ref, o_ref):
    ch, cl = _split_bf16(_silu(c_ref[...]))
    wh, wl = _split_bf16(w_ref[0])
    rows = ch.shape[0]
    both = jnp.dot(jnp.concatenate([ch, cl], axis=0), wh, preferred_element_type=f32)
    o_ref[0] = both[:rows] + both[rows:] + jnp.dot(ch, wl, preferred_element_type=f32) + b_ref[0]


def _ada(c_all, w_ada, b_ada):
    rows = c_all.shape[0]
    tn = 1536
    n_out = w_ada.shape[2]
    return pl.pallas_call(
        _ada_kernel,
        grid=(DEPTH, n_out // tn),
        in_specs=[
            pl.BlockSpec((rows, D_MODEL), lambda l, n: (0, 0)),
            pl.BlockSpec((1, D_MODEL, tn), lambda l, n: (l, 0, n)),
            pl.BlockSpec((1, 1, tn), lambda l, n: (l, 0, n)),
        ],
        out_specs=pl.BlockSpec((1, rows, tn), lambda l, n: (l, 0, n)),
        out_shape=jax.ShapeDtypeStruct((DEPTH, rows, n_out), f32),
        compiler_params=_cparams(("parallel", "parallel")),
        name="ada_mod",
    )(c_all, w_ada, b_ada.reshape(DEPTH, 1, n_out))


def _rope(x, tab):
    w = x.shape[1]
    reps = w // LANES
    cosf, up, dn = (jnp.concatenate([tab[i]] * reps, axis=1) for i in range(3))
    half = ROT_DIM // 2
    return x * cosf + pltpu.roll(x, w - half, 1) * up + pltpu.roll(x, half, 1) * dn


def _norm_proj_kernel(x_ref, g_ref, scale_ref, shift_ref, w_ref, tab_ref, o_ref, h_ref):
    n = pl.program_id(2)

    @pl.when(n == 0)
    def _():
        x = x_ref[0]
        ms = jnp.mean(x * x, axis=-1, keepdims=True)
        y = x * lax.rsqrt(ms + NORM_EPS) * g_ref[...]
        h_ref[...] = (y * (1.0 + scale_ref[0]) + shift_ref[0]).astype(bf16)

    tm, tn = o_ref.shape[1], o_ref.shape[2]
    q_tile, q_lo = divmod(P_Q, tn)
    k_tile, k_lo = divmod(P_KA, tn)
    assert q_lo + ATT_WIDTH <= tn and k_lo + KV_WIDTH <= tn and q_tile != k_tile
    rc = min(tm, 256)

    def rotated_tile(lo, width, scale):
        for i in range(tm // rc):
            rs = slice(i * rc, (i + 1) * rc)
            res = jnp.dot(h_ref[rs, :], w_ref[...], preferred_element_type=f32)
            parts = [res[:, :lo]] if lo else []
            parts.append(_rope(res[:, lo:lo + width], tab_ref[:, rs, :]) * scale)
            if lo + width < tn:
                parts.append(res[:, lo + width:])
            o_ref[0, rs, :] = jnp.concatenate(parts, axis=1) if len(parts) > 1 else parts[0]

    @pl.when(n == q_tile)
    def _():
        rotated_tile(q_lo, ATT_WIDTH, HEAD_DIM ** -0.5)

    @pl.when(n == k_tile)
    def _():
        rotated_tile(k_lo, KV_WIDTH, 1.0)

    @pl.when((n != q_tile) & (n != k_tile))
    def _():
        o_ref[0] = jnp.dot(h_ref[...], w_ref[...], preferred_element_type=f32)


def _mod_spec(mod, tm):
    if mod.shape[1] == 1:
        return pl.BlockSpec((1, 1, D_MODEL), lambda g, m, *_: (g, 0, 0))
    return pl.BlockSpec((1, tm, D_MODEL), lambda g, m, *_: (g, m, 0))


def _norm_proj(x, norm_g, scale, shift, w_bf, tab, tm):
    G, R, _ = x.shape
    return pl.pallas_call(
        _norm_proj_kernel,
        grid=(G, R // tm, P_WIDTH // PROJ_TN),
        in_specs=[
            pl.BlockSpec((1, tm, D_MODEL), lambda g, m, n: (g, m, 0)),
            pl.BlockSpec((1, D_MODEL), lambda g, m, n: (0, 0)),
            _mod_spec(scale, tm),
            _mod_spec(shift, tm),
            pl.BlockSpec((D_MODEL, PROJ_TN), lambda g, m, n: (0, n)),
            pl.BlockSpec((3, tm, LANES), lambda g, m, n: (0, m, 0)),
        ],
        out_specs=pl.BlockSpec((1, tm, PROJ_TN), lambda g, m, n: (g, m, n)),
        out_shape=jax.ShapeDtypeStruct((G, R, P_WIDTH), f32),
        scratch_shapes=[pltpu.VMEM((tm, D_MODEL), bf16)],
        compiler_params=_cparams(("parallel", "parallel", "arbitrary")),
        name="norm_proj",
    )(x, norm_g.reshape(1, D_MODEL), scale, shift, w_bf, tab)


def _rope_tables(pos):
    half = ROT_DIM // 2
    inv_freq = ROPE_THETA ** (-jnp.arange(half, dtype=f32) * (2.0 / ROT_DIM))
    ang = pos.astype(f32)[:, None] * inv_freq[None, :]
    cos, sin = jnp.cos(ang), jnp.sin(ang)
    t = pos.shape[0]
    z8 = jnp.zeros((t, half), f32)
    rest = HEAD_DIM - ROT_DIM
    cos64 = jnp.concatenate([cos, cos, jnp.ones((t, rest), f32)], axis=1)
    up64 = jnp.concatenate([-sin, z8, jnp.zeros((t, rest), f32)], axis=1)
    dn64 = jnp.concatenate([z8, sin, jnp.zeros((t, rest), f32)], axis=1)
    rep = LANES // HEAD_DIM
    return jnp.stack([jnp.tile(a, (1, rep)) for a in (cos64, up64, dn64)])


ATTN_BLOCKS = 4


def _attn_prompt_kernel(sinks_ref, q_ref, kc_ref, kp_ref, vc_ref, vp_ref, ga_ref, o_ref):
    n = pl.program_id(1)
    wn = WINDOW
    half = HEAD_DIM
    k_t_all = jnp.concatenate([kp_ref[0], kc_ref[0]], axis=0).T.astype(bf16)
    v_all = jnp.concatenate([vp_ref[0], vc_ref[0]], axis=0)

    qi = lax.broadcasted_iota(jnp.int32, (2 * wn, 2 * wn), 0) & (wn - 1)
    kj = lax.broadcasted_iota(jnp.int32, (2 * wn, 2 * wn), 1)
    rel = wn + qi - kj
    band = (rel >= 0) & (rel <= wn)
    top = lax.broadcasted_iota(jnp.int32, (2 * wn, 1), 0) < wn
    lo = lax.broadcasted_iota(jnp.int32, (2 * wn, LANES), 1) < half
    zeros_k = jnp.zeros((half, 2 * wn), bf16)
    ones_lo = jnp.where(lo, 1.0, 0.0).astype(bf16)
    ones_hi = jnp.where(lo, 0.0, 1.0).astype(bf16)

    for sb in range(q_ref.shape[1] // wn):
        rows = slice(sb * wn, (sb + 1) * wn)
        q = q_ref[0, rows, :].astype(bf16)
        ga = ga_ref[0, rows, :]
        k_t = k_t_all[:, sb * wn:(sb + 2) * wn]
        vcat = v_all[sb * wn:(sb + 2) * wn]
        mask = band & ((kj >= wn) | (n > 0)) if sb == 0 else band
        for j in range(N_KV_HEADS // 2):
            vblk = vcat[:, j * LANES:(j + 1) * LANES]
            vswap = pltpu.roll(vblk, half, 1)
            for g in (2 * j, 2 * j + 1):
                own, other = (vblk, vswap) if g % 2 == 0 else (vswap, vblk)
                v_lo = jnp.where(lo, own, 0.0).astype(bf16)
                v_hi = jnp.where(lo, 0.0, other).astype(bf16)
                rhs_pv = jnp.concatenate([jnp.concatenate([v_lo, ones_lo], axis=1),
                                          jnp.concatenate([v_hi, ones_hi], axis=1)], axis=0)
                kg = k_t[g * half:(g + 1) * half, :]
                rhs_qk = jnp.concatenate([jnp.concatenate([kg, zeros_k], axis=0),
                                          jnp.concatenate([zeros_k, kg], axis=0)], axis=1)
                b0, b1 = 2 * g, 2 * g + 1
                qg = jnp.concatenate([q[:, b0 * LANES:(b0 + 1) * LANES], q[:, b1 * LANES:(b1 + 1) * LANES]],
                                     axis=0)
                s_all = jnp.dot(qg, rhs_qk, preferred_element_type=f32)
                ps, es = [], []
                for hh in range(2):
                    s = jnp.where(mask, s_all[:, hh * 2 * wn:(hh + 1) * 2 * wn], NEG_BIG)
                    sink = jnp.where(top, sinks_ref[2 * b0 + hh], sinks_ref[2 * b1 + hh])
                    m = jnp.maximum(jnp.max(s, axis=-1, keepdims=True), sink)
                    ps.append(jnp.exp(s - m).astype(bf16))
                    es.append(jnp.exp(sink - m))
                res = jnp.dot(jnp.concatenate(ps, axis=1), rhs_pv, preferred_element_type=f32)
                out = res[:, :LANES] / (res[:, LANES:] + jnp.where(lo, es[0], es[1]))
                for i, blk in enumerate((b0, b1)):
                    sl = slice(blk * LANES, (blk + 1) * LANES)
                    o_ref[0, rows, sl] = (out[i * wn:(i + 1) * wn] * _silu(ga[:, sl])).astype(o_ref.dtype)


def _attn_prompt(proj, sinks):
    B, T, _ = proj.shape
    rows = ATTN_BLOCKS * WINDOW
    kvb = KV_WIDTH
    prev = lambda b, n: jnp.maximum(n * ATTN_BLOCKS - 1, 0)
    return pl.pallas_call(
        _attn_prompt_kernel,
        grid=(B, T // rows),
        in_specs=[
            pl.BlockSpec(memory_space=pltpu.SMEM),
            pl.BlockSpec((1, rows, ATT_WIDTH), lambda b, n: (b, n, P_Q // ATT_WIDTH)),
            pl.BlockSpec((1, rows, kvb), lambda b, n: (b, n, P_KA // kvb)),
            pl.BlockSpec((1, WINDOW, kvb), lambda b, n: (b, prev(b, n), P_KA // kvb)),
            pl.BlockSpec((1, rows, kvb), lambda b, n: (b, n, P_VA // kvb)),
            pl.BlockSpec((1, WINDOW, kvb), lambda b, n: (b, prev(b, n), P_VA // kvb)),
            pl.BlockSpec((1, rows, ATT_WIDTH), lambda b, n: (b, n, P_GA // ATT_WIDTH)),
        ],
        out_specs=pl.BlockSpec((1, rows, ATT_WIDTH), lambda b, n: (b, n, 0)),
        out_shape=jax.ShapeDtypeStruct((B, T, ATT_WIDTH), bf16),
        compiler_params=_cparams(("parallel", "arbitrary")),
        name="attn_prompt",
    )(sinks, proj, proj, proj, proj, proj, proj)


SAMPLE_ROWS = 16


def _attn_sample_kernel(sinks_ref, q_ref, kn_ref, vn_ref, ga_ref, ck_ref, cv_ref, *refs):
    o_ref, nk_ref, nv_ref = refs[-3:]
    rb = q_ref.shape[1]
    q_all, kn_all, vn_all = q_ref[0], kn_ref[0], vn_ref[0]
    work = [(r, g) for r in range(rb) for g in range(N_KV_HEADS)]
    span = lambda g: slice(g * HEAD_DIM, (g + 1) * HEAD_DIM)
    ckb = [ck_ref[0, r].astype(bf16) for r in range(rb)]
    cvb = [cv_ref[0, r].astype(bf16) for r in range(rb)]
    qg = {(r, g): jnp.concatenate(
        [q_all[r:r + 1, (g * GQA_GROUP + i) * HEAD_DIM:(g * GQA_GROUP + i + 1) * HEAD_DIM]
         for i in range(GQA_GROUP)], axis=0) for r, g in work}
    s = {rg: lax.dot_general(qg[rg].astype(bf16), ckb[rg[0]][:, span(rg[1])], _NT,
                             preferred_element_type=f32) for rg in work}
    outs = {}
    for r, g in work:
        s_new = jnp.sum(qg[r, g] * kn_all[r:r + 1, span(g)], axis=-1, keepdims=True)
        sink = sinks_ref[g * GQA_GROUP:(g + 1) * GQA_GROUP, :]
        m = jnp.maximum(jnp.maximum(jnp.max(s[r, g], axis=-1, keepdims=True), s_new), sink)
        p = jnp.exp(s[r, g] - m)
        p_new = jnp.exp(s_new - m)
        den = jnp.sum(p, axis=-1, keepdims=True) + p_new + jnp.exp(sink - m)
        o = (jnp.dot(p.astype(bf16), cvb[r][:, span(g)], preferred_element_type=f32)
             + p_new * vn_all[r:r + 1, span(g)]) / den
        outs[r, g] = [o[i:i + 1, :] for i in range(GQA_GROUP)]
    att = jnp.concatenate(
        [jnp.concatenate([h for g in range(N_KV_HEADS) for h in outs[r, g]], axis=1) for r in range(rb)],
        axis=0)
    o_ref[0] = (att * _silu(ga_ref[0])).astype(o_ref.dtype)
    last = lax.broadcasted_iota(jnp.int32, (WINDOW, KV_WIDTH), 0) == WINDOW - 1
    for r in range(rb):
        nk_ref[0, r] = jnp.where(last, kn_all[r:r + 1], pltpu.roll(ck_ref[0, r], WINDOW - 1, 0))
        nv_ref[0, r] = jnp.where(last, vn_all[r:r + 1], pltpu.roll(cv_ref[0, r], WINDOW - 1, 0))


def _attn_sample(proj, ck_all, cv_all, layer, stacked, sinks):
    Bd = proj.shape[1]
    rb = SAMPLE_ROWS
    col = lambda w, off: pl.BlockSpec((1, rb, w), lambda i: (0, i, off // w))
    cache = pl.BlockSpec((1, rb, WINDOW, KV_WIDTH), lambda i: (layer, i, 0, 0))
    args = [sinks.reshape(N_Q_HEADS, 1), proj, proj, proj, proj, ck_all, cv_all, *stacked]
    hbm = pl.BlockSpec(memory_space=pl.ANY)
    return pl.pallas_call(
        _attn_sample_kernel,
        grid=(Bd // rb,),
        in_specs=[
            pl.BlockSpec((N_Q_HEADS, 1), lambda i: (0, 0)),
            col(ATT_WIDTH, P_Q), col(KV_WIDTH, P_KA), col(KV_WIDTH, P_VA), col(ATT_WIDTH, P_GA),
            cache, cache, hbm, hbm,
        ],
        out_specs=[pl.BlockSpec((1, rb, ATT_WIDTH), lambda i: (0, i, 0)), cache, cache],
        out_shape=[
            jax.ShapeDtypeStruct((1, Bd, ATT_WIDTH), bf16),
            jax.ShapeDtypeStruct(ck_all.shape, f32),
            jax.ShapeDtypeStruct(cv_all.shape, f32),
        ],
        input_output_aliases={len(args) - 2: 1, len(args) - 1: 2},
        compiler_params=_cparams(("parallel",)),
        name="attn_sample",
    )(*args)


def _rwkv_prep_kernel(r_ref, kr_ref, vr_ref, wa_ref, pr_ref, pkr_ref, pvr_ref, pwa_ref,
                      mu_r_ref, mu_kr_ref, mu_vr_ref, mu_wa_ref, w0_ref, a0_ref, kk_ref, ka_ref,
                      wd_ref, wi_ref, ro_ref, wo_ref, ko_ref, vo_ref, kko_ref, ao_ref):
    def mixed(cur_ref, prev_ref, mu_ref):
        cur = cur_ref[0]
        return cur + (prev_ref[0] - cur) * mu_ref[...]

    r = mixed(r_ref, pr_ref, mu_r_ref)
    kr = mixed(kr_ref, pkr_ref, mu_kr_ref)
    vr = mixed(vr_ref, pvr_ref, mu_vr_ref)
    wa = mixed(wa_ref, pwa_ref, mu_wa_ref)
    hi = lax.Precision.HIGHEST
    z = w0_ref[...] + jnp.dot(jnp.tanh(wa), wd_ref[...], precision=hi, preferred_element_type=f32)
    nz = -z
    softplus = jnp.maximum(nz, 0.0) + jnp.log1p(jnp.exp(-jnp.abs(nz)))
    w_log = -softplus - 0.5
    a = jax.nn.sigmoid(a0_ref[...] + jnp.dot(wa, wi_ref[...], precision=hi, preferred_element_type=f32))
    outs = (r, -jnp.exp(w_log), kr * (1.0 + (a - 1.0) * ka_ref[...]), vr, kr * kk_ref[...], a)
    for o_ref, val in zip((ro_ref, wo_ref, ko_ref, vo_ref, kko_ref, ao_ref), outs):
        for h in range(RW_HEADS):
            o_ref[:, h, :] = val[:, h * RW_HEAD:(h + 1) * RW_HEAD]


def _rwkv_prep(proj, prev, mu, w0, a0, k_k, k_a, wd_pad, wi_pad, tm):
    G, R, _ = proj.shape
    assert G == 1
    W = RWKV_WIDTH
    cur_specs = [
        pl.BlockSpec((1, tm, W), lambda g, m: (g, m, P_R // W)),
        pl.BlockSpec((1, tm, W), lambda g, m: (g, m, P_KR // W)),
        pl.BlockSpec((1, tm, W), lambda g, m: (g, m, P_VR // W)),
        pl.BlockSpec((1, tm, LANES), lambda g, m: (g, m, P_WA // LANES)),
    ]
    prev_specs = [
        pl.BlockSpec((1, tm, W), lambda g, m: (g, m, 0)),
        pl.BlockSpec((1, tm, W), lambda g, m: (g, m, 0)),
        pl.BlockSpec((1, tm, W), lambda g, m: (g, m, 0)),
        pl.BlockSpec((1, tm, LANES), lambda g, m: (g, m, 0)),
    ]
    prev_args = list(prev)
    vec = lambda w: pl.BlockSpec((1, w), lambda g, m: (0, 0))
    lora = pl.BlockSpec((LORA_PAIR, W), lambda g, m: (0, 0))
    out_spec = pl.BlockSpec((tm, RW_HEADS, RW_HEAD), lambda g, m: (m, 0, 0))
    return pl.pallas_call(
        _rwkv_prep_kernel,
        grid=(G, R // tm),
        in_specs=cur_specs + prev_specs + [vec(W), vec(W), vec(W), vec(LANES),
                                           vec(W), vec(W), vec(W), vec(W), lora, lora],
        out_specs=[out_spec] * 6,
        out_shape=[jax.ShapeDtypeStruct((R, RW_HEADS, RW_HEAD), f32)] * 6,
        compiler_params=_cparams(("parallel", "arbitrary")),
        name="rwkv_prep",
    )(proj, proj, proj, proj, *prev_args, *mu, w0, a0, k_k, k_a, wd_pad, wi_pad)


def _wkv_kernel(tc, r_ref, w_ref, k_ref, v_ref, kk_ref, a_ref, rk_ref, lnw_ref, lnb_ref, s0_ref,
                *refs):
    y_ref, s_ref = refs[-2:]

    @pl.when(pl.program_id(1) == 0)
    def _():
        s_ref[...] = s0_ref[...]

    eye = (lax.broadcasted_iota(jnp.int32, (RW_HEAD, RW_HEAD), 0)
           == lax.broadcasted_iota(jnp.int32, (RW_HEAD, RW_HEAD), 1)).astype(f32)
    rk = rk_ref[...]
    lnw = lnw_ref[...]
    lnb = lnb_ref[...]

    def step(t, carry):
        r, lw, k, v, kkr, a = (ref[0, t] for ref in (r_ref, w_ref, k_ref, v_ref, kk_ref, a_ref))
        w = jnp.exp(lw)
        norm = jnp.sqrt(jnp.sum(kkr * kkr, axis=-1, keepdims=True))
        kk = kkr / jnp.maximum(norm, 1e-12)
        b = kk * a
        bonus = jnp.sum(r * k * rk, axis=-1, keepdims=True) * v
        rows = []
        for h in range(RW_HEADS):
            hs = slice(h, h + 1)
            S = s_ref[0, 0, h]
            sa = jnp.sum(S * (-kk[hs]), axis=-1, keepdims=True)
            v_col = jnp.sum(eye * v[hs], axis=-1, keepdims=True)
            S = S * w[hs] + sa * b[hs] + v_col * k[hs]
            s_ref[0, 0, h] = S
            y = jnp.sum(S * r[hs], axis=-1, keepdims=True)
            mu = jnp.mean(y, axis=0, keepdims=True)
            var = jnp.mean(jnp.square(y - mu), axis=0, keepdims=True)
            yn = (y - mu) * lax.rsqrt(var + GN_EPS)
            rows.append(jnp.sum(yn * eye, axis=0, keepdims=True))
        y_ref[0, t] = jnp.concatenate(rows, axis=0) * lnw + lnb + bonus
        return carry

    lax.fori_loop(0, tc, step, 0)


def _wkv(r, w, k, v, kk, a, r_k, ln_w, ln_b, s0_all, layer, stacked, tc):
    B, T = r.shape[0], r.shape[1]
    hd = (RW_HEADS, RW_HEAD)
    seq = pl.BlockSpec((1, tc, *hd), lambda b, c: (b, c, 0, 0))
    par = pl.BlockSpec(hd, lambda b, c: (0, 0))
    state = pl.BlockSpec((1, 1, RW_HEADS, RW_HEAD, RW_HEAD), lambda b, c: (layer, b, 0, 0, 0))
    args = [r, w, k, v, kk, a, r_k, ln_w.reshape(hd), ln_b.reshape(hd), s0_all, stacked]
    y, s_new = pl.pallas_call(
        functools.partial(_wkv_kernel, tc),
        grid=(B, T // tc),
        in_specs=[seq] * 6 + [par] * 3 + [state, pl.BlockSpec(memory_space=pl.ANY)],
        out_specs=[seq, state],
        out_shape=[
            jax.ShapeDtypeStruct((B, T, *hd), f32),
            jax.ShapeDtypeStruct(s0_all.shape, f32),
        ],
        input_output_aliases={len(args) - 1: 1},
        compiler_params=_cparams(("parallel", "arbitrary")),
        name="wkv_steps",
    )(*args)
    return y.reshape(B, T, RWKV_WIDTH), s_new


CHUNK = 64
PAIR = 2 * RW_HEAD
N_PAIRS = RW_HEADS // 2


def _mmb(a, b, dims=_NN):
    return lax.dot_general(a.astype(bf16), b.astype(bf16), dims, preferred_element_type=f32)


def _lora_dot(x, wh_ref, wl_ref):
    xh, xl = _split_bf16(x)
    rows = x.shape[0]
    both = jnp.dot(jnp.concatenate([xh, xl], axis=0), wh_ref[...], preferred_element_type=f32)
    return both[:rows] + both[rows:] + jnp.dot(xh, wl_ref[...], preferred_element_type=f32)


CHUNKS_PER_STEP = 4


def _wkv_chunk_kernel(r_ref, kr_ref, vr_ref, wa_ref, gr_ref, pr_ref, pkr_ref, pvr_ref, pwa_ref,
                      mu_r_ref, mu_kr_ref, mu_vr_ref, mu_wa_ref, w0_ref, a0_ref, kk_ref, ka_ref,
                      wdh_ref, wdl_ref, wih_ref, wil_ref, rk_ref, lnw_ref, lnb_ref, s0_ref,
                      y_ref, s_ref, sp_ref, cr_ref, ckr_ref, cvr_ref, cwa_ref):
    C = CHUNK
    rows_blk = r_ref.shape[1]
    c = pl.program_id(1)

    @pl.when(c == 0)
    def _():
        for p in range(N_PAIRS):
            sp_ref[p] = jnp.concatenate([s0_ref[0, 2 * p], s0_ref[0, 2 * p + 1]], axis=1)
        for carry, first in ((cr_ref, pr_ref), (ckr_ref, pkr_ref), (cvr_ref, pvr_ref), (cwa_ref, pwa_ref)):
            carry[...] = first[0]

    def mixed(cur_ref, carry_ref, mu_ref):
        cur = cur_ref[0]
        first = lax.broadcasted_iota(jnp.int32, cur.shape, 0) == 0
        prev = jnp.where(first, carry_ref[...], pltpu.roll(cur, 1, 0))
        carry_ref[...] = cur[rows_blk - 1:rows_blk, :]
        return cur + (prev - cur) * mu_ref[...]

    r_blk = mixed(r_ref, cr_ref, mu_r_ref)
    kr_blk = mixed(kr_ref, ckr_ref, mu_kr_ref)
    v_blk = mixed(vr_ref, cvr_ref, mu_vr_ref)
    wa = mixed(wa_ref, cwa_ref, mu_wa_ref)
    nz = -(w0_ref[...] + _lora_dot(jnp.tanh(wa), wdh_ref, wdl_ref))
    softplus = jnp.maximum(nz, 0.0) + jnp.log1p(jnp.exp(-jnp.abs(nz)))
    lw_blk = -jnp.exp(-softplus - 0.5)
    icl_blk = jax.nn.sigmoid(a0_ref[...] + _lora_dot(wa, wih_ref, wil_ref))
    k_blk = kr_blk * (1.0 + (icl_blk - 1.0) * ka_ref[...])
    kkr_blk = kr_blk * kk_ref[...]

    row = lax.broadcasted_iota(jnp.int32, (PAIR, PAIR), 0)
    col = lax.broadcasted_iota(jnp.int32, (PAIR, PAIR), 1)
    tril = row >= col
    stril = row > col
    tril2 = jnp.concatenate([tril, tril], axis=1)
    eye = (row == col).astype(f32)
    lane_lo = lax.broadcasted_iota(jnp.int32, (C, PAIR), 1) < RW_HEAD
    pairs = range(N_PAIRS)
    sls = [slice(p * PAIR, (p + 1) * PAIR) for p in pairs]

    def bd(x):
        zero = jnp.zeros_like(x)
        return jnp.concatenate([jnp.where(lane_lo, x, zero), jnp.where(lane_lo, zero, x)], axis=0)

    def head_sums(x):
        lo_sum = jnp.sum(jnp.where(lane_lo, x, 0.0), axis=-1, keepdims=True)
        hi_sum = jnp.sum(jnp.where(lane_lo, 0.0, x), axis=-1, keepdims=True)
        return jnp.where(lane_lo, lo_sum, hi_sum)

    state = [sp_ref[p] for p in pairs]
    for j in range(rows_blk // C):
        rs = slice(j * C, (j + 1) * C)
        lw = lw_blk[rs]
        width = lw.shape[1]
        lw_a = lw.astype(bf16)
        rest = lw - lw_a.astype(f32)
        lw_b = rest.astype(bf16)
        lw_c = (rest - lw_b.astype(f32)).astype(bf16)
        g3 = jnp.dot(tril[:C, :C].astype(bf16), jnp.concatenate([lw_a, lw_b, lw_c], axis=1),
                     preferred_element_type=f32)
        g = g3[:, :width] + g3[:, width:2 * width] + g3[:, 2 * width:]
        e_g = jnp.exp(g)
        e_ng = jnp.exp(-g)
        e_gm = jnp.exp(g - lw)
        e_end = e_g[C - 1:C, :]

        ins = [[t[rs, sl] for t in (r_blk, k_blk, v_blk, kkr_blk, icl_blk)] for sl in sls]
        norms = [jnp.sqrt(head_sums(x[3] * x[3])) for x in ins]
        at, rt, bt, kt, vb = ([] for _ in range(5))
        for (r, k, v, kkr, icl), norm, sl in zip(ins, norms, sls):
            kk = kkr / jnp.maximum(norm, 1e-12)
            b = kk * icl
            at.append(bd((-kk * e_gm[:, sl]).astype(bf16)))
            rt.append(bd((r * e_g[:, sl]).astype(bf16)))
            bt.append(bd((b * e_ng[:, sl]).astype(bf16)))
            kt.append(bd((k * e_ng[:, sl]).astype(bf16)))
            vb.append(bd(v.astype(bf16)))

        gram = [_mmb(jnp.concatenate([at[p], rt[p]], axis=0), jnp.concatenate([bt[p], kt[p]], axis=0), _NT)
                for p in pairs]
        lmat = [jnp.where(stril, gm[:PAIR, :PAIR], 0.0) for gm in gram]
        mv = [_mmb(jnp.where(stril, gram[p][:PAIR, PAIR:], 0.0), vb[p]) for p in pairs]
        lower = [jnp.where(tril2, gm[PAIR:, :], 0.0).astype(bf16) for gm in gram]

        tinv = [eye + lm for lm in lmat]
        pw = [_mmb(lm, lm) for lm in lmat]
        for _ in range(4):
            z = [_mmb(jnp.concatenate([x.astype(bf16), t.astype(bf16)], axis=0), x)
                 for t, x in zip(tinv, pw)]
            pw = [zz[:PAIR] for zz in z]
            tinv = [t + zz[PAIR:] for t, zz in zip(tinv, z)]
        tinv = [t + _mmb(t, x) for t, x in zip(tinv, pw)]

        wx = [_mmb(tinv[p], jnp.concatenate([at[p], mv[p].astype(bf16)], axis=1)) for p in pairs]
        uy0 = [_mmb(jnp.concatenate([wx[p][:, :PAIR].astype(bf16), rt[p]], axis=0),
                    bd(state[p].astype(bf16)), _NT) for p in pairs]
        uv = [jnp.concatenate([(uy0[p][:PAIR] + wx[p][:, PAIR:]).astype(bf16), vb[p]], axis=0)
              for p in pairs]
        ys = [uy0[p][PAIR:] + _mmb(lower[p], uv[p]) for p in pairs]
        s_add = [_mmb(uv[p], jnp.concatenate([bt[p], kt[p]], axis=0), _TN) for p in pairs]
        state = [(state[p] + s_add[p][:RW_HEAD] + s_add[p][RW_HEAD:]) * e_end[:, sls[p]] for p in pairs]

        ys = [y[:C] + y[C:] for y in ys]
        mus = [head_sums(y) * (1.0 / RW_HEAD) for y in ys]
        ds = [y - mu for y, mu in zip(ys, mus)]
        var = [head_sums(d * d) * (1.0 / RW_HEAD) for d in ds]
        bonus = [head_sums(x[0] * x[1] * rk_ref[:, sl]) * x[2] for x, sl in zip(ins, sls)]
        for p in pairs:
            sl = sls[p]
            y_rw = ds[p] * lax.rsqrt(var[p] + GN_EPS) * lnw_ref[:, sl] + lnb_ref[:, sl] + bonus[p]
            y_ref[0, rs, sl] = (y_rw * _silu(gr_ref[0, rs, sl])).astype(y_ref.dtype)

    for p in pairs:
        sp_ref[p] = state[p]

    @pl.when(c == pl.num_programs(1) - 1)
    def _():
        for p in range(N_PAIRS):
            s_ref[0, 2 * p] = state[p][:, :RW_HEAD]
            s_ref[0, 2 * p + 1] = state[p][:, RW_HEAD:]


def _wkv_chunked(proj, prev, mu, w0, a0, k_k, k_a, lora, r_k, ln_w, ln_b, s0):
    B, T, _ = proj.shape
    W = RWKV_WIDTH
    rows = CHUNKS_PER_STEP * CHUNK
    col = lambda w, off: pl.BlockSpec((1, rows, w), lambda b, c: (b, c, off // w))
    first = lambda w: pl.BlockSpec((1, 1, w), lambda b, c: (b, 0, 0))
    vec = lambda w: pl.BlockSpec((1, w), lambda b, c: (0, 0))
    lora_spec = pl.BlockSpec((LORA_PAIR, W), lambda b, c: (0, 0))
    state = pl.BlockSpec((1, RW_HEADS, RW_HEAD, RW_HEAD), lambda b, c: (b, 0, 0, 0))
    return pl.pallas_call(
        _wkv_chunk_kernel,
        grid=(B, T // rows),
        in_specs=[col(W, P_R), col(W, P_KR), col(W, P_VR), col(LANES, P_WA), col(W, P_GR),
                  first(W), first(W), first(W), first(LANES),
                  vec(W), vec(W), vec(W), vec(LANES), vec(W), vec(W), vec(W), vec(W),
                  lora_spec, lora_spec, lora_spec, lora_spec, vec(W), vec(W), vec(W), state],
        out_specs=[col(W, 0), state],
        out_shape=[
            jax.ShapeDtypeStruct((B, T, W), bf16),
            jax.ShapeDtypeStruct((B, RW_HEADS, RW_HEAD, RW_HEAD), f32),
        ],
        scratch_shapes=[pltpu.VMEM((N_PAIRS, RW_HEAD, PAIR), f32),
                        pltpu.VMEM((1, W), f32), pltpu.VMEM((1, W), f32), pltpu.VMEM((1, W), f32),
                        pltpu.VMEM((1, LANES), f32)],
        compiler_params=_cparams(("parallel", "arbitrary")),
        name="wkv_chunks",
    )(proj, proj, proj, proj, proj, *prev, *mu, w0, a0, k_k, k_a, *lora,
      r_k.reshape(1, W), ln_w.reshape(1, W), ln_b.reshape(1, W), s0)


def _post_kernel(final, gated, att_ref, y_ref, *refs):
    if gated:
        y = y_ref[0]
    else:
        y = (y_ref[0] * _silu(refs[0][0])).astype(bf16)
        refs = refs[1:]
    x_ref, gate_ref, w_ref, fg_ref, o_ref = refs
    cat = jnp.concatenate([att_ref[0], y], axis=1)
    out = jnp.dot(cat, w_ref[...], preferred_element_type=f32)
    x = x_ref[0] + gate_ref[0] * out
    if final:
        ms = jnp.mean(x * x, axis=-1, keepdims=True)
        x = x * lax.rsqrt(ms + NORM_EPS) * fg_ref[...]
    o_ref[0] = x


def _post(att, y_rw, proj, x, gate, w_out_bf, final_g, final, tm):
    G, R, _ = x.shape
    W = RWKV_WIDTH
    gated = proj is None
    gate_in = [] if gated else [pl.BlockSpec((1, tm, W), lambda g, m: (g, m, P_GR // W))]
    gate_arg = [] if gated else [proj]
    return pl.pallas_call(
        functools.partial(_post_kernel, final, gated),
        grid=(G, R // tm),
        in_specs=[
            pl.BlockSpec((1, tm, ATT_WIDTH), lambda g, m: (g, m, 0)),
            pl.BlockSpec((1, tm, W), lambda g, m: (g, m, 0)),
            *gate_in,
            pl.BlockSpec((1, tm, D_MODEL), lambda g, m: (g, m, 0)),
            _mod_spec(gate, tm),
            pl.BlockSpec((D_MODEL, D_MODEL), lambda g, m: (0, 0)),
            pl.BlockSpec((1, D_MODEL), lambda g, m: (0, 0)),
        ],
        out_specs=pl.BlockSpec((1, tm, D_MODEL), lambda g, m: (g, m, 0)),
        out_shape=jax.ShapeDtypeStruct((G, R, D_MODEL), f32),
        compiler_params=_cparams(("parallel", "parallel")),
        name="post_proj",
    )(att, y_rw, *gate_arg, x, gate, w_out_bf, final_g.reshape(1, D_MODEL))


def _arrange_w_in(w):
    pad = jnp.zeros((w.shape[0], P_WIDTH - IN_WIDTH), w.dtype)
    parts = [w[:, Q_OFF:KA_OFF], w[:, GA_OFF:GR_OFF], w[:, GR_OFF:IN_WIDTH], w[:, R_OFF:WD_OFF],
             w[:, KA_OFF:R_OFF], w[:, WD_OFF:GA_OFF], pad]
    return jnp.concatenate(parts, axis=1).astype(bf16)


def _shift_cols(t):
    return jnp.concatenate([t[..., P_R:P_KA], t[..., P_WA:P_WA + LORA_PAIR]], axis=-1)


def kernel(x_prompt, x_sample, cache_k, cache_v, state_wkv, state_shift, c_prompt, c_sample,
           norm_g, w_ada, b_ada, w_in, mu_shift, w0, w_decay, a0, w_iclr, k_k, k_a, r_k,
           ln_w, ln_b, sinks, w_out, final_g):
    Bp, Tp = x_prompt.shape[0], x_prompt.shape[1]
    Bd = x_sample.shape[0]
    W = RWKV_WIDTH

    n_c = Bp + Bd
    c_rows = -(-n_c // 16) * 16
    c_all = jnp.concatenate([c_prompt, c_sample, jnp.zeros((c_rows - n_c, D_MODEL), f32)], axis=0)
    mod = _ada(c_all, w_ada, b_ada)

    tab_p = _rope_tables(jnp.arange(Tp, dtype=jnp.int32))
    tab_s = _rope_tables(jnp.full((Bd,), PAST_LEN, jnp.int32))

    hp = x_prompt
    hs = x_sample.reshape(1, Bd, D_MODEL)
    s0_p = jnp.zeros((Bp, RW_HEADS, RW_HEAD, RW_HEAD), f32)
    shift0_p = [jnp.zeros((Bp, 1, w), f32) for w in (W, W, W, LORA_PAIR)]
    ck_all = cache_k.reshape(DEPTH, Bd, WINDOW, KV_WIDTH)
    cv_all = cache_v.reshape(DEPTH, Bd, WINDOW, KV_WIDTH)
    new_state_s = jnp.zeros(state_wkv.shape, f32)
    new_cache_s = [jnp.zeros(ck_all.shape, f32), jnp.zeros(cv_all.shape, f32)]
    outs = {k: [] for k in ("kp", "vp", "sp", "shp", "shs")}
    for l in range(DEPTH):
        final = l == DEPTH - 1
        w_bf = _arrange_w_in(w_in[l])
        w_out_bf = w_out[l].astype(bf16)
        mu_l = mu_shift[l]
        mu = [mu_l[0:W].reshape(1, W), mu_l[W:2 * W].reshape(1, W), mu_l[2 * W:3 * W].reshape(1, W),
              mu_l[3 * W:].reshape(1, LORA_PAIR)]
        vecs = [t[l].reshape(1, W) for t in (w0, a0, k_k, k_a)]
        wd_pad = jnp.concatenate([w_decay[l], jnp.zeros((ICLR_LORA, W), f32)], axis=0)
        wi_pad = jnp.concatenate([jnp.zeros((DECAY_LORA, W), f32), w_iclr[l]], axis=0)
        shift_p, scale_p, gate_p = (mod[l, :Bp, i * D_MODEL:(i + 1) * D_MODEL].reshape(Bp, 1, D_MODEL)
                                    for i in range(3))
        shift_s, scale_s, gate_s = (mod[l, Bp:n_c, i * D_MODEL:(i + 1) * D_MODEL].reshape(1, Bd, D_MODEL)
                                    for i in range(3))

        proj = _norm_proj(hp, norm_g[l], scale_p, shift_p, w_bf, tab_p, tm=1024)
        att = _attn_prompt(proj, sinks[l])
        lora = [piece for wp in (wd_pad, wi_pad) for piece in _split_bf16(wp)]
        y_rw, s_t = _wkv_chunked(proj, shift0_p, mu, *vecs, lora, r_k[l], ln_w[l], ln_b[l], s0_p)
        hp = _post(att, y_rw, None, hp, gate_p, w_out_bf, final_g, final, tm=512)
        tail = proj[:, Tp - WINDOW:]
        outs["kp"].append(tail[..., P_KA:P_KA + KV_WIDTH].reshape(Bp, WINDOW, N_KV_HEADS, HEAD_DIM))
        outs["vp"].append(tail[..., P_VA:P_VA + KV_WIDTH].reshape(Bp, WINDOW, N_KV_HEADS, HEAD_DIM))
        outs["sp"].append(s_t)
        outs["shp"].append(_shift_cols(proj[:, Tp - 1]))

        proj = _norm_proj(hs, norm_g[l], scale_s, shift_s, w_bf, tab_s, tm=Bd)
        att, *new_cache_s = _attn_sample(proj, ck_all, cv_all, l, new_cache_s, sinks[l])
        sh = state_shift[l]
        prev = [sh[None, :, 0:W], sh[None, :, W:2 * W], sh[None, :, 2 * W:3 * W], sh[None, :, 3 * W:]]
        prep = _rwkv_prep(proj, prev, mu, *vecs, wd_pad, wi_pad, tm=Bd)
        prep = [t[:, None] for t in prep]
        y_rw, new_state_s = _wkv(*prep, r_k[l], ln_w[l], ln_b[l], state_wkv, l, new_state_s, tc=1)
        hs = _post(att, y_rw.reshape(1, Bd, W), proj, hs, gate_s, w_out_bf, final_g, final, tm=Bd)
        outs["shs"].append(_shift_cols(proj[0]))

    st = lambda k: jnp.stack(outs[k])
    return (hp, hs.reshape(Bd, 1, D_MODEL), st("kp"), st("vp"), st("sp"), st("shp"),
            *(t.reshape(cache_k.shape) for t in new_cache_s), new_state_s, st("shs"))
```

```python
import functools

import jax
import jax.numpy as jnp
from jax import lax
from jax.experimental import pallas as pl
from jax.experimental.pallas import tpu as pltpu

f32 = jnp.float32
bf16 = jnp.bfloat16

D_MODEL = 2048
DEPTH = 2
PAST_LEN = 16384
ATT_WIDTH = 1024
RWKV_WIDTH = 1024
HEAD_DIM = 64
N_Q_HEADS = 16
N_KV_HEADS = 4
GQA_GROUP = 4
KV_WIDTH = 256
WINDOW = 128
ROT_DIM = 16
ROPE_THETA = 500000.0
RW_HEAD = 64
RW_HEADS = 16
DECAY_LORA = 64
ICLR_LORA = 64
LORA_PAIR = DECAY_LORA + ICLR_LORA
NORM_EPS = 1e-5
GN_EPS = 64e-5
NEG_BIG = -1e30

Q_OFF = 0
KA_OFF = Q_OFF + ATT_WIDTH
VA_OFF = KA_OFF + KV_WIDTH
R_OFF = VA_OFF + KV_WIDTH
KR_OFF = R_OFF + RWKV_WIDTH
VR_OFF = KR_OFF + RWKV_WIDTH
WD_OFF = VR_OFF + RWKV_WIDTH
AD_OFF = WD_OFF + DECAY_LORA
GA_OFF = AD_OFF + ICLR_LORA
GR_OFF = GA_OFF + ATT_WIDTH
IN_WIDTH = GR_OFF + RWKV_WIDTH
SHIFT_DIM = GA_OFF - R_OFF

LANES = 128
P_Q = 0
P_GA = 1024
P_GR = 2048
P_R = 3072
P_KR = 4096
P_VR = 5120
P_KA = 6144
P_VA = 6400
P_WA = 6656
P_WIDTH = 7168
PROJ_TN = 1792

VMEM_LIMIT = 56 * 1024 * 1024


_NN = (((1,), (0,)), ((), ()))
_NT = (((1,), (1,)), ((), ()))
_TN = (((0,), (0,)), ((), ()))


def _silu(x):
    return x * jax.nn.sigmoid(x)


def _cparams(sem):
    return pltpu.CompilerParams(dimension_semantics=sem, vmem_limit_bytes=VMEM_LIMIT)


def _split_bf16(x):
    hi = x.astype(bf16)
    return hi, (x - hi.astype(f32)).astype(bf16)


def _ada_kernel(c_ref, w_ref, b_ref, o_ref):
    ch, cl = _split_bf16(_silu(c_ref[...]))
    wh, wl = _split_bf16(w_ref[0])
    rows = ch.shape[0]
    both = jnp.dot(jnp.concatenate([ch, cl], axis=0), wh, preferred_element_type=f32)
    o_ref[0] = both[:rows] + both[rows:] + jnp.dot(ch, wl, preferred_element_type=f32) + b_ref[0]


def _ada(c_all, w_ada, b_ada):
    rows = c_all.shape[0]
    tn = 1536
    n_out = w_ada.shape[2]
    return pl.pallas_call(
        _ada_kernel,
        grid=(DEPTH, n_out // tn),
        in_specs=[
            pl.BlockSpec((rows, D_MODEL), lambda l, n: (0, 0)),
            pl.BlockSpec((1, D_MODEL, tn), lambda l, n: (l, 0, n)),
            pl.BlockSpec((1, 1, tn), lambda l, n: (l, 0, n)),
        ],
        out_specs=pl.BlockSpec((1, rows, tn), lambda l, n: (l, 0, n)),
        out_shape=jax.ShapeDtypeStruct((DEPTH, rows, n_out), f32),
        compiler_params=_cparams(("parallel", "parallel")),
        name="ada_mod",
    )(c_all, w_ada, b_ada.reshape(DEPTH, 1, n_out))


def _rope(x, tab):
    w = x.shape[1]
    reps = w // LANES
    cosf, up, dn = (jnp.concatenate([tab[i]] * reps, axis=1) for i in range(3))
    half = ROT_DIM // 2
    return x * cosf + pltpu.roll(x, w - half, 1) * up + pltpu.roll(x, half, 1) * dn


def _norm_proj_kernel(x_ref, g_ref, scale_ref, shift_ref, w_ref, tab_ref, o_ref, h_ref):
    n = pl.program_id(2)

    @pl.when(n == 0)
    def _():
        x = x_ref[0]
        ms = jnp.mean(x * x, axis=-1, keepdims=True)
        y = x * lax.rsqrt(ms + NORM_EPS) * g_ref[...]
        h_ref[...] = (y * (1.0 + scale_ref[0]) + shift_ref[0]).astype(bf16)

    tm, tn = o_ref.shape[1], o_ref.shape[2]
    q_tile, q_lo = divmod(P_Q, tn)
    k_tile, k_lo = divmod(P_KA, tn)
    assert q_lo + ATT_WIDTH <= tn and k_lo + KV_WIDTH <= tn and q_tile != k_tile
    rc = min(tm, 256)

    def rotated_tile(lo, width, scale):
        for i in range(tm // rc):
            rs = slice(i * rc, (i + 1) * rc)
            res = jnp.dot(h_ref[rs, :], w_ref[...], preferred_element_type=f32)
            parts = [res[:, :lo]] if lo else []
            parts.append(_rope(res[:, lo:lo + width], tab_ref[:, rs, :]) * scale)
            if lo + width < tn:
                parts.append(res[:, lo + width:])
            o_ref[0, rs, :] = jnp.concatenate(parts, axis=1) if len(parts) > 1 else parts[0]

    @pl.when(n == q_tile)
    def _():
        rotated_tile(q_lo, ATT_WIDTH, HEAD_DIM ** -0.5)

    @pl.when(n == k_tile)
    def _():
        rotated_tile(k_lo, KV_WIDTH, 1.0)

    @pl.when((n != q_tile) & (n != k_tile))
    def _():
        o_ref[0] = jnp.dot(h_ref[...], w_ref[...], preferred_element_type=f32)


def _mod_spec(mod, tm):
    if mod.shape[1] == 1:
        return pl.BlockSpec((1, 1, D_MODEL), lambda g, m, *_: (g, 0, 0))
    return pl.BlockSpec((1, tm, D_MODEL), lambda g, m, *_: (g, m, 0))


def _norm_proj(x, norm_g, scale, shift, w_bf, tab, tm):
    G, R, _ = x.shape
    return pl.pallas_call(
        _norm_proj_kernel,
        grid=(G, R // tm, P_WIDTH // PROJ_TN),
        in_specs=[
            pl.BlockSpec((1, tm, D_MODEL), lambda g, m, n: (g, m, 0)),
            pl.BlockSpec((1, D_MODEL), lambda g, m, n: (0, 0)),
            _mod_spec(scale, tm),
            _mod_spec(shift, tm),
            pl.BlockSpec((D_MODEL, PROJ_TN), lambda g, m, n: (0, n)),
            pl.BlockSpec((3, tm, LANES), lambda g, m, n: (0, m, 0)),
        ],
        out_specs=pl.BlockSpec((1, tm, PROJ_TN), lambda g, m, n: (g, m, n)),
        out_shape=jax.ShapeDtypeStruct((G, R, P_WIDTH), f32),
        scratch_shapes=[pltpu.VMEM((tm, D_MODEL), bf16)],
        compiler_params=_cparams(("parallel", "parallel", "arbitrary")),
        name="norm_proj",
    )(x, norm_g.reshape(1, D_MODEL), scale, shift, w_bf, tab)


def _rope_tables(pos):
    half = ROT_DIM // 2
    inv_freq = ROPE_THETA ** (-jnp.arange(half, dtype=f32) * (2.0 / ROT_DIM))
    ang = pos.astype(f32)[:, None] * inv_freq[None, :]
    cos, sin = jnp.cos(ang), jnp.sin(ang)
    t = pos.shape[0]
    z8 = jnp.zeros((t, half), f32)
    rest = HEAD_DIM - ROT_DIM
    cos64 = jnp.concatenate([cos, cos, jnp.ones((t, rest), f32)], axis=1)
    up64 = jnp.concatenate([-sin, z8, jnp.zeros((t, rest), f32)], axis=1)
    dn64 = jnp.concatenate([z8, sin, jnp.zeros((t, rest), f32)], axis=1)
    rep = LANES // HEAD_DIM
    return jnp.stack([jnp.tile(a, (1, rep)) for a in (cos64, up64, dn64)])


ATTN_BLOCKS = 8


def _attn_prompt_kernel(sinks_ref, q_ref, kc_ref, kp_ref, vc_ref, vp_ref, ga_ref, o_ref):
    n = pl.program_id(1)
    wn = WINDOW
    half = HEAD_DIM
    k_t_all = jnp.concatenate([kp_ref[0], kc_ref[0]], axis=0).T.astype(bf16)
    v_all = jnp.concatenate([vp_ref[0], vc_ref[0]], axis=0)

    qi = lax.broadcasted_iota(jnp.int32, (2 * wn, 2 * wn), 0) & (wn - 1)
    kj = lax.broadcasted_iota(jnp.int32, (2 * wn, 2 * wn), 1)
    rel = wn + qi - kj
    band = (rel >= 0) & (rel <= wn)
    top = lax.broadcasted_iota(jnp.int32, (2 * wn, 1), 0) < wn
    lo = lax.broadcasted_iota(jnp.int32, (2 * wn, LANES), 1) < half
    zeros_k = jnp.zeros((half, 2 * wn), bf16)
    ones_lo = jnp.where(lo, 1.0, 0.0).astype(bf16)
    ones_hi = jnp.where(lo, 0.0, 1.0).astype(bf16)

    for sb in range(q_ref.shape[1] // wn):
        rows = slice(sb * wn, (sb + 1) * wn)
        q = q_ref[0, rows, :].astype(bf16)
        ga = ga_ref[0, rows, :]
        k_t = k_t_all[:, sb * wn:(sb + 2) * wn]
        vcat = v_all[sb * wn:(sb + 2) * wn]
        mask = band & ((kj >= wn) | (n > 0)) if sb == 0 else band
        for j in range(N_KV_HEADS // 2):
            vblk = vcat[:, j * LANES:(j + 1) * LANES]
            vswap = pltpu.roll(vblk, half, 1)
            for g in (2 * j, 2 * j + 1):
                own, other = (vblk, vswap) if g % 2 == 0 else (vswap, vblk)
                v_lo = jnp.where(lo, own, 0.0).astype(bf16)
                v_hi = jnp.where(lo, 0.0, other).astype(bf16)
                rhs_pv = jnp.concatenate([jnp.concatenate([v_lo, ones_lo], axis=1),
                                          jnp.concatenate([v_hi, ones_hi], axis=1)], axis=0)
                kg = k_t[g * half:(g + 1) * half, :]
                rhs_qk = jnp.concatenate([jnp.concatenate([kg, zeros_k], axis=0),
                                          jnp.concatenate([zeros_k, kg], axis=0)], axis=1)
                b0, b1 = 2 * g, 2 * g + 1
                qg = jnp.concatenate([q[:, b0 * LANES:(b0 + 1) * LANES], q[:, b1 * LANES:(b1 + 1) * LANES]],
                                     axis=0)
                s_all = jnp.dot(qg, rhs_qk, preferred_element_type=f32)
                ps, es = [], []
                for hh in range(2):
                    s = jnp.where(mask, s_all[:, hh * 2 * wn:(hh + 1) * 2 * wn], NEG_BIG)
                    sink = jnp.where(top, sinks_ref[2 * b0 + hh], sinks_ref[2 * b1 + hh])
                    m = jnp.maximum(jnp.max(s, axis=-1, keepdims=True), sink)
                    ps.append(jnp.exp(s - m).astype(bf16))
                    es.append(jnp.exp(sink - m))
                res = jnp.dot(jnp.concatenate(ps, axis=1), rhs_pv, preferred_element_type=f32)
                out = res[:, :LANES] / (res[:, LANES:] + jnp.where(lo, es[0], es[1]))
                for i, blk in enumerate((b0, b1)):
                    sl = slice(blk * LANES, (blk + 1) * LANES)
                    o_ref[0, rows, sl] = (out[i * wn:(i + 1) * wn] * _silu(ga[:, sl])).astype(o_ref.dtype)


def _attn_prompt(proj, sinks):
    B, T, _ = proj.shape
    rows = ATTN_BLOCKS * WINDOW
    kvb = KV_WIDTH
    prev = lambda b, n: jnp.maximum(n * ATTN_BLOCKS - 1, 0)
    return pl.pallas_call(
        _attn_prompt_kernel,
        grid=(B, T // rows),
        in_specs=[
            pl.BlockSpec(memory_space=pltpu.SMEM),
            pl.BlockSpec((1, rows, ATT_WIDTH), lambda b, n: (b, n, P_Q // ATT_WIDTH)),
            pl.BlockSpec((1, rows, kvb), lambda b, n: (b, n, P_KA // kvb)),
            pl.BlockSpec((1, WINDOW, kvb), lambda b, n: (b, prev(b, n), P_KA // kvb)),
            pl.BlockSpec((1, rows, kvb), lambda b, n: (b, n, P_VA // kvb)),
            pl.BlockSpec((1, WINDOW, kvb), lambda b, n: (b, prev(b, n), P_VA // kvb)),
            pl.BlockSpec((1, rows, ATT_WIDTH), lambda b, n: (b, n, P_GA // ATT_WIDTH)),
        ],
        out_specs=pl.BlockSpec((1, rows, ATT_WIDTH), lambda b, n: (b, n, 0)),
        out_shape=jax.ShapeDtypeStruct((B, T, ATT_WIDTH), bf16),
        compiler_params=_cparams(("parallel", "arbitrary")),
        name="attn_prompt",
    )(sinks, proj, proj, proj, proj, proj, proj)


SAMPLE_ROWS = 16


def _attn_sample_kernel(sinks_ref, q_ref, kn_ref, vn_ref, ga_ref, ck_ref, cv_ref, *refs):
    o_ref, nk_ref, nv_ref = refs[-3:]
    rb = q_ref.shape[1]
    q_all, kn_all, vn_all = q_ref[0], kn_ref[0], vn_ref[0]
    work = [(r, g) for r in range(rb) for g in range(N_KV_HEADS)]
    span = lambda g: slice(g * HEAD_DIM, (g + 1) * HEAD_DIM)
    ckb = [ck_ref[0, r].astype(bf16) for r in range(rb)]
    cvb = [cv_ref[0, r].astype(bf16) for r in range(rb)]
    qg = {(r, g): jnp.concatenate(
        [q_all[r:r + 1, (g * GQA_GROUP + i) * HEAD_DIM:(g * GQA_GROUP + i + 1) * HEAD_DIM]
         for i in range(GQA_GROUP)], axis=0) for r, g in work}
    s = {rg: lax.dot_general(qg[rg].astype(bf16), ckb[rg[0]][:, span(rg[1])], _NT,
                             preferred_element_type=f32) for rg in work}
    outs = {}
    for r, g in work:
        s_new = jnp.sum(qg[r, g] * kn_all[r:r + 1, span(g)], axis=-1, keepdims=True)
        sink = sinks_ref[g * GQA_GROUP:(g + 1) * GQA_GROUP, :]
        m = jnp.maximum(jnp.maximum(jnp.max(s[r, g], axis=-1, keepdims=True), s_new), sink)
        p = jnp.exp(s[r, g] - m)
        p_new = jnp.exp(s_new - m)
        den = jnp.sum(p, axis=-1, keepdims=True) + p_new + jnp.exp(sink - m)
        o = (jnp.dot(p.astype(bf16), cvb[r][:, span(g)], preferred_element_type=f32)
             + p_new * vn_all[r:r + 1, span(g)]) / den
        outs[r, g] = [o[i:i + 1, :] for i in range(GQA_GROUP)]
    att = jnp.concatenate(
        [jnp.concatenate([h for g in range(N_KV_HEADS) for h in outs[r, g]], axis=1) for r in range(rb)],
        axis=0)
    o_ref[0] = (att * _silu(ga_ref[0])).astype(o_ref.dtype)
    last = lax.broadcasted_iota(jnp.int32, (WINDOW, KV_WIDTH), 0) == WINDOW - 1
    for r in range(rb):
        nk_ref[0, r] = jnp.where(last, kn_all[r:r + 1], pltpu.roll(ck_ref[0, r], WINDOW - 1, 0))
        nv_ref[0, r] = jnp.where(last, vn_all[r:r + 1], pltpu.roll(cv_ref[0, r], WINDOW - 1, 0))


def _attn_sample(proj, ck_all, cv_all, layer, stacked, sinks):
    Bd = proj.shape[1]
    rb = SAMPLE_ROWS
    col = lambda w, off: pl.BlockSpec((1, rb, w), lambda i: (0, i, off // w))
    cache = pl.BlockSpec((1, rb, WINDOW, KV_WIDTH), lambda i: (layer, i, 0, 0))
    args = [sinks.reshape(N_Q_HEADS, 1), proj, proj, proj, proj, ck_all, cv_all, *stacked]
    hbm = pl.BlockSpec(memory_space=pl.ANY)
    return pl.pallas_call(
        _attn_sample_kernel,
        grid=(Bd // rb,),
        in_specs=[
            pl.BlockSpec((N_Q_HEADS, 1), lambda i: (0, 0)),
            col(ATT_WIDTH, P_Q), col(KV_WIDTH, P_KA), col(KV_WIDTH, P_VA), col(ATT_WIDTH, P_GA),
            cache, cache, hbm, hbm,
        ],
        out_specs=[pl.BlockSpec((1, rb, ATT_WIDTH), lambda i: (0, i, 0)), cache, cache],
        out_shape=[
            jax.ShapeDtypeStruct((1, Bd, ATT_WIDTH), bf16),
            jax.ShapeDtypeStruct(ck_all.shape, f32),
            jax.ShapeDtypeStruct(cv_all.shape, f32),
        ],
        input_output_aliases={len(args) - 2: 1, len(args) - 1: 2},
        compiler_params=_cparams(("parallel",)),
        name="attn_sample",
    )(*args)


def _rwkv_prep_kernel(r_ref, kr_ref, vr_ref, wa_ref, pr_ref, pkr_ref, pvr_ref, pwa_ref,
                      mu_r_ref, mu_kr_ref, mu_vr_ref, mu_wa_ref, w0_ref, a0_ref, kk_ref, ka_ref,
                      wd_ref, wi_ref, ro_ref, wo_ref, ko_ref, vo_ref, kko_ref, ao_ref):
    def mixed(cur_ref, prev_ref, mu_ref):
        cur = cur_ref[0]
        return cur + (prev_ref[0] - cur) * mu_ref[...]

    r = mixed(r_ref, pr_ref, mu_r_ref)
    kr = mixed(kr_ref, pkr_ref, mu_kr_ref)
    vr = mixed(vr_ref, pvr_ref, mu_vr_ref)
    wa = mixed(wa_ref, pwa_ref, mu_wa_ref)
    hi = lax.Precision.HIGHEST
    z = w0_ref[...] + jnp.dot(jnp.tanh(wa), wd_ref[...], precision=hi, preferred_element_type=f32)
    nz = -z
    softplus = jnp.maximum(nz, 0.0) + jnp.log1p(jnp.exp(-jnp.abs(nz)))
    w_log = -softplus - 0.5
    a = jax.nn.sigmoid(a0_ref[...] + jnp.dot(wa, wi_ref[...], precision=hi, preferred_element_type=f32))
    outs = (r, -jnp.exp(w_log), kr * (1.0 + (a - 1.0) * ka_ref[...]), vr, kr * kk_ref[...], a)
    for o_ref, val in zip((ro_ref, wo_ref, ko_ref, vo_ref, kko_ref, ao_ref), outs):
        for h in range(RW_HEADS):
            o_ref[:, h, :] = val[:, h * RW_HEAD:(h + 1) * RW_HEAD]


def _rwkv_prep(proj, prev, mu, w0, a0, k_k, k_a, wd_pad, wi_pad, tm):
    G, R, _ = proj.shape
    assert G == 1
    W = RWKV_WIDTH
    cur_specs = [
        pl.BlockSpec((1, tm, W), lambda g, m: (g, m, P_R // W)),
        pl.BlockSpec((1, tm, W), lambda g, m: (g, m, P_KR // W)),
        pl.BlockSpec((1, tm, W), lambda g, m: (g, m, P_VR // W)),
        pl.BlockSpec((1, tm, LANES), lambda g, m: (g, m, P_WA // LANES)),
    ]
    prev_specs = [
        pl.BlockSpec((1, tm, W), lambda g, m: (g, m, 0)),
        pl.BlockSpec((1, tm, W), lambda g, m: (g, m, 0)),
        pl.BlockSpec((1, tm, W), lambda g, m: (g, m, 0)),
        pl.BlockSpec((1, tm, LANES), lambda g, m: (g, m, 0)),
    ]
    prev_args = list(prev)
    vec = lambda w: pl.BlockSpec((1, w), lambda g, m: (0, 0))
    lora = pl.BlockSpec((LORA_PAIR, W), lambda g, m: (0, 0))
    out_spec = pl.BlockSpec((tm, RW_HEADS, RW_HEAD), lambda g, m: (m, 0, 0))
    return pl.pallas_call(
        _rwkv_prep_kernel,
        grid=(G, R // tm),
        in_specs=cur_specs + prev_specs + [vec(W), vec(W), vec(W), vec(LANES),
                                           vec(W), vec(W), vec(W), vec(W), lora, lora],
        out_specs=[out_spec] * 6,
        out_shape=[jax.ShapeDtypeStruct((R, RW_HEADS, RW_HEAD), f32)] * 6,
        compiler_params=_cparams(("parallel", "arbitrary")),
        name="rwkv_prep",
    )(proj, proj, proj, proj, *prev_args, *mu, w0, a0, k_k, k_a, wd_pad, wi_pad)


def _wkv_kernel(tc, r_ref, w_ref, k_ref, v_ref, kk_ref, a_ref, rk_ref, lnw_ref, lnb_ref, s0_ref,
                *refs):
    y_ref, s_ref = refs[-2:]

    @pl.when(pl.program_id(1) == 0)
    def _():
        s_ref[...] = s0_ref[...]

    eye = (lax.broadcasted_iota(jnp.int32, (RW_HEAD, RW_HEAD), 0)
           == lax.broadcasted_iota(jnp.int32, (RW_HEAD, RW_HEAD), 1)).astype(f32)
    rk = rk_ref[...]
    lnw = lnw_ref[...]
    lnb = lnb_ref[...]

    def step(t, carry):
        r, lw, k, v, kkr, a = (ref[0, t] for ref in (r_ref, w_ref, k_ref, v_ref, kk_ref, a_ref))
        w = jnp.exp(lw)
        norm = jnp.sqrt(jnp.sum(kkr * kkr, axis=-1, keepdims=True))
        kk = kkr / jnp.maximum(norm, 1e-12)
        b = kk * a
        bonus = jnp.sum(r * k * rk, axis=-1, keepdims=True) * v
        rows = []
        for h in range(RW_HEADS):
            hs = slice(h, h + 1)
            S = s_ref[0, 0, h]
            sa = jnp.sum(S * (-kk[hs]), axis=-1, keepdims=True)
            v_col = jnp.sum(eye * v[hs], axis=-1, keepdims=True)
            S = S * w[hs] + sa * b[hs] + v_col * k[hs]
            s_ref[0, 0, h] = S
            y = jnp.sum(S * r[hs], axis=-1, keepdims=True)
            mu = jnp.mean(y, axis=0, keepdims=True)
            var = jnp.mean(jnp.square(y - mu), axis=0, keepdims=True)
            yn = (y - mu) * lax.rsqrt(var + GN_EPS)
            rows.append(jnp.sum(yn * eye, axis=0, keepdims=True))
        y_ref[0, t] = jnp.concatenate(rows, axis=0) * lnw + lnb + bonus
        return carry

    lax.fori_loop(0, tc, step, 0)


def _wkv(r, w, k, v, kk, a, r_k, ln_w, ln_b, s0_all, layer, stacked, tc):
    B, T = r.shape[0], r.shape[1]
    hd = (RW_HEADS, RW_HEAD)
    seq = pl.BlockSpec((1, tc, *hd), lambda b, c: (b, c, 0, 0))
    par = pl.BlockSpec(hd, lambda b, c: (0, 0))
    state = pl.BlockSpec((1, 1, RW_HEADS, RW_HEAD, RW_HEAD), lambda b, c: (layer, b, 0, 0, 0))
    args = [r, w, k, v, kk, a, r_k, ln_w.reshape(hd), ln_b.reshape(hd), s0_all, stacked]
    y, s_new = pl.pallas_call(
        functools.partial(_wkv_kernel, tc),
        grid=(B, T // tc),
        in_specs=[seq] * 6 + [par] * 3 + [state, pl.BlockSpec(memory_space=pl.ANY)],
        out_specs=[seq, state],
        out_shape=[
            jax.ShapeDtypeStruct((B, T, *hd), f32),
            jax.ShapeDtypeStruct(s0_all.shape, f32),
        ],
        input_output_aliases={len(args) - 1: 1},
        compiler_params=_cparams(("parallel", "arbitrary")),
        name="wkv_steps",
    )(*args)
    return y.reshape(B, T, RWKV_WIDTH), s_new


CHUNK = 64
PAIR = 2 * RW_HEAD
N_PAIRS = RW_HEADS // 2


def _mmb(a, b, dims=_NN):
    return lax.dot_general(a.astype(bf16), b.astype(bf16), dims, preferred_element_type=f32)


def _lora_dot(x, wh_ref, wl_ref):
    xh, xl = _split_bf16(x)
    rows = x.shape[0]
    both = jnp.dot(jnp.concatenate([xh, xl], axis=0), wh_ref[...], preferred_element_type=f32)
    return both[:rows] + both[rows:] + jnp.dot(xh, wl_ref[...], preferred_element_type=f32)


CHUNKS_PER_STEP = 8


def _wkv_chunk_kernel(r_ref, kr_ref, vr_ref, wa_ref, gr_ref, pr_ref, pkr_ref, pvr_ref, pwa_ref,
                      mu_r_ref, mu_kr_ref, mu_vr_ref, mu_wa_ref, w0_ref, a0_ref, kk_ref, ka_ref,
                      wdh_ref, wdl_ref, wih_ref, wil_ref, rk_ref, lnw_ref, lnb_ref, s0_ref,
                      y_ref, s_ref, sp_ref, cr_ref, ckr_ref, cvr_ref, cwa_ref):
    C = CHUNK
    rows_blk = r_ref.shape[1]
    c = pl.program_id(1)

    @pl.when(c == 0)
    def _():
        for p in range(N_PAIRS):
            sp_ref[p] = jnp.concatenate([s0_ref[0, 2 * p], s0_ref[0, 2 * p + 1]], axis=1)
        for carry, first in ((cr_ref, pr_ref), (ckr_ref, pkr_ref), (cvr_ref, pvr_ref), (cwa_ref, pwa_ref)):
            carry[...] = first[0]

    def mixed(cur_ref, carry_ref, mu_ref):
        cur = cur_ref[0]
        first = lax.broadcasted_iota(jnp.int32, cur.shape, 0) == 0
        prev = jnp.where(first, carry_ref[...], pltpu.roll(cur, 1, 0))
        carry_ref[...] = cur[rows_blk - 1:rows_blk, :]
        return cur + (prev - cur) * mu_ref[...]

    r_blk = mixed(r_ref, cr_ref, mu_r_ref)
    kr_blk = mixed(kr_ref, ckr_ref, mu_kr_ref)
    v_blk = mixed(vr_ref, cvr_ref, mu_vr_ref)
    wa = mixed(wa_ref, cwa_ref, mu_wa_ref)
    nz = -(w0_ref[...] + _lora_dot(jnp.tanh(wa), wdh_ref, wdl_ref))
    softplus = jnp.maximum(nz, 0.0) + jnp.log1p(jnp.exp(-jnp.abs(nz)))
    lw_blk = -jnp.exp(-softplus - 0.5)
    icl_blk = jax.nn.sigmoid(a0_ref[...] + _lora_dot(wa, wih_ref, wil_ref))
    k_blk = kr_blk * (1.0 + (icl_blk - 1.0) * ka_ref[...])
    kkr_blk = kr_blk * kk_ref[...]

    row = lax.broadcasted_iota(jnp.int32, (PAIR, PAIR), 0)
    col = lax.broadcasted_iota(jnp.int32, (PAIR, PAIR), 1)
    tril = row >= col
    stril = row > col
    tril2 = jnp.concatenate([tril, tril], axis=1)
    eye = (row == col).astype(f32)
    lane_lo = lax.broadcasted_iota(jnp.int32, (C, PAIR), 1) < RW_HEAD
    pairs = range(N_PAIRS)
    sls = [slice(p * PAIR, (p + 1) * PAIR) for p in pairs]

    def bd(x):
        zero = jnp.zeros_like(x)
        return jnp.concatenate([jnp.where(lane_lo, x, zero), jnp.where(lane_lo, zero, x)], axis=0)

    def head_sums(x):
        lo_sum = jnp.sum(jnp.where(lane_lo, x, 0.0), axis=-1, keepdims=True)
        hi_sum = jnp.sum(jnp.where(lane_lo, 0.0, x), axis=-1, keepdims=True)
        return jnp.where(lane_lo, lo_sum, hi_sum)

    state = [sp_ref[p] for p in pairs]
    for j in range(rows_blk // C):
        rs = slice(j * C, (j + 1) * C)
        lw = lw_blk[rs]
        width = lw.shape[1]
        lw_a = lw.astype(bf16)
        rest = lw - lw_a.astype(f32)
        lw_b = rest.astype(bf16)
        lw_c = (rest - lw_b.astype(f32)).astype(bf16)
        g3 = jnp.dot(tril[:C, :C].astype(bf16), jnp.concatenate([lw_a, lw_b, lw_c], axis=1),
                     preferred_element_type=f32)
        g = g3[:, :width] + g3[:, width:2 * width] + g3[:, 2 * width:]
        e_g = jnp.exp(g)
        e_ng = jnp.exp(-g)
        e_gm = jnp.exp(g - lw)
        e_end = e_g[C - 1:C, :]

        ins = [[t[rs, sl] for t in (r_blk, k_blk, v_blk, kkr_blk, icl_blk)] for sl in sls]
        norms = [jnp.sqrt(head_sums(x[3] * x[3])) for x in ins]
        at, rt, bt, kt, vb = ([] for _ in range(5))
        for (r, k, v, kkr, icl), norm, sl in zip(ins, norms, sls):
            kk = kkr / jnp.maximum(norm, 1e-12)
            b = kk * icl
            at.append(bd((-kk * e_gm[:, sl]).astype(bf16)))
            rt.append(bd((r * e_g[:, sl]).astype(bf16)))
            bt.append(bd((b * e_ng[:, sl]).astype(bf16)))
            kt.append(bd((k * e_ng[:, sl]).astype(bf16)))
            vb.append(bd(v.astype(bf16)))

        gram = [_mmb(jnp.concatenate([at[p], rt[p]], axis=0), jnp.concatenate([bt[p], kt[p]], axis=0), _NT)
                for p in pairs]
        lmat = [jnp.where(stril, gm[:PAIR, :PAIR], 0.0) for gm in gram]
        mv = [_mmb(jnp.where(stril, gram[p][:PAIR, PAIR:], 0.0), vb[p]) for p in pairs]
        lower = [jnp.where(tril2, gm[PAIR:, :], 0.0).astype(bf16) for gm in gram]

        tinv = [eye + lm for lm in lmat]
        pw = [_mmb(lm, lm) for lm in lmat]
        for _ in range(4):
            z = [_mmb(jnp.concatenate([x.astype(bf16), t.astype(bf16)], axis=0), x)
                 for t, x in zip(tinv, pw)]
            pw = [zz[:PAIR] for zz in z]
            tinv = [t + zz[PAIR:] for t, zz in zip(tinv, z)]
        tinv = [t + _mmb(t, x) for t, x in zip(tinv, pw)]

        wx = [_mmb(tinv[p], jnp.concatenate([at[p], mv[p].astype(bf16)], axis=1)) for p in pairs]
        uy0 = [_mmb(jnp.concatenate([wx[p][:, :PAIR].astype(bf16), rt[p]], axis=0),
                    bd(state[p].astype(bf16)), _NT) for p in pairs]
        uv = [jnp.concatenate([(uy0[p][:PAIR] + wx[p][:, PAIR:]).astype(bf16), vb[p]], axis=0)
              for p in pairs]
        ys = [uy0[p][PAIR:] + _mmb(lower[p], uv[p]) for p in pairs]
        s_add = [_mmb(uv[p], jnp.concatenate([bt[p], kt[p]], axis=0), _TN) for p in pairs]
        state = [(state[p] + s_add[p][:RW_HEAD] + s_add[p][RW_HEAD:]) * e_end[:, sls[p]] for p in pairs]

        ys = [y[:C] + y[C:] for y in ys]
        mus = [head_sums(y) * (1.0 / RW_HEAD) for y in ys]
        ds = [y - mu for y, mu in zip(ys, mus)]
        var = [head_sums(d * d) * (1.0 / RW_HEAD) for d in ds]
        bonus = [head_sums(x[0] * x[1] * rk_ref[:, sl]) * x[2] for x, sl in zip(ins, sls)]
        for p in pairs:
            sl = sls[p]
            y_rw = ds[p] * lax.rsqrt(var[p] + GN_EPS) * lnw_ref[:, sl] + lnb_ref[:, sl] + bonus[p]
            y_ref[0, rs, sl] = (y_rw * _silu(gr_ref[0, rs, sl])).astype(y_ref.dtype)

    for p in pairs:
        sp_ref[p] = state[p]

    @pl.when(c == pl.num_programs(1) - 1)
    def _():
        for p in range(N_PAIRS):
            s_ref[0, 2 * p] = state[p][:, :RW_HEAD]
            s_ref[0, 2 * p + 1] = state[p][:, RW_HEAD:]


def _wkv_chunked(proj, prev, mu, w0, a0, k_k, k_a, lora, r_k, ln_w, ln_b, s0):
    B, T, _ = proj.shape
    W = RWKV_WIDTH
    rows = CHUNKS_PER_STEP * CHUNK
    col = lambda w, off: pl.BlockSpec((1, rows, w), lambda b, c: (b, c, off // w))
    first = lambda w: pl.BlockSpec((1, 1, w), lambda b, c: (b, 0, 0))
    vec = lambda w: pl.BlockSpec((1, w), lambda b, c: (0, 0))
    lora_spec = pl.BlockSpec((LORA_PAIR, W), lambda b, c: (0, 0))
    state = pl.BlockSpec((1, RW_HEADS, RW_HEAD, RW_HEAD), lambda b, c: (b, 0, 0, 0))
    return pl.pallas_call(
        _wkv_chunk_kernel,
        grid=(B, T // rows),
        in_specs=[col(W, P_R), col(W, P_KR), col(W, P_VR), col(LANES, P_WA), col(W, P_GR),
                  first(W), first(W), first(W), first(LANES),
                  vec(W), vec(W), vec(W), vec(LANES), vec(W), vec(W), vec(W), vec(W),
                  lora_spec, lora_spec, lora_spec, lora_spec, vec(W), vec(W), vec(W), state],
        out_specs=[col(W, 0), state],
        out_shape=[
            jax.ShapeDtypeStruct((B, T, W), bf16),
            jax.ShapeDtypeStruct((B, RW_HEADS, RW_HEAD, RW_HEAD), f32),
        ],
        scratch_shapes=[pltpu.VMEM((N_PAIRS, RW_HEAD, PAIR), f32),
                        pltpu.VMEM((1, W), f32), pltpu.VMEM((1, W), f32), pltpu.VMEM((1, W), f32),
                        pltpu.VMEM((1, LANES), f32)],
        compiler_params=_cparams(("parallel", "arbitrary")),
        name="wkv_chunks",
    )(proj, proj, proj, proj, proj, *prev, *mu, w0, a0, k_k, k_a, *lora,
      r_k.reshape(1, W), ln_w.reshape(1, W), ln_b.reshape(1, W), s0)


def _post_kernel(final, gated, att_ref, y_ref, *refs):
    if gated:
        y = y_ref[0]
    else:
        y = (y_ref[0] * _silu(refs[0][0])).astype(bf16)
        refs = refs[1:]
    x_ref, gate_ref, w_ref, fg_ref, o_ref = refs
    cat = jnp.concatenate([att_ref[0], y], axis=1)
    out = jnp.dot(cat, w_ref[...], preferred_element_type=f32)
    x = x_ref[0] + gate_ref[0] * out
    if final:
        ms = jnp.mean(x * x, axis=-1, keepdims=True)
        x = x * lax.rsqrt(ms + NORM_EPS) * fg_ref[...]
    o_ref[0] = x


def _post(att, y_rw, proj, x, gate, w_out_bf, final_g, final, tm):
    G, R, _ = x.shape
    W = RWKV_WIDTH
    gated = proj is None
    gate_in = [] if gated else [pl.BlockSpec((1, tm, W), lambda g, m: (g, m, P_GR // W))]
    gate_arg = [] if gated else [proj]
    return pl.pallas_call(
        functools.partial(_post_kernel, final, gated),
        grid=(G, R // tm),
        in_specs=[
            pl.BlockSpec((1, tm, ATT_WIDTH), lambda g, m: (g, m, 0)),
            pl.BlockSpec((1, tm, W), lambda g, m: (g, m, 0)),
            *gate_in,
            pl.BlockSpec((1, tm, D_MODEL), lambda g, m: (g, m, 0)),
            _mod_spec(gate, tm),
            pl.BlockSpec((D_MODEL, D_MODEL), lambda g, m: (0, 0)),
            pl.BlockSpec((1, D_MODEL), lambda g, m: (0, 0)),
        ],
        out_specs=pl.BlockSpec((1, tm, D_MODEL), lambda g, m: (g, m, 0)),
        out_shape=jax.ShapeDtypeStruct((G, R, D_MODEL), f32),
        compiler_params=_cparams(("parallel", "parallel")),
        name="post_proj",
    )(att, y_rw, *gate_arg, x, gate, w_out_bf, final_g.reshape(1, D_MODEL))


def _arrange_w_in(w):
    pad = jnp.zeros((w.shape[0], P_WIDTH - IN_WIDTH), w.dtype)
    parts = [w[:, Q_OFF:KA_OFF], w[:, GA_OFF:GR_OFF], w[:, GR_OFF:IN_WIDTH], w[:, R_OFF:WD_OFF],
             w[:, KA_OFF:R_OFF], w[:, WD_OFF:GA_OFF], pad]
    return jnp.concatenate(parts, axis=1).astype(bf16)


def _shift_cols(t):
    return jnp.concatenate([t[..., P_R:P_KA], t[..., P_WA:P_WA + LORA_PAIR]], axis=-1)


def kernel(x_prompt, x_sample, cache_k, cache_v, state_wkv, state_shift, c_prompt, c_sample,
           norm_g, w_ada, b_ada, w_in, mu_shift, w0, w_decay, a0, w_iclr, k_k, k_a, r_k,
           ln_w, ln_b, sinks, w_out, final_g):
    Bp, Tp = x_prompt.shape[0], x_prompt.shape[1]
    Bd = x_sample.shape[0]
    W = RWKV_WIDTH

    n_c = Bp + Bd
    c_rows = -(-n_c // 16) * 16
    c_all = jnp.concatenate([c_prompt, c_sample, jnp.zeros((c_rows - n_c, D_MODEL), f32)], axis=0)
    mod = _ada(c_all, w_ada, b_ada)

    tab_p = _rope_tables(jnp.arange(Tp, dtype=jnp.int32))
    tab_s = _rope_tables(jnp.full((Bd,), PAST_LEN, jnp.int32))

    hp = x_prompt
    hs = x_sample.reshape(1, Bd, D_MODEL)
    s0_p = jnp.zeros((Bp, RW_HEADS, RW_HEAD, RW_HEAD), f32)
    shift0_p = [jnp.zeros((Bp, 1, w), f32) for w in (W, W, W, LORA_PAIR)]
    ck_all = cache_k.reshape(DEPTH, Bd, WINDOW, KV_WIDTH)
    cv_all = cache_v.reshape(DEPTH, Bd, WINDOW, KV_WIDTH)
    new_state_s = jnp.zeros(state_wkv.shape, f32)
    new_cache_s = [jnp.zeros(ck_all.shape, f32), jnp.zeros(cv_all.shape, f32)]
    outs = {k: [] for k in ("kp", "vp", "sp", "shp", "shs")}
    for l in range(DEPTH):
        final = l == DEPTH - 1
        w_bf = _arrange_w_in(w_in[l])
        w_out_bf = w_out[l].astype(bf16)
        mu_l = mu_shift[l]
        mu = [mu_l[0:W].reshape(1, W), mu_l[W:2 * W].reshape(1, W), mu_l[2 * W:3 * W].reshape(1, W),
              mu_l[3 * W:].reshape(1, LORA_PAIR)]
        vecs = [t[l].reshape(1, W) for t in (w0, a0, k_k, k_a)]
        wd_pad = jnp.concatenate([w_decay[l], jnp.zeros((ICLR_LORA, W), f32)], axis=0)
        wi_pad = jnp.concatenate([jnp.zeros((DECAY_LORA, W), f32), w_iclr[l]], axis=0)
        shift_p, scale_p, gate_p = (mod[l, :Bp, i * D_MODEL:(i + 1) * D_MODEL].reshape(Bp, 1, D_MODEL)
                                    for i in range(3))
        shift_s, scale_s, gate_s = (mod[l, Bp:n_c, i * D_MODEL:(i + 1) * D_MODEL].reshape(1, Bd, D_MODEL)
                                    for i in range(3))

        proj = _norm_proj(hp, norm_g[l], scale_p, shift_p, w_bf, tab_p, tm=1024)
        att = _attn_prompt(proj, sinks[l])
        lora = [piece for wp in (wd_pad, wi_pad) for piece in _split_bf16(wp)]
        y_rw, s_t = _wkv_chunked(proj, shift0_p, mu, *vecs, lora, r_k[l], ln_w[l], ln_b[l], s0_p)
        hp = _post(att, y_rw, None, hp, gate_p, w_out_bf, final_g, final, tm=512)
        tail = proj[:, Tp - WINDOW:]
        outs["kp"].append(tail[..., P_KA:P_KA + KV_WIDTH].reshape(Bp, WINDOW, N_KV_HEADS, HEAD_DIM))
        outs["vp"].append(tail[..., P_VA:P_VA + KV_WIDTH].reshape(Bp, WINDOW, N_KV_HEADS, HEAD_DIM))
        outs["sp"].append(s_t)
        outs["shp"].append(_shift_cols(proj[:, Tp - 1]))

        proj = _norm_proj(hs, norm_g[l], scale_s, shift_s, w_bf, tab_s, tm=Bd)
        att, *new_cache_s = _attn_sample(proj, ck_all, cv_all, l, new_cache_s, sinks[l])
        sh = state_shift[l]
        prev = [sh[None, :, 0:W], sh[None, :, W:2 * W], sh[None, :, 2 * W:3 * W], sh[None, :, 3 * W:]]
        prep = _rwkv_prep(proj, prev, mu, *vecs, wd_pad, wi_pad, tm=Bd)
        prep = [t[:, None] for t in prep]
        y_rw, new_state_s = _wkv(*prep, r_k[l], ln_w[l], ln_b[l], state_wkv, l, new_state_s, tc=1)
        hs = _post(att, y_rw.reshape(1, Bd, W), proj, hs, gate_s, w_out_bf, final_g, final, tm=Bd)
        outs["shs"].append(_shift_cols(proj[0]))

    st = lambda k: jnp.stack(outs[k])
    return (hp, hs.reshape(Bd, 1, D_MODEL), st("kp"), st("vp"), st("sp"), st("shp"),
            *(t.reshape(cache_k.shape) for t in new_cache_s), new_state_s, st("shs"))
```

```python
import functools

import jax
import jax.numpy as jnp
from jax import lax
from jax.experimental import pallas as pl
from jax.experimental.pallas import tpu as pltpu

f32 = jnp.float32
bf16 = jnp.bfloat16

D_MODEL = 2048
DEPTH = 2
PAST_LEN = 16384
ATT_WIDTH = 1024
RWKV_WIDTH = 1024
HEAD_DIM = 64
N_Q_HEADS = 16
N_KV_HEADS = 4
GQA_GROUP = 4
KV_WIDTH = 256
WINDOW = 128
ROT_DIM = 16
ROPE_THETA = 500000.0
RW_HEAD = 64
RW_HEADS = 16
DECAY_LORA = 64
ICLR_LORA = 64
LORA_PAIR = DECAY_LORA + ICLR_LORA
NORM_EPS = 1e-5
GN_EPS = 64e-5
NEG_BIG = -1e30

Q_OFF = 0
KA_OFF = Q_OFF + ATT_WIDTH
VA_OFF = KA_OFF + KV_WIDTH
R_OFF = VA_OFF + KV_WIDTH
KR_OFF = R_OFF + RWKV_WIDTH
VR_OFF = KR_OFF + RWKV_WIDTH
WD_OFF = VR_OFF + RWKV_WIDTH
AD_OFF = WD_OFF + DECAY_LORA
GA_OFF = AD_OFF + ICLR_LORA
GR_OFF = GA_OFF + ATT_WIDTH
IN_WIDTH = GR_OFF + RWKV_WIDTH
SHIFT_DIM = GA_OFF - R_OFF

LANES = 128
P_Q = 0
P_GA = 1024
P_GR = 2048
P_R = 3072
P_KR = 4096
P_VR = 5120
P_KA = 6144
P_VA = 6400
P_WA = 6656
P_WIDTH = 7168
PROJ_TN = 1792

VMEM_LIMIT = 56 * 1024 * 1024


_NN = (((1,), (0,)), ((), ()))
_NT = (((1,), (1,)), ((), ()))
_TN = (((0,), (0,)), ((), ()))


def _silu(x):
    return x * jax.nn.sigmoid(x)


def _cparams(sem):
    return pltpu.CompilerParams(dimension_semantics=sem, vmem_limit_bytes=VMEM_LIMIT)


def _split_bf16(x):
    hi = x.astype(bf16)
    return hi, (x - hi.astype(f32)).astype(bf16)


def _ada_kernel(c_ref, w_ref, b_ref, o_ref):
    ch, cl = _split_bf16(_silu(c_ref[...]))
    wh, wl = _split_bf16(w_ref[0])
    rows = ch.shape[0]
    both = jnp.dot(jnp.concatenate([ch, cl], axis=0), wh, preferred_element_type=f32)
    o_ref[0] = both[:rows] + both[rows:] + jnp.dot(ch, wl, preferred_element_type=f32) + b_ref[0]


def _ada(c_all, w_ada, b_ada):
    rows = c_all.shape[0]
    tn = 1536
    n_out = w_ada.shape[2]
    return pl.pallas_call(
        _ada_kernel,
        grid=(DEPTH, n_out // tn),
        in_specs=[
            pl.BlockSpec((rows, D_MODEL), lambda l, n: (0, 0)),
            pl.BlockSpec((1, D_MODEL, tn), lambda l, n: (l, 0, n)),
            pl.BlockSpec((1, 1, tn), lambda l, n: (l, 0, n)),
        ],
        out_specs=pl.BlockSpec((1, rows, tn), lambda l, n: (l, 0, n)),
        out_shape=jax.ShapeDtypeStruct((DEPTH, rows, n_out), f32),
        compiler_params=_cparams(("parallel", "parallel")),
        name="ada_mod",
    )(c_all, w_ada, b_ada.reshape(DEPTH, 1, n_out))


def _rope(x, tab):
    w = x.shape[1]
    reps = w // LANES
    cosf, up, dn = (jnp.concatenate([tab[i]] * reps, axis=1) for i in range(3))
    half = ROT_DIM // 2
    return x * cosf + pltpu.roll(x, w - half, 1) * up + pltpu.roll(x, half, 1) * dn


def _norm_proj_kernel(x_ref, g_ref, scale_ref, shift_ref, w_ref, tab_ref, o_ref, h_ref):
    n = pl.program_id(2)

    @pl.when(n == 0)
    def _():
        x = x_ref[0]
        ms = jnp.mean(x * x, axis=-1, keepdims=True)
        y = x * lax.rsqrt(ms + NORM_EPS) * g_ref[...]
        h_ref[...] = (y * (1.0 + scale_ref[0]) + shift_ref[0]).astype(bf16)

    tm, tn = o_ref.shape[1], o_ref.shape[2]
    q_tile, q_lo = divmod(P_Q, tn)
    k_tile, k_lo = divmod(P_KA, tn)
    assert q_lo + ATT_WIDTH <= tn and k_lo + KV_WIDTH <= tn and q_tile != k_tile
    rc = min(tm, 256)

    def rotated_tile(lo, width, scale):
        for i in range(tm // rc):
            rs = slice(i * rc, (i + 1) * rc)
            res = jnp.dot(h_ref[rs, :], w_ref[...], preferred_element_type=f32)
            parts = [res[:, :lo]] if lo else []
            parts.append(_rope(res[:, lo:lo + width], tab_ref[:, rs, :]) * scale)
            if lo + width < tn:
                parts.append(res[:, lo + width:])
            o_ref[0, rs, :] = jnp.concatenate(parts, axis=1) if len(parts) > 1 else parts[0]

    @pl.when(n == q_tile)
    def _():
        rotated_tile(q_lo, ATT_WIDTH, HEAD_DIM ** -0.5)

    @pl.when(n == k_tile)
    def _():
        rotated_tile(k_lo, KV_WIDTH, 1.0)

    @pl.when((n != q_tile) & (n != k_tile))
    def _():
        o_ref[0] = jnp.dot(h_ref[...], w_ref[...], preferred_element_type=f32)


def _mod_spec(mod, tm):
    if mod.shape[1] == 1:
        return pl.BlockSpec((1, 1, D_MODEL), lambda g, m, *_: (g, 0, 0))
    return pl.BlockSpec((1, tm, D_MODEL), lambda g, m, *_: (g, m, 0))


def _norm_proj(x, norm_g, scale, shift, w_bf, tab, tm):
    G, R, _ = x.shape
    return pl.pallas_call(
        _norm_proj_kernel,
        grid=(G, R // tm, P_WIDTH // PROJ_TN),
        in_specs=[
            pl.BlockSpec((1, tm, D_MODEL), lambda g, m, n: (g, m, 0)),
            pl.BlockSpec((1, D_MODEL), lambda g, m, n: (0, 0)),
            _mod_spec(scale, tm),
            _mod_spec(shift, tm),
            pl.BlockSpec((D_MODEL, PROJ_TN), lambda g, m, n: (0, n)),
            pl.BlockSpec((3, tm, LANES), lambda g, m, n: (0, m, 0)),
        ],
        out_specs=pl.BlockSpec((1, tm, PROJ_TN), lambda g, m, n: (g, m, n)),
        out_shape=jax.ShapeDtypeStruct((G, R, P_WIDTH), f32),
        scratch_shapes=[pltpu.VMEM((tm, D_MODEL), bf16)],
        compiler_params=_cparams(("parallel", "parallel", "arbitrary")),
        name="norm_proj",
    )(x, norm_g.reshape(1, D_MODEL), scale, shift, w_bf, tab)


def _rope_tables(pos):
    half = ROT_DIM // 2
    inv_freq = ROPE_THETA ** (-jnp.arange(half, dtype=f32) * (2.0 / ROT_DIM))
    ang = pos.astype(f32)[:, None] * inv_freq[None, :]
    cos, sin = jnp.cos(ang), jnp.sin(ang)
    t = pos.shape[0]
    z8 = jnp.zeros((t, half), f32)
    rest = HEAD_DIM - ROT_DIM
    cos64 = jnp.concatenate([cos, cos, jnp.ones((t, rest), f32)], axis=1)
    up64 = jnp.concatenate([-sin, z8, jnp.zeros((t, rest), f32)], axis=1)
    dn64 = jnp.concatenate([z8, sin, jnp.zeros((t, rest), f32)], axis=1)
    rep = LANES // HEAD_DIM
    return jnp.stack([jnp.tile(a, (1, rep)) for a in (cos64, up64, dn64)])


ATTN_BLOCKS = 4


def _attn_prompt_kernel(sinks_ref, q_ref, kc_ref, kp_ref, vc_ref, vp_ref, ga_ref, o_ref):
    n = pl.program_id(1)
    wn = WINDOW
    half = HEAD_DIM
    k_t_all = jnp.concatenate([kp_ref[0], kc_ref[0]], axis=0).T.astype(bf16)
    v_all = jnp.concatenate([vp_ref[0], vc_ref[0]], axis=0)

    qi = lax.broadcasted_iota(jnp.int32, (2 * wn, 2 * wn), 0) & (wn - 1)
    kj = lax.broadcasted_iota(jnp.int32, (2 * wn, 2 * wn), 1)
    rel = wn + qi - kj
    band = (rel >= 0) & (rel <= wn)
    top = lax.broadcasted_iota(jnp.int32, (2 * wn, 1), 0) < wn
    lo = lax.broadcasted_iota(jnp.int32, (2 * wn, LANES), 1) < half
    zeros_k = jnp.zeros((half, 2 * wn), bf16)
    ones_lo = jnp.where(lo, 1.0, 0.0).astype(bf16)
    ones_hi = jnp.where(lo, 0.0, 1.0).astype(bf16)

    for sb in range(q_ref.shape[1] // wn):
        rows = slice(sb * wn, (sb + 1) * wn)
        q = q_ref[0, rows, :].astype(bf16)
        ga = ga_ref[0, rows, :]
        k_t = k_t_all[:, sb * wn:(sb + 2) * wn]
        vcat = v_all[sb * wn:(sb + 2) * wn]
        mask = band & ((kj >= wn) | (n > 0)) if sb == 0 else band
        for j in range(N_KV_HEADS // 2):
            vblk = vcat[:, j * LANES:(j + 1) * LANES]
            vswap = pltpu.roll(vblk, half, 1)
            for g in (2 * j, 2 * j + 1):
                own, other = (vblk, vswap) if g % 2 == 0 else (vswap, vblk)
                v_lo = jnp.where(lo, own, 0.0).astype(bf16)
                v_hi = jnp.where(lo, 0.0, other).astype(bf16)
                rhs_pv = jnp.concatenate([jnp.concatenate([v_lo, ones_lo], axis=1),
                                          jnp.concatenate([v_hi, ones_hi], axis=1)], axis=0)
                kg = k_t[g * half:(g + 1) * half, :]
                rhs_qk = jnp.concatenate([jnp.concatenate([kg, zeros_k], axis=0),
                                          jnp.concatenate([zeros_k, kg], axis=0)], axis=1)
                b0, b1 = 2 * g, 2 * g + 1
                qg = jnp.concatenate([q[:, b0 * LANES:(b0 + 1) * LANES], q[:, b1 * LANES:(b1 + 1) * LANES]],
                                     axis=0)
                s_all = jnp.dot(qg, rhs_qk, preferred_element_type=f32)
                ps, es = [], []
                for hh in range(2):
                    s = jnp.where(mask, s_all[:, hh * 2 * wn:(hh + 1) * 2 * wn], NEG_BIG)
                    sink = jnp.where(top, sinks_ref[2 * b0 + hh], sinks_ref[2 * b1 + hh])
                    m = jnp.maximum(jnp.max(s, axis=-1, keepdims=True), sink)
                    ps.append(jnp.exp(s - m).astype(bf16))
                    es.append(jnp.exp(sink - m))
                res = jnp.dot(jnp.concatenate(ps, axis=1), rhs_pv, preferred_element_type=f32)
                out = res[:, :LANES] / (res[:, LANES:] + jnp.where(lo, es[0], es[1]))
                for i, blk in enumerate((b0, b1)):
                    sl = slice(blk * LANES, (blk + 1) * LANES)
                    o_ref[0, rows, sl] = (out[i * wn:(i + 1) * wn] * _silu(ga[:, sl])).astype(o_ref.dtype)


def _attn_prompt(proj, sinks):
    B, T, _ = proj.shape
    rows = ATTN_BLOCKS * WINDOW
    kvb = KV_WIDTH
    prev = lambda b, n: jnp.maximum(n * ATTN_BLOCKS - 1, 0)
    return pl.pallas_call(
        _attn_prompt_kernel,
        grid=(B, T // rows),
        in_specs=[
            pl.BlockSpec(memory_space=pltpu.SMEM),
            pl.BlockSpec((1, rows, ATT_WIDTH), lambda b, n: (b, n, P_Q // ATT_WIDTH)),
            pl.BlockSpec((1, rows, kvb), lambda b, n: (b, n, P_KA // kvb)),
            pl.BlockSpec((1, WINDOW, kvb), lambda b, n: (b, prev(b, n), P_KA // kvb)),
            pl.BlockSpec((1, rows, kvb), lambda b, n: (b, n, P_VA // kvb)),
            pl.BlockSpec((1, WINDOW, kvb), lambda b, n: (b, prev(b, n), P_VA // kvb)),
            pl.BlockSpec((1, rows, ATT_WIDTH), lambda b, n: (b, n, P_GA // ATT_WIDTH)),
        ],
        out_specs=pl.BlockSpec((1, rows, ATT_WIDTH), lambda b, n: (b, n, 0)),
        out_shape=jax.ShapeDtypeStruct((B, T, ATT_WIDTH), bf16),
        compiler_params=_cparams(("parallel", "arbitrary")),
        name="attn_prompt",
    )(sinks, proj, proj, proj, proj, proj, proj)


SAMPLE_ROWS = 16


def _attn_sample_kernel(sinks_ref, q_ref, kn_ref, vn_ref, ga_ref, ck_ref, cv_ref, *refs):
    o_ref, nk_ref, nv_ref = refs[-3:]
    rb = q_ref.shape[1]
    q_all, kn_all, vn_all = q_ref[0], kn_ref[0], vn_ref[0]
    work = [(r, g) for r in range(rb) for g in range(N_KV_HEADS)]
    span = lambda g: slice(g * HEAD_DIM, (g + 1) * HEAD_DIM)
    ckb = [ck_ref[0, r].astype(bf16) for r in range(rb)]
    cvb = [cv_ref[0, r].astype(bf16) for r in range(rb)]
    qg = {(r, g): jnp.concatenate(
        [q_all[r:r + 1, (g * GQA_GROUP + i) * HEAD_DIM:(g * GQA_GROUP + i + 1) * HEAD_DIM]
         for i in range(GQA_GROUP)], axis=0) for r, g in work}
    s = {rg: lax.dot_general(qg[rg].astype(bf16), ckb[rg[0]][:, span(rg[1])], _NT,
                             preferred_element_type=f32) for rg in work}
    sink = {g: sinks_ref[g * GQA_GROUP:(g + 1) * GQA_GROUP, :] for g in range(N_KV_HEADS)}
    s_new = {(r, g): jnp.sum(qg[r, g] * kn_all[r:r + 1, span(g)], axis=-1, keepdims=True) for r, g in work}
    m = {rg: jnp.maximum(jnp.maximum(jnp.max(s[rg], axis=-1, keepdims=True), s_new[rg]), sink[rg[1]])
         for rg in work}
    p = {rg: jnp.exp(s[rg] - m[rg]) for rg in work}
    p_new = {rg: jnp.exp(s_new[rg] - m[rg]) for rg in work}
    den = {rg: jnp.sum(p[rg], axis=-1, keepdims=True) + p_new[rg] + jnp.exp(sink[rg[1]] - m[rg]) for rg in work}
    pv = {(r, g): jnp.dot(p[r, g].astype(bf16), cvb[r][:, span(g)], preferred_element_type=f32) for r, g in work}
    outs = {}
    for r, g in work:
        o = (pv[r, g] + p_new[r, g] * vn_all[r:r + 1, span(g)]) / den[r, g]
        outs[r, g] = [o[i:i + 1, :] for i in range(GQA_GROUP)]
    att = jnp.concatenate(
        [jnp.concatenate([h for g in range(N_KV_HEADS) for h in outs[r, g]], axis=1) for r in range(rb)],
        axis=0)
    o_ref[0] = (att * _silu(ga_ref[0])).astype(o_ref.dtype)
    last = lax.broadcasted_iota(jnp.int32, (WINDOW, KV_WIDTH), 0) == WINDOW - 1
    for r in range(rb):
        nk_ref[0, r] = jnp.where(last, kn_all[r:r + 1], pltpu.roll(ck_ref[0, r], WINDOW - 1, 0))
        nv_ref[0, r] = jnp.where(last, vn_all[r:r + 1], pltpu.roll(cv_ref[0, r], WINDOW - 1, 0))


def _attn_sample(proj, ck_all, cv_all, layer, stacked, sinks):
    Bd = proj.shape[1]
    rb = SAMPLE_ROWS
    col = lambda w, off: pl.BlockSpec((1, rb, w), lambda i: (0, i, off // w))
    cache = pl.BlockSpec((1, rb, WINDOW, KV_WIDTH), lambda i: (layer, i, 0, 0))
    args = [sinks.reshape(N_Q_HEADS, 1), proj, proj, proj, proj, ck_all, cv_all, *stacked]
    hbm = pl.BlockSpec(memory_space=pl.ANY)
    return pl.pallas_call(
        _attn_sample_kernel,
        grid=(Bd // rb,),
        in_specs=[
            pl.BlockSpec((N_Q_HEADS, 1), lambda i: (0, 0)),
            col(ATT_WIDTH, P_Q), col(KV_WIDTH, P_KA), col(KV_WIDTH, P_VA), col(ATT_WIDTH, P_GA),
            cache, cache, hbm, hbm,
        ],
        out_specs=[pl.BlockSpec((1, rb, ATT_WIDTH), lambda i: (0, i, 0)), cache, cache],
        out_shape=[
            jax.ShapeDtypeStruct((1, Bd, ATT_WIDTH), bf16),
            jax.ShapeDtypeStruct(ck_all.shape, f32),
            jax.ShapeDtypeStruct(cv_all.shape, f32),
        ],
        input_output_aliases={len(args) - 2: 1, len(args) - 1: 2},
        compiler_params=_cparams(("parallel",)),
        name="attn_sample",
    )(*args)


def _rwkv_prep_kernel(r_ref, kr_ref, vr_ref, wa_ref, pr_ref, pkr_ref, pvr_ref, pwa_ref,
                      mu_r_ref, mu_kr_ref, mu_vr_ref, mu_wa_ref, w0_ref, a0_ref, kk_ref, ka_ref,
                      wd_ref, wi_ref, ro_ref, wo_ref, ko_ref, vo_ref, kko_ref, ao_ref):
    def mixed(cur_ref, prev_ref, mu_ref):
        cur = cur_ref[0]
        return cur + (prev_ref[0] - cur) * mu_ref[...]

    r = mixed(r_ref, pr_ref, mu_r_ref)
    kr = mixed(kr_ref, pkr_ref, mu_kr_ref)
    vr = mixed(vr_ref, pvr_ref, mu_vr_ref)
    wa = mixed(wa_ref, pwa_ref, mu_wa_ref)
    hi = lax.Precision.HIGHEST
    z = w0_ref[...] + jnp.dot(jnp.tanh(wa), wd_ref[...], precision=hi, preferred_element_type=f32)
    nz = -z
    softplus = jnp.maximum(nz, 0.0) + jnp.log1p(jnp.exp(-jnp.abs(nz)))
    w_log = -softplus - 0.5
    a = jax.nn.sigmoid(a0_ref[...] + jnp.dot(wa, wi_ref[...], precision=hi, preferred_element_type=f32))
    outs = (r, -jnp.exp(w_log), kr * (1.0 + (a - 1.0) * ka_ref[...]), vr, kr * kk_ref[...], a)
    for o_ref, val in zip((ro_ref, wo_ref, ko_ref, vo_ref, kko_ref, ao_ref), outs):
        for h in range(RW_HEADS):
            o_ref[:, h, :] = val[:, h * RW_HEAD:(h + 1) * RW_HEAD]


def _rwkv_prep(proj, prev, mu, w0, a0, k_k, k_a, wd_pad, wi_pad, tm):
    G, R, _ = proj.shape
    assert G == 1
    W = RWKV_WIDTH
    cur_specs = [
        pl.BlockSpec((1, tm, W), lambda g, m: (g, m, P_R // W)),
        pl.BlockSpec((1, tm, W), lambda g, m: (g, m, P_KR // W)),
        pl.BlockSpec((1, tm, W), lambda g, m: (g, m, P_VR // W)),
        pl.BlockSpec((1, tm, LANES), lambda g, m: (g, m, P_WA // LANES)),
    ]
    prev_specs = [
        pl.BlockSpec((1, tm, W), lambda g, m: (g, m, 0)),
        pl.BlockSpec((1, tm, W), lambda g, m: (g, m, 0)),
        pl.BlockSpec((1, tm, W), lambda g, m: (g, m, 0)),
        pl.BlockSpec((1, tm, LANES), lambda g, m: (g, m, 0)),
    ]
    prev_args = list(prev)
    vec = lambda w: pl.BlockSpec((1, w), lambda g, m: (0, 0))
    lora = pl.BlockSpec((LORA_PAIR, W), lambda g, m: (0, 0))
    out_spec = pl.BlockSpec((tm, RW_HEADS, RW_HEAD), lambda g, m: (m, 0, 0))
    return pl.pallas_call(
        _rwkv_prep_kernel,
        grid=(G, R // tm),
        in_specs=cur_specs + prev_specs + [vec(W), vec(W), vec(W), vec(LANES),
                                           vec(W), vec(W), vec(W), vec(W), lora, lora],
        out_specs=[out_spec] * 6,
        out_shape=[jax.ShapeDtypeStruct((R, RW_HEADS, RW_HEAD), f32)] * 6,
        compiler_params=_cparams(("parallel", "arbitrary")),
        name="rwkv_prep",
    )(proj, proj, proj, proj, *prev_args, *mu, w0, a0, k_k, k_a, wd_pad, wi_pad)


def _wkv_kernel(tc, r_ref, w_ref, k_ref, v_ref, kk_ref, a_ref, rk_ref, lnw_ref, lnb_ref, s0_ref,
                *refs):
    y_ref, s_ref = refs[-2:]

    @pl.when(pl.program_id(1) == 0)
    def _():
        s_ref[...] = s0_ref[...]

    eye = (lax.broadcasted_iota(jnp.int32, (RW_HEAD, RW_HEAD), 0)
           == lax.broadcasted_iota(jnp.int32, (RW_HEAD, RW_HEAD), 1)).astype(f32)
    rk = rk_ref[...]
    lnw = lnw_ref[...]
    lnb = lnb_ref[...]

    def step(t, carry):
        r, lw, k, v, kkr, a = (ref[0, t] for ref in (r_ref, w_ref, k_ref, v_ref, kk_ref, a_ref))
        w = jnp.exp(lw)
        norm = jnp.sqrt(jnp.sum(kkr * kkr, axis=-1, keepdims=True))
        kk = kkr / jnp.maximum(norm, 1e-12)
        b = kk * a
        bonus = jnp.sum(r * k * rk, axis=-1, keepdims=True) * v
        heads = range(RW_HEADS)
        row = lambda x, h: x[h:h + 1]
        s_in = [s_ref[0, 0, h] for h in heads]
        sa = [jnp.sum(s_in[h] * (-row(kk, h)), axis=-1, keepdims=True) for h in heads]
        v_col = [jnp.sum(eye * row(v, h), axis=-1, keepdims=True) for h in heads]
        s_out = [s_in[h] * row(w, h) + sa[h] * row(b, h) + v_col[h] * row(k, h) for h in heads]
        for h in heads:
            s_ref[0, 0, h] = s_out[h]
        ys = [jnp.sum(s_out[h] * row(r, h), axis=-1, keepdims=True) for h in heads]
        mus = [jnp.mean(y, axis=0, keepdims=True) for y in ys]
        var = [jnp.mean(jnp.square(y - mu), axis=0, keepdims=True) for y, mu in zip(ys, mus)]
        rows = [jnp.sum((ys[h] - mus[h]) * lax.rsqrt(var[h] + GN_EPS) * eye, axis=0, keepdims=True)
                for h in heads]
        y_ref[0, t] = jnp.concatenate(rows, axis=0) * lnw + lnb + bonus
        return carry

    lax.fori_loop(0, tc, step, 0)


def _wkv(r, w, k, v, kk, a, r_k, ln_w, ln_b, s0_all, layer, stacked, tc):
    B, T = r.shape[0], r.shape[1]
    hd = (RW_HEADS, RW_HEAD)
    seq = pl.BlockSpec((1, tc, *hd), lambda b, c: (b, c, 0, 0))
    par = pl.BlockSpec(hd, lambda b, c: (0, 0))
    state = pl.BlockSpec((1, 1, RW_HEADS, RW_HEAD, RW_HEAD), lambda b, c: (layer, b, 0, 0, 0))
    args = [r, w, k, v, kk, a, r_k, ln_w.reshape(hd), ln_b.reshape(hd), s0_all, stacked]
    y, s_new = pl.pallas_call(
        functools.partial(_wkv_kernel, tc),
        grid=(B, T // tc),
        in_specs=[seq] * 6 + [par] * 3 + [state, pl.BlockSpec(memory_space=pl.ANY)],
        out_specs=[seq, state],
        out_shape=[
            jax.ShapeDtypeStruct((B, T, *hd), f32),
            jax.ShapeDtypeStruct(s0_all.shape, f32),
        ],
        input_output_aliases={len(args) - 1: 1},
        compiler_params=_cparams(("parallel", "arbitrary")),
        name="wkv_steps",
    )(*args)
    return y.reshape(B, T, RWKV_WIDTH), s_new


CHUNK = 64
PAIR = 2 * RW_HEAD
N_PAIRS = RW_HEADS // 2


def _mmb(a, b, dims=_NN):
    return lax.dot_general(a.astype(bf16), b.astype(bf16), dims, preferred_element_type=f32)


def _lora_dot(x, wh_ref, wl_ref):
    xh, xl = _split_bf16(x)
    rows = x.shape[0]
    both = jnp.dot(jnp.concatenate([xh, xl], axis=0), wh_ref[...], preferred_element_type=f32)
    return both[:rows] + both[rows:] + jnp.dot(xh, wl_ref[...], preferred_element_type=f32)


CHUNKS_PER_STEP = 4


def _wkv_chunk_kernel(r_ref, kr_ref, vr_ref, wa_ref, gr_ref, pr_ref, pkr_ref, pvr_ref, pwa_ref,
                      mu_r_ref, mu_kr_ref, mu_vr_ref, mu_wa_ref, w0_ref, a0_ref, kk_ref, ka_ref,
                      wdh_ref, wdl_ref, wih_ref, wil_ref, rk_ref, lnw_ref, lnb_ref, s0_ref,
                      y_ref, s_ref, sp_ref, cr_ref, ckr_ref, cvr_ref, cwa_ref):
    C = CHUNK
    rows_blk = r_ref.shape[1]
    c = pl.program_id(1)

    @pl.when(c == 0)
    def _():
        for p in range(N_PAIRS):
            sp_ref[p] = jnp.concatenate([s0_ref[0, 2 * p], s0_ref[0, 2 * p + 1]], axis=1)
        for carry, first in ((cr_ref, pr_ref), (ckr_ref, pkr_ref), (cvr_ref, pvr_ref), (cwa_ref, pwa_ref)):
            carry[...] = first[0]

    def mixed(cur_ref, carry_ref, mu_ref):
        cur = cur_ref[0]
        first = lax.broadcasted_iota(jnp.int32, cur.shape, 0) == 0
        prev = jnp.where(first, carry_ref[...], pltpu.roll(cur, 1, 0))
        carry_ref[...] = cur[rows_blk - 1:rows_blk, :]
        return cur + (prev - cur) * mu_ref[...]

    r_blk = mixed(r_ref, cr_ref, mu_r_ref)
    kr_blk = mixed(kr_ref, ckr_ref, mu_kr_ref)
    v_blk = mixed(vr_ref, cvr_ref, mu_vr_ref)
    wa = mixed(wa_ref, cwa_ref, mu_wa_ref)
    nz = -(w0_ref[...] + _lora_dot(jnp.tanh(wa), wdh_ref, wdl_ref))
    softplus = jnp.maximum(nz, 0.0) + jnp.log1p(jnp.exp(-jnp.abs(nz)))
    lw_blk = -jnp.exp(-softplus - 0.5)
    icl_blk = jax.nn.sigmoid(a0_ref[...] + _lora_dot(wa, wih_ref, wil_ref))
    k_blk = kr_blk * (1.0 + (icl_blk - 1.0) * ka_ref[...])
    kkr_blk = kr_blk * kk_ref[...]

    row = lax.broadcasted_iota(jnp.int32, (PAIR, PAIR), 0)
    col = lax.broadcasted_iota(jnp.int32, (PAIR, PAIR), 1)
    tril = row >= col
    stril = row > col
    tril2 = jnp.concatenate([tril, tril], axis=1)
    eye = (row == col).astype(f32)
    lane_lo = lax.broadcasted_iota(jnp.int32, (C, PAIR), 1) < RW_HEAD
    pairs = range(N_PAIRS)
    sls = [slice(p * PAIR, (p + 1) * PAIR) for p in pairs]

    def bd(x):
        zero = jnp.zeros_like(x)
        return jnp.concatenate([jnp.where(lane_lo, x, zero), jnp.where(lane_lo, zero, x)], axis=0)

    def head_sums(x):
        lo_sum = jnp.sum(jnp.where(lane_lo, x, 0.0), axis=-1, keepdims=True)
        hi_sum = jnp.sum(jnp.where(lane_lo, 0.0, x), axis=-1, keepdims=True)
        return jnp.where(lane_lo, lo_sum, hi_sum)

    state = [sp_ref[p] for p in pairs]
    for j in range(rows_blk // C):
        rs = slice(j * C, (j + 1) * C)
        lw = lw_blk[rs]
        width = lw.shape[1]
        lw_a = lw.astype(bf16)
        rest = lw - lw_a.astype(f32)
        lw_b = rest.astype(bf16)
        lw_c = (rest - lw_b.astype(f32)).astype(bf16)
        g3 = jnp.dot(tril[:C, :C].astype(bf16), jnp.concatenate([lw_a, lw_b, lw_c], axis=1),
                     preferred_element_type=f32)
        g = g3[:, :width] + g3[:, width:2 * width] + g3[:, 2 * width:]
        e_g = jnp.exp(g)
        e_ng = jnp.exp(-g)
        e_gm = jnp.exp(g - lw)
        e_end = e_g[C - 1:C, :]

        ins = [[t[rs, sl] for t in (r_blk, k_blk, v_blk, kkr_blk, icl_blk)] for sl in sls]
        norms = [jnp.sqrt(head_sums(x[3] * x[3])) for x in ins]
        at, rt, bt, kt, vb = ([] for _ in range(5))
        for (r, k, v, kkr, icl), norm, sl in zip(ins, norms, sls):
            kk = kkr / jnp.maximum(norm, 1e-12)
            b = kk * icl
            at.append(bd((-kk * e_gm[:, sl]).astype(bf16)))
            rt.append(bd((r * e_g[:, sl]).astype(bf16)))
            bt.append(bd((b * e_ng[:, sl]).astype(bf16)))
            kt.append(bd((k * e_ng[:, sl]).astype(bf16)))
            vb.append(bd(v.astype(bf16)))

        gram = [_mmb(jnp.concatenate([at[p], rt[p]], axis=0), jnp.concatenate([bt[p], kt[p]], axis=0), _NT)
                for p in pairs]
        lmat = [jnp.where(stril, gm[:PAIR, :PAIR], 0.0) for gm in gram]
        mv = [_mmb(jnp.where(stril, gram[p][:PAIR, PAIR:], 0.0), vb[p]) for p in pairs]
        lower = [jnp.where(tril2, gm[PAIR:, :], 0.0).astype(bf16) for gm in gram]

        tinv = [eye + lm for lm in lmat]
        pw = [_mmb(lm, lm) for lm in lmat]
        for _ in range(4):
            z = [_mmb(jnp.concatenate([x.astype(bf16), t.astype(bf16)], axis=0), x)
                 for t, x in zip(tinv, pw)]
            pw = [zz[:PAIR] for zz in z]
            tinv = [t + zz[PAIR:] for t, zz in zip(tinv, z)]
        tinv = [t + _mmb(t, x) for t, x in zip(tinv, pw)]

        wx = [_mmb(tinv[p], jnp.concatenate([at[p], mv[p].astype(bf16)], axis=1)) for p in pairs]
        uy0 = [_mmb(jnp.concatenate([wx[p][:, :PAIR].astype(bf16), rt[p]], axis=0),
                    bd(state[p].astype(bf16)), _NT) for p in pairs]
        uv = [jnp.concatenate([(uy0[p][:PAIR] + wx[p][:, PAIR:]).astype(bf16), vb[p]], axis=0)
              for p in pairs]
        ys = [uy0[p][PAIR:] + _mmb(lower[p], uv[p]) for p in pairs]
        s_add = [_mmb(uv[p], jnp.concatenate([bt[p], kt[p]], axis=0), _TN) for p in pairs]
        state = [(state[p] + s_add[p][:RW_HEAD] + s_add[p][RW_HEAD:]) * e_end[:, sls[p]] for p in pairs]

        ys = [y[:C] + y[C:] for y in ys]
        mus = [head_sums(y) * (1.0 / RW_HEAD) for y in ys]
        ds = [y - mu for y, mu in zip(ys, mus)]
        var = [head_sums(d * d) * (1.0 / RW_HEAD) for d in ds]
        bonus = [head_sums(x[0] * x[1] * rk_ref[:, sl]) * x[2] for x, sl in zip(ins, sls)]
        for p in pairs:
            sl = sls[p]
            y_rw = ds[p] * lax.rsqrt(var[p] + GN_EPS) * lnw_ref[:, sl] + lnb_ref[:, sl] + bonus[p]
            y_ref[0, rs, sl] = (y_rw * _silu(gr_ref[0, rs, sl])).astype(y_ref.dtype)

    for p in pairs:
        sp_ref[p] = state[p]

    @pl.when(c == pl.num_programs(1) - 1)
    def _():
        for p in range(N_PAIRS):
            s_ref[0, 2 * p] = state[p][:, :RW_HEAD]
            s_ref[0, 2 * p + 1] = state[p][:, RW_HEAD:]


def _wkv_chunked(proj, prev, mu, w0, a0, k_k, k_a, lora, r_k, ln_w, ln_b, s0):
    B, T, _ = proj.shape
    W = RWKV_WIDTH
    rows = CHUNKS_PER_STEP * CHUNK
    col = lambda w, off: pl.BlockSpec((1, rows, w), lambda b, c: (b, c, off // w))
    first = lambda w: pl.BlockSpec((1, 1, w), lambda b, c: (b, 0, 0))
    vec = lambda w: pl.BlockSpec((1, w), lambda b, c: (0, 0))
    lora_spec = pl.BlockSpec((LORA_PAIR, W), lambda b, c: (0, 0))
    state = pl.BlockSpec((1, RW_HEADS, RW_HEAD, RW_HEAD), lambda b, c: (b, 0, 0, 0))
    return pl.pallas_call(
        _wkv_chunk_kernel,
        grid=(B, T // rows),
        in_specs=[col(W, P_R), col(W, P_KR), col(W, P_VR), col(LANES, P_WA), col(W, P_GR),
                  first(W), first(W), first(W), first(LANES),
                  vec(W), vec(W), vec(W), vec(LANES), vec(W), vec(W), vec(W), vec(W),
                  lora_spec, lora_spec, lora_spec, lora_spec, vec(W), vec(W), vec(W), state],
        out_specs=[col(W, 0), state],
        out_shape=[
            jax.ShapeDtypeStruct((B, T, W), bf16),
            jax.ShapeDtypeStruct((B, RW_HEADS, RW_HEAD, RW_HEAD), f32),
        ],
        scratch_shapes=[pltpu.VMEM((N_PAIRS, RW_HEAD, PAIR), f32),
                        pltpu.VMEM((1, W), f32), pltpu.VMEM((1, W), f32), pltpu.VMEM((1, W), f32),
                        pltpu.VMEM((1, LANES), f32)],
        compiler_params=_cparams(("parallel", "arbitrary")),
        name="wkv_chunks",
    )(proj, proj, proj, proj, proj, *prev, *mu, w0, a0, k_k, k_a, *lora,
      r_k.reshape(1, W), ln_w.reshape(1, W), ln_b.reshape(1, W), s0)


def _post_kernel(final, gated, att_ref, y_ref, *refs):
    if gated:
        y = y_ref[0]
    else:
        y = (y_ref[0] * _silu(refs[0][0])).astype(bf16)
        refs = refs[1:]
    x_ref, gate_ref, w_ref, fg_ref, o_ref = refs
    cat = jnp.concatenate([att_ref[0], y], axis=1)
    out = jnp.dot(cat, w_ref[...], preferred_element_type=f32)
    x = x_ref[0] + gate_ref[0] * out
    if final:
        ms = jnp.mean(x * x, axis=-1, keepdims=True)
        x = x * lax.rsqrt(ms + NORM_EPS) * fg_ref[...]
    o_ref[0] = x


def _post(att, y_rw, proj, x, gate, w_out_bf, final_g, final, tm):
    G, R, _ = x.shape
    W = RWKV_WIDTH
    gated = proj is None
    gate_in = [] if gated else [pl.BlockSpec((1, tm, W), lambda g, m: (g, m, P_GR // W))]
    gate_arg = [] if gated else [proj]
    return pl.pallas_call(
        functools.partial(_post_kernel, final, gated),
        grid=(G, R // tm),
        in_specs=[
            pl.BlockSpec((1, tm, ATT_WIDTH), lambda g, m: (g, m, 0)),
            pl.BlockSpec((1, tm, W), lambda g, m: (g, m, 0)),
            *gate_in,
            pl.BlockSpec((1, tm, D_MODEL), lambda g, m: (g, m, 0)),
            _mod_spec(gate, tm),
            pl.BlockSpec((D_MODEL, D_MODEL), lambda g, m: (0, 0)),
            pl.BlockSpec((1, D_MODEL), lambda g, m: (0, 0)),
        ],
        out_specs=pl.BlockSpec((1, tm, D_MODEL), lambda g, m: (g, m, 0)),
        out_shape=jax.ShapeDtypeStruct((G, R, D_MODEL), f32),
        compiler_params=_cparams(("parallel", "parallel")),
        name="post_proj",
    )(att, y_rw, *gate_arg, x, gate, w_out_bf, final_g.reshape(1, D_MODEL))


def _arrange_w_in(w):
    pad = jnp.zeros((w.shape[0], P_WIDTH - IN_WIDTH), w.dtype)
    parts = [w[:, Q_OFF:KA_OFF], w[:, GA_OFF:GR_OFF], w[:, GR_OFF:IN_WIDTH], w[:, R_OFF:WD_OFF],
             w[:, KA_OFF:R_OFF], w[:, WD_OFF:GA_OFF], pad]
    return jnp.concatenate(parts, axis=1).astype(bf16)


def _shift_cols(t):
    return jnp.concatenate([t[..., P_R:P_KA], t[..., P_WA:P_WA + LORA_PAIR]], axis=-1)


def kernel(x_prompt, x_sample, cache_k, cache_v, state_wkv, state_shift, c_prompt, c_sample,
           norm_g, w_ada, b_ada, w_in, mu_shift, w0, w_decay, a0, w_iclr, k_k, k_a, r_k,
           ln_w, ln_b, sinks, w_out, final_g):
    Bp, Tp = x_prompt.shape[0], x_prompt.shape[1]
    Bd = x_sample.shape[0]
    W = RWKV_WIDTH

    n_c = Bp + Bd
    c_rows = -(-n_c // 16) * 16
    c_all = jnp.concatenate([c_prompt, c_sample, jnp.zeros((c_rows - n_c, D_MODEL), f32)], axis=0)
    mod = _ada(c_all, w_ada, b_ada)

    tab_p = _rope_tables(jnp.arange(Tp, dtype=jnp.int32))
    tab_s = _rope_tables(jnp.full((Bd,), PAST_LEN, jnp.int32))

    hp = x_prompt
    hs = x_sample.reshape(1, Bd, D_MODEL)
    s0_p = jnp.zeros((Bp, RW_HEADS, RW_HEAD, RW_HEAD), f32)
    shift0_p = [jnp.zeros((Bp, 1, w), f32) for w in (W, W, W, LORA_PAIR)]
    ck_all = cache_k.reshape(DEPTH, Bd, WINDOW, KV_WIDTH)
    cv_all = cache_v.reshape(DEPTH, Bd, WINDOW, KV_WIDTH)
    new_state_s = jnp.zeros(state_wkv.shape, f32)
    new_cache_s = [jnp.zeros(ck_all.shape, f32), jnp.zeros(cv_all.shape, f32)]
    outs = {k: [] for k in ("kp", "vp", "sp", "shp", "shs")}
    for l in range(DEPTH):
        final = l == DEPTH - 1
        w_bf = _arrange_w_in(w_in[l])
        w_out_bf = w_out[l].astype(bf16)
        mu_l = mu_shift[l]
        mu = [mu_l[0:W].reshape(1, W), mu_l[W:2 * W].reshape(1, W), mu_l[2 * W:3 * W].reshape(1, W),
              mu_l[3 * W:].reshape(1, LORA_PAIR)]
        vecs = [t[l].reshape(1, W) for t in (w0, a0, k_k, k_a)]
        wd_pad = jnp.concatenate([w_decay[l], jnp.zeros((ICLR_LORA, W), f32)], axis=0)
        wi_pad = jnp.concatenate([jnp.zeros((DECAY_LORA, W), f32), w_iclr[l]], axis=0)
        shift_p, scale_p, gate_p = (mod[l, :Bp, i * D_MODEL:(i + 1) * D_MODEL].reshape(Bp, 1, D_MODEL)
                                    for i in range(3))
        shift_s, scale_s, gate_s = (mod[l, Bp:n_c, i * D_MODEL:(i + 1) * D_MODEL].reshape(1, Bd, D_MODEL)
                                    for i in range(3))

        proj = _norm_proj(hp, norm_g[l], scale_p, shift_p, w_bf, tab_p, tm=1024)
        att = _attn_prompt(proj, sinks[l])
        lora = [piece for wp in (wd_pad, wi_pad) for piece in _split_bf16(wp)]
        y_rw, s_t = _wkv_chunked(proj, shift0_p, mu, *vecs, lora, r_k[l], ln_w[l], ln_b[l], s0_p)
        hp = _post(att, y_rw, None, hp, gate_p, w_out_bf, final_g, final, tm=512)
        tail = proj[:, Tp - WINDOW:]
        outs["kp"].append(tail[..., P_KA:P_KA + KV_WIDTH].reshape(Bp, WINDOW, N_KV_HEADS, HEAD_DIM))
        outs["vp"].append(tail[..., P_VA:P_VA + KV_WIDTH].reshape(Bp, WINDOW, N_KV_HEADS, HEAD_DIM))
        outs["sp"].append(s_t)
        outs["shp"].append(_shift_cols(proj[:, Tp - 1]))

        proj = _norm_proj(hs, norm_g[l], scale_s, shift_s, w_bf, tab_s, tm=Bd)
        att, *new_cache_s = _attn_sample(proj, ck_all, cv_all, l, new_cache_s, sinks[l])
        sh = state_shift[l]
        prev = [sh[None, :, 0:W], sh[None, :, W:2 * W], sh[None, :, 2 * W:3 * W], sh[None, :, 3 * W:]]
        prep = _rwkv_prep(proj, prev, mu, *vecs, wd_pad, wi_pad, tm=Bd)
        prep = [t[:, None] for t in prep]
        y_rw, new_state_s = _wkv(*prep, r_k[l], ln_w[l], ln_b[l], state_wkv, l, new_state_s, tc=1)
        hs = _post(att, y_rw.reshape(1, Bd, W), proj, hs, gate_s, w_out_bf, final_g, final, tm=Bd)
        outs["shs"].append(_shift_cols(proj[0]))

    st = lambda k: jnp.stack(outs[k])
    return (hp, hs.reshape(Bd, 1, D_MODEL), st("kp"), st("vp"), st("sp"), st("shp"),
            *(t.reshape(cache_k.shape) for t in new_cache_s), new_state_s, st("shs"))
```

```python
import functools

import jax
import jax.numpy as jnp
from jax import lax
from jax.experimental import pallas as pl
from jax.experimental.pallas import tpu as pltpu

f32 = jnp.float32
bf16 = jnp.bfloat16

D_MODEL = 2048
DEPTH = 2
PAST_LEN = 16384
ATT_WIDTH = 1024
RWKV_WIDTH = 1024
HEAD_DIM = 64
N_Q_HEADS = 16
N_KV_HEADS = 4
GQA_GROUP = 4
KV_WIDTH = 256
WINDOW = 128
ROT_DIM = 16
ROPE_THETA = 500000.0
RW_HEAD = 64
RW_HEADS = 16
DECAY_LORA = 64
ICLR_LORA = 64
LORA_PAIR = DECAY_LORA + ICLR_LORA
NORM_EPS = 1e-5
GN_EPS = 64e-5
NEG_BIG = -1e30

Q_OFF = 0
KA_OFF = Q_OFF + ATT_WIDTH
VA_OFF = KA_OFF + KV_WIDTH
R_OFF = VA_OFF + KV_WIDTH
KR_OFF = R_OFF + RWKV_WIDTH
VR_OFF = KR_OFF + RWKV_WIDTH
WD_OFF = VR_OFF + RWKV_WIDTH
AD_OFF = WD_OFF + DECAY_LORA
GA_OFF = AD_OFF + ICLR_LORA
GR_OFF = GA_OFF + ATT_WIDTH
IN_WIDTH = GR_OFF + RWKV_WIDTH
SHIFT_DIM = GA_OFF - R_OFF

LANES = 128
P_Q = 0
P_GA = 1024
P_GR = 2048
P_R = 3072
P_KR = 4096
P_VR = 5120
P_KA = 6144
P_VA = 6400
P_WA = 6656
P_WIDTH = 7168
PROJ_TN = 1792
PROJ_TM = 1024
POST_TM = 512

VMEM_LIMIT = 56 * 1024 * 1024


_NN = (((1,), (0,)), ((), ()))
_NT = (((1,), (1,)), ((), ()))
_TN = (((0,), (0,)), ((), ()))


def _silu(x):
    return x * jax.nn.sigmoid(x)


def _cparams(sem):
    return pltpu.CompilerParams(dimension_semantics=sem, vmem_limit_bytes=VMEM_LIMIT)


def _split_bf16(x):
    hi = x.astype(bf16)
    return hi, (x - hi.astype(f32)).astype(bf16)


def _ada_kernel(c_ref, w_ref, b_ref, o_ref):
    ch, cl = _split_bf16(_silu(c_ref[...]))
    wh, wl = _split_bf16(w_ref[0])
    rows = ch.shape[0]
    both = jnp.dot(jnp.concatenate([ch, cl], axis=0), wh, preferred_element_type=f32)
    o_ref[0] = both[:rows] + both[rows:] + jnp.dot(ch, wl, preferred_element_type=f32) + b_ref[0]


def _ada(c_all, w_ada, b_ada):
    rows = c_all.shape[0]
    tn = 1536
    n_out = w_ada.shape[2]
    return pl.pallas_call(
        _ada_kernel,
        grid=(DEPTH, n_out // tn),
        in_specs=[
            pl.BlockSpec((rows, D_MODEL), lambda l, n: (0, 0)),
            pl.BlockSpec((1, D_MODEL, tn), lambda l, n: (l, 0, n)),
            pl.BlockSpec((1, 1, tn), lambda l, n: (l, 0, n)),
        ],
        out_specs=pl.BlockSpec((1, rows, tn), lambda l, n: (l, 0, n)),
        out_shape=jax.ShapeDtypeStruct((DEPTH, rows, n_out), f32),
        compiler_params=_cparams(("parallel", "parallel")),
        name="ada_mod",
    )(c_all, w_ada, b_ada.reshape(DEPTH, 1, n_out))


def _rope(x, tab):
    w = x.shape[1]
    reps = w // LANES
    cosf, up, dn = (jnp.concatenate([tab[i]] * reps, axis=1) for i in range(3))
    half = ROT_DIM // 2
    return x * cosf + pltpu.roll(x, w - half, 1) * up + pltpu.roll(x, half, 1) * dn


def _norm_proj_kernel(x_ref, g_ref, scale_ref, shift_ref, w_ref, tab_ref, o_ref, h_ref):
    n = pl.program_id(2)

    @pl.when(n == 0)
    def _():
        x = x_ref[0]
        ms = jnp.mean(x * x, axis=-1, keepdims=True)
        y = x * lax.rsqrt(ms + NORM_EPS) * g_ref[...]
        h_ref[...] = (y * (1.0 + scale_ref[0]) + shift_ref[0]).astype(bf16)

    tm, tn = o_ref.shape[1], o_ref.shape[2]
    q_tile, q_lo = divmod(P_Q, tn)
    k_tile, k_lo = divmod(P_KA, tn)
    assert q_lo + ATT_WIDTH <= tn and k_lo + KV_WIDTH <= tn and q_tile != k_tile
    rc = min(tm, 256)

    def rotated_tile(lo, width, scale):
        for i in range(tm // rc):
            rs = slice(i * rc, (i + 1) * rc)
            res = jnp.dot(h_ref[rs, :], w_ref[...], preferred_element_type=f32)
            parts = [res[:, :lo]] if lo else []
            parts.append(_rope(res[:, lo:lo + width], tab_ref[:, rs, :]) * scale)
            if lo + width < tn:
                parts.append(res[:, lo + width:])
            o_ref[0, rs, :] = jnp.concatenate(parts, axis=1) if len(parts) > 1 else parts[0]

    @pl.when(n == q_tile)
    def _():
        rotated_tile(q_lo, ATT_WIDTH, HEAD_DIM ** -0.5)

    @pl.when(n == k_tile)
    def _():
        rotated_tile(k_lo, KV_WIDTH, 1.0)

    @pl.when((n != q_tile) & (n != k_tile))
    def _():
        o_ref[0] = jnp.dot(h_ref[...], w_ref[...], preferred_element_type=f32)


def _mod_spec(mod, tm):
    if mod.shape[1] == 1:
        return pl.BlockSpec((1, 1, D_MODEL), lambda g, m, *_: (g, 0, 0))
    return pl.BlockSpec((1, tm, D_MODEL), lambda g, m, *_: (g, m, 0))


def _norm_proj(x, norm_g, scale, shift, w_bf, tab, tm):
    G, R, _ = x.shape
    return pl.pallas_call(
        _norm_proj_kernel,
        grid=(G, R // tm, P_WIDTH // PROJ_TN),
        in_specs=[
            pl.BlockSpec((1, tm, D_MODEL), lambda g, m, n: (g, m, 0)),
            pl.BlockSpec((1, D_MODEL), lambda g, m, n: (0, 0)),
            _mod_spec(scale, tm),
            _mod_spec(shift, tm),
            pl.BlockSpec((D_MODEL, PROJ_TN), lambda g, m, n: (0, n)),
            pl.BlockSpec((3, tm, LANES), lambda g, m, n: (0, m, 0)),
        ],
        out_specs=pl.BlockSpec((1, tm, PROJ_TN), lambda g, m, n: (g, m, n)),
        out_shape=jax.ShapeDtypeStruct((G, R, P_WIDTH), f32),
        scratch_shapes=[pltpu.VMEM((tm, D_MODEL), bf16)],
        compiler_params=_cparams(("parallel", "parallel", "arbitrary")),
        name="norm_proj",
    )(x, norm_g.reshape(1, D_MODEL), scale, shift, w_bf, tab)


def _rope_tables(pos):
    half = ROT_DIM // 2
    inv_freq = ROPE_THETA ** (-jnp.arange(half, dtype=f32) * (2.0 / ROT_DIM))
    ang = pos.astype(f32)[:, None] * inv_freq[None, :]
    cos, sin = jnp.cos(ang), jnp.sin(ang)
    t = pos.shape[0]
    z8 = jnp.zeros((t, half), f32)
    rest = HEAD_DIM - ROT_DIM
    cos64 = jnp.concatenate([cos, cos, jnp.ones((t, rest), f32)], axis=1)
    up64 = jnp.concatenate([-sin, z8, jnp.zeros((t, rest), f32)], axis=1)
    dn64 = jnp.concatenate([z8, sin, jnp.zeros((t, rest), f32)], axis=1)
    rep = LANES // HEAD_DIM
    return jnp.stack([jnp.tile(a, (1, rep)) for a in (cos64, up64, dn64)])


ATTN_BLOCKS = 4


def _attn_prompt_kernel(sinks_ref, q_ref, kc_ref, kp_ref, vc_ref, vp_ref, ga_ref, o_ref):
    n = pl.program_id(1)
    wn = WINDOW
    half = HEAD_DIM
    k_t_all = jnp.concatenate([kp_ref[0], kc_ref[0]], axis=0).T.astype(bf16)
    v_all = jnp.concatenate([vp_ref[0], vc_ref[0]], axis=0)

    qi = lax.broadcasted_iota(jnp.int32, (2 * wn, 2 * wn), 0) & (wn - 1)
    kj = lax.broadcasted_iota(jnp.int32, (2 * wn, 2 * wn), 1)
    rel = wn + qi - kj
    band = (rel >= 0) & (rel <= wn)
    top = lax.broadcasted_iota(jnp.int32, (2 * wn, 1), 0) < wn
    lo = lax.broadcasted_iota(jnp.int32, (2 * wn, LANES), 1) < half
    zeros_k = jnp.zeros((half, 2 * wn), bf16)
    ones_lo = jnp.where(lo, 1.0, 0.0).astype(bf16)
    ones_hi = jnp.where(lo, 0.0, 1.0).astype(bf16)

    for sb in range(q_ref.shape[1] // wn):
        rows = slice(sb * wn, (sb + 1) * wn)
        q = q_ref[0, rows, :].astype(bf16)
        ga = ga_ref[0, rows, :]
        k_t = k_t_all[:, sb * wn:(sb + 2) * wn]
        vcat = v_all[sb * wn:(sb + 2) * wn]
        mask = band & ((kj >= wn) | (n > 0)) if sb == 0 else band
        for j in range(N_KV_HEADS // 2):
            vblk = vcat[:, j * LANES:(j + 1) * LANES]
            vswap = pltpu.roll(vblk, half, 1)
            for g in (2 * j, 2 * j + 1):
                own, other = (vblk, vswap) if g % 2 == 0 else (vswap, vblk)
                v_lo = jnp.where(lo, own, 0.0).astype(bf16)
                v_hi = jnp.where(lo, 0.0, other).astype(bf16)
                rhs_pv = jnp.concatenate([jnp.concatenate([v_lo, ones_lo], axis=1),
                                          jnp.concatenate([v_hi, ones_hi], axis=1)], axis=0)
                kg = k_t[g * half:(g + 1) * half, :]
                rhs_qk = jnp.concatenate([jnp.concatenate([kg, zeros_k], axis=0),
                                          jnp.concatenate([zeros_k, kg], axis=0)], axis=1)
                b0, b1 = 2 * g, 2 * g + 1
                qg = jnp.concatenate([q[:, b0 * LANES:(b0 + 1) * LANES], q[:, b1 * LANES:(b1 + 1) * LANES]],
                                     axis=0)
                s_all = jnp.dot(qg, rhs_qk, preferred_element_type=f32)
                ps, es = [], []
                for hh in range(2):
                    s = jnp.where(mask, s_all[:, hh * 2 * wn:(hh + 1) * 2 * wn], NEG_BIG)
                    sink = jnp.where(top, sinks_ref[2 * b0 + hh], sinks_ref[2 * b1 + hh])
                    m = jnp.maximum(jnp.max(s, axis=-1, keepdims=True), sink)
                    ps.append(jnp.exp(s - m).astype(bf16))
                    es.append(jnp.exp(sink - m))
                res = jnp.dot(jnp.concatenate(ps, axis=1), rhs_pv, preferred_element_type=f32)
                out = res[:, :LANES] / (res[:, LANES:] + jnp.where(lo, es[0], es[1]))
                for i, blk in enumerate((b0, b1)):
                    sl = slice(blk * LANES, (blk + 1) * LANES)
                    o_ref[0, rows, sl] = (out[i * wn:(i + 1) * wn] * _silu(ga[:, sl])).astype(o_ref.dtype)


def _attn_prompt(proj, sinks):
    B, T, _ = proj.shape
    rows = ATTN_BLOCKS * WINDOW
    kvb = KV_WIDTH
    prev = lambda b, n: jnp.maximum(n * ATTN_BLOCKS - 1, 0)
    return pl.pallas_call(
        _attn_prompt_kernel,
        grid=(B, T // rows),
        in_specs=[
            pl.BlockSpec(memory_space=pltpu.SMEM),
            pl.BlockSpec((1, rows, ATT_WIDTH), lambda b, n: (b, n, P_Q // ATT_WIDTH)),
            pl.BlockSpec((1, rows, kvb), lambda b, n: (b, n, P_KA // kvb)),
            pl.BlockSpec((1, WINDOW, kvb), lambda b, n: (b, prev(b, n), P_KA // kvb)),
            pl.BlockSpec((1, rows, kvb), lambda b, n: (b, n, P_VA // kvb)),
            pl.BlockSpec((1, WINDOW, kvb), lambda b, n: (b, prev(b, n), P_VA // kvb)),
            pl.BlockSpec((1, rows, ATT_WIDTH), lambda b, n: (b, n, P_GA // ATT_WIDTH)),
        ],
        out_specs=pl.BlockSpec((1, rows, ATT_WIDTH), lambda b, n: (b, n, 0)),
        out_shape=jax.ShapeDtypeStruct((B, T, ATT_WIDTH), bf16),
        compiler_params=_cparams(("parallel", "arbitrary")),
        name="attn_prompt",
    )(sinks, proj, proj, proj, proj, proj, proj)


SAMPLE_ROWS = 32


def _attn_sample_kernel(sinks_ref, q_ref, kn_ref, vn_ref, ga_ref, ck_ref, cv_ref, *refs):
    o_ref, nk_ref, nv_ref = refs[-3:]
    rb = q_ref.shape[1]
    q_all, kn_all, vn_all = q_ref[0], kn_ref[0], vn_ref[0]
    work = [(r, g) for r in range(rb) for g in range(N_KV_HEADS)]
    span = lambda g: slice(g * HEAD_DIM, (g + 1) * HEAD_DIM)
    ckb = [ck_ref[0, r].astype(bf16) for r in range(rb)]
    cvb = [cv_ref[0, r].astype(bf16) for r in range(rb)]
    qg = {(r, g): jnp.concatenate(
        [q_all[r:r + 1, (g * GQA_GROUP + i) * HEAD_DIM:(g * GQA_GROUP + i + 1) * HEAD_DIM]
         for i in range(GQA_GROUP)], axis=0) for r, g in work}
    s = {rg: lax.dot_general(qg[rg].astype(bf16), ckb[rg[0]][:, span(rg[1])], _NT,
                             preferred_element_type=f32) for rg in work}
    sink = {g: sinks_ref[g * GQA_GROUP:(g + 1) * GQA_GROUP, :] for g in range(N_KV_HEADS)}
    s_new = {(r, g): jnp.sum(qg[r, g] * kn_all[r:r + 1, span(g)], axis=-1, keepdims=True) for r, g in work}
    m = {rg: jnp.maximum(jnp.maximum(jnp.max(s[rg], axis=-1, keepdims=True), s_new[rg]), sink[rg[1]])
         for rg in work}
    p = {rg: jnp.exp(s[rg] - m[rg]) for rg in work}
    p_new = {rg: jnp.exp(s_new[rg] - m[rg]) for rg in work}
    den = {rg: jnp.sum(p[rg], axis=-1, keepdims=True) + p_new[rg] + jnp.exp(sink[rg[1]] - m[rg]) for rg in work}
    pv = {(r, g): jnp.dot(p[r, g].astype(bf16), cvb[r][:, span(g)], preferred_element_type=f32) for r, g in work}
    outs = {}
    for r, g in work:
        o = (pv[r, g] + p_new[r, g] * vn_all[r:r + 1, span(g)]) / den[r, g]
        outs[r, g] = [o[i:i + 1, :] for i in range(GQA_GROUP)]
    att = jnp.concatenate(
        [jnp.concatenate([h for g in range(N_KV_HEADS) for h in outs[r, g]], axis=1) for r in range(rb)],
        axis=0)
    o_ref[0] = (att * _silu(ga_ref[0])).astype(o_ref.dtype)
    last = lax.broadcasted_iota(jnp.int32, (WINDOW, KV_WIDTH), 0) == WINDOW - 1
    for r in range(rb):
        nk_ref[0, r] = jnp.where(last, kn_all[r:r + 1], pltpu.roll(ck_ref[0, r], WINDOW - 1, 0))
        nv_ref[0, r] = jnp.where(last, vn_all[r:r + 1], pltpu.roll(cv_ref[0, r], WINDOW - 1, 0))


def _attn_sample(proj, ck_all, cv_all, layer, stacked, sinks):
    Bd = proj.shape[1]
    rb = SAMPLE_ROWS
    col = lambda w, off: pl.BlockSpec((1, rb, w), lambda i: (0, i, off // w))
    cache = pl.BlockSpec((1, rb, WINDOW, KV_WIDTH), lambda i: (layer, i, 0, 0))
    args = [sinks.reshape(N_Q_HEADS, 1), proj, proj, proj, proj, ck_all, cv_all, *stacked]
    hbm = pl.BlockSpec(memory_space=pl.ANY)
    return pl.pallas_call(
        _attn_sample_kernel,
        grid=(Bd // rb,),
        in_specs=[
            pl.BlockSpec((N_Q_HEADS, 1), lambda i: (0, 0)),
            col(ATT_WIDTH, P_Q), col(KV_WIDTH, P_KA), col(KV_WIDTH, P_VA), col(ATT_WIDTH, P_GA),
            cache, cache, hbm, hbm,
        ],
        out_specs=[pl.BlockSpec((1, rb, ATT_WIDTH), lambda i: (0, i, 0)), cache, cache],
        out_shape=[
            jax.ShapeDtypeStruct((1, Bd, ATT_WIDTH), bf16),
            jax.ShapeDtypeStruct(ck_all.shape, f32),
            jax.ShapeDtypeStruct(cv_all.shape, f32),
        ],
        input_output_aliases={len(args) - 2: 1, len(args) - 1: 2},
        compiler_params=_cparams(("parallel",)),
        name="attn_sample",
    )(*args)


def _rwkv_prep_kernel(r_ref, kr_ref, vr_ref, wa_ref, pr_ref, pkr_ref, pvr_ref, pwa_ref,
                      mu_r_ref, mu_kr_ref, mu_vr_ref, mu_wa_ref, w0_ref, a0_ref, kk_ref, ka_ref,
                      wd_ref, wi_ref, ro_ref, wo_ref, ko_ref, vo_ref, kko_ref, ao_ref):
    def mixed(cur_ref, prev_ref, mu_ref):
        cur = cur_ref[0]
        return cur + (prev_ref[0] - cur) * mu_ref[...]

    r = mixed(r_ref, pr_ref, mu_r_ref)
    kr = mixed(kr_ref, pkr_ref, mu_kr_ref)
    vr = mixed(vr_ref, pvr_ref, mu_vr_ref)
    wa = mixed(wa_ref, pwa_ref, mu_wa_ref)
    hi = lax.Precision.HIGHEST
    z = w0_ref[...] + jnp.dot(jnp.tanh(wa), wd_ref[...], precision=hi, preferred_element_type=f32)
    nz = -z
    softplus = jnp.maximum(nz, 0.0) + jnp.log1p(jnp.exp(-jnp.abs(nz)))
    w_log = -softplus - 0.5
    a = jax.nn.sigmoid(a0_ref[...] + jnp.dot(wa, wi_ref[...], precision=hi, preferred_element_type=f32))
    outs = (r, -jnp.exp(w_log), kr * (1.0 + (a - 1.0) * ka_ref[...]), vr, kr * kk_ref[...], a)
    for o_ref, val in zip((ro_ref, wo_ref, ko_ref, vo_ref, kko_ref, ao_ref), outs):
        for h in range(RW_HEADS):
            o_ref[:, h, :] = val[:, h * RW_HEAD:(h + 1) * RW_HEAD]


def _rwkv_prep(proj, prev, mu, w0, a0, k_k, k_a, wd_pad, wi_pad, tm):
    G, R, _ = proj.shape
    assert G == 1
    W = RWKV_WIDTH
    cur_specs = [
        pl.BlockSpec((1, tm, W), lambda g, m: (g, m, P_R // W)),
        pl.BlockSpec((1, tm, W), lambda g, m: (g, m, P_KR // W)),
        pl.BlockSpec((1, tm, W), lambda g, m: (g, m, P_VR // W)),
        pl.BlockSpec((1, tm, LANES), lambda g, m: (g, m, P_WA // LANES)),
    ]
    prev_specs = [
        pl.BlockSpec((1, tm, W), lambda g, m: (g, m, 0)),
        pl.BlockSpec((1, tm, W), lambda g, m: (g, m, 0)),
        pl.BlockSpec((1, tm, W), lambda g, m: (g, m, 0)),
        pl.BlockSpec((1, tm, LANES), lambda g, m: (g, m, 0)),
    ]
    prev_args = list(prev)
    vec = lambda w: pl.BlockSpec((1, w), lambda g, m: (0, 0))
    lora = pl.BlockSpec((LORA_PAIR, W), lambda g, m: (0, 0))
    out_spec = pl.BlockSpec((tm, RW_HEADS, RW_HEAD), lambda g, m: (m, 0, 0))
    return pl.pallas_call(
        _rwkv_prep_kernel,
        grid=(G, R // tm),
        in_specs=cur_specs + prev_specs + [vec(W), vec(W), vec(W), vec(LANES),
                                           vec(W), vec(W), vec(W), vec(W), lora, lora],
        out_specs=[out_spec] * 6,
        out_shape=[jax.ShapeDtypeStruct((R, RW_HEADS, RW_HEAD), f32)] * 6,
        compiler_params=_cparams(("parallel", "arbitrary")),
        name="rwkv_prep",
    )(proj, proj, proj, proj, *prev_args, *mu, w0, a0, k_k, k_a, wd_pad, wi_pad)


def _wkv_kernel(tc, r_ref, w_ref, k_ref, v_ref, kk_ref, a_ref, rk_ref, lnw_ref, lnb_ref, s0_ref,
                *refs):
    y_ref, s_ref = refs[-2:]

    @pl.when(pl.program_id(1) == 0)
    def _():
        s_ref[...] = s0_ref[...]

    eye = (lax.broadcasted_iota(jnp.int32, (RW_HEAD, RW_HEAD), 0)
           == lax.broadcasted_iota(jnp.int32, (RW_HEAD, RW_HEAD), 1)).astype(f32)
    rk = rk_ref[...]
    lnw = lnw_ref[...]
    lnb = lnb_ref[...]

    def step(t, carry):
        r, lw, k, v, kkr, a = (ref[0, t] for ref in (r_ref, w_ref, k_ref, v_ref, kk_ref, a_ref))
        w = jnp.exp(lw)
        norm = jnp.sqrt(jnp.sum(kkr * kkr, axis=-1, keepdims=True))
        kk = kkr / jnp.maximum(norm, 1e-12)
        b = kk * a
        bonus = jnp.sum(r * k * rk, axis=-1, keepdims=True) * v
        heads = range(RW_HEADS)
        row = lambda x, h: x[h:h + 1]
        s_in = [s_ref[0, 0, h] for h in heads]
        sa = [jnp.sum(s_in[h] * (-row(kk, h)), axis=-1, keepdims=True) for h in heads]
        v_col = [jnp.sum(eye * row(v, h), axis=-1, keepdims=True) for h in heads]
        s_out = [s_in[h] * row(w, h) + sa[h] * row(b, h) + v_col[h] * row(k, h) for h in heads]
        for h in heads:
            s_ref[0, 0, h] = s_out[h]
        ys = [jnp.sum(s_out[h] * row(r, h), axis=-1, keepdims=True) for h in heads]
        mus = [jnp.mean(y, axis=0, keepdims=True) for y in ys]
        var = [jnp.mean(jnp.square(y - mu), axis=0, keepdims=True) for y, mu in zip(ys, mus)]
        rows = [jnp.sum((ys[h] - mus[h]) * lax.rsqrt(var[h] + GN_EPS) * eye, axis=0, keepdims=True)
                for h in heads]
        y_ref[0, t] = jnp.concatenate(rows, axis=0) * lnw + lnb + bonus
        return carry

    lax.fori_loop(0, tc, step, 0)


def _wkv(r, w, k, v, kk, a, r_k, ln_w, ln_b, s0_all, layer, stacked, tc):
    B, T = r.shape[0], r.shape[1]
    hd = (RW_HEADS, RW_HEAD)
    seq = pl.BlockSpec((1, tc, *hd), lambda b, c: (b, c, 0, 0))
    par = pl.BlockSpec(hd, lambda b, c: (0, 0))
    state = pl.BlockSpec((1, 1, RW_HEADS, RW_HEAD, RW_HEAD), lambda b, c: (layer, b, 0, 0, 0))
    args = [r, w, k, v, kk, a, r_k, ln_w.reshape(hd), ln_b.reshape(hd), s0_all, stacked]
    y, s_new = pl.pallas_call(
        functools.partial(_wkv_kernel, tc),
        grid=(B, T // tc),
        in_specs=[seq] * 6 + [par] * 3 + [state, pl.BlockSpec(memory_space=pl.ANY)],
        out_specs=[seq, state],
        out_shape=[
            jax.ShapeDtypeStruct((B, T, *hd), f32),
            jax.ShapeDtypeStruct(s0_all.shape, f32),
        ],
        input_output_aliases={len(args) - 1: 1},
        compiler_params=_cparams(("parallel", "arbitrary")),
        name="wkv_steps",
    )(*args)
    return y.reshape(B, T, RWKV_WIDTH), s_new


CHUNK = 64
PAIR = 2 * RW_HEAD
N_PAIRS = RW_HEADS // 2


def _mmb(a, b, dims=_NN):
    return lax.dot_general(a.astype(bf16), b.astype(bf16), dims, preferred_element_type=f32)


def _lora_dot(x, wh_ref, wl_ref):
    xh, xl = _split_bf16(x)
    rows = x.shape[0]
    both = jnp.dot(jnp.concatenate([xh, xl], axis=0), wh_ref[...], preferred_element_type=f32)
    return both[:rows] + both[rows:] + jnp.dot(xh, wl_ref[...], preferred_element_type=f32)


CHUNKS_PER_STEP = 4


def _wkv_chunk_kernel(r_ref, kr_ref, vr_ref, wa_ref, gr_ref, pr_ref, pkr_ref, pvr_ref, pwa_ref,
                      mu_r_ref, mu_kr_ref, mu_vr_ref, mu_wa_ref, w0_ref, a0_ref, kk_ref, ka_ref,
                      wdh_ref, wdl_ref, wih_ref, wil_ref, rk_ref, lnw_ref, lnb_ref, s0_ref,
                      y_ref, s_ref, sp_ref, cr_ref, ckr_ref, cvr_ref, cwa_ref):
    C = CHUNK
    rows_blk = r_ref.shape[1]
    c = pl.program_id(1)

    @pl.when(c == 0)
    def _():
        for p in range(N_PAIRS):
            sp_ref[p] = jnp.concatenate([s0_ref[0, 2 * p], s0_ref[0, 2 * p + 1]], axis=1)
        for carry, first in ((cr_ref, pr_ref), (ckr_ref, pkr_ref), (cvr_ref, pvr_ref), (cwa_ref, pwa_ref)):
            carry[...] = first[0]

    def mixed(cur_ref, carry_ref, mu_ref):
        cur = cur_ref[0]
        first = lax.broadcasted_iota(jnp.int32, cur.shape, 0) == 0
        prev = jnp.where(first, carry_ref[...], pltpu.roll(cur, 1, 0))
        carry_ref[...] = cur[rows_blk - 1:rows_blk, :]
        return cur + (prev - cur) * mu_ref[...]

    r_blk = mixed(r_ref, cr_ref, mu_r_ref)
    kr_blk = mixed(kr_ref, ckr_ref, mu_kr_ref)
    v_blk = mixed(vr_ref, cvr_ref, mu_vr_ref)
    wa = mixed(wa_ref, cwa_ref, mu_wa_ref)
    nz = -(w0_ref[...] + _lora_dot(jnp.tanh(wa), wdh_ref, wdl_ref))
    softplus = jnp.maximum(nz, 0.0) + jnp.log1p(jnp.exp(-jnp.abs(nz)))
    lw_blk = -jnp.exp(-softplus - 0.5)
    icl_blk = jax.nn.sigmoid(a0_ref[...] + _lora_dot(wa, wih_ref, wil_ref))
    k_blk = kr_blk * (1.0 + (icl_blk - 1.0) * ka_ref[...])
    kkr_blk = kr_blk * kk_ref[...]

    row = lax.broadcasted_iota(jnp.int32, (PAIR, PAIR), 0)
    col = lax.broadcasted_iota(jnp.int32, (PAIR, PAIR), 1)
    tril = row >= col
    stril = row > col
    tril2 = jnp.concatenate([tril, tril], axis=1)
    eye = (row == col).astype(f32)
    lane_lo = lax.broadcasted_iota(jnp.int32, (C, PAIR), 1) < RW_HEAD
    pairs = range(N_PAIRS)
    sls = [slice(p * PAIR, (p + 1) * PAIR) for p in pairs]

    def bd(x):
        zero = jnp.zeros_like(x)
        return jnp.concatenate([jnp.where(lane_lo, x, zero), jnp.where(lane_lo, zero, x)], axis=0)

    def head_sums(x):
        lo_sum = jnp.sum(jnp.where(lane_lo, x, 0.0), axis=-1, keepdims=True)
        hi_sum = jnp.sum(jnp.where(lane_lo, 0.0, x), axis=-1, keepdims=True)
        return jnp.where(lane_lo, lo_sum, hi_sum)

    state = [sp_ref[p] for p in pairs]
    for j in range(rows_blk // C):
        rs = slice(j * C, (j + 1) * C)
        lw = lw_blk[rs]
        width = lw.shape[1]
        lw_a = lw.astype(bf16)
        rest = lw - lw_a.astype(f32)
        lw_b = rest.astype(bf16)
        lw_c = (rest - lw_b.astype(f32)).astype(bf16)
        g3 = jnp.dot(tril[:C, :C].astype(bf16), jnp.concatenate([lw_a, lw_b, lw_c], axis=1),
                     preferred_element_type=f32)
        g = g3[:, :width] + g3[:, width:2 * width] + g3[:, 2 * width:]
        e_g = jnp.exp(g)
        e_ng = jnp.exp(-g)
        e_gm = jnp.exp(g - lw)
        e_end = e_g[C - 1:C, :]

        ins = [[t[rs, sl] for t in (r_blk, k_blk, v_blk, kkr_blk, icl_blk)] for sl in sls]
        norms = [jnp.sqrt(head_sums(x[3] * x[3])) for x in ins]
        at, rt, bt, kt, vb = ([] for _ in range(5))
        for (r, k, v, kkr, icl), norm, sl in zip(ins, norms, sls):
            kk = kkr / jnp.maximum(norm, 1e-12)
            b = kk * icl
            at.append(bd((-kk * e_gm[:, sl]).astype(bf16)))
            rt.append(bd((r * e_g[:, sl]).astype(bf16)))
            bt.append(bd((b * e_ng[:, sl]).astype(bf16)))
            kt.append(bd((k * e_ng[:, sl]).astype(bf16)))
            vb.append(bd(v.astype(bf16)))

        gram = [_mmb(jnp.concatenate([at[p], rt[p]], axis=0), jnp.concatenate([bt[p], kt[p]], axis=0), _NT)
                for p in pairs]
        lmat = [jnp.where(stril, gm[:PAIR, :PAIR], 0.0) for gm in gram]
        mv = [_mmb(jnp.where(stril, gram[p][:PAIR, PAIR:], 0.0), vb[p]) for p in pairs]
        lower = [jnp.where(tril2, gm[PAIR:, :], 0.0).astype(bf16) for gm in gram]

        tinv = [eye + lm for lm in lmat]
        pw = [_mmb(lm, lm) for lm in lmat]
        for _ in range(4):
            z = [_mmb(jnp.concatenate([x.astype(bf16), t.astype(bf16)], axis=0), x)
                 for t, x in zip(tinv, pw)]
            pw = [zz[:PAIR] for zz in z]
            tinv = [t + zz[PAIR:] for t, zz in zip(tinv, z)]
        tinv = [t + _mmb(t, x) for t, x in zip(tinv, pw)]

        wx = [_mmb(tinv[p], jnp.concatenate([at[p], mv[p].astype(bf16)], axis=1)) for p in pairs]
        uy0 = [_mmb(jnp.concatenate([wx[p][:, :PAIR].astype(bf16), rt[p]], axis=0),
                    bd(state[p].astype(bf16)), _NT) for p in pairs]
        uv = [jnp.concatenate([(uy0[p][:PAIR] + wx[p][:, PAIR:]).astype(bf16), vb[p]], axis=0)
              for p in pairs]
        ys = [uy0[p][PAIR:] + _mmb(lower[p], uv[p]) for p in pairs]
        s_add = [_mmb(uv[p], jnp.concatenate([bt[p], kt[p]], axis=0), _TN) for p in pairs]
        state = [(state[p] + s_add[p][:RW_HEAD] + s_add[p][RW_HEAD:]) * e_end[:, sls[p]] for p in pairs]

        ys = [y[:C] + y[C:] for y in ys]
        mus = [head_sums(y) * (1.0 / RW_HEAD) for y in ys]
        ds = [y - mu for y, mu in zip(ys, mus)]
        var = [head_sums(d * d) * (1.0 / RW_HEAD) for d in ds]
        bonus = [head_sums(x[0] * x[1] * rk_ref[:, sl]) * x[2] for x, sl in zip(ins, sls)]
        for p in pairs:
            sl = sls[p]
            y_rw = ds[p] * lax.rsqrt(var[p] + GN_EPS) * lnw_ref[:, sl] + lnb_ref[:, sl] + bonus[p]
            y_ref[0, rs, sl] = (y_rw * _silu(gr_ref[0, rs, sl])).astype(y_ref.dtype)

    for p in pairs:
        sp_ref[p] = state[p]

    @pl.when(c == pl.num_programs(1) - 1)
    def _():
        for p in range(N_PAIRS):
            s_ref[0, 2 * p] = state[p][:, :RW_HEAD]
            s_ref[0, 2 * p + 1] = state[p][:, RW_HEAD:]


def _wkv_chunked(proj, prev, mu, w0, a0, k_k, k_a, lora, r_k, ln_w, ln_b, s0):
    B, T, _ = proj.shape
    W = RWKV_WIDTH
    rows = CHUNKS_PER_STEP * CHUNK
    col = lambda w, off: pl.BlockSpec((1, rows, w), lambda b, c: (b, c, off // w))
    first = lambda w: pl.BlockSpec((1, 1, w), lambda b, c: (b, 0, 0))
    vec = lambda w: pl.BlockSpec((1, w), lambda b, c: (0, 0))
    lora_spec = pl.BlockSpec((LORA_PAIR, W), lambda b, c: (0, 0))
    state = pl.BlockSpec((1, RW_HEADS, RW_HEAD, RW_HEAD), lambda b, c: (b, 0, 0, 0))
    return pl.pallas_call(
        _wkv_chunk_kernel,
        grid=(B, T // rows),
        in_specs=[col(W, P_R), col(W, P_KR), col(W, P_VR), col(LANES, P_WA), col(W, P_GR),
                  first(W), first(W), first(W), first(LANES),
                  vec(W), vec(W), vec(W), vec(LANES), vec(W), vec(W), vec(W), vec(W),
                  lora_spec, lora_spec, lora_spec, lora_spec, vec(W), vec(W), vec(W), state],
        out_specs=[col(W, 0), state],
        out_shape=[
            jax.ShapeDtypeStruct((B, T, W), bf16),
            jax.ShapeDtypeStruct((B, RW_HEADS, RW_HEAD, RW_HEAD), f32),
        ],
        scratch_shapes=[pltpu.VMEM((N_PAIRS, RW_HEAD, PAIR), f32),
                        pltpu.VMEM((1, W), f32), pltpu.VMEM((1, W), f32), pltpu.VMEM((1, W), f32),
                        pltpu.VMEM((1, LANES), f32)],
        compiler_params=_cparams(("parallel", "arbitrary")),
        name="wkv_chunks",
    )(proj, proj, proj, proj, proj, *prev, *mu, w0, a0, k_k, k_a, *lora,
      r_k.reshape(1, W), ln_w.reshape(1, W), ln_b.reshape(1, W), s0)


def _post_kernel(final, gated, att_ref, y_ref, *refs):
    if gated:
        y = y_ref[0]
    else:
        y = (y_ref[0] * _silu(refs[0][0])).astype(bf16)
        refs = refs[1:]
    x_ref, gate_ref, w_ref, fg_ref, o_ref = refs
    cat = jnp.concatenate([att_ref[0], y], axis=1)
    out = jnp.dot(cat, w_ref[...], preferred_element_type=f32)
    x = x_ref[0] + gate_ref[0] * out
    if final:
        ms = jnp.mean(x * x, axis=-1, keepdims=True)
        x = x * lax.rsqrt(ms + NORM_EPS) * fg_ref[...]
    o_ref[0] = x


def _post(att, y_rw, proj, x, gate, w_out_bf, final_g, final, tm):
    G, R, _ = x.shape
    W = RWKV_WIDTH
    gated = proj is None
    gate_in = [] if gated else [pl.BlockSpec((1, tm, W), lambda g, m: (g, m, P_GR // W))]
    gate_arg = [] if gated else [proj]
    return pl.pallas_call(
        functools.partial(_post_kernel, final, gated),
        grid=(G, R // tm),
        in_specs=[
            pl.BlockSpec((1, tm, ATT_WIDTH), lambda g, m: (g, m, 0)),
            pl.BlockSpec((1, tm, W), lambda g, m: (g, m, 0)),
            *gate_in,
            pl.BlockSpec((1, tm, D_MODEL), lambda g, m: (g, m, 0)),
            _mod_spec(gate, tm),
            pl.BlockSpec((D_MODEL, D_MODEL), lambda g, m: (0, 0)),
            pl.BlockSpec((1, D_MODEL), lambda g, m: (0, 0)),
        ],
        out_specs=pl.BlockSpec((1, tm, D_MODEL), lambda g, m: (g, m, 0)),
        out_shape=jax.ShapeDtypeStruct((G, R, D_MODEL), f32),
        compiler_params=_cparams(("parallel", "parallel")),
        name="post_proj",
    )(att, y_rw, *gate_arg, x, gate, w_out_bf, final_g.reshape(1, D_MODEL))


def _arrange_w_in(w):
    pad = jnp.zeros((w.shape[0], P_WIDTH - IN_WIDTH), w.dtype)
    parts = [w[:, Q_OFF:KA_OFF], w[:, GA_OFF:GR_OFF], w[:, GR_OFF:IN_WIDTH], w[:, R_OFF:WD_OFF],
             w[:, KA_OFF:R_OFF], w[:, WD_OFF:GA_OFF], pad]
    return jnp.concatenate(parts, axis=1).astype(bf16)


def _shift_cols(t):
    return jnp.concatenate([t[..., P_R:P_KA], t[..., P_WA:P_WA + LORA_PAIR]], axis=-1)


def kernel(x_prompt, x_sample, cache_k, cache_v, state_wkv, state_shift, c_prompt, c_sample,
           norm_g, w_ada, b_ada, w_in, mu_shift, w0, w_decay, a0, w_iclr, k_k, k_a, r_k,
           ln_w, ln_b, sinks, w_out, final_g):
    Bp, Tp = x_prompt.shape[0], x_prompt.shape[1]
    Bd = x_sample.shape[0]
    W = RWKV_WIDTH

    n_c = Bp + Bd
    c_rows = -(-n_c // 16) * 16
    c_all = jnp.concatenate([c_prompt, c_sample, jnp.zeros((c_rows - n_c, D_MODEL), f32)], axis=0)
    mod = _ada(c_all, w_ada, b_ada)

    tab_p = _rope_tables(jnp.arange(Tp, dtype=jnp.int32))
    tab_s = _rope_tables(jnp.full((Bd,), PAST_LEN, jnp.int32))

    hp = x_prompt
    hs = x_sample.reshape(1, Bd, D_MODEL)
    s0_p = jnp.zeros((Bp, RW_HEADS, RW_HEAD, RW_HEAD), f32)
    shift0_p = [jnp.zeros((Bp, 1, w), f32) for w in (W, W, W, LORA_PAIR)]
    ck_all = cache_k.reshape(DEPTH, Bd, WINDOW, KV_WIDTH)
    cv_all = cache_v.reshape(DEPTH, Bd, WINDOW, KV_WIDTH)
    new_state_s = jnp.zeros(state_wkv.shape, f32)
    new_cache_s = [jnp.zeros(ck_all.shape, f32), jnp.zeros(cv_all.shape, f32)]
    outs = {k: [] for k in ("kp", "vp", "sp", "shp", "shs")}
    for l in range(DEPTH):
        final = l == DEPTH - 1
        w_bf = _arrange_w_in(w_in[l])
        w_out_bf = w_out[l].astype(bf16)
        mu_l = mu_shift[l]
        mu = [mu_l[0:W].reshape(1, W), mu_l[W:2 * W].reshape(1, W), mu_l[2 * W:3 * W].reshape(1, W),
              mu_l[3 * W:].reshape(1, LORA_PAIR)]
        vecs = [t[l].reshape(1, W) for t in (w0, a0, k_k, k_a)]
        wd_pad = jnp.concatenate([w_decay[l], jnp.zeros((ICLR_LORA, W), f32)], axis=0)
        wi_pad = jnp.concatenate([jnp.zeros((DECAY_LORA, W), f32), w_iclr[l]], axis=0)
        shift_p, scale_p, gate_p = (mod[l, :Bp, i * D_MODEL:(i + 1) * D_MODEL].reshape(Bp, 1, D_MODEL)
                                    for i in range(3))
        shift_s, scale_s, gate_s = (mod[l, Bp:n_c, i * D_MODEL:(i + 1) * D_MODEL].reshape(1, Bd, D_MODEL)
                                    for i in range(3))

        proj = _norm_proj(hp, norm_g[l], scale_p, shift_p, w_bf, tab_p, tm=PROJ_TM)
        att = _attn_prompt(proj, sinks[l])
        lora = [piece for wp in (wd_pad, wi_pad) for piece in _split_bf16(wp)]
        y_rw, s_t = _wkv_chunked(proj, shift0_p, mu, *vecs, lora, r_k[l], ln_w[l], ln_b[l], s0_p)
        hp = _post(att, y_rw, None, hp, gate_p, w_out_bf, final_g, final, tm=POST_TM)
        tail = proj[:, Tp - WINDOW:]
        outs["kp"].append(tail[..., P_KA:P_KA + KV_WIDTH].reshape(Bp, WINDOW, N_KV_HEADS, HEAD_DIM))
        outs["vp"].append(tail[..., P_VA:P_VA + KV_WIDTH].reshape(Bp, WINDOW, N_KV_HEADS, HEAD_DIM))
        outs["sp"].append(s_t)
        outs["shp"].append(_shift_cols(proj[:, Tp - 1]))

        proj = _norm_proj(hs, norm_g[l], scale_s, shift_s, w_bf, tab_s, tm=Bd)
        att, *new_cache_s = _attn_sample(proj, ck_all, cv_all, l, new_cache_s, sinks[l])
        sh = state_shift[l]
        prev = [sh[None, :, 0:W], sh[None, :, W:2 * W], sh[None, :, 2 * W:3 * W], sh[None, :, 3 * W:]]
        prep = _rwkv_prep(proj, prev, mu, *vecs, wd_pad, wi_pad, tm=Bd)
        prep = [t[:, None] for t in prep]
        y_rw, new_state_s = _wkv(*prep, r_k[l], ln_w[l], ln_b[l], state_wkv, l, new_state_s, tc=1)
        hs = _post(att, y_rw.reshape(1, Bd, W), proj, hs, gate_s, w_out_bf, final_g, final, tm=Bd)
        outs["shs"].append(_shift_cols(proj[0]))

    st = lambda k: jnp.stack(outs[k])
    return (hp, hs.reshape(Bd, 1, D_MODEL), st("kp"), st("vp"), st("sp"), st("shp"),
            *(t.reshape(cache_k.shape) for t in new_cache_s), new_state_s, st("shs"))
```

```python
import functools

import jax
import jax.numpy as jnp
from jax import lax
from jax.experimental import pallas as pl
from jax.experimental.pallas import tpu as pltpu

f32 = jnp.float32
bf16 = jnp.bfloat16

D_MODEL = 2048
DEPTH = 2
PAST_LEN = 16384
ATT_WIDTH = 1024
RWKV_WIDTH = 1024
HEAD_DIM = 64
N_Q_HEADS = 16
N_KV_HEADS = 4
GQA_GROUP = 4
KV_WIDTH = 256
WINDOW = 128
ROT_DIM = 16
ROPE_THETA = 500000.0
RW_HEAD = 64
RW_HEADS = 16
DECAY_LORA = 64
ICLR_LORA = 64
LORA_PAIR = DECAY_LORA + ICLR_LORA
NORM_EPS = 1e-5
GN_EPS = 64e-5
NEG_BIG = -1e30

Q_OFF = 0
KA_OFF = Q_OFF + ATT_WIDTH
VA_OFF = KA_OFF + KV_WIDTH
R_OFF = VA_OFF + KV_WIDTH
KR_OFF = R_OFF + RWKV_WIDTH
VR_OFF = KR_OFF + RWKV_WIDTH
WD_OFF = VR_OFF + RWKV_WIDTH
AD_OFF = WD_OFF + DECAY_LORA
GA_OFF = AD_OFF + ICLR_LORA
GR_OFF = GA_OFF + ATT_WIDTH
IN_WIDTH = GR_OFF + RWKV_WIDTH
SHIFT_DIM = GA_OFF - R_OFF

LANES = 128
P_Q = 0
P_GA = 1024
P_GR = 2048
P_R = 3072
P_KR = 4096
P_VR = 5120
P_KA = 6144
P_VA = 6400
P_WA = 6656
P_WIDTH = 7168
PROJ_TN = 1792
PROJ_TM = 1024
POST_TM = 512

VMEM_LIMIT = 56 * 1024 * 1024


_NN = (((1,), (0,)), ((), ()))
_NT = (((1,), (1,)), ((), ()))
_TN = (((0,), (0,)), ((), ()))


def _silu(x):
    return x * jax.nn.sigmoid(x)


def _cparams(sem):
    return pltpu.CompilerParams(dimension_semantics=sem, vmem_limit_bytes=VMEM_LIMIT)


def _split_bf16(x):
    hi = x.astype(bf16)
    return hi, (x - hi.astype(f32)).astype(bf16)


def _ada_kernel(c_ref, w_ref, b_ref, o_ref):
    ch, cl = _split_bf16(_silu(c_ref[...]))
    wh, wl = _split_bf16(w_ref[0])
    rows = ch.shape[0]
    both = jnp.dot(jnp.concatenate([ch, cl], axis=0), wh, preferred_element_type=f32)
    o_ref[0] = both[:rows] + both[rows:] + jnp.dot(ch, wl, preferred_element_type=f32) + b_ref[0]


def _ada(c_all, w_ada, b_ada):
    rows = c_all.shape[0]
    tn = 1536
    n_out = w_ada.shape[2]
    return pl.pallas_call(
        _ada_kernel,
        grid=(DEPTH, n_out // tn),
        in_specs=[
            pl.BlockSpec((rows, D_MODEL), lambda l, n: (0, 0)),
            pl.BlockSpec((1, D_MODEL, tn), lambda l, n: (l, 0, n)),
            pl.BlockSpec((1, 1, tn), lambda l, n: (l, 0, n)),
        ],
        out_specs=pl.BlockSpec((1, rows, tn), lambda l, n: (l, 0, n)),
        out_shape=jax.ShapeDtypeStruct((DEPTH, rows, n_out), f32),
        compiler_params=_cparams(("parallel", "parallel")),
        name="ada_mod",
    )(c_all, w_ada, b_ada.reshape(DEPTH, 1, n_out))


def _rope(x, tab):
    w = x.shape[1]
    reps = w // LANES
    cosf, up, dn = (jnp.concatenate([tab[i]] * reps, axis=1) for i in range(3))
    half = ROT_DIM // 2
    return x * cosf + pltpu.roll(x, w - half, 1) * up + pltpu.roll(x, half, 1) * dn


def _norm_proj_kernel(x_ref, g_ref, scale_ref, shift_ref, w_ref, tab_ref, o_ref, h_ref):
    n = pl.program_id(2)

    @pl.when(n == 0)
    def _():
        x = x_ref[0]
        ms = jnp.mean(x * x, axis=-1, keepdims=True)
        y = x * lax.rsqrt(ms + NORM_EPS) * g_ref[...]
        h_ref[...] = (y * (1.0 + scale_ref[0]) + shift_ref[0]).astype(bf16)

    tm, tn = o_ref.shape[1], o_ref.shape[2]
    q_tile, q_lo = divmod(P_Q, tn)
    k_tile, k_lo = divmod(P_KA, tn)
    assert q_lo + ATT_WIDTH <= tn and k_lo + KV_WIDTH <= tn and q_tile != k_tile
    rc = min(tm, 256)

    def rotated_tile(lo, width, scale):
        for i in range(tm // rc):
            rs = slice(i * rc, (i + 1) * rc)
            res = jnp.dot(h_ref[rs, :], w_ref[...], preferred_element_type=f32)
            parts = [res[:, :lo]] if lo else []
            parts.append(_rope(res[:, lo:lo + width], tab_ref[:, rs, :]) * scale)
            if lo + width < tn:
                parts.append(res[:, lo + width:])
            o_ref[0, rs, :] = jnp.concatenate(parts, axis=1) if len(parts) > 1 else parts[0]

    @pl.when(n == q_tile)
    def _():
        rotated_tile(q_lo, ATT_WIDTH, HEAD_DIM ** -0.5)

    @pl.when(n == k_tile)
    def _():
        rotated_tile(k_lo, KV_WIDTH, 1.0)

    @pl.when((n != q_tile) & (n != k_tile))
    def _():
        o_ref[0] = jnp.dot(h_ref[...], w_ref[...], preferred_element_type=f32)


def _mod_spec(mod, tm):
    if mod.shape[1] == 1:
        return pl.BlockSpec((1, 1, D_MODEL), lambda g, m, *_: (g, 0, 0))
    return pl.BlockSpec((1, tm, D_MODEL), lambda g, m, *_: (g, m, 0))


def _norm_proj(x, norm_g, scale, shift, w_bf, tab, tm):
    G, R, _ = x.shape
    return pl.pallas_call(
        _norm_proj_kernel,
        grid=(G, R // tm, P_WIDTH // PROJ_TN),
        in_specs=[
            pl.BlockSpec((1, tm, D_MODEL), lambda g, m, n: (g, m, 0)),
            pl.BlockSpec((1, D_MODEL), lambda g, m, n: (0, 0)),
            _mod_spec(scale, tm),
            _mod_spec(shift, tm),
            pl.BlockSpec((D_MODEL, PROJ_TN), lambda g, m, n: (0, n)),
            pl.BlockSpec((3, tm, LANES), lambda g, m, n: (0, m, 0)),
        ],
        out_specs=pl.BlockSpec((1, tm, PROJ_TN), lambda g, m, n: (g, m, n)),
        out_shape=jax.ShapeDtypeStruct((G, R, P_WIDTH), f32),
        scratch_shapes=[pltpu.VMEM((tm, D_MODEL), bf16)],
        compiler_params=_cparams(("parallel", "parallel", "arbitrary")),
        name="norm_proj",
    )(x, norm_g.reshape(1, D_MODEL), scale, shift, w_bf, tab)


def _rope_tables(pos):
    half = ROT_DIM // 2
    inv_freq = ROPE_THETA ** (-jnp.arange(half, dtype=f32) * (2.0 / ROT_DIM))
    ang = pos.astype(f32)[:, None] * inv_freq[None, :]
    cos, sin = jnp.cos(ang), jnp.sin(ang)
    t = pos.shape[0]
    z8 = jnp.zeros((t, half), f32)
    rest = HEAD_DIM - ROT_DIM
    cos64 = jnp.concatenate([cos, cos, jnp.ones((t, rest), f32)], axis=1)
    up64 = jnp.concatenate([-sin, z8, jnp.zeros((t, rest), f32)], axis=1)
    dn64 = jnp.concatenate([z8, sin, jnp.zeros((t, rest), f32)], axis=1)
    rep = LANES // HEAD_DIM
    return jnp.stack([jnp.tile(a, (1, rep)) for a in (cos64, up64, dn64)])


ATTN_BLOCKS = 4


def _attn_prompt_kernel(sinks_ref, q_ref, kc_ref, kp_ref, vc_ref, vp_ref, ga_ref, o_ref):
    n = pl.program_id(1)
    wn = WINDOW
    half = HEAD_DIM
    k_t_all = jnp.concatenate([kp_ref[0], kc_ref[0]], axis=0).T.astype(bf16)
    v_all = jnp.concatenate([vp_ref[0], vc_ref[0]], axis=0)

    qi = lax.broadcasted_iota(jnp.int32, (2 * wn, 2 * wn), 0) & (wn - 1)
    kj = lax.broadcasted_iota(jnp.int32, (2 * wn, 2 * wn), 1)
    rel = wn + qi - kj
    band = (rel >= 0) & (rel <= wn)
    top = lax.broadcasted_iota(jnp.int32, (2 * wn, 1), 0) < wn
    lo = lax.broadcasted_iota(jnp.int32, (2 * wn, LANES), 1) < half
    zeros_k = jnp.zeros((half, 2 * wn), bf16)
    ones_lo = jnp.where(lo, 1.0, 0.0).astype(bf16)
    ones_hi = jnp.where(lo, 0.0, 1.0).astype(bf16)

    for sb in range(q_ref.shape[1] // wn):
        rows = slice(sb * wn, (sb + 1) * wn)
        q = q_ref[0, rows, :].astype(bf16)
        ga = ga_ref[0, rows, :]
        k_t = k_t_all[:, sb * wn:(sb + 2) * wn]
        vcat = v_all[sb * wn:(sb + 2) * wn]
        mask = band & ((kj >= wn) | (n > 0)) if sb == 0 else band
        for j in range(N_KV_HEADS // 2):
            vblk = vcat[:, j * LANES:(j + 1) * LANES]
            vswap = pltpu.roll(vblk, half, 1)
            for g in (2 * j, 2 * j + 1):
                own, other = (vblk, vswap) if g % 2 == 0 else (vswap, vblk)
                v_lo = jnp.where(lo, own, 0.0).astype(bf16)
                v_hi = jnp.where(lo, 0.0, other).astype(bf16)
                rhs_pv = jnp.concatenate([jnp.concatenate([v_lo, ones_lo], axis=1),
                                          jnp.concatenate([v_hi, ones_hi], axis=1)], axis=0)
                kg = k_t[g * half:(g + 1) * half, :]
                rhs_qk = jnp.concatenate([jnp.concatenate([kg, zeros_k], axis=0),
                                          jnp.concatenate([zeros_k, kg], axis=0)], axis=1)
                b0, b1 = 2 * g, 2 * g + 1
                qg = jnp.concatenate([q[:, b0 * LANES:(b0 + 1) * LANES], q[:, b1 * LANES:(b1 + 1) * LANES]],
                                     axis=0)
                s_all = jnp.dot(qg, rhs_qk, preferred_element_type=f32)
                ps, es = [], []
                for hh in range(2):
                    s = jnp.where(mask, s_all[:, hh * 2 * wn:(hh + 1) * 2 * wn], NEG_BIG)
                    sink = jnp.where(top, sinks_ref[2 * b0 + hh], sinks_ref[2 * b1 + hh])
                    m = jnp.maximum(jnp.max(s, axis=-1, keepdims=True), sink)
                    ps.append(jnp.exp(s - m).astype(bf16))
                    es.append(jnp.exp(sink - m))
                res = jnp.dot(jnp.concatenate(ps, axis=1), rhs_pv, preferred_element_type=f32)
                out = res[:, :LANES] / (res[:, LANES:] + jnp.where(lo, es[0], es[1]))
                for i, blk in enumerate((b0, b1)):
                    sl = slice(blk * LANES, (blk + 1) * LANES)
                    o_ref[0, rows, sl] = (out[i * wn:(i + 1) * wn] * _silu(ga[:, sl])).astype(o_ref.dtype)


def _attn_prompt(proj, sinks):
    B, T, _ = proj.shape
    rows = ATTN_BLOCKS * WINDOW
    kvb = KV_WIDTH
    prev = lambda b, n: jnp.maximum(n * ATTN_BLOCKS - 1, 0)
    return pl.pallas_call(
        _attn_prompt_kernel,
        grid=(B, T // rows),
        in_specs=[
            pl.BlockSpec(memory_space=pltpu.SMEM),
            pl.BlockSpec((1, rows, ATT_WIDTH), lambda b, n: (b, n, P_Q // ATT_WIDTH)),
            pl.BlockSpec((1, rows, kvb), lambda b, n: (b, n, P_KA // kvb)),
            pl.BlockSpec((1, WINDOW, kvb), lambda b, n: (b, prev(b, n), P_KA // kvb)),
            pl.BlockSpec((1, rows, kvb), lambda b, n: (b, n, P_VA // kvb)),
            pl.BlockSpec((1, WINDOW, kvb), lambda b, n: (b, prev(b, n), P_VA // kvb)),
            pl.BlockSpec((1, rows, ATT_WIDTH), lambda b, n: (b, n, P_GA // ATT_WIDTH)),
        ],
        out_specs=pl.BlockSpec((1, rows, ATT_WIDTH), lambda b, n: (b, n, 0)),
        out_shape=jax.ShapeDtypeStruct((B, T, ATT_WIDTH), bf16),
        compiler_params=_cparams(("parallel", "arbitrary")),
        name="attn_prompt",
    )(sinks, proj, proj, proj, proj, proj, proj)


SAMPLE_ROWS = 16


def _attn_sample_kernel(sinks_ref, q_ref, kn_ref, vn_ref, ga_ref, ck_ref, cv_ref, *refs):
    o_ref, nk_ref, nv_ref = refs[-3:]
    rb = q_ref.shape[1]
    q_all, kn_all, vn_all = q_ref[0], kn_ref[0], vn_ref[0]
    work = [(r, g) for r in range(rb) for g in range(N_KV_HEADS)]
    span = lambda g: slice(g * HEAD_DIM, (g + 1) * HEAD_DIM)
    ckb = [ck_ref[0, r].astype(bf16) for r in range(rb)]
    cvb = [cv_ref[0, r].astype(bf16) for r in range(rb)]
    qg = {(r, g): jnp.concatenate(
        [q_all[r:r + 1, (g * GQA_GROUP + i) * HEAD_DIM:(g * GQA_GROUP + i + 1) * HEAD_DIM]
         for i in range(GQA_GROUP)], axis=0) for r, g in work}
    s = {rg: lax.dot_general(qg[rg].astype(bf16), ckb[rg[0]][:, span(rg[1])], _NT,
                             preferred_element_type=f32) for rg in work}
    sink = {g: sinks_ref[g * GQA_GROUP:(g + 1) * GQA_GROUP, :] for g in range(N_KV_HEADS)}
    s_new = {(r, g): jnp.sum(qg[r, g] * kn_all[r:r + 1, span(g)], axis=-1, keepdims=True) for r, g in work}
    m = {rg: jnp.maximum(jnp.maximum(jnp.max(s[rg], axis=-1, keepdims=True), s_new[rg]), sink[rg[1]])
         for rg in work}
    p = {rg: jnp.exp(s[rg] - m[rg]) for rg in work}
    p_new = {rg: jnp.exp(s_new[rg] - m[rg]) for rg in work}
    den = {rg: jnp.sum(p[rg], axis=-1, keepdims=True) + p_new[rg] + jnp.exp(sink[rg[1]] - m[rg]) for rg in work}
    pv = {(r, g): jnp.dot(p[r, g].astype(bf16), cvb[r][:, span(g)], preferred_element_type=f32) for r, g in work}
    outs = {}
    for r, g in work:
        o = (pv[r, g] + p_new[r, g] * vn_all[r:r + 1, span(g)]) / den[r, g]
        outs[r, g] = [o[i:i + 1, :] for i in range(GQA_GROUP)]
    att = jnp.concatenate(
        [jnp.concatenate([h for g in range(N_KV_HEADS) for h in outs[r, g]], axis=1) for r in range(rb)],
        axis=0)
    o_ref[0] = (att * _silu(ga_ref[0])).astype(o_ref.dtype)
    last = lax.broadcasted_iota(jnp.int32, (WINDOW, KV_WIDTH), 0) == WINDOW - 1
    for r in range(rb):
        nk_ref[0, r] = jnp.where(last, kn_all[r:r + 1], pltpu.roll(ck_ref[0, r], WINDOW - 1, 0))
        nv_ref[0, r] = jnp.where(last, vn_all[r:r + 1], pltpu.roll(cv_ref[0, r], WINDOW - 1, 0))


def _attn_sample(proj, ck_all, cv_all, layer, stacked, sinks):
    Bd = proj.shape[1]
    rb = SAMPLE_ROWS
    col = lambda w, off: pl.BlockSpec((1, rb, w), lambda i: (0, i, off // w))
    cache = pl.BlockSpec((1, rb, WINDOW, KV_WIDTH), lambda i: (layer, i, 0, 0))
    args = [sinks.reshape(N_Q_HEADS, 1), proj, proj, proj, proj, ck_all, cv_all, *stacked]
    hbm = pl.BlockSpec(memory_space=pl.ANY)
    return pl.pallas_call(
        _attn_sample_kernel,
        grid=(Bd // rb,),
        in_specs=[
            pl.BlockSpec((N_Q_HEADS, 1), lambda i: (0, 0)),
            col(ATT_WIDTH, P_Q), col(KV_WIDTH, P_KA), col(KV_WIDTH, P_VA), col(ATT_WIDTH, P_GA),
            cache, cache, hbm, hbm,
        ],
        out_specs=[pl.BlockSpec((1, rb, ATT_WIDTH), lambda i: (0, i, 0)), cache, cache],
        out_shape=[
            jax.ShapeDtypeStruct((1, Bd, ATT_WIDTH), bf16),
            jax.ShapeDtypeStruct(ck_all.shape, f32),
            jax.ShapeDtypeStruct(cv_all.shape, f32),
        ],
        input_output_aliases={len(args) - 2: 1, len(args) - 1: 2},
        compiler_params=_cparams(("parallel",)),
        name="attn_sample",
    )(*args)


def _rwkv_prep_kernel(r_ref, kr_ref, vr_ref, wa_ref, pr_ref, pkr_ref, pvr_ref, pwa_ref,
                      mu_r_ref, mu_kr_ref, mu_vr_ref, mu_wa_ref, w0_ref, a0_ref, kk_ref, ka_ref,
                      wd_ref, wi_ref, ro_ref, wo_ref, ko_ref, vo_ref, kko_ref, ao_ref):
    def mixed(cur_ref, prev_ref, mu_ref):
        cur = cur_ref[0]
        return cur + (prev_ref[0] - cur) * mu_ref[...]

    r = mixed(r_ref, pr_ref, mu_r_ref)
    kr = mixed(kr_ref, pkr_ref, mu_kr_ref)
    vr = mixed(vr_ref, pvr_ref, mu_vr_ref)
    wa = mixed(wa_ref, pwa_ref, mu_wa_ref)
    hi = lax.Precision.HIGHEST
    z = w0_ref[...] + jnp.dot(jnp.tanh(wa), wd_ref[...], precision=hi, preferred_element_type=f32)
    nz = -z
    softplus = jnp.maximum(nz, 0.0) + jnp.log1p(jnp.exp(-jnp.abs(nz)))
    w_log = -softplus - 0.5
    a = jax.nn.sigmoid(a0_ref[...] + jnp.dot(wa, wi_ref[...], precision=hi, preferred_element_type=f32))
    outs = (r, -jnp.exp(w_log), kr * (1.0 + (a - 1.0) * ka_ref[...]), vr, kr * kk_ref[...], a)
    for o_ref, val in zip((ro_ref, wo_ref, ko_ref, vo_ref, kko_ref, ao_ref), outs):
        for h in range(RW_HEADS):
            o_ref[:, h, :] = val[:, h * RW_HEAD:(h + 1) * RW_HEAD]


def _rwkv_prep(proj, prev, mu, w0, a0, k_k, k_a, wd_pad, wi_pad, tm):
    G, R, _ = proj.shape
    assert G == 1
    W = RWKV_WIDTH
    cur_specs = [
        pl.BlockSpec((1, tm, W), lambda g, m: (g, m, P_R // W)),
        pl.BlockSpec((1, tm, W), lambda g, m: (g, m, P_KR // W)),
        pl.BlockSpec((1, tm, W), lambda g, m: (g, m, P_VR // W)),
        pl.BlockSpec((1, tm, LANES), lambda g, m: (g, m, P_WA // LANES)),
    ]
    prev_specs = [
        pl.BlockSpec((1, tm, W), lambda g, m: (g, m, 0)),
        pl.BlockSpec((1, tm, W), lambda g, m: (g, m, 0)),
        pl.BlockSpec((1, tm, W), lambda g, m: (g, m, 0)),
        pl.BlockSpec((1, tm, LANES), lambda g, m: (g, m, 0)),
    ]
    prev_args = list(prev)
    vec = lambda w: pl.BlockSpec((1, w), lambda g, m: (0, 0))
    lora = pl.BlockSpec((LORA_PAIR, W), lambda g, m: (0, 0))
    out_spec = pl.BlockSpec((tm, RW_HEADS, RW_HEAD), lambda g, m: (m, 0, 0))
    return pl.pallas_call(
        _rwkv_prep_kernel,
        grid=(G, R // tm),
        in_specs=cur_specs + prev_specs + [vec(W), vec(W), vec(W), vec(LANES),
                                           vec(W), vec(W), vec(W), vec(W), lora, lora],
        out_specs=[out_spec] * 6,
        out_shape=[jax.ShapeDtypeStruct((R, RW_HEADS, RW_HEAD), f32)] * 6,
        compiler_params=_cparams(("parallel", "arbitrary")),
        name="rwkv_prep",
    )(proj, proj, proj, proj, *prev_args, *mu, w0, a0, k_k, k_a, wd_pad, wi_pad)


def _wkv_kernel(tc, r_ref, w_ref, k_ref, v_ref, kk_ref, a_ref, rk_ref, lnw_ref, lnb_ref, s0_ref,
                *refs):
    y_ref, s_ref = refs[-2:]

    @pl.when(pl.program_id(1) == 0)
    def _():
        s_ref[...] = s0_ref[...]

    eye = (lax.broadcasted_iota(jnp.int32, (RW_HEAD, RW_HEAD), 0)
           == lax.broadcasted_iota(jnp.int32, (RW_HEAD, RW_HEAD), 1)).astype(f32)
    rk = rk_ref[...]
    lnw = lnw_ref[...]
    lnb = lnb_ref[...]

    def step(t, carry):
        r, lw, k, v, kkr, a = (ref[0, t] for ref in (r_ref, w_ref, k_ref, v_ref, kk_ref, a_ref))
        w = jnp.exp(lw)
        norm = jnp.sqrt(jnp.sum(kkr * kkr, axis=-1, keepdims=True))
        kk = kkr / jnp.maximum(norm, 1e-12)
        b = kk * a
        bonus = jnp.sum(r * k * rk, axis=-1, keepdims=True) * v
        heads = range(RW_HEADS)
        row = lambda x, h: x[h:h + 1]
        s_in = [s_ref[0, 0, h] for h in heads]
        sa = [jnp.sum(s_in[h] * (-row(kk, h)), axis=-1, keepdims=True) for h in heads]
        v_col = [jnp.sum(eye * row(v, h), axis=-1, keepdims=True) for h in heads]
        s_out = [s_in[h] * row(w, h) + sa[h] * row(b, h) + v_col[h] * row(k, h) for h in heads]
        for h in heads:
            s_ref[0, 0, h] = s_out[h]
        ys = [jnp.sum(s_out[h] * row(r, h), axis=-1, keepdims=True) for h in heads]
        mus = [jnp.mean(y, axis=0, keepdims=True) for y in ys]
        var = [jnp.mean(jnp.square(y - mu), axis=0, keepdims=True) for y, mu in zip(ys, mus)]
        rows = [jnp.sum((ys[h] - mus[h]) * lax.rsqrt(var[h] + GN_EPS) * eye, axis=0, keepdims=True)
                for h in heads]
        y_ref[0, t] = jnp.concatenate(rows, axis=0) * lnw + lnb + bonus
        return carry

    lax.fori_loop(0, tc, step, 0)


def _wkv(r, w, k, v, kk, a, r_k, ln_w, ln_b, s0_all, layer, stacked, tc):
    B, T = r.shape[0], r.shape[1]
    hd = (RW_HEADS, RW_HEAD)
    seq = pl.BlockSpec((1, tc, *hd), lambda b, c: (b, c, 0, 0))
    par = pl.BlockSpec(hd, lambda b, c: (0, 0))
    state = pl.BlockSpec((1, 1, RW_HEADS, RW_HEAD, RW_HEAD), lambda b, c: (layer, b, 0, 0, 0))
    args = [r, w, k, v, kk, a, r_k, ln_w.reshape(hd), ln_b.reshape(hd), s0_all, stacked]
    y, s_new = pl.pallas_call(
        functools.partial(_wkv_kernel, tc),
        grid=(B, T // tc),
        in_specs=[seq] * 6 + [par] * 3 + [state, pl.BlockSpec(memory_space=pl.ANY)],
        out_specs=[seq, state],
        out_shape=[
            jax.ShapeDtypeStruct((B, T, *hd), f32),
            jax.ShapeDtypeStruct(s0_all.shape, f32),
        ],
        input_output_aliases={len(args) - 1: 1},
        compiler_params=_cparams(("parallel", "arbitrary")),
        name="wkv_steps",
    )(*args)
    return y.reshape(B, T, RWKV_WIDTH), s_new


CHUNK = 64
PAIR = 2 * RW_HEAD
N_PAIRS = RW_HEADS // 2


def _mmb(a, b, dims=_NN):
    return lax.dot_general(a.astype(bf16), b.astype(bf16), dims, preferred_element_type=f32)


def _lora_dot(x, wh_ref, wl_ref):
    xh, xl = _split_bf16(x)
    rows = x.shape[0]
    both = jnp.dot(jnp.concatenate([xh, xl], axis=0), wh_ref[...], preferred_element_type=f32)
    return both[:rows] + both[rows:] + jnp.dot(xh, wl_ref[...], preferred_element_type=f32)


CHUNKS_PER_STEP = 4


def _wkv_chunk_kernel(r_ref, kr_ref, vr_ref, wa_ref, gr_ref, pr_ref, pkr_ref, pvr_ref, pwa_ref,
                      mu_r_ref, mu_kr_ref, mu_vr_ref, mu_wa_ref, w0_ref, a0_ref, kk_ref, ka_ref,
                      wdh_ref, wdl_ref, wih_ref, wil_ref, rk_ref, lnw_ref, lnb_ref, s0_ref,
                      y_ref, s_ref, sp_ref, cr_ref, ckr_ref, cvr_ref, cwa_ref):
    C = CHUNK
    rows_blk = r_ref.shape[1]
    c = pl.program_id(1)

    @pl.when(c == 0)
    def _():
        for p in range(N_PAIRS):
            sp_ref[p] = jnp.concatenate([s0_ref[0, 2 * p], s0_ref[0, 2 * p + 1]], axis=1)
        for carry, first in ((cr_ref, pr_ref), (ckr_ref, pkr_ref), (cvr_ref, pvr_ref), (cwa_ref, pwa_ref)):
            carry[...] = first[0]

    def mixed(cur_ref, carry_ref, mu_ref):
        cur = cur_ref[0]
        first = lax.broadcasted_iota(jnp.int32, cur.shape, 0) == 0
        prev = jnp.where(first, carry_ref[...], pltpu.roll(cur, 1, 0))
        carry_ref[...] = cur[rows_blk - 1:rows_blk, :]
        return cur + (prev - cur) * mu_ref[...]

    r_blk = mixed(r_ref, cr_ref, mu_r_ref)
    kr_blk = mixed(kr_ref, ckr_ref, mu_kr_ref)
    v_blk = mixed(vr_ref, cvr_ref, mu_vr_ref)
    wa = mixed(wa_ref, cwa_ref, mu_wa_ref)
    nz = -(w0_ref[...] + _lora_dot(jnp.tanh(wa), wdh_ref, wdl_ref))
    softplus = jnp.maximum(nz, 0.0) + jnp.log1p(jnp.exp(-jnp.abs(nz)))
    lw_blk = -jnp.exp(-softplus - 0.5)
    icl_blk = jax.nn.sigmoid(a0_ref[...] + _lora_dot(wa, wih_ref, wil_ref))
    k_blk = kr_blk * (1.0 + (icl_blk - 1.0) * ka_ref[...])
    kkr_blk = kr_blk * kk_ref[...]

    row = lax.broadcasted_iota(jnp.int32, (PAIR, PAIR), 0)
    col = lax.broadcasted_iota(jnp.int32, (PAIR, PAIR), 1)
    tril = row >= col
    stril = row > col
    tril2 = jnp.concatenate([tril, tril], axis=1)
    eye = (row == col).astype(f32)
    lane_lo = lax.broadcasted_iota(jnp.int32, (C, PAIR), 1) < RW_HEAD
    pairs = range(N_PAIRS)
    sls = [slice(p * PAIR, (p + 1) * PAIR) for p in pairs]

    def bd(x):
        zero = jnp.zeros_like(x)
        return jnp.concatenate([jnp.where(lane_lo, x, zero), jnp.where(lane_lo, zero, x)], axis=0)

    def head_sums(x):
        lo_sum = jnp.sum(jnp.where(lane_lo, x, 0.0), axis=-1, keepdims=True)
        hi_sum = jnp.sum(jnp.where(lane_lo, 0.0, x), axis=-1, keepdims=True)
        return jnp.where(lane_lo, lo_sum, hi_sum)

    state = [sp_ref[p] for p in pairs]
    for j in range(rows_blk // C):
        rs = slice(j * C, (j + 1) * C)
        lw = lw_blk[rs]
        width = lw.shape[1]
        lw_a = lw.astype(bf16)
        rest = lw - lw_a.astype(f32)
        lw_b = rest.astype(bf16)
        lw_c = (rest - lw_b.astype(f32)).astype(bf16)
        g3 = jnp.dot(tril[:C, :C].astype(bf16), jnp.concatenate([lw_a, lw_b, lw_c], axis=1),
                     preferred_element_type=f32)
        g = g3[:, :width] + g3[:, width:2 * width] + g3[:, 2 * width:]
        e_g = jnp.exp(g)
        e_ng = jnp.exp(-g)
        e_gm = jnp.exp(g - lw)
        e_end = e_g[C - 1:C, :]

        ins = [[t[rs, sl] for t in (r_blk, k_blk, v_blk, kkr_blk, icl_blk)] for sl in sls]
        norms = [jnp.sqrt(head_sums(x[3] * x[3])) for x in ins]
        at, rt, bt, kt, vb = ([] for _ in range(5))
        for (r, k, v, kkr, icl), norm, sl in zip(ins, norms, sls):
            kk = kkr / jnp.maximum(norm, 1e-12)
            b = kk * icl
            at.append(bd((-kk * e_gm[:, sl]).astype(bf16)))
            rt.append(bd((r * e_g[:, sl]).astype(bf16)))
            bt.append(bd((b * e_ng[:, sl]).astype(bf16)))
            kt.append(bd((k * e_ng[:, sl]).astype(bf16)))
            vb.append(bd(v.astype(bf16)))

        gram = [_mmb(jnp.concatenate([at[p], rt[p]], axis=0), jnp.concatenate([bt[p], kt[p]], axis=0), _NT)
                for p in pairs]
        lmat = [jnp.where(stril, gm[:PAIR, :PAIR], 0.0) for gm in gram]
        mv = [_mmb(jnp.where(stril, gram[p][:PAIR, PAIR:], 0.0), vb[p]) for p in pairs]
        lower = [jnp.where(tril2, gm[PAIR:, :], 0.0).astype(bf16) for gm in gram]

        tinv = [eye + lm for lm in lmat]
        pw = [_mmb(lm, lm) for lm in lmat]
        for _ in range(4):
            z = [_mmb(jnp.concatenate([x.astype(bf16), t.astype(bf16)], axis=0), x)
                 for t, x in zip(tinv, pw)]
            pw = [zz[:PAIR] for zz in z]
            tinv = [t + zz[PAIR:] for t, zz in zip(tinv, z)]
        tinv = [t + _mmb(t, x) for t, x in zip(tinv, pw)]

        wx = [_mmb(tinv[p], jnp.concatenate([at[p], mv[p].astype(bf16)], axis=1)) for p in pairs]
        uy0 = [_mmb(jnp.concatenate([wx[p][:, :PAIR].astype(bf16), rt[p]], axis=0),
                    bd(state[p].astype(bf16)), _NT) for p in pairs]
        uv = [jnp.concatenate([(uy0[p][:PAIR] + wx[p][:, PAIR:]).astype(bf16), vb[p]], axis=0)
              for p in pairs]
        ys = [uy0[p][PAIR:] + _mmb(lower[p], uv[p]) for p in pairs]
        s_add = [_mmb(uv[p], jnp.concatenate([bt[p], kt[p]], axis=0), _TN) for p in pairs]
        state = [(state[p] + s_add[p][:RW_HEAD] + s_add[p][RW_HEAD:]) * e_end[:, sls[p]] for p in pairs]

        ys = [y[:C] + y[C:] for y in ys]
        mus = [head_sums(y) * (1.0 / RW_HEAD) for y in ys]
        ds = [y - mu for y, mu in zip(ys, mus)]
        var = [head_sums(d * d) * (1.0 / RW_HEAD) for d in ds]
        bonus = [head_sums(x[0] * x[1] * rk_ref[:, sl]) * x[2] for x, sl in zip(ins, sls)]
        for p in pairs:
            sl = sls[p]
            y_rw = ds[p] * lax.rsqrt(var[p] + GN_EPS) * lnw_ref[:, sl] + lnb_ref[:, sl] + bonus[p]
            y_ref[0, rs, sl] = (y_rw * _silu(gr_ref[0, rs, sl])).astype(y_ref.dtype)

    for p in pairs:
        sp_ref[p] = state[p]

    @pl.when(c == pl.num_programs(1) - 1)
    def _():
        for p in range(N_PAIRS):
            s_ref[0, 2 * p] = state[p][:, :RW_HEAD]
            s_ref[0, 2 * p + 1] = state[p][:, RW_HEAD:]


def _wkv_chunked(proj, prev, mu, w0, a0, k_k, k_a, lora, r_k, ln_w, ln_b, s0):
    B, T, _ = proj.shape
    W = RWKV_WIDTH
    rows = CHUNKS_PER_STEP * CHUNK
    col = lambda w, off: pl.BlockSpec((1, rows, w), lambda b, c: (b, c, off // w))
    first = lambda w: pl.BlockSpec((1, 1, w), lambda b, c: (b, 0, 0))
    vec = lambda w: pl.BlockSpec((1, w), lambda b, c: (0, 0))
    lora_spec = pl.BlockSpec((LORA_PAIR, W), lambda b, c: (0, 0))
    state = pl.BlockSpec((1, RW_HEADS, RW_HEAD, RW_HEAD), lambda b, c: (b, 0, 0, 0))
    return pl.pallas_call(
        _wkv_chunk_kernel,
        grid=(B, T // rows),
        in_specs=[col(W, P_R), col(W, P_KR), col(W, P_VR), col(LANES, P_WA), col(W, P_GR),
                  first(W), first(W), first(W), first(LANES),
                  vec(W), vec(W), vec(W), vec(LANES), vec(W), vec(W), vec(W), vec(W),
                  lora_spec, lora_spec, lora_spec, lora_spec, vec(W), vec(W), vec(W), state],
        out_specs=[col(W, 0), state],
        out_shape=[
            jax.ShapeDtypeStruct((B, T, W), bf16),
            jax.ShapeDtypeStruct((B, RW_HEADS, RW_HEAD, RW_HEAD), f32),
        ],
        scratch_shapes=[pltpu.VMEM((N_PAIRS, RW_HEAD, PAIR), f32),
                        pltpu.VMEM((1, W), f32), pltpu.VMEM((1, W), f32), pltpu.VMEM((1, W), f32),
                        pltpu.VMEM((1, LANES), f32)],
        compiler_params=_cparams(("parallel", "arbitrary")),
        name="wkv_chunks",
    )(proj, proj, proj, proj, proj, *prev, *mu, w0, a0, k_k, k_a, *lora,
      r_k.reshape(1, W), ln_w.reshape(1, W), ln_b.reshape(1, W), s0)


def _post_kernel(final, gated, att_ref, y_ref, *refs):
    if gated:
        y = y_ref[0]
    else:
        y = (y_ref[0] * _silu(refs[0][0])).astype(bf16)
        refs = refs[1:]
    x_ref, gate_ref, w_ref, fg_ref, o_ref = refs
    cat = jnp.concatenate([att_ref[0], y], axis=1)
    out = jnp.dot(cat, w_ref[...], preferred_element_type=f32)
    x = x_ref[0] + gate_ref[0] * out
    if final:
        ms = jnp.mean(x * x, axis=-1, keepdims=True)
        x = x * lax.rsqrt(ms + NORM_EPS) * fg_ref[...]
    o_ref[0] = x


def _post(att, y_rw, proj, x, gate, w_out_bf, final_g, final, tm):
    G, R, _ = x.shape
    W = RWKV_WIDTH
    gated = proj is None
    gate_in = [] if gated else [pl.BlockSpec((1, tm, W), lambda g, m: (g, m, P_GR // W))]
    gate_arg = [] if gated else [proj]
    return pl.pallas_call(
        functools.partial(_post_kernel, final, gated),
        grid=(G, R // tm),
        in_specs=[
            pl.BlockSpec((1, tm, ATT_WIDTH), lambda g, m: (g, m, 0)),
            pl.BlockSpec((1, tm, W), lambda g, m: (g, m, 0)),
            *gate_in,
            pl.BlockSpec((1, tm, D_MODEL), lambda g, m: (g, m, 0)),
            _mod_spec(gate, tm),
            pl.BlockSpec((D_MODEL, D_MODEL), lambda g, m: (0, 0)),
            pl.BlockSpec((1, D_MODEL), lambda g, m: (0, 0)),
        ],
        out_specs=pl.BlockSpec((1, tm, D_MODEL), lambda g, m: (g, m, 0)),
        out_shape=jax.ShapeDtypeStruct((G, R, D_MODEL), f32),
        compiler_params=_cparams(("parallel", "parallel")),
        name="post_proj",
    )(att, y_rw, *gate_arg, x, gate, w_out_bf, final_g.reshape(1, D_MODEL))


def _arrange_w_in(w):
    pad = jnp.zeros((w.shape[0], P_WIDTH - IN_WIDTH), w.dtype)
    parts = [w[:, Q_OFF:KA_OFF], w[:, GA_OFF:GR_OFF], w[:, GR_OFF:IN_WIDTH], w[:, R_OFF:WD_OFF],
             w[:, KA_OFF:R_OFF], w[:, WD_OFF:GA_OFF], pad]
    return jnp.concatenate(parts, axis=1).astype(bf16)


def _shift_cols(t):
    return jnp.concatenate([t[..., P_R:P_KA], t[..., P_WA:P_WA + LORA_PAIR]], axis=-1)


def kernel(x_prompt, x_sample, cache_k, cache_v, state_wkv, state_shift, c_prompt, c_sample,
           norm_g, w_ada, b_ada, w_in, mu_shift, w0, w_decay, a0, w_iclr, k_k, k_a, r_k,
           ln_w, ln_b, sinks, w_out, final_g):
    Bp, Tp = x_prompt.shape[0], x_prompt.shape[1]
    Bd = x_sample.shape[0]
    W = RWKV_WIDTH

    n_c = Bp + Bd
    c_rows = -(-n_c // 16) * 16
    c_all = jnp.concatenate([c_prompt, c_sample, jnp.zeros((c_rows - n_c, D_MODEL), f32)], axis=0)
    mod = _ada(c_all, w_ada, b_ada)

    tab_p = _rope_tables(jnp.arange(Tp, dtype=jnp.int32))
    tab_s = _rope_tables(jnp.full((Bd,), PAST_LEN, jnp.int32))

    hp = x_prompt
    hs = x_sample.reshape(1, Bd, D_MODEL)
    s0_p = jnp.zeros((Bp, RW_HEADS, RW_HEAD, RW_HEAD), f32)
    shift0_p = [jnp.zeros((Bp, 1, w), f32) for w in (W, W, W, LORA_PAIR)]
    ck_all = cache_k.reshape(DEPTH, Bd, WINDOW, KV_WIDTH)
    cv_all = cache_v.reshape(DEPTH, Bd, WINDOW, KV_WIDTH)
    new_state_s = jnp.zeros(state_wkv.shape, f32)
    new_cache_s = [jnp.zeros(ck_all.shape, f32), jnp.zeros(cv_all.shape, f32)]
    outs = {k: [] for k in ("kp", "vp", "sp", "shp", "shs")}
    for l in range(DEPTH):
        final = l == DEPTH - 1
        w_bf = _arrange_w_in(w_in[l])
        w_out_bf = w_out[l].astype(bf16)
        mu_l = mu_shift[l]
        mu = [mu_l[0:W].reshape(1, W), mu_l[W:2 * W].reshape(1, W), mu_l[2 * W:3 * W].reshape(1, W),
              mu_l[3 * W:].reshape(1, LORA_PAIR)]
        vecs = [t[l].reshape(1, W) for t in (w0, a0, k_k, k_a)]
        wd_pad = jnp.concatenate([w_decay[l], jnp.zeros((ICLR_LORA, W), f32)], axis=0)
        wi_pad = jnp.concatenate([jnp.zeros((DECAY_LORA, W), f32), w_iclr[l]], axis=0)
        shift_p, scale_p, gate_p = (mod[l, :Bp, i * D_MODEL:(i + 1) * D_MODEL].reshape(Bp, 1, D_MODEL)
                                    for i in range(3))
        shift_s, scale_s, gate_s = (mod[l, Bp:n_c, i * D_MODEL:(i + 1) * D_MODEL].reshape(1, Bd, D_MODEL)
                                    for i in range(3))

        proj = _norm_proj(hp, norm_g[l], scale_p, shift_p, w_bf, tab_p, tm=PROJ_TM)
        att = _attn_prompt(proj, sinks[l])
        lora = [piece for wp in (wd_pad, wi_pad) for piece in _split_bf16(wp)]
        y_rw, s_t = _wkv_chunked(proj, shift0_p, mu, *vecs, lora, r_k[l], ln_w[l], ln_b[l], s0_p)
        hp = _post(att, y_rw, None, hp, gate_p, w_out_bf, final_g, final, tm=POST_TM)
        tail = proj[:, Tp - WINDOW:]
        outs["kp"].append(tail[..., P_KA:P_KA + KV_WIDTH].reshape(Bp, WINDOW, N_KV_HEADS, HEAD_DIM))
        outs["vp"].append(tail[..., P_VA:P_VA + KV_WIDTH].reshape(Bp, WINDOW, N_KV_HEADS, HEAD_DIM))
        outs["sp"].append(s_t)
        outs["shp"].append(_shift_cols(proj[:, Tp - 1]))

        proj = _norm_proj(hs, norm_g[l], scale_s, shift_s, w_bf, tab_s, tm=Bd)
        att, *new_cache_s = _attn_sample(proj, ck_all, cv_all, l, new_cache_s, sinks[l])
        sh = state_shift[l]
        prev = [sh[None, :, 0:W], sh[None, :, W:2 * W], sh[None, :, 2 * W:3 * W], sh[None, :, 3 * W:]]
        prep = _rwkv_prep(proj, prev, mu, *vecs, wd_pad, wi_pad, tm=Bd)
        prep = [t[:, None] for t in prep]
        y_rw, new_state_s = _wkv(*prep, r_k[l], ln_w[l], ln_b[l], state_wkv, l, new_state_s, tc=1)
        hs = _post(att, y_rw.reshape(1, Bd, W), proj, hs, gate_s, w_out_bf, final_g, final, tm=Bd)
        outs["shs"].append(_shift_cols(proj[0]))

    st = lambda k: jnp.stack(outs[k])
    return (hp, hs.reshape(Bd, 1, D_MODEL), st("kp"), st("vp"), st("sp"), st("shp"),
            *(t.reshape(cache_k.shape) for t in new_cache_s), new_state_s, st("shs"))
```

```python
import functools

import jax
import jax.numpy as jnp
from jax import lax
from jax.experimental import pallas as pl
from jax.experimental.pallas import tpu as pltpu

f32 = jnp.float32
bf16 = jnp.bfloat16

D_MODEL = 2048
DEPTH = 2
PAST_LEN = 16384
ATT_WIDTH = 1024
RWKV_WIDTH = 1024
HEAD_DIM = 64
N_Q_HEADS = 16
N_KV_HEADS = 4
GQA_GROUP = 4
KV_WIDTH = 256
WINDOW = 128
ROT_DIM = 16
ROPE_THETA = 500000.0
RW_HEAD = 64
RW_HEADS = 16
DECAY_LORA = 64
ICLR_LORA = 64
LORA_PAIR = DECAY_LORA + ICLR_LORA
NORM_EPS = 1e-5
GN_EPS = 64e-5
NEG_BIG = -1e30

Q_OFF = 0
KA_OFF = Q_OFF + ATT_WIDTH
VA_OFF = KA_OFF + KV_WIDTH
R_OFF = VA_OFF + KV_WIDTH
KR_OFF = R_OFF + RWKV_WIDTH
VR_OFF = KR_OFF + RWKV_WIDTH
WD_OFF = VR_OFF + RWKV_WIDTH
AD_OFF = WD_OFF + DECAY_LORA
GA_OFF = AD_OFF + ICLR_LORA
GR_OFF = GA_OFF + ATT_WIDTH
IN_WIDTH = GR_OFF + RWKV_WIDTH
SHIFT_DIM = GA_OFF - R_OFF

LANES = 128
P_Q = 0
P_GA = 1024
P_GR = 2048
P_R = 3072
P_KR = 4096
P_VR = 5120
P_KA = 6144
P_VA = 6400
P_WA = 6656
P_WIDTH = 7168
PROJ_TN = 1792

VMEM_LIMIT = 56 * 1024 * 1024


_NN = (((1,), (0,)), ((), ()))
_NT = (((1,), (1,)), ((), ()))
_TN = (((0,), (0,)), ((), ()))


def _silu(x):
    return x * jax.nn.sigmoid(x)


def _cparams(sem):
    return pltpu.CompilerParams(dimension_semantics=sem, vmem_limit_bytes=VMEM_LIMIT)


def _split_bf16(x):
    hi = x.astype(bf16)
    return hi, (x - hi.astype(f32)).astype(bf16)


def _ada_kernel(c_ref, w_ref, b_ref, o_ref):
    ch, cl = _split_bf16(_silu(c_ref[...]))
    wh, wl = _split_bf16(w_ref[0])
    rows = ch.shape[0]
    both = jnp.dot(jnp.concatenate([ch, cl], axis=0), wh, preferred_element_type=f32)
    o_ref[0] = both[:rows] + both[rows:] + jnp.dot(ch, wl, preferred_element_type=f32) + b_ref[0]


def _ada(c_all, w_ada, b_ada):
    rows = c_all.shape[0]
    tn = 1536
    n_out = w_ada.shape[2]
    return pl.pallas_call(
        _ada_kernel,
        grid=(DEPTH, n_out // tn),
        in_specs=[
            pl.BlockSpec((rows, D_MODEL), lambda l, n: (0, 0)),
            pl.BlockSpec((1, D_MODEL, tn), lambda l, n: (l, 0, n)),
            pl.BlockSpec((1, 1, tn), lambda l, n: (l, 0, n)),
        ],
        out_specs=pl.BlockSpec((1, rows, tn), lambda l, n: (l, 0, n)),
        out_shape=jax.ShapeDtypeStruct((DEPTH, rows, n_out), f32),
        compiler_params=_cparams(("parallel", "parallel")),
        name="ada_mod",
    )(c_all, w_ada, b_ada.reshape(DEPTH, 1, n_out))


def _rope(x, tab):
    w = x.shape[1]
    reps = w // LANES
    cosf, up, dn = (jnp.concatenate([tab[i]] * reps, axis=1) for i in range(3))
    half = ROT_DIM // 2
    return x * cosf + pltpu.roll(x, w - half, 1) * up + pltpu.roll(x, half, 1) * dn


def _norm_proj_kernel(x_ref, g_ref, scale_ref, shift_ref, w_ref, tab_ref, o_ref, h_ref):
    n = pl.program_id(2)

    @pl.when(n == 0)
    def _():
        x = x_ref[0]
        ms = jnp.mean(x * x, axis=-1, keepdims=True)
        y = x * lax.rsqrt(ms + NORM_EPS) * g_ref[...]
        h_ref[...] = (y * (1.0 + scale_ref[0]) + shift_ref[0]).astype(bf16)

    tm, tn = o_ref.shape[1], o_ref.shape[2]
    q_tile, q_lo = divmod(P_Q, tn)
    k_tile, k_lo = divmod(P_KA, tn)
    assert q_lo + ATT_WIDTH <= tn and k_lo + KV_WIDTH <= tn and q_tile != k_tile
    rc = min(tm, 256)

    def rotated_tile(lo, width, scale):
        for i in range(tm // rc):
            rs = slice(i * rc, (i + 1) * rc)
            res = jnp.dot(h_ref[rs, :], w_ref[...], preferred_element_type=f32)
            parts = [res[:, :lo]] if lo else []
            parts.append(_rope(res[:, lo:lo + width], tab_ref[:, rs, :]) * scale)
            if lo + width < tn:
                parts.append(res[:, lo + width:])
            o_ref[0, rs, :] = jnp.concatenate(parts, axis=1) if len(parts) > 1 else parts[0]

    @pl.when(n == q_tile)
    def _():
        rotated_tile(q_lo, ATT_WIDTH, HEAD_DIM ** -0.5)

    @pl.when(n == k_tile)
    def _():
        rotated_tile(k_lo, KV_WIDTH, 1.0)

    @pl.when((n != q_tile) & (n != k_tile))
    def _():
        o_ref[0] = jnp.dot(h_ref[...], w_ref[...], preferred_element_type=f32)


def _mod_spec(mod, tm):
    if mod.shape[1] == 1:
        return pl.BlockSpec((1, 1, D_MODEL), lambda g, m, *_: (g, 0, 0))
    return pl.BlockSpec((1, tm, D_MODEL), lambda g, m, *_: (g, m, 0))


def _norm_proj(x, norm_g, scale, shift, w_bf, tab, tm):
    G, R, _ = x.shape
    return pl.pallas_call(
        _norm_proj_kernel,
        grid=(G, R // tm, P_WIDTH // PROJ_TN),
        in_specs=[
            pl.BlockSpec((1, tm, D_MODEL), lambda g, m, n: (g, m, 0)),
            pl.BlockSpec((1, D_MODEL), lambda g, m, n: (0, 0)),
            _mod_spec(scale, tm),
            _mod_spec(shift, tm),
            pl.BlockSpec((D_MODEL, PROJ_TN), lambda g, m, n: (0, n)),
            pl.BlockSpec((3, tm, LANES), lambda g, m, n: (0, m, 0)),
        ],
        out_specs=pl.BlockSpec((1, tm, PROJ_TN), lambda g, m, n: (g, m, n)),
        out_shape=jax.ShapeDtypeStruct((G, R, P_WIDTH), f32),
        scratch_shapes=[pltpu.VMEM((tm, D_MODEL), bf16)],
        compiler_params=_cparams(("parallel", "parallel", "arbitrary")),
        name="norm_proj",
    )(x, norm_g.reshape(1, D_MODEL), scale, shift, w_bf, tab)


def _rope_tables(pos):
    half = ROT_DIM // 2
    inv_freq = ROPE_THETA ** (-jnp.arange(half, dtype=f32) * (2.0 / ROT_DIM))
    ang = pos.astype(f32)[:, None] * inv_freq[None, :]
    cos, sin = jnp.cos(ang), jnp.sin(ang)
    t = pos.shape[0]
    z8 = jnp.zeros((t, half), f32)
    rest = HEAD_DIM - ROT_DIM
    cos64 = jnp.concatenate([cos, cos, jnp.ones((t, rest), f32)], axis=1)
    up64 = jnp.concatenate([-sin, z8, jnp.zeros((t, rest), f32)], axis=1)
    dn64 = jnp.concatenate([z8, sin, jnp.zeros((t, rest), f32)], axis=1)
    rep = LANES // HEAD_DIM
    return jnp.stack([jnp.tile(a, (1, rep)) for a in (cos64, up64, dn64)])


ATTN_BLOCKS = 4


def _attn_prompt_kernel(sinks_ref, q_ref, kc_ref, kp_ref, vc_ref, vp_ref, ga_ref, o_ref):
    n = pl.program_id(1)
    wn = WINDOW
    half = HEAD_DIM
    k_t_all = jnp.concatenate([kp_ref[0], kc_ref[0]], axis=0).T.astype(bf16)
    v_all = jnp.concatenate([vp_ref[0], vc_ref[0]], axis=0)

    qi = lax.broadcasted_iota(jnp.int32, (2 * wn, 2 * wn), 0) & (wn - 1)
    kj = lax.broadcasted_iota(jnp.int32, (2 * wn, 2 * wn), 1)
    rel = wn + qi - kj
    band = (rel >= 0) & (rel <= wn)
    top = lax.broadcasted_iota(jnp.int32, (2 * wn, 1), 0) < wn
    lo = lax.broadcasted_iota(jnp.int32, (2 * wn, LANES), 1) < half
    zeros_k = jnp.zeros((half, 2 * wn), bf16)
    ones_lo = jnp.where(lo, 1.0, 0.0).astype(bf16)
    ones_hi = jnp.where(lo, 0.0, 1.0).astype(bf16)

    for sb in range(q_ref.shape[1] // wn):
        rows = slice(sb * wn, (sb + 1) * wn)
        q = q_ref[0, rows, :].astype(bf16)
        ga = ga_ref[0, rows, :]
        k_t = k_t_all[:, sb * wn:(sb + 2) * wn]
        vcat = v_all[sb * wn:(sb + 2) * wn]
        mask = band & ((kj >= wn) | (n > 0)) if sb == 0 else band
        for j in range(N_KV_HEADS // 2):
            vblk = vcat[:, j * LANES:(j + 1) * LANES]
            vswap = pltpu.roll(vblk, half, 1)
            for g in (2 * j, 2 * j + 1):
                own, other = (vblk, vswap) if g % 2 == 0 else (vswap, vblk)
                v_lo = jnp.where(lo, own, 0.0).astype(bf16)
                v_hi = jnp.where(lo, 0.0, other).astype(bf16)
                rhs_pv = jnp.concatenate([jnp.concatenate([v_lo, ones_lo], axis=1),
                                          jnp.concatenate([v_hi, ones_hi], axis=1)], axis=0)
                kg = k_t[g * half:(g + 1) * half, :]
                rhs_qk = jnp.concatenate([jnp.concatenate([kg, zeros_k], axis=0),
                                          jnp.concatenate([zeros_k, kg], axis=0)], axis=1)
                b0, b1 = 2 * g, 2 * g + 1
                qg = jnp.concatenate([q[:, b0 * LANES:(b0 + 1) * LANES], q[:, b1 * LANES:(b1 + 1) * LANES]],
                                     axis=0)
                s_all = jnp.dot(qg, rhs_qk, preferred_element_type=f32)
                ps, es = [], []
                for hh in range(2):
                    s = jnp.where(mask, s_all[:, hh * 2 * wn:(hh + 1) * 2 * wn], NEG_BIG)
                    sink = jnp.where(top, sinks_ref[2 * b0 + hh], sinks_ref[2 * b1 + hh])
                    m = jnp.maximum(jnp.max(s, axis=-1, keepdims=True), sink)
                    ps.append(jnp.exp(s - m).astype(bf16))
                    es.append(jnp.exp(sink - m))
                res = jnp.dot(jnp.concatenate(ps, axis=1), rhs_pv, preferred_element_type=f32)
                out = res[:, :LANES] / (res[:, LANES:] + jnp.where(lo, es[0], es[1]))
                for i, blk in enumerate((b0, b1)):
                    sl = slice(blk * LANES, (blk + 1) * LANES)
                    o_ref[0, rows, sl] = (out[i * wn:(i + 1) * wn] * _silu(ga[:, sl])).astype(o_ref.dtype)


def _attn_prompt(proj, sinks):
    B, T, _ = proj.shape
    rows = ATTN_BLOCKS * WINDOW
    kvb = KV_WIDTH
    prev = lambda b, n: jnp.maximum(n * ATTN_BLOCKS - 1, 0)
    return pl.pallas_call(
        _attn_prompt_kernel,
        grid=(B, T // rows),
        in_specs=[
            pl.BlockSpec(memory_space=pltpu.SMEM),
            pl.BlockSpec((1, rows, ATT_WIDTH), lambda b, n: (b, n, P_Q // ATT_WIDTH)),
            pl.BlockSpec((1, rows, kvb), lambda b, n: (b, n, P_KA // kvb)),
            pl.BlockSpec((1, WINDOW, kvb), lambda b, n: (b, prev(b, n), P_KA // kvb)),
            pl.BlockSpec((1, rows, kvb), lambda b, n: (b, n, P_VA // kvb)),
            pl.BlockSpec((1, WINDOW, kvb), lambda b, n: (b, prev(b, n), P_VA // kvb)),
            pl.BlockSpec((1, rows, ATT_WIDTH), lambda b, n: (b, n, P_GA // ATT_WIDTH)),
        ],
        out_specs=pl.BlockSpec((1, rows, ATT_WIDTH), lambda b, n: (b, n, 0)),
        out_shape=jax.ShapeDtypeStruct((B, T, ATT_WIDTH), bf16),
        compiler_params=_cparams(("parallel", "arbitrary")),
        name="attn_prompt",
    )(sinks, proj, proj, proj, proj, proj, proj)


SAMPLE_ROWS = 16


def _attn_sample_kernel(sinks_ref, q_ref, kn_ref, vn_ref, ga_ref, ck_ref, cv_ref, *refs):
    o_ref, nk_ref, nv_ref = refs[-3:]
    rb = q_ref.shape[1]
    q_all, kn_all, vn_all = q_ref[0], kn_ref[0], vn_ref[0]
    work = [(r, g) for r in range(rb) for g in range(N_KV_HEADS)]
    span = lambda g: slice(g * HEAD_DIM, (g + 1) * HEAD_DIM)
    ckb = [ck_ref[0, r].astype(bf16) for r in range(rb)]
    cvb = [cv_ref[0, r].astype(bf16) for r in range(rb)]
    qg = {(r, g): jnp.concatenate(
        [q_all[r:r + 1, (g * GQA_GROUP + i) * HEAD_DIM:(g * GQA_GROUP + i + 1) * HEAD_DIM]
         for i in range(GQA_GROUP)], axis=0) for r, g in work}
    s = {rg: lax.dot_general(qg[rg].astype(bf16), ckb[rg[0]][:, span(rg[1])], _NT,
                             preferred_element_type=f32) for rg in work}
    sink = {g: sinks_ref[g * GQA_GROUP:(g + 1) * GQA_GROUP, :] for g in range(N_KV_HEADS)}
    s_new = {(r, g): jnp.sum(qg[r, g] * kn_all[r:r + 1, span(g)], axis=-1, keepdims=True) for r, g in work}
    m = {rg: jnp.maximum(jnp.maximum(jnp.max(s[rg], axis=-1, keepdims=True), s_new[rg]), sink[rg[1]])
         for rg in work}
    p = {rg: jnp.exp(s[rg] - m[rg]) for rg in work}
    p_new = {rg: jnp.exp(s_new[rg] - m[rg]) for rg in work}
    den = {rg: jnp.sum(p[rg], axis=-1, keepdims=True) + p_new[rg] + jnp.exp(sink[rg[1]] - m[rg]) for rg in work}
    pv = {(r, g): jnp.dot(p[r, g].astype(bf16), cvb[r][:, span(g)], preferred_element_type=f32) for r, g in work}
    outs = {}
    for r, g in work:
        o = (pv[r, g] + p_new[r, g] * vn_all[r:r + 1, span(g)]) / den[r, g]
        outs[r, g] = [o[i:i + 1, :] for i in range(GQA_GROUP)]
    att = jnp.concatenate(
        [jnp.concatenate([h for g in range(N_KV_HEADS) for h in outs[r, g]], axis=1) for r in range(rb)],
        axis=0)
    o_ref[0] = (att * _silu(ga_ref[0])).astype(o_ref.dtype)
    last = lax.broadcasted_iota(jnp.int32, (WINDOW, KV_WIDTH), 0) == WINDOW - 1
    for r in range(rb):
        nk_ref[0, r] = jnp.where(last, kn_all[r:r + 1], pltpu.roll(ck_ref[0, r], WINDOW - 1, 0))
        nv_ref[0, r] = jnp.where(last, vn_all[r:r + 1], pltpu.roll(cv_ref[0, r], WINDOW - 1, 0))


def _attn_sample(proj, ck_all, cv_all, layer, stacked, sinks):
    Bd = proj.shape[1]
    rb = SAMPLE_ROWS
    col = lambda w, off: pl.BlockSpec((1, rb, w), lambda i: (0, i, off // w))
    cache = pl.BlockSpec((1, rb, WINDOW, KV_WIDTH), lambda i: (layer, i, 0, 0))
    args = [sinks.reshape(N_Q_HEADS, 1), proj, proj, proj, proj, ck_all, cv_all, *stacked]
    hbm = pl.BlockSpec(memory_space=pl.ANY)
    return pl.pallas_call(
        _attn_sample_kernel,
        grid=(Bd // rb,),
        in_specs=[
            pl.BlockSpec((N_Q_HEADS, 1), lambda i: (0, 0)),
            col(ATT_WIDTH, P_Q), col(KV_WIDTH, P_KA), col(KV_WIDTH, P_VA), col(ATT_WIDTH, P_GA),
            cache, cache, hbm, hbm,
        ],
        out_specs=[pl.BlockSpec((1, rb, ATT_WIDTH), lambda i: (0, i, 0)), cache, cache],
        out_shape=[
            jax.ShapeDtypeStruct((1, Bd, ATT_WIDTH), bf16),
            jax.ShapeDtypeStruct(ck_all.shape, f32),
            jax.ShapeDtypeStruct(cv_all.shape, f32),
        ],
        input_output_aliases={len(args) - 2: 1, len(args) - 1: 2},
        compiler_params=_cparams(("parallel",)),
        name="attn_sample",
    )(*args)


def _rwkv_prep_kernel(r_ref, kr_ref, vr_ref, wa_ref, pr_ref, pkr_ref, pvr_ref, pwa_ref,
                      mu_r_ref, mu_kr_ref, mu_vr_ref, mu_wa_ref, w0_ref, a0_ref, kk_ref, ka_ref,
                      wd_ref, wi_ref, ro_ref, wo_ref, ko_ref, vo_ref, kko_ref, ao_ref):
    def mixed(cur_ref, prev_ref, mu_ref):
        cur = cur_ref[0]
        return cur + (prev_ref[0] - cur) * mu_ref[...]

    r = mixed(r_ref, pr_ref, mu_r_ref)
    kr = mixed(kr_ref, pkr_ref, mu_kr_ref)
    vr = mixed(vr_ref, pvr_ref, mu_vr_ref)
    wa = mixed(wa_ref, pwa_ref, mu_wa_ref)
    hi = lax.Precision.HIGHEST
    z = w0_ref[...] + jnp.dot(jnp.tanh(wa), wd_ref[...], precision=hi, preferred_element_type=f32)
    nz = -z
    softplus = jnp.maximum(nz, 0.0) + jnp.log1p(jnp.exp(-jnp.abs(nz)))
    w_log = -softplus - 0.5
    a = jax.nn.sigmoid(a0_ref[...] + jnp.dot(wa, wi_ref[...], precision=hi, preferred_element_type=f32))
    outs = (r, -jnp.exp(w_log), kr * (1.0 + (a - 1.0) * ka_ref[...]), vr, kr * kk_ref[...], a)
    for o_ref, val in zip((ro_ref, wo_ref, ko_ref, vo_ref, kko_ref, ao_ref), outs):
        for h in range(RW_HEADS):
            o_ref[:, h, :] = val[:, h * RW_HEAD:(h + 1) * RW_HEAD]


def _rwkv_prep(proj, prev, mu, w0, a0, k_k, k_a, wd_pad, wi_pad, tm):
    G, R, _ = proj.shape
    assert G == 1
    W = RWKV_WIDTH
    cur_specs = [
        pl.BlockSpec((1, tm, W), lambda g, m: (g, m, P_R // W)),
        pl.BlockSpec((1, tm, W), lambda g, m: (g, m, P_KR // W)),
        pl.BlockSpec((1, tm, W), lambda g, m: (g, m, P_VR // W)),
        pl.BlockSpec((1, tm, LANES), lambda g, m: (g, m, P_WA // LANES)),
    ]
    prev_specs = [
        pl.BlockSpec((1, tm, W), lambda g, m: (g, m, 0)),
        pl.BlockSpec((1, tm, W), lambda g, m: (g, m, 0)),
        pl.BlockSpec((1, tm, W), lambda g, m: (g, m, 0)),
        pl.BlockSpec((1, tm, LANES), lambda g, m: (g, m, 0)),
    ]
    prev_args = list(prev)
    vec = lambda w: pl.BlockSpec((1, w), lambda g, m: (0, 0))
    lora = pl.BlockSpec((LORA_PAIR, W), lambda g, m: (0, 0))
    out_spec = pl.BlockSpec((tm, RW_HEADS, RW_HEAD), lambda g, m: (m, 0, 0))
    return pl.pallas_call(
        _rwkv_prep_kernel,
        grid=(G, R // tm),
        in_specs=cur_specs + prev_specs + [vec(W), vec(W), vec(W), vec(LANES),
                                           vec(W), vec(W), vec(W), vec(W), lora, lora],
        out_specs=[out_spec] * 6,
        out_shape=[jax.ShapeDtypeStruct((R, RW_HEADS, RW_HEAD), f32)] * 6,
        compiler_params=_cparams(("parallel", "arbitrary")),
        name="rwkv_prep",
    )(proj, proj, proj, proj, *prev_args, *mu, w0, a0, k_k, k_a, wd_pad, wi_pad)


def _wkv_kernel(tc, r_ref, w_ref, k_ref, v_ref, kk_ref, a_ref, rk_ref, lnw_ref, lnb_ref, s0_ref,
                *refs):
    y_ref, s_ref = refs[-2:]

    @pl.when(pl.program_id(1) == 0)
    def _():
        s_ref[...] = s0_ref[...]

    eye = (lax.broadcasted_iota(jnp.int32, (RW_HEAD, RW_HEAD), 0)
           == lax.broadcasted_iota(jnp.int32, (RW_HEAD, RW_HEAD), 1)).astype(f32)
    rk = rk_ref[...]
    lnw = lnw_ref[...]
    lnb = lnb_ref[...]

    nb = r_ref.shape[0]

    def step(t, carry):
        work = [(i, h) for i in range(nb) for h in range(RW_HEADS)]
        r, lw, k, v, kkr, a = ([ref[i, t] for i in range(nb)] for ref in (r_ref, w_ref, k_ref, v_ref, kk_ref, a_ref))
        w = [jnp.exp(x) for x in lw]
        kk = [x / jnp.maximum(jnp.sqrt(jnp.sum(x * x, axis=-1, keepdims=True)), 1e-12) for x in kkr]
        b = [x * y for x, y in zip(kk, a)]
        bonus = [jnp.sum(r[i] * k[i] * rk, axis=-1, keepdims=True) * v[i] for i in range(nb)]
        row = lambda x, ih: x[ih[0]][ih[1]:ih[1] + 1]
        s_in = {ih: s_ref[0, ih[0], ih[1]] for ih in work}
        sa = {ih: jnp.sum(s_in[ih] * (-row(kk, ih)), axis=-1, keepdims=True) for ih in work}
        v_col = {ih: jnp.sum(eye * row(v, ih), axis=-1, keepdims=True) for ih in work}
        s_out = {ih: s_in[ih] * row(w, ih) + sa[ih] * row(b, ih) + v_col[ih] * row(k, ih) for ih in work}
        for ih in work:
            s_ref[0, ih[0], ih[1]] = s_out[ih]
        ys = {ih: jnp.sum(s_out[ih] * row(r, ih), axis=-1, keepdims=True) for ih in work}
        mus = {ih: jnp.mean(ys[ih], axis=0, keepdims=True) for ih in work}
        var = {ih: jnp.mean(jnp.square(ys[ih] - mus[ih]), axis=0, keepdims=True) for ih in work}
        rows = {ih: jnp.sum((ys[ih] - mus[ih]) * lax.rsqrt(var[ih] + GN_EPS) * eye, axis=0, keepdims=True)
                for ih in work}
        for i in range(nb):
            y_ref[i, t] = jnp.concatenate([rows[i, h] for h in range(RW_HEADS)], axis=0) * lnw + lnb + bonus[i]
        return carry

    lax.fori_loop(0, tc, step, 0)


STEP_ROWS = 4


def _wkv(r, w, k, v, kk, a, r_k, ln_w, ln_b, s0_all, layer, stacked, tc):
    B, T = r.shape[0], r.shape[1]
    hd = (RW_HEADS, RW_HEAD)
    nb = STEP_ROWS
    seq = pl.BlockSpec((nb, tc, *hd), lambda b, c: (b, c, 0, 0))
    par = pl.BlockSpec(hd, lambda b, c: (0, 0))
    state = pl.BlockSpec((1, nb, RW_HEADS, RW_HEAD, RW_HEAD), lambda b, c: (layer, b, 0, 0, 0))
    args = [r, w, k, v, kk, a, r_k, ln_w.reshape(hd), ln_b.reshape(hd), s0_all, stacked]
    y, s_new = pl.pallas_call(
        functools.partial(_wkv_kernel, tc),
        grid=(B // nb, T // tc),
        in_specs=[seq] * 6 + [par] * 3 + [state, pl.BlockSpec(memory_space=pl.ANY)],
        out_specs=[seq, state],
        out_shape=[
            jax.ShapeDtypeStruct((B, T, *hd), f32),
            jax.ShapeDtypeStruct(s0_all.shape, f32),
        ],
        input_output_aliases={len(args) - 1: 1},
        compiler_params=_cparams(("parallel", "arbitrary")),
        name="wkv_steps",
    )(*args)
    return y.reshape(B, T, RWKV_WIDTH), s_new


CHUNK = 64
PAIR = 2 * RW_HEAD
N_PAIRS = RW_HEADS // 2


def _mmb(a, b, dims=_NN):
    return lax.dot_general(a.astype(bf16), b.astype(bf16), dims, preferred_element_type=f32)


def _lora_dot(x, wh_ref, wl_ref):
    xh, xl = _split_bf16(x)
    rows = x.shape[0]
    both = jnp.dot(jnp.concatenate([xh, xl], axis=0), wh_ref[...], preferred_element_type=f32)
    return both[:rows] + both[rows:] + jnp.dot(xh, wl_ref[...], preferred_element_type=f32)


CHUNKS_PER_STEP = 4


def _wkv_chunk_kernel(r_ref, kr_ref, vr_ref, wa_ref, gr_ref, pr_ref, pkr_ref, pvr_ref, pwa_ref,
                      mu_r_ref, mu_kr_ref, mu_vr_ref, mu_wa_ref, w0_ref, a0_ref, kk_ref, ka_ref,
                      wdh_ref, wdl_ref, wih_ref, wil_ref, rk_ref, lnw_ref, lnb_ref, s0_ref,
                      y_ref, s_ref, sp_ref, cr_ref, ckr_ref, cvr_ref, cwa_ref):
    C = CHUNK
    rows_blk = r_ref.shape[1]
    c = pl.program_id(1)

    @pl.when(c == 0)
    def _():
        for p in range(N_PAIRS):
            sp_ref[p] = jnp.concatenate([s0_ref[0, 2 * p], s0_ref[0, 2 * p + 1]], axis=1)
        for carry, first in ((cr_ref, pr_ref), (ckr_ref, pkr_ref), (cvr_ref, pvr_ref), (cwa_ref, pwa_ref)):
            carry[...] = first[0]

    def mixed(cur_ref, carry_ref, mu_ref):
        cur = cur_ref[0]
        first = lax.broadcasted_iota(jnp.int32, cur.shape, 0) == 0
        prev = jnp.where(first, carry_ref[...], pltpu.roll(cur, 1, 0))
        carry_ref[...] = cur[rows_blk - 1:rows_blk, :]
        return cur + (prev - cur) * mu_ref[...]

    r_blk = mixed(r_ref, cr_ref, mu_r_ref)
    kr_blk = mixed(kr_ref, ckr_ref, mu_kr_ref)
    v_blk = mixed(vr_ref, cvr_ref, mu_vr_ref)
    wa = mixed(wa_ref, cwa_ref, mu_wa_ref)
    nz = -(w0_ref[...] + _lora_dot(jnp.tanh(wa), wdh_ref, wdl_ref))
    softplus = jnp.maximum(nz, 0.0) + jnp.log1p(jnp.exp(-jnp.abs(nz)))
    lw_blk = -jnp.exp(-softplus - 0.5)
    icl_blk = jax.nn.sigmoid(a0_ref[...] + _lora_dot(wa, wih_ref, wil_ref))
    k_blk = kr_blk * (1.0 + (icl_blk - 1.0) * ka_ref[...])
    kkr_blk = kr_blk * kk_ref[...]

    row = lax.broadcasted_iota(jnp.int32, (PAIR, PAIR), 0)
    col = lax.broadcasted_iota(jnp.int32, (PAIR, PAIR), 1)
    tril = row >= col
    stril = row > col
    tril2 = jnp.concatenate([tril, tril], axis=1)
    eye = (row == col).astype(f32)
    lane_lo = lax.broadcasted_iota(jnp.int32, (C, PAIR), 1) < RW_HEAD
    pairs = range(N_PAIRS)
    sls = [slice(p * PAIR, (p + 1) * PAIR) for p in pairs]

    def bd(x):
        zero = jnp.zeros_like(x)
        return jnp.concatenate([jnp.where(lane_lo, x, zero), jnp.where(lane_lo, zero, x)], axis=0)

    def head_sums(x):
        lo_sum = jnp.sum(jnp.where(lane_lo, x, 0.0), axis=-1, keepdims=True)
        hi_sum = jnp.sum(jnp.where(lane_lo, 0.0, x), axis=-1, keepdims=True)
        return jnp.where(lane_lo, lo_sum, hi_sum)

    state = [sp_ref[p] for p in pairs]
    for j in range(rows_blk // C):
        rs = slice(j * C, (j + 1) * C)
        lw = lw_blk[rs]
        width = lw.shape[1]
        lw_a = lw.astype(bf16)
        rest = lw - lw_a.astype(f32)
        lw_b = rest.astype(bf16)
        lw_c = (rest - lw_b.astype(f32)).astype(bf16)
        g3 = jnp.dot(tril[:C, :C].astype(bf16), jnp.concatenate([lw_a, lw_b, lw_c], axis=1),
                     preferred_element_type=f32)
        g = g3[:, :width] + g3[:, width:2 * width] + g3[:, 2 * width:]
        e_g = jnp.exp(g)
        e_ng = jnp.exp(-g)
        e_gm = jnp.exp(g - lw)
        e_end = e_g[C - 1:C, :]

        ins = [[t[rs, sl] for t in (r_blk, k_blk, v_blk, kkr_blk, icl_blk)] for sl in sls]
        norms = [jnp.sqrt(head_sums(x[3] * x[3])) for x in ins]
        at, rt, bt, kt, vb = ([] for _ in range(5))
        for (r, k, v, kkr, icl), norm, sl in zip(ins, norms, sls):
            kk = kkr / jnp.maximum(norm, 1e-12)
            b = kk * icl
            at.append(bd((-kk * e_gm[:, sl]).astype(bf16)))
            rt.append(bd((r * e_g[:, sl]).astype(bf16)))
            bt.append(bd((b * e_ng[:, sl]).astype(bf16)))
            kt.append(bd((k * e_ng[:, sl]).astype(bf16)))
            vb.append(bd(v.astype(bf16)))

        gram = [_mmb(jnp.concatenate([at[p], rt[p]], axis=0), jnp.concatenate([bt[p], kt[p]], axis=0), _NT)
                for p in pairs]
        lmat = [jnp.where(stril, gm[:PAIR, :PAIR], 0.0) for gm in gram]
        mv = [_mmb(jnp.where(stril, gram[p][:PAIR, PAIR:], 0.0), vb[p]) for p in pairs]
        lower = [jnp.where(tril2, gm[PAIR:, :], 0.0).astype(bf16) for gm in gram]

        tinv = [eye + lm for lm in lmat]
        pw = [_mmb(lm, lm) for lm in lmat]
        for _ in range(4):
            z = [_mmb(jnp.concatenate([x.astype(bf16), t.astype(bf16)], axis=0), x)
                 for t, x in zip(tinv, pw)]
            pw = [zz[:PAIR] for zz in z]
            tinv = [t + zz[PAIR:] for t, zz in zip(tinv, z)]
        tinv = [t + _mmb(t, x) for t, x in zip(tinv, pw)]

        wx = [_mmb(tinv[p], jnp.concatenate([at[p], mv[p].astype(bf16)], axis=1)) for p in pairs]
        uy0 = [_mmb(jnp.concatenate([wx[p][:, :PAIR].astype(bf16), rt[p]], axis=0),
                    bd(state[p].astype(bf16)), _NT) for p in pairs]
        uv = [jnp.concatenate([(uy0[p][:PAIR] + wx[p][:, PAIR:]).astype(bf16), vb[p]], axis=0)
              for p in pairs]
        ys = [uy0[p][PAIR:] + _mmb(lower[p], uv[p]) for p in pairs]
        s_add = [_mmb(uv[p], jnp.concatenate([bt[p], kt[p]], axis=0), _TN) for p in pairs]
        state = [(state[p] + s_add[p][:RW_HEAD] + s_add[p][RW_HEAD:]) * e_end[:, sls[p]] for p in pairs]

        ys = [y[:C] + y[C:] for y in ys]
        mus = [head_sums(y) * (1.0 / RW_HEAD) for y in ys]
        ds = [y - mu for y, mu in zip(ys, mus)]
        var = [head_sums(d * d) * (1.0 / RW_HEAD) for d in ds]
        bonus = [head_sums(x[0] * x[1] * rk_ref[:, sl]) * x[2] for x, sl in zip(ins, sls)]
        for p in pairs:
            sl = sls[p]
            y_rw = ds[p] * lax.rsqrt(var[p] + GN_EPS) * lnw_ref[:, sl] + lnb_ref[:, sl] + bonus[p]
            y_ref[0, rs, sl] = (y_rw * _silu(gr_ref[0, rs, sl])).astype(y_ref.dtype)

    for p in pairs:
        sp_ref[p] = state[p]

    @pl.when(c == pl.num_programs(1) - 1)
    def _():
        for p in range(N_PAIRS):
            s_ref[0, 2 * p] = state[p][:, :RW_HEAD]
            s_ref[0, 2 * p + 1] = state[p][:, RW_HEAD:]


def _wkv_chunked(proj, prev, mu, w0, a0, k_k, k_a, lora, r_k, ln_w, ln_b, s0):
    B, T, _ = proj.shape
    W = RWKV_WIDTH
    rows = CHUNKS_PER_STEP * CHUNK
    col = lambda w, off: pl.BlockSpec((1, rows, w), lambda b, c: (b, c, off // w))
    first = lambda w: pl.BlockSpec((1, 1, w), lambda b, c: (b, 0, 0))
    vec = lambda w: pl.BlockSpec((1, w), lambda b, c: (0, 0))
    lora_spec = pl.BlockSpec((LORA_PAIR, W), lambda b, c: (0, 0))
    state = pl.BlockSpec((1, RW_HEADS, RW_HEAD, RW_HEAD), lambda b, c: (b, 0, 0, 0))
    return pl.pallas_call(
        _wkv_chunk_kernel,
        grid=(B, T // rows),
        in_specs=[col(W, P_R), col(W, P_KR), col(W, P_VR), col(LANES, P_WA), col(W, P_GR),
                  first(W), first(W), first(W), first(LANES),
                  vec(W), vec(W), vec(W), vec(LANES), vec(W), vec(W), vec(W), vec(W),
                  lora_spec, lora_spec, lora_spec, lora_spec, vec(W), vec(W), vec(W), state],
        out_specs=[col(W, 0), state],
        out_shape=[
            jax.ShapeDtypeStruct((B, T, W), bf16),
            jax.ShapeDtypeStruct((B, RW_HEADS, RW_HEAD, RW_HEAD), f32),
        ],
        scratch_shapes=[pltpu.VMEM((N_PAIRS, RW_HEAD, PAIR), f32),
                        pltpu.VMEM((1, W), f32), pltpu.VMEM((1, W), f32), pltpu.VMEM((1, W), f32),
                        pltpu.VMEM((1, LANES), f32)],
        compiler_params=_cparams(("parallel", "arbitrary")),
        name="wkv_chunks",
    )(proj, proj, proj, proj, proj, *prev, *mu, w0, a0, k_k, k_a, *lora,
      r_k.reshape(1, W), ln_w.reshape(1, W), ln_b.reshape(1, W), s0)


def _post_kernel(final, gated, att_ref, y_ref, *refs):
    if gated:
        y = y_ref[0]
    else:
        y = (y_ref[0] * _silu(refs[0][0])).astype(bf16)
        refs = refs[1:]
    x_ref, gate_ref, w_ref, fg_ref, o_ref = refs
    cat = jnp.concatenate([att_ref[0], y], axis=1)
    out = jnp.dot(cat, w_ref[...], preferred_element_type=f32)
    x = x_ref[0] + gate_ref[0] * out
    if final:
        ms = jnp.mean(x * x, axis=-1, keepdims=True)
        x = x * lax.rsqrt(ms + NORM_EPS) * fg_ref[...]
    o_ref[0] = x


def _post(att, y_rw, proj, x, gate, w_out_bf, final_g, final, tm):
    G, R, _ = x.shape
    W = RWKV_WIDTH
    gated = proj is None
    gate_in = [] if gated else [pl.BlockSpec((1, tm, W), lambda g, m: (g, m, P_GR // W))]
    gate_arg = [] if gated else [proj]
    return pl.pallas_call(
        functools.partial(_post_kernel, final, gated),
        grid=(G, R // tm),
        in_specs=[
            pl.BlockSpec((1, tm, ATT_WIDTH), lambda g, m: (g, m, 0)),
            pl.BlockSpec((1, tm, W), lambda g, m: (g, m, 0)),
            *gate_in,
            pl.BlockSpec((1, tm, D_MODEL), lambda g, m: (g, m, 0)),
            _mod_spec(gate, tm),
            pl.BlockSpec((D_MODEL, D_MODEL), lambda g, m: (0, 0)),
            pl.BlockSpec((1, D_MODEL), lambda g, m: (0, 0)),
        ],
        out_specs=pl.BlockSpec((1, tm, D_MODEL), lambda g, m: (g, m, 0)),
        out_shape=jax.ShapeDtypeStruct((G, R, D_MODEL), f32),
        compiler_params=_cparams(("parallel", "parallel")),
        name="post_proj",
    )(att, y_rw, *gate_arg, x, gate, w_out_bf, final_g.reshape(1, D_MODEL))


def _arrange_w_in(w):
    pad = jnp.zeros((w.shape[0], P_WIDTH - IN_WIDTH), w.dtype)
    parts = [w[:, Q_OFF:KA_OFF], w[:, GA_OFF:GR_OFF], w[:, GR_OFF:IN_WIDTH], w[:, R_OFF:WD_OFF],
             w[:, KA_OFF:R_OFF], w[:, WD_OFF:GA_OFF], pad]
    return jnp.concatenate(parts, axis=1).astype(bf16)


def _shift_cols(t):
    return jnp.concatenate([t[..., P_R:P_KA], t[..., P_WA:P_WA + LORA_PAIR]], axis=-1)


def kernel(x_prompt, x_sample, cache_k, cache_v, state_wkv, state_shift, c_prompt, c_sample,
           norm_g, w_ada, b_ada, w_in, mu_shift, w0, w_decay, a0, w_iclr, k_k, k_a, r_k,
           ln_w, ln_b, sinks, w_out, final_g):
    Bp, Tp = x_prompt.shape[0], x_prompt.shape[1]
    Bd = x_sample.shape[0]
    W = RWKV_WIDTH

    n_c = Bp + Bd
    c_rows = -(-n_c // 16) * 16
    c_all = jnp.concatenate([c_prompt, c_sample, jnp.zeros((c_rows - n_c, D_MODEL), f32)], axis=0)
    mod = _ada(c_all, w_ada, b_ada)

    tab_p = _rope_tables(jnp.arange(Tp, dtype=jnp.int32))
    tab_s = _rope_tables(jnp.full((Bd,), PAST_LEN, jnp.int32))

    hp = x_prompt
    hs = x_sample.reshape(1, Bd, D_MODEL)
    s0_p = jnp.zeros((Bp, RW_HEADS, RW_HEAD, RW_HEAD), f32)
    shift0_p = [jnp.zeros((Bp, 1, w), f32) for w in (W, W, W, LORA_PAIR)]
    ck_all = cache_k.reshape(DEPTH, Bd, WINDOW, KV_WIDTH)
    cv_all = cache_v.reshape(DEPTH, Bd, WINDOW, KV_WIDTH)
    new_state_s = jnp.zeros(state_wkv.shape, f32)
    new_cache_s = [jnp.zeros(ck_all.shape, f32), jnp.zeros(cv_all.shape, f32)]
    outs = {k: [] for k in ("kp", "vp", "sp", "shp", "shs")}
    for l in range(DEPTH):
        final = l == DEPTH - 1
        w_bf = _arrange_w_in(w_in[l])
        w_out_bf = w_out[l].astype(bf16)
        mu_l = mu_shift[l]
        mu = [mu_l[0:W].reshape(1, W), mu_l[W:2 * W].reshape(1, W), mu_l[2 * W:3 * W].reshape(1, W),
              mu_l[3 * W:].reshape(1, LORA_PAIR)]
        vecs = [t[l].reshape(1, W) for t in (w0, a0, k_k, k_a)]
        wd_pad = jnp.concatenate([w_decay[l], jnp.zeros((ICLR_LORA, W), f32)], axis=0)
        wi_pad = jnp.concatenate([jnp.zeros((DECAY_LORA, W), f32), w_iclr[l]], axis=0)
        shift_p, scale_p, gate_p = (mod[l, :Bp, i * D_MODEL:(i + 1) * D_MODEL].reshape(Bp, 1, D_MODEL)
                                    for i in range(3))
        shift_s, scale_s, gate_s = (mod[l, Bp:n_c, i * D_MODEL:(i + 1) * D_MODEL].reshape(1, Bd, D_MODEL)
                                    for i in range(3))

        proj = _norm_proj(hp, norm_g[l], scale_p, shift_p, w_bf, tab_p, tm=1024)
        att = _attn_prompt(proj, sinks[l])
        lora = [piece for wp in (wd_pad, wi_pad) for piece in _split_bf16(wp)]
        y_rw, s_t = _wkv_chunked(proj, shift0_p, mu, *vecs, lora, r_k[l], ln_w[l], ln_b[l], s0_p)
        hp = _post(att, y_rw, None, hp, gate_p, w_out_bf, final_g, final, tm=512)
        tail = proj[:, Tp - WINDOW:]
        outs["kp"].append(tail[..., P_KA:P_KA + KV_WIDTH].reshape(Bp, WINDOW, N_KV_HEADS, HEAD_DIM))
        outs["vp"].append(tail[..., P_VA:P_VA + KV_WIDTH].reshape(Bp, WINDOW, N_KV_HEADS, HEAD_DIM))
        outs["sp"].append(s_t)
        outs["shp"].append(_shift_cols(proj[:, Tp - 1]))

        proj = _norm_proj(hs, norm_g[l], scale_s, shift_s, w_bf, tab_s, tm=Bd)
        att, *new_cache_s = _attn_sample(proj, ck_all, cv_all, l, new_cache_s, sinks[l])
        sh = state_shift[l]
        prev = [sh[None, :, 0:W], sh[None, :, W:2 * W], sh[None, :, 2 * W:3 * W], sh[None, :, 3 * W:]]
        prep = _rwkv_prep(proj, prev, mu, *vecs, wd_pad, wi_pad, tm=Bd)
        prep = [t[:, None] for t in prep]
        y_rw, new_state_s = _wkv(*prep, r_k[l], ln_w[l], ln_b[l], state_wkv, l, new_state_s, tc=1)
        hs = _post(att, y_rw.reshape(1, Bd, W), proj, hs, gate_s, w_out_bf, final_g, final, tm=Bd)
        outs["shs"].append(_shift_cols(proj[0]))

    st = lambda k: jnp.stack(outs[k])
    return (hp, hs.reshape(Bd, 1, D_MODEL), st("kp"), st("vp"), st("sp"), st("shp"),
            *(t.reshape(cache_k.shape) for t in new_cache_s), new_state_s, st("shs"))
```
